```python
import math
import jax, jax.numpy as jnp
from jax import lax
import numpy as np

D_MODEL = 2048
BATCH = 2
SEQ = 4096
DEPTH = 1
DEC_BATCH = 8
DEC_SEQ = 4
PAST_LEN = 16384
PAGE_SIZE = 128

HEAD_DIM = 128
MIX_WIDTH = D_MODEL
CONV_CH = D_MODEL // 4
ATTN_WIDTH = MIX_WIDTH - CONV_CH
N_ATTN_HEADS = ATTN_WIDTH // HEAD_DIM
DILATION_GROUPS = ((128, 1), (512, 4), (2048, 16))
HEADS_PER_GROUP = N_ATTN_HEADS // len(DILATION_GROUPS)
CONV_K = 31
FFN_HIDDEN = ((8 * D_MODEL + 2) // 3 + 255) // 256 * 256
ROPE_THETA = 10000.0
DEEPNORM_ALPHA = (2.0 * DEPTH) ** 0.25
DEEPNORM_BETA = (8.0 * DEPTH) ** -0.25
LN_EPS = 1e-5
Q_BLOCK = 128
ATTN_SCALE = HEAD_DIM ** -0.5
W_IN_COLS = 3 * ATTN_WIDTH + 2 * CONV_CH
SPLITS = (ATTN_WIDTH, 2 * ATTN_WIDTH, 3 * ATTN_WIDTH, 3 * ATTN_WIDTH + CONV_CH)

kernel_name = 'hymba_dilated_conformer_deepnorm_step'


def _layer_norm(x, g, b):
    xf = x.astype(jnp.float32)
    mu = jnp.mean(xf, axis=-1, keepdims=True)
    var = jnp.mean(jnp.square(xf - mu), axis=-1, keepdims=True)
    return ((xf - mu) * lax.rsqrt(var + LN_EPS) * g.astype(jnp.float32) + b.astype(jnp.float32)).astype(x.dtype)


def _rope(x, pos):
    half = HEAD_DIM // 2
    inv = ROPE_THETA ** (-jnp.arange(half, dtype=jnp.float32) / half)
    ang = pos.astype(jnp.float32)[:, None] * inv[None, :]
    cos = jnp.cos(ang)[:, None, :]
    sin = jnp.sin(ang)[:, None, :]
    x1 = x[..., :half].astype(jnp.float32)
    x2 = x[..., half:].astype(jnp.float32)
    return jnp.concatenate([x1 * cos - x2 * sin, x2 * cos + x1 * sin], axis=-1).astype(x.dtype)


def _attend(s, v, spec):
    m = jnp.max(s, axis=-1, keepdims=True)
    e = jnp.exp(s - m)
    den = jnp.sum(e, axis=-1, keepdims=True)
    out = jnp.einsum(spec, (e / den).astype(v.dtype), v)
    lse = (m + jnp.log(den))[..., 0]
    return out, lse


def _dilated_prompt(q, k, v, window, dilation):
    B, S, H, Dh = q.shape
    L = S // dilation
    nk = window // dilation
    n_blk = -(-L // Q_BLOCK)
    Lp = n_blk * Q_BLOCK

    def to_classes(t):
        return t.reshape(B, L, dilation, H, Dh).transpose(0, 2, 1, 3, 4)

    qb = jnp.pad(to_classes(q), ((0, 0), (0, 0), (0, Lp - L), (0, 0), (0, 0)))
    qb = qb.reshape(B, dilation, n_blk, Q_BLOCK, H, Dh)
    kpad = ((0, 0), (0, 0), (nk, Lp - L), (0, 0), (0, 0))
    kc = jnp.pad(to_classes(k), kpad)
    vc = jnp.pad(to_classes(v), kpad)
    blk = jnp.arange(n_blk)[:, None]
    idx = blk * Q_BLOCK + jnp.arange(Q_BLOCK + nk)[None, :]
    kb = kc[:, :, idx]
    vb = vc[:, :, idx]
    s = jnp.einsum('brnqhe,brnkhe->brnhqk', qb, kb).astype(jnp.float32) * ATTN_SCALE
    qq = jnp.arange(Q_BLOCK)[:, None]
    kk = jnp.arange(Q_BLOCK + nk)[None, :]
    dist = qq - kk + nk
    kidx = blk[:, :, None] * Q_BLOCK + kk[None] - nk
    valid = (dist >= 0)[None] & (dist <= nk)[None] & (kidx >= 0)
    s = jnp.where(valid[None, None, :, None], s, -jnp.inf)
    out, lse = _attend(s, vb, 'brnhqk,brnkhe->brnqhe')
    out = out.reshape(B, dilation, Lp, H, Dh)[:, :, :L].transpose(0, 2, 1, 3, 4).reshape(B, S, H, Dh)
    lse = lse.transpose(0, 1, 2, 4, 3).reshape(B, dilation, Lp, H)[:, :, :L]
    lse = lse.transpose(0, 2, 1, 3).reshape(B, S, H)
    return out, lse


def _dilated_sample(q, k_hist, v_hist, window, dilation):
    B, T, H, Dh = q.shape
    buf = k_hist.shape[1] - T
    nk = window // dilation
    idx = buf + jnp.arange(T)[:, None] - dilation * jnp.arange(nk + 1)[None, :]
    valid = idx >= 0
    idx = jnp.maximum(idx, 0)
    kg = k_hist[:, idx]
    vg = v_hist[:, idx]
    s = jnp.einsum('bthe,btjhe->bhtj', q, kg).astype(jnp.float32) * ATTN_SCALE
    s = jnp.where(valid[None, None], s, -jnp.inf)
    out, lse = _attend(s, vg, 'bhtj,btjhe->bthe')
    return out, lse.transpose(0, 2, 1)


def _layer(x, pos, kv_bufs, conv_buf, w_in, w_out, conv_w, conv_b, conv_ln_g, conv_ln_b,
           ln1_g, ln1_b, w_gate, w_up, w_down, ln2_g, ln2_b):
    B, T, _ = x.shape
    h = jnp.einsum('btd,de->bte', x, w_in)
    q, k, v, a, g = jnp.split(h, SPLITS, axis=-1)
    q = _rope(q.reshape(B, T, N_ATTN_HEADS, HEAD_DIM), pos)
    k = _rope(k.reshape(B, T, N_ATTN_HEADS, HEAD_DIM), pos)
    v = v.reshape(B, T, N_ATTN_HEADS, HEAD_DIM)

    outs, lses, new_kv = [], [], []
    for gi, (window, dilation) in enumerate(DILATION_GROUPS):
        hs = slice(gi * HEADS_PER_GROUP, (gi + 1) * HEADS_PER_GROUP)
        qg, kg, vg = q[:, :, hs], k[:, :, hs], v[:, :, hs]
        kv_new = jnp.stack([kg, vg], axis=2)
        if kv_bufs is None:
            o, l = _dilated_prompt(qg, kg, vg, window, dilation)
            keep = min(window, T)
            new_kv.append(kv_new[:, T - keep:])
        else:
            kv_hist = jnp.concatenate([kv_bufs[gi], kv_new], axis=1)
            o, l = _dilated_sample(qg, kv_hist[:, :, 0], kv_hist[:, :, 1], window, dilation)
            new_kv.append(kv_hist[:, T:])
        outs.append(o)
        lses.append(l)
    alpha = jax.nn.softmax(jnp.stack(lses, axis=0), axis=0)
    attn = jnp.concatenate([o * alpha[i][..., None].astype(o.dtype) for i, o in enumerate(outs)], axis=2)
    attn = attn.reshape(B, T, ATTN_WIDTH)

    u = a * jax.nn.sigmoid(g)
    if conv_buf is None:
        conv_buf = jnp.zeros((B, CONV_K - 1, CONV_CH), u.dtype)
    u_hist = jnp.concatenate([conv_buf, u], axis=1)
    c = lax.conv_general_dilated(u_hist, conv_w[:, None, :].astype(u.dtype), (1,), 'VALID',
                                 dimension_numbers=('NWC', 'WIO', 'NWC'),
                                 feature_group_count=CONV_CH) + conv_b
    c = jax.nn.silu(_layer_norm(c, conv_ln_g, conv_ln_b))
    new_conv = u_hist[:, -(CONV_K - 1):]

    mix = jnp.einsum('bte,ed->btd', jnp.concatenate([attn, c], axis=-1), w_out)
    x1 = _layer_norm(DEEPNORM_ALPHA * x + mix, ln1_g, ln1_b)
    f = jax.nn.silu(jnp.einsum('btd,df->btf', x1, w_gate)) * jnp.einsum('btd,df->btf', x1, w_up)
    f = jnp.einsum('btf,fd->btd', f, w_down)
    x2 = _layer_norm(DEEPNORM_ALPHA * x1 + f, ln2_g, ln2_b)
    return x2, new_kv, new_conv


def setup_inputs(seed: int = 0) -> dict:
    key = jax.random.key(seed)
    ks = jax.random.split(key, 20)
    f32 = jnp.float32
    nrm = lambda k, shape: jax.random.normal(k, shape, f32)
    bufs = [min(w, PAST_LEN) for (w, _) in DILATION_GROUPS]
    kv_shape = lambda n: (DEPTH, DEC_BATCH, n, 2, HEADS_PER_GROUP, HEAD_DIM)
    col_scale = jnp.concatenate([jnp.ones((2 * ATTN_WIDTH,), f32),
                                 jnp.full((ATTN_WIDTH + CONV_CH,), DEEPNORM_BETA, f32),
                                 jnp.ones((CONV_CH,), f32)])
    return {
        'x_prompt': nrm(ks[0], (BATCH, SEQ, D_MODEL)),
        'x_sample': nrm(ks[1], (DEC_BATCH, DEC_SEQ, D_MODEL)),
        'cache_kv_w128': nrm(ks[2], kv_shape(bufs[0])),
        'cache_kv_w512': nrm(ks[3], kv_shape(bufs[1])),
        'cache_kv_w2048': nrm(ks[4], kv_shape(bufs[2])),
        'state_conv': 0.5 * nrm(ks[5], (DEPTH, DEC_BATCH, CONV_K - 1, CONV_CH)),
        'w_in': nrm(ks[6], (DEPTH, D_MODEL, W_IN_COLS)) * D_MODEL ** -0.5 * col_scale,
        'w_out': nrm(ks[7], (DEPTH, MIX_WIDTH, D_MODEL)) * MIX_WIDTH ** -0.5 * DEEPNORM_BETA,
        'conv_w': nrm(ks[8], (DEPTH, CONV_K, CONV_CH)) * CONV_K ** -0.5,
        'conv_b': 0.01 * nrm(ks[9], (DEPTH, CONV_CH)),
        'conv_ln_g': 1.0 + 0.01 * nrm(ks[10], (DEPTH, CONV_CH)),
        'conv_ln_b': 0.01 * nrm(ks[11], (DEPTH, CONV_CH)),
        'ln1_g': 1.0 + 0.01 * nrm(ks[12], (DEPTH, D_MODEL)),
        'ln1_b': 0.01 * nrm(ks[13], (DEPTH, D_MODEL)),
        'w_gate': nrm(ks[14], (DEPTH, D_MODEL, FFN_HIDDEN)) * D_MODEL ** -0.5 * DEEPNORM_BETA,
        'w_up': nrm(ks[15], (DEPTH, D_MODEL, FFN_HIDDEN)) * D_MODEL ** -0.5 * DEEPNORM_BETA,
        'w_down': nrm(ks[16], (DEPTH, FFN_HIDDEN, D_MODEL)) * FFN_HIDDEN ** -0.5 * DEEPNORM_BETA,
        'ln2_g': 1.0 + 0.01 * nrm(ks[17], (DEPTH, D_MODEL)),
        'ln2_b': 0.01 * nrm(ks[18], (DEPTH, D_MODEL)),
    }


def reference(x_prompt, x_sample, cache_kv_w128, cache_kv_w512, cache_kv_w2048, state_conv,
              w_in, w_out, conv_w, conv_b, conv_ln_g, conv_ln_b, ln1_g, ln1_b,
              w_gate, w_up, w_down, ln2_g, ln2_b):
    pos_p = jnp.arange(x_prompt.shape[1], dtype=jnp.int32)
    pos_s = PAST_LEN + jnp.arange(x_sample.shape[1], dtype=jnp.int32)
    xp, xs = x_prompt, x_sample
    kvp = [[], [], []]
    kvs = [[], [], []]
    convp, convs = [], []
    for l in range(DEPTH):
        params = (w_in[l], w_out[l], conv_w[l], conv_b[l], conv_ln_g[l], conv_ln_b[l],
                  ln1_g[l], ln1_b[l], w_gate[l], w_up[l], w_down[l], ln2_g[l], ln2_b[l])
        xp, nkv_p, nconv_p = _layer(xp, pos_p, None, None, *params)
        bufs = (cache_kv_w128[l], cache_kv_w512[l], cache_kv_w2048[l])
        xs, nkv_s, nconv_s = _layer(xs, pos_s, bufs, state_conv[l], *params)
        for gi in range(len(DILATION_GROUPS)):
            kvp[gi].append(nkv_p[gi])
            kvs[gi].append(nkv_s[gi])
        convp.append(nconv_p)
        convs.append(nconv_s)
    y_prompt, y_sample = xp, xs
    new_kv_w128_prompt = jnp.stack(kvp[0], axis=0)
    new_kv_w512_prompt = jnp.stack(kvp[1], axis=0)
    new_kv_w2048_prompt = jnp.stack(kvp[2], axis=0)
    new_conv_prompt = jnp.stack(convp, axis=0)
    new_kv_w128_sample = jnp.stack(kvs[0], axis=0)
    new_kv_w512_sample = jnp.stack(kvs[1], axis=0)
    new_kv_w2048_sample = jnp.stack(kvs[2], axis=0)
    new_conv_sample = jnp.stack(convs, axis=0)
    return (y_prompt, y_sample, new_kv_w128_prompt, new_kv_w512_prompt, new_kv_w2048_prompt, new_conv_prompt,
            new_kv_w128_sample, new_kv_w512_sample, new_kv_w2048_sample, new_conv_sample)
```

```python
import functools

import numpy as np
import jax
import jax.numpy as jnp
from jax import lax
from jax.experimental import pallas as pl
from jax.experimental.pallas import tpu as pltpu

D_MODEL = 2048
HEAD_DIM = 128
CONV_CH = D_MODEL // 4
ATTN_WIDTH = D_MODEL - CONV_CH
DILATION_GROUPS = ((128, 1), (512, 4), (2048, 16))
N_GROUPS = len(DILATION_GROUPS)
HEADS_PER_GROUP = ATTN_WIDTH // HEAD_DIM // N_GROUPS
GROUP_W = HEADS_PER_GROUP * HEAD_DIM
CONV_K = 31
ROPE_THETA = 10000.0
LN_EPS = 1e-5
Q_BLOCK = 128
ATTN_SCALE = HEAD_DIM ** -0.5
NEG_BIG = -1e30
PAST_LEN = 16384

F32 = jnp.float32
BF16 = jnp.bfloat16

VMEM_LIMIT = 56 * 1024 * 1024


def _cparams(n_axes):
    return pltpu.CompilerParams(dimension_semantics=("arbitrary",) * n_axes,
                                vmem_limit_bytes=VMEM_LIMIT)


def _layer_norm_rows(y, g, b):
    mu = jnp.mean(y, axis=-1, keepdims=True)
    yc = y - mu
    var = jnp.mean(yc * yc, axis=-1, keepdims=True)
    return yc * lax.rsqrt(var + LN_EPS) * g + b


def _rope_tables(pos):
    half = HEAD_DIM // 2
    inv = ROPE_THETA ** (-jnp.arange(half, dtype=F32) / half)
    ang = pos.astype(F32)[:, None] * inv[None, :]
    cos, sin = jnp.cos(ang), jnp.sin(ang)
    return jnp.concatenate([cos, cos], axis=1), jnp.concatenate([-sin, sin], axis=1)


def _rope(h, cos, sin):
    parts = []
    for hh in range(HEADS_PER_GROUP):
        hs = h[:, hh * HEAD_DIM:(hh + 1) * HEAD_DIM]
        parts.append(hs * cos + pltpu.roll(hs, HEAD_DIM // 2, axis=1) * sin)
    return jnp.concatenate(parts, axis=1)


IN_TM = 1024
IN_TN = GROUP_W


def _kv_window_plan(seq, batch, n_j):
    tiles_per_batch = seq // IN_TM
    n_i = batch * tiles_per_batch
    plans = []
    for gi, (window, _) in enumerate(DILATION_GROUPS):
        keep = min(window, seq)
        rb = min(keep, IN_TM)
        first_tile = (seq - keep) // IN_TM
        row_lo = (seq - keep) - first_tile * IN_TM
        writes = []
        for i in range(n_i):
            b, it = divmod(i, tiles_per_batch)
            if it >= first_tile:
                for c, j in enumerate((N_GROUPS + gi, 2 * N_GROUPS + gi)):
                    writes.append((i * n_j + j, (b, it - first_tile, c)))
        writes.sort()
        tab = np.zeros((n_i * n_j, 3), np.int32)
        w = 0
        for step in range(n_i * n_j):
            while w < len(writes) - 1 and writes[w][0] < step:
                w += 1
            tab[step] = writes[w][1]
        plans.append(dict(keep=keep, rb=rb, first_tile=first_tile, row_lo=row_lo, tab=tab))
    return plans


def _in_proj_kernel(tab_ref, x_ref, w_ref, cos_ref, sin_ref, xs_ref, coss_ref, sins_ref,
                    qkv_ref, u_ref, kv0_ref, kv1_ref, kv2_ref, qkvs_ref, us_ref,
                    xbf, a_scr, xsbf, as_scr, *, n_i, tiles_per_batch, plans):
    del tab_ref
    i = pl.program_id(0)
    j = pl.program_id(1)
    kv_refs = (kv0_ref, kv1_ref, kv2_ref)

    @pl.when(j == 0)
    def _():
        xbf[...] = x_ref[...].astype(BF16)

    acc = jnp.dot(xbf[...], w_ref[...], preferred_element_type=F32)

    @pl.when(j < N_GROUPS)
    def _():
        qkv_ref[...] = _rope(acc, cos_ref[...], sin_ref[...]).astype(BF16)

    it = i % tiles_per_batch
    for gi in range(N_GROUPS):
        plan = plans[gi]
        in_window = it >= plan["first_tile"]

        @pl.when(j == N_GROUPS + gi)
        def _(plan=plan, in_window=in_window, gi=gi):
            r = _rope(acc, cos_ref[...], sin_ref[...])
            qkv_ref[...] = r.astype(BF16)

            @pl.when(in_window)
            def _():
                kv_refs[gi][0] = r[plan["row_lo"]:plan["row_lo"] + plan["rb"], :]

        @pl.when(j == 2 * N_GROUPS + gi)
        def _(plan=plan, in_window=in_window, gi=gi):
            qkv_ref[...] = acc.astype(BF16)

            @pl.when(in_window)
            def _():
                kv_refs[gi][0] = acc[plan["row_lo"]:plan["row_lo"] + plan["rb"], :]

    @pl.when(j == 3 * N_GROUPS)
    def _():
        a_scr[...] = acc

    @pl.when(j == 3 * N_GROUPS + 1)
    def _():
        u_ref[...] = a_scr[...] * jax.nn.sigmoid(acc)

    @pl.when(i == n_i - 1)
    def _():
        @pl.when(j == 0)
        def _():
            xsbf[...] = xs_ref[...].astype(BF16)

        accs = jnp.dot(xsbf[...], w_ref[...], preferred_element_type=F32)

        @pl.when(j < 2 * N_GROUPS)
        def _():
            qkvs_ref[...] = _rope(accs, coss_ref[...], sins_ref[...])

        @pl.when((j >= 2 * N_GROUPS) & (j < 3 * N_GROUPS))
        def _():
            qkvs_ref[...] = accs

        @pl.when(j == 3 * N_GROUPS)
        def _():
            as_scr[...] = accs

        @pl.when(j == 3 * N_GROUPS + 1)
        def _():
            us_ref[...] = as_scr[...] * jax.nn.sigmoid(accs)


def _in_proj(x2, xs2, w_bf, cos_p, sin_p, cos_s, sin_s, batch, seq):
    m = x2.shape[0]
    ms = xs2.shape[0]
    n_i = m // IN_TM
    n_j = w_bf.shape[1] // IN_TN
    n_qkv = 3 * N_GROUPS
    tiles_per_batch = seq // IN_TM
    plans = _kv_window_plan(seq, batch, n_j)
    tab = jnp.asarray(np.stack([p["tab"] for p in plans]).reshape(-1))
    n_steps = n_i * n_j

    def kv_map(gi):
        def f(i, j, tab_ref):
            base = (gi * n_steps + i * n_j + j) * 3
            return tab_ref[base], tab_ref[base + 1], tab_ref[base + 2]
        return f

    last = n_i - 1
    in_specs = [
        pl.BlockSpec((IN_TM, D_MODEL), lambda i, j, t: (i, 0)),
        pl.BlockSpec((D_MODEL, IN_TN), lambda i, j, t: (0, j)),
        pl.BlockSpec((IN_TM, HEAD_DIM), lambda i, j, t: (i % tiles_per_batch, 0)),
        pl.BlockSpec((IN_TM, HEAD_DIM), lambda i, j, t: (i % tiles_per_batch, 0)),
        pl.BlockSpec((ms, D_MODEL), lambda i, j, t: (0, 0)),
        pl.BlockSpec((ms, HEAD_DIM), lambda i, j, t: (0, 0)),
        pl.BlockSpec((ms, HEAD_DIM), lambda i, j, t: (0, 0)),
    ]
    out_shape = [
        jax.ShapeDtypeStruct((m, n_qkv * IN_TN), BF16),
        jax.ShapeDtypeStruct((m, CONV_CH), F32),
    ]
    out_specs = [
        pl.BlockSpec((IN_TM, IN_TN), lambda i, j, t: (i, jnp.minimum(j, n_qkv - 1))),
        pl.BlockSpec((IN_TM, CONV_CH), lambda i, j, t: (i, 0)),
    ]
    for gi, p in enumerate(plans):
        out_shape.append(jax.ShapeDtypeStruct((batch, p["keep"], 2 * GROUP_W), F32))
        out_specs.append(pl.BlockSpec((1, p["rb"], GROUP_W), kv_map(gi)))
    out_shape += [jax.ShapeDtypeStruct((ms, n_qkv * IN_TN), F32),
                  jax.ShapeDtypeStruct((ms, CONV_CH), F32)]
    out_specs += [
        pl.BlockSpec((ms, IN_TN),
                     lambda i, j, t: (0, jnp.where(i == last, jnp.minimum(j, n_qkv - 1), 0))),
        pl.BlockSpec((ms, CONV_CH), lambda i, j, t: (0, 0)),
    ]
    grid_spec = pltpu.PrefetchScalarGridSpec(
        num_scalar_prefetch=1, grid=(n_i, n_j), in_specs=in_specs, out_specs=out_specs,
        scratch_shapes=[pltpu.VMEM((IN_TM, D_MODEL), BF16), pltpu.VMEM((IN_TM, CONV_CH), F32),
                        pltpu.VMEM((ms, D_MODEL), BF16), pltpu.VMEM((ms, CONV_CH), F32)])
    kern = functools.partial(_in_proj_kernel, n_i=n_i, tiles_per_batch=tiles_per_batch, plans=plans)
    return pl.pallas_call(kern, grid_spec=grid_spec, out_shape=out_shape,
                          compiler_params=_cparams(2), name="in_proj")(
        tab, x2, w_bf, cos_p, sin_p, xs2, cos_s, sin_s)


def _attn_kernel(q_ref, kc_ref, kp_ref, vc_ref, vp_ref, o_ref, lse_ref, k_scr, v_scr, *, lc):
    n = pl.program_id(2)
    k_scr[0:Q_BLOCK, :] = kp_ref[0]
    k_scr[Q_BLOCK:, :] = kc_ref[0]
    v_scr[0:Q_BLOCK, :] = vp_ref[0]
    v_scr[Q_BLOCK:, :] = vc_ref[0]
    qq = lax.broadcasted_iota(jnp.int32, (Q_BLOCK, 2 * Q_BLOCK), 0)
    kk = lax.broadcasted_iota(jnp.int32, (Q_BLOCK, 2 * Q_BLOCK), 1)
    band = (kk >= qq) & (kk <= qq + Q_BLOCK)
    band_first = band & ((kk >= Q_BLOCK) | (n > 0))
    lane = lax.broadcasted_iota(jnp.int32, (Q_BLOCK, HEAD_DIM), 1)
    for qb in range(lc // Q_BLOCK):
        valid = band_first if qb == 0 else band
        r0 = qb * Q_BLOCK
        lse_blk = jnp.zeros((Q_BLOCK, HEAD_DIM), F32)
        for h in range(HEADS_PER_GROUP):
            c0 = h * HEAD_DIM
            q = q_ref[0, r0:r0 + Q_BLOCK, c0:c0 + HEAD_DIM]
            k = k_scr[r0:r0 + 2 * Q_BLOCK, c0:c0 + HEAD_DIM]
            v = v_scr[r0:r0 + 2 * Q_BLOCK, c0:c0 + HEAD_DIM]
            s = lax.dot_general(q, k, (((1,), (1,)), ((), ())), preferred_element_type=F32)
            s = jnp.where(valid, s * ATTN_SCALE, NEG_BIG)
            m = jnp.max(s, axis=1, keepdims=True)
            e = jnp.exp(s - m)
            den = jnp.sum(e, axis=1, keepdims=True)
            o = jnp.dot(e.astype(BF16), v, preferred_element_type=F32) / den
            o_ref[0, r0:r0 + Q_BLOCK, c0:c0 + HEAD_DIM] = o.astype(BF16)
            lse_blk = jnp.where(lane == h, m + jnp.log(den), lse_blk)
        lse_ref[0, r0:r0 + Q_BLOCK, :] = lse_blk


def _attn_group(qkv2, gi, batch, seq):
    _, d = DILATION_GROUPS[gi]
    n_cls = seq // d
    lc = min(n_cls, 512)
    n_c = n_cls // lc
    row_w = qkv2.shape[1]
    cols = row_w // GROUP_W
    qkv3 = qkv2.reshape(batch, n_cls, d * row_w)
    sub = lc // Q_BLOCK

    def cur(off):
        return pl.BlockSpec((1, lc, GROUP_W), lambda b, r, n: (b, n, r * cols + off + gi))

    def prev(off):
        return pl.BlockSpec((1, Q_BLOCK, GROUP_W),
                            lambda b, r, n: (b, jnp.maximum(n * sub - 1, 0), r * cols + off + gi))

    o, lse = pl.pallas_call(
        functools.partial(_attn_kernel, lc=lc),
        grid=(batch, d, n_c),
        in_specs=[cur(0), cur(N_GROUPS), prev(N_GROUPS), cur(2 * N_GROUPS), prev(2 * N_GROUPS)],
        out_specs=[pl.BlockSpec((1, lc, GROUP_W), lambda b, r, n: (b, n, r)),
                   pl.BlockSpec((1, lc, HEAD_DIM), lambda b, r, n: (b, n, r))],
        out_shape=[jax.ShapeDtypeStruct((batch, n_cls, d * GROUP_W), BF16),
                   jax.ShapeDtypeStruct((batch, n_cls, d * HEAD_DIM), F32)],
        scratch_shapes=[pltpu.VMEM((lc + Q_BLOCK, GROUP_W), BF16),
                        pltpu.VMEM((lc + Q_BLOCK, GROUP_W), BF16)],
        compiler_params=_cparams(3), name=f"attn_g{gi}")(qkv3, qkv3, qkv3, qkv3, qkv3)
    return o.reshape(batch * seq, GROUP_W), lse.reshape(batch * seq, HEAD_DIM)


CONV_TC = 512
CONV_HALO = 32
CONV_ROWS = 64


def _conv_kernel(u_ref, halo_ref, w_ref, b_ref, g_ref, beta_ref, c_ref, hist):
    t = pl.program_id(1)
    hist[0:CONV_HALO, :] = jnp.where(t > 0, halo_ref[0], 0.0)
    hist[CONV_HALO:, :] = u_ref[0]
    lead = CONV_HALO - (CONV_K - 1)
    for rc in range(CONV_TC // CONV_ROWS):
        r0 = rc * CONV_ROWS
        acc = jnp.broadcast_to(b_ref[...], (CONV_ROWS, CONV_CH))
        for k in range(CONV_K):
            acc = acc + hist[r0 + lead + k:r0 + lead + k + CONV_ROWS, :] * w_ref[k:k + 1, :]
        y = _layer_norm_rows(acc, g_ref[...], beta_ref[...])
        c_ref[0, r0:r0 + CONV_ROWS, :] = (y * jax.nn.sigmoid(y)).astype(BF16)


def _conv_prompt(u2, conv_w, conv_b, ln_g, ln_b, batch, seq):
    u3 = u2.reshape(batch, seq, CONV_CH)
    per = CONV_TC // CONV_HALO
    vec = pl.BlockSpec((1, CONV_CH), lambda b, t: (0, 0))
    c = pl.pallas_call(
        _conv_kernel,
        grid=(batch, seq // CONV_TC),
        in_specs=[pl.BlockSpec((1, CONV_TC, CONV_CH), lambda b, t: (b, t, 0)),
                  pl.BlockSpec((1, CONV_HALO, CONV_CH), lambda b, t: (b, jnp.maximum(t * per - 1, 0), 0)),
                  pl.BlockSpec((CONV_K, CONV_CH), lambda b, t: (0, 0)), vec, vec, vec],
        out_specs=pl.BlockSpec((1, CONV_TC, CONV_CH), lambda b, t: (b, t, 0)),
        out_shape=jax.ShapeDtypeStruct((batch, seq, CONV_CH), BF16),
        scratch_shapes=[pltpu.VMEM((CONV_TC + CONV_HALO, CONV_CH), F32)],
        compiler_params=_cparams(2), name="conv_prompt")(
        u3, u3, conv_w, conv_b[None, :], ln_g[None, :], ln_b[None, :])
    return c.reshape(batch * seq, CONV_CH)


def _sample_kernel(qkv_ref, c0_ref, c1_ref, c2_ref, st_ref, u_ref, w_ref, b_ref, g_ref, beta_ref,
                   mix_ref, nconv_ref, hist, *, t_new):
    cache_refs = (c0_ref, c1_ref, c2_ref)
    kv_w = 2 * GROUP_W
    row = lax.broadcasted_iota(jnp.int32, (cache_refs[0].shape[1], 1), 0)
    outs = [[None] * N_GROUPS for _ in range(t_new)]
    lses = [[None] * N_GROUPS for _ in range(t_new)]
    for gi, (_, d) in enumerate(DILATION_GROUPS):
        cref = cache_refs[gi]
        for t in range(t_new):
            col0 = (t % d) * kv_w
            first_row = t // d
            newest = [t - d * jj for jj in range(t // d + 1)]
            o_heads, l_heads = [], []
            for h in range(HEADS_PER_GROUP):
                c = h * HEAD_DIM
                q = qkv_ref[0, t:t + 1, gi * GROUP_W + c:gi * GROUP_W + c + HEAD_DIM]
                kc = cref[0, :, col0 + c:col0 + c + HEAD_DIM]
                vc = cref[0, :, col0 + GROUP_W + c:col0 + GROUP_W + c + HEAD_DIM]
                s_c = jnp.sum(kc * q, axis=1, keepdims=True) * ATTN_SCALE
                if first_row > 0:
                    s_c = jnp.where(row >= first_row, s_c, NEG_BIG)
                kn, vn, s_n = [], [], []
                for tn in newest:
                    k_off = ATTN_WIDTH + gi * GROUP_W + c
                    v_off = 2 * ATTN_WIDTH + gi * GROUP_W + c
                    kn.append(qkv_ref[0, tn:tn + 1, k_off:k_off + HEAD_DIM])
                    vn.append(qkv_ref[0, tn:tn + 1, v_off:v_off + HEAD_DIM])
                    s_n.append(jnp.sum(kn[-1] * q, axis=1, keepdims=True) * ATTN_SCALE)
                m = jnp.max(s_c, axis=0, keepdims=True)
                for sn in s_n:
                    m = jnp.maximum(m, sn)
                e_c = jnp.exp(s_c - m)
                den = jnp.sum(e_c, axis=0, keepdims=True)
                acc = jnp.sum(e_c * vc, axis=0, keepdims=True)
                for sn, v1 in zip(s_n, vn):
                    e_n = jnp.exp(sn - m)
                    den = den + e_n
                    acc = acc + e_n * v1
                o_heads.append(acc / den)
                l_heads.append(m + jnp.log(den))
            outs[t][gi] = o_heads
            lses[t][gi] = l_heads
    for t in range(t_new):
        for h in range(HEADS_PER_GROUP):
            ls = [lses[t][gi][h] for gi in range(N_GROUPS)]
            m = functools.reduce(jnp.maximum, ls)
            es = [jnp.exp(l - m) for l in ls]
            tot = functools.reduce(lambda a, b: a + b, es)
            for gi in range(N_GROUPS):
                c0 = gi * GROUP_W + h * HEAD_DIM
                mix_ref[0, t:t + 1, c0:c0 + HEAD_DIM] = outs[t][gi][h] * (es[gi] / tot)
    n_state = CONV_K - 1
    hist[0:n_state, :] = st_ref[0]
    hist[n_state:n_state + t_new, :] = u_ref[0]
    acc = jnp.broadcast_to(b_ref[...], (t_new, CONV_CH))
    for k in range(CONV_K):
        acc = acc + hist[k:k + t_new, :] * w_ref[k:k + 1, :]
    y = _layer_norm_rows(acc, g_ref[...], beta_ref[...])
    mix_ref[0, :, ATTN_WIDTH:] = y * jax.nn.sigmoid(y)
    nconv_ref[0] = hist[t_new:t_new + n_state, :]


def _sample_mixers(qkvs, us, caches, state, conv_w, conv_b, ln_g, ln_b, dec_batch, t_new):
    row_w = qkvs.shape[1]
    qkv3 = qkvs.reshape(dec_batch, t_new, row_w)
    u3 = us.reshape(dec_batch, t_new, CONV_CH)
    kv_w = 2 * GROUP_W
    cache_in, cache_specs = [], []
    for (window, d), cache in zip(DILATION_GROUPS, caches):
        buf = cache.shape[1]
        assert buf == window and buf % d == 0
        n_rows = buf // d
        cache_in.append(cache.reshape(dec_batch, n_rows, d * kv_w))
        used = min(d, t_new) * kv_w
        cache_specs.append(pl.BlockSpec((1, n_rows, used), lambda b: (b, 0, 0)))
    vec = pl.BlockSpec((1, CONV_CH), lambda b: (0, 0))
    n_state = CONV_K - 1
    mix, nconv = pl.pallas_call(
        functools.partial(_sample_kernel, t_new=t_new),
        grid=(dec_batch,),
        in_specs=[pl.BlockSpec((1, t_new, row_w), lambda b: (b, 0, 0))] + cache_specs + [
            pl.BlockSpec((1, n_state, CONV_CH), lambda b: (b, 0, 0)),
            pl.BlockSpec((1, t_new, CONV_CH), lambda b: (b, 0, 0)),
            pl.BlockSpec((CONV_K, CONV_CH), lambda b: (0, 0)), vec, vec, vec],
        out_specs=[pl.BlockSpec((1, t_new, D_MODEL), lambda b: (b, 0, 0)),
                   pl.BlockSpec((1, n_state, CONV_CH), lambda b: (b, 0, 0))],
        out_shape=[jax.ShapeDtypeStruct((dec_batch, t_new, D_MODEL), F32),
                   jax.ShapeDtypeStruct((dec_batch, n_state, CONV_CH), F32)],
        scratch_shapes=[pltpu.VMEM((n_state + t_new + 6, CONV_CH), F32)],
        compiler_params=_cparams(1), name="sample_mixers")(
        qkv3, *cache_in, state, u3, conv_w, conv_b[None, :], ln_g[None, :], ln_b[None, :])
    return mix.reshape(dec_batch * t_new, D_MODEL), nconv


OUT_TM = 512


def _out_proj_kernel(o0_ref, o1_ref, o2_ref, l0_ref, l1_ref, l2_ref, c_ref, x_ref, w_ref, g_ref, b_ref,
                     mixs_ref, xs_ref, x1_ref, x1bf_ref, x1s_ref, mix_scr, *, n_i, alpha):
    i = pl.program_id(0)
    o_refs = (o0_ref, o1_ref, o2_ref)
    ls = [l0_ref[...], l1_ref[...], l2_ref[...]]
    m = jnp.maximum(jnp.maximum(ls[0], ls[1]), ls[2])
    es = [jnp.exp(l - m) for l in ls]
    inv = 1.0 / (es[0] + es[1] + es[2])
    for gi in range(N_GROUPS):
        a = es[gi] * inv
        for h in range(HEADS_PER_GROUP):
            c0 = h * HEAD_DIM
            og = o_refs[gi][:, c0:c0 + HEAD_DIM].astype(F32)
            mix_scr[:, gi * GROUP_W + c0:gi * GROUP_W + c0 + HEAD_DIM] = (og * a[:, h:h + 1]).astype(BF16)
    mix_scr[:, ATTN_WIDTH:] = c_ref[...]
    y = jnp.dot(mix_scr[...], w_ref[...], preferred_element_type=F32) + alpha * x_ref[...]
    x1 = _layer_norm_rows(y, g_ref[...], b_ref[...])
    x1_ref[...] = x1
    x1bf_ref[...] = x1.astype(BF16)

    @pl.when(i == n_i - 1)
    def _():
        ys = jnp.dot(mixs_ref[...].astype(BF16), w_ref[...], preferred_element_type=F32) + alpha * xs_ref[...]
        x1s_ref[...] = _layer_norm_rows(ys, g_ref[...], b_ref[...])


def _out_proj(os, lses, c, x2, w_bf, ln_g, ln_b, mixs, xs2, alpha):
    m = x2.shape[0]
    ms = xs2.shape[0]
    n_i = m // OUT_TM
    row = lambda w: pl.BlockSpec((OUT_TM, w), lambda i: (i, 0))
    whole = lambda a: pl.BlockSpec(a.shape, lambda i: (0,) * a.ndim)
    g2, b2 = ln_g[None, :], ln_b[None, :]
    return pl.pallas_call(
        functools.partial(_out_proj_kernel, n_i=n_i, alpha=alpha),
        grid=(n_i,),
        in_specs=[row(GROUP_W)] * 3 + [row(HEAD_DIM)] * 3 + [row(CONV_CH), row(D_MODEL),
                  whole(w_bf), whole(g2), whole(b2), whole(mixs), whole(xs2)],
        out_specs=[row(D_MODEL), row(D_MODEL), pl.BlockSpec((ms, D_MODEL), lambda i: (0, 0))],
        out_shape=[jax.ShapeDtypeStruct((m, D_MODEL), F32), jax.ShapeDtypeStruct((m, D_MODEL), BF16),
                   jax.ShapeDtypeStruct((ms, D_MODEL), F32)],
        scratch_shapes=[pltpu.VMEM((OUT_TM, D_MODEL), BF16)],
        compiler_params=_cparams(1), name="out_proj")(
        *os, *lses, c, x2, w_bf, g2, b2, mixs, xs2)


FFN_TM = 1024
FFN_TF = 512
FFN_RES_W = 256


def _ffn_kernel(xbf_ref, xres_ref, wg_ref, wu_ref, wd_ref, g_ref, b_ref, xs_ref,
                y_ref, ys_ref, xsbf, *, n_i, n_f, alpha):
    i = pl.program_id(0)
    f = pl.program_id(1)

    def swiglu_down(xb):
        gate = jnp.dot(xb, wg_ref[...], preferred_element_type=F32)
        up = jnp.dot(xb, wu_ref[...], preferred_element_type=F32)
        act = (gate * jax.nn.sigmoid(gate) * up).astype(BF16)
        return jnp.dot(act, wd_ref[...], preferred_element_type=F32)

    down = swiglu_down(xbf_ref[...])

    @pl.when(f == 0)
    def _():
        y_ref[...] = down

    @pl.when(f > 0)
    def _():
        y_ref[...] += down

    for c in range(D_MODEL // FFN_RES_W):
        @pl.when(f == c)
        def _(c=c):
            y_ref[:, c * FFN_RES_W:(c + 1) * FFN_RES_W] += alpha * xres_ref[...]

    @pl.when(f == n_f - 1)
    def _():
        y_ref[...] = _layer_norm_rows(y_ref[...], g_ref[...], b_ref[...])

    @pl.when(i == n_i - 1)
    def _():
        @pl.when(f == 0)
        def _():
            xsbf[...] = xs_ref[...].astype(BF16)

        downs = swiglu_down(xsbf[...])

        @pl.when(f == 0)
        def _():
            ys_ref[...] = downs + alpha * xs_ref[...]

        @pl.when(f > 0)
        def _():
            ys_ref[...] += downs

        @pl.when(f == n_f - 1)
        def _():
            ys_ref[...] = _layer_norm_rows(ys_ref[...], g_ref[...], b_ref[...])


def _ffn(x1, x1bf, x1s, wg_bf, wu_bf, wd_bf, ln_g, ln_b, alpha):
    m = x1.shape[0]
    ms = x1s.shape[0]
    hidden = wg_bf.shape[1]
    n_i = m // FFN_TM
    n_f = hidden // FFN_TF
    n_res = D_MODEL // FFN_RES_W
    assert n_f >= n_res
    g2, b2 = ln_g[None, :], ln_b[None, :]
    return pl.pallas_call(
        functools.partial(_ffn_kernel, n_i=n_i, n_f=n_f, alpha=alpha),
        grid=(n_i, n_f),
        in_specs=[pl.BlockSpec((FFN_TM, D_MODEL), lambda i, f: (i, 0)),
                  pl.BlockSpec((FFN_TM, FFN_RES_W), lambda i, f: (i, jnp.minimum(f, n_res - 1))),
                  pl.BlockSpec((D_MODEL, FFN_TF), lambda i, f: (0, f)),
                  pl.BlockSpec((D_MODEL, FFN_TF), lambda i, f: (0, f)),
                  pl.BlockSpec((FFN_TF, D_MODEL), lambda i, f: (f, 0)),
                  pl.BlockSpec((1, D_MODEL), lambda i, f: (0, 0)),
                  pl.BlockSpec((1, D_MODEL), lambda i, f: (0, 0)),
                  pl.BlockSpec((ms, D_MODEL), lambda i, f: (0, 0))],
        out_specs=[pl.BlockSpec((FFN_TM, D_MODEL), lambda i, f: (i, 0)),
                   pl.BlockSpec((ms, D_MODEL), lambda i, f: (0, 0))],
        out_shape=[jax.ShapeDtypeStruct((m, D_MODEL), F32), jax.ShapeDtypeStruct((ms, D_MODEL), F32)],
        scratch_shapes=[pltpu.VMEM((ms, D_MODEL), BF16)],
        compiler_params=_cparams(2), name="ffn")(
        x1bf, x1, wg_bf, wu_bf, wd_bf, g2, b2, x1s)


def kernel(x_prompt, x_sample, cache_kv_w128, cache_kv_w512, cache_kv_w2048, state_conv, w_in, w_out,
           conv_w, conv_b, conv_ln_g, conv_ln_b, ln1_g, ln1_b, w_gate, w_up, w_down, ln2_g, ln2_b):
    depth = w_in.shape[0]
    batch, seq, _ = x_prompt.shape
    dec_batch, t_new, _ = x_sample.shape
    caches = (cache_kv_w128, cache_kv_w512, cache_kv_w2048)
    alpha = (2.0 * depth) ** 0.25

    cos_p, sin_p = _rope_tables(jnp.arange(seq, dtype=jnp.int32))
    pos_s = PAST_LEN + jnp.arange(t_new, dtype=jnp.int32)
    cos_s, sin_s = _rope_tables(jnp.tile(pos_s, dec_batch))

    xp = x_prompt.reshape(batch * seq, D_MODEL)
    xs = x_sample.reshape(dec_batch * t_new, D_MODEL)
    kvp = [[] for _ in range(N_GROUPS)]
    kvs = [[] for _ in range(N_GROUPS)]
    convp, convs = [], []
    for l in range(depth):
        qkv, u, kv0, kv1, kv2, qkvs, us = _in_proj(
            xp, xs, w_in[l].astype(BF16), cos_p, sin_p, cos_s, sin_s, batch, seq)
        os, lses = zip(*[_attn_group(qkv, gi, batch, seq) for gi in range(N_GROUPS)])
        c = _conv_prompt(u, conv_w[l], conv_b[l], conv_ln_g[l], conv_ln_b[l], batch, seq)
        mixs, nconv_s = _sample_mixers(qkvs, us, [cc[l] for cc in caches], state_conv[l],
                                       conv_w[l], conv_b[l], conv_ln_g[l], conv_ln_b[l], dec_batch, t_new)
        x1, x1bf, x1s = _out_proj(os, lses, c, xp, w_out[l].astype(BF16), ln1_g[l], ln1_b[l], mixs, xs, alpha)
        xp, xs = _ffn(x1, x1bf, x1s, w_gate[l].astype(BF16), w_up[l].astype(BF16), w_down[l].astype(BF16),
                      ln2_g[l], ln2_b[l], alpha)

        for gi, kv in enumerate((kv0, kv1, kv2)):
            kvp[gi].append(kv.reshape(batch, kv.shape[1], 2, HEADS_PER_GROUP, HEAD_DIM))
            k_new = qkvs[:, ATTN_WIDTH + gi * GROUP_W:ATTN_WIDTH + (gi + 1) * GROUP_W]
            v_new = qkvs[:, 2 * ATTN_WIDTH + gi * GROUP_W:2 * ATTN_WIDTH + (gi + 1) * GROUP_W]
            kv_new = jnp.stack([k_new, v_new], axis=1).reshape(dec_batch, t_new, 2, HEADS_PER_GROUP, HEAD_DIM)
            kvs[gi].append(jnp.concatenate([caches[gi][l][:, t_new:], kv_new], axis=1))
        convp.append(u.reshape(batch, seq, CONV_CH)[:, seq - (CONV_K - 1):])
        convs.append(nconv_s)

    y_prompt = xp.reshape(batch, seq, D_MODEL)
    y_sample = xs.reshape(dec_batch, t_new, D_MODEL)
    return (y_prompt, y_sample, jnp.stack(kvp[0]), jnp.stack(kvp[1]), jnp.stack(kvp[2]), jnp.stack(convp),
            jnp.stack(kvs[0]), jnp.stack(kvs[1]), jnp.stack(kvs[2]), jnp.stack(convs))
```

```python
import functools

import numpy as np
import jax
import jax.numpy as jnp
from jax import lax
from jax.experimental import pallas as pl
from jax.experimental.pallas import tpu as pltpu

D_MODEL = 2048
HEAD_DIM = 128
CONV_CH = D_MODEL // 4
ATTN_WIDTH = D_MODEL - CONV_CH
DILATION_GROUPS = ((128, 1), (512, 4), (2048, 16))
N_GROUPS = len(DILATION_GROUPS)
HEADS_PER_GROUP = ATTN_WIDTH // HEAD_DIM // N_GROUPS
GROUP_W = HEADS_PER_GROUP * HEAD_DIM
CONV_K = 31
ROPE_THETA = 10000.0
LN_EPS = 1e-5
Q_BLOCK = 128
ATTN_SCALE = HEAD_DIM ** -0.5
NEG_BIG = -1e30
PAST_LEN = 16384

F32 = jnp.float32
BF16 = jnp.bfloat16

VMEM_LIMIT = 56 * 1024 * 1024


def _cparams(n_axes):
    return pltpu.CompilerParams(dimension_semantics=("arbitrary",) * n_axes,
                                vmem_limit_bytes=VMEM_LIMIT)


def _layer_norm_rows(y, g, b):
    mu = jnp.mean(y, axis=-1, keepdims=True)
    yc = y - mu
    var = jnp.mean(yc * yc, axis=-1, keepdims=True)
    return yc * lax.rsqrt(var + LN_EPS) * g + b


def _rope_tables(pos):
    half = HEAD_DIM // 2
    inv = ROPE_THETA ** (-jnp.arange(half, dtype=F32) / half)
    ang = pos.astype(F32)[:, None] * inv[None, :]
    cos, sin = jnp.cos(ang), jnp.sin(ang)
    return jnp.concatenate([cos, cos], axis=1), jnp.concatenate([-sin, sin], axis=1)


def _rope(h, cos, sin):
    parts = []
    for hh in range(HEADS_PER_GROUP):
        hs = h[:, hh * HEAD_DIM:(hh + 1) * HEAD_DIM]
        parts.append(hs * cos + pltpu.roll(hs, HEAD_DIM // 2, axis=1) * sin)
    return jnp.concatenate(parts, axis=1)


IN_TM = 1024
IN_TN = GROUP_W


def _kv_window_plan(seq, batch, n_j):
    tiles_per_batch = seq // IN_TM
    n_i = batch * tiles_per_batch
    plans = []
    for gi, (window, _) in enumerate(DILATION_GROUPS):
        keep = min(window, seq)
        rb = min(keep, IN_TM)
        first_tile = (seq - keep) // IN_TM
        row_lo = (seq - keep) - first_tile * IN_TM
        writes = []
        for i in range(n_i):
            b, it = divmod(i, tiles_per_batch)
            if it >= first_tile:
                for c, j in enumerate((N_GROUPS + gi, 2 * N_GROUPS + gi)):
                    writes.append((i * n_j + j, (b, it - first_tile, c)))
        writes.sort()
        tab = np.zeros((n_i * n_j, 3), np.int32)
        w = 0
        for step in range(n_i * n_j):
            while w < len(writes) - 1 and writes[w][0] < step:
                w += 1
            tab[step] = writes[w][1]
        plans.append(dict(keep=keep, rb=rb, first_tile=first_tile, row_lo=row_lo, tab=tab))
    return plans


def _in_proj_kernel(tab_ref, x_ref, w_ref, cos_ref, sin_ref, xs_ref, coss_ref, sins_ref,
                    cm0_ref, cm1_ref, cm2_ref, u_ref, kv0_ref, kv1_ref, kv2_ref, qkvs_ref, us_ref,
                    xbf, a_scr, de_scr, xsbf, as_scr, *, n_i, tiles_per_batch, plans):
    del tab_ref
    i = pl.program_id(0)
    j = pl.program_id(1)
    kv_refs = (kv0_ref, kv1_ref, kv2_ref)
    cm_refs = (cm0_ref, cm1_ref, cm2_ref)

    @pl.when(j == 0)
    def _():
        xbf[...] = x_ref[...].astype(BF16)

    acc = jnp.dot(xbf[...], w_ref[...], preferred_element_type=F32)

    def store_class_major(gi, val):
        d = DILATION_GROUPS[gi][1]
        ref = cm_refs[gi]
        if d == 1:
            ref[0, 0, 0] = val.astype(BF16)
            return
        for h in range(HEADS_PER_GROUP):
            de_scr[h] = val[:, h * HEAD_DIM:(h + 1) * HEAD_DIM]
        rows = IN_TM // d
        for r in range(d):
            for h in range(HEADS_PER_GROUP):
                ref[0, 0, r, :, h * HEAD_DIM:(h + 1) * HEAD_DIM] = (
                    de_scr[h, pl.ds(r, rows, stride=d), :].astype(BF16))

    it = i % tiles_per_batch
    for gi in range(N_GROUPS):
        plan = plans[gi]
        in_window = it >= plan["first_tile"]

        @pl.when(j == gi)
        def _(gi=gi):
            store_class_major(gi, _rope(acc, cos_ref[...], sin_ref[...]))

        @pl.when(j == N_GROUPS + gi)
        def _(plan=plan, in_window=in_window, gi=gi):
            r = _rope(acc, cos_ref[...], sin_ref[...])
            store_class_major(gi, r)

            @pl.when(in_window)
            def _():
                kv_refs[gi][0] = r[plan["row_lo"]:plan["row_lo"] + plan["rb"], :]

        @pl.when(j == 2 * N_GROUPS + gi)
        def _(plan=plan, in_window=in_window, gi=gi):
            store_class_major(gi, acc)

            @pl.when(in_window)
            def _():
                kv_refs[gi][0] = acc[plan["row_lo"]:plan["row_lo"] + plan["rb"], :]

    @pl.when(j == 3 * N_GROUPS)
    def _():
        a_scr[...] = acc

    @pl.when(j == 3 * N_GROUPS + 1)
    def _():
        u_ref[...] = a_scr[...] * jax.nn.sigmoid(acc)

    @pl.when(i == n_i - 1)
    def _():
        @pl.when(j == 0)
        def _():
            xsbf[...] = xs_ref[...].astype(BF16)

        accs = jnp.dot(xsbf[...], w_ref[...], preferred_element_type=F32)

        def store_heads(val):
            for h in range(HEADS_PER_GROUP):
                qkvs_ref[:, 0, h, :] = val[:, h * HEAD_DIM:(h + 1) * HEAD_DIM]

        @pl.when(j < 2 * N_GROUPS)
        def _():
            store_heads(_rope(accs, coss_ref[...], sins_ref[...]))

        @pl.when((j >= 2 * N_GROUPS) & (j < 3 * N_GROUPS))
        def _():
            store_heads(accs)

        @pl.when(j == 3 * N_GROUPS)
        def _():
            as_scr[...] = accs

        @pl.when(j == 3 * N_GROUPS + 1)
        def _():
            us_ref[...] = as_scr[...] * jax.nn.sigmoid(accs)


def _in_proj(x2, xs2, w_bf, cos_p, sin_p, cos_s, sin_s, batch, seq):
    m = x2.shape[0]
    ms = xs2.shape[0]
    n_i = m // IN_TM
    n_j = w_bf.shape[1] // IN_TN
    n_qkv = 3 * N_GROUPS
    tiles_per_batch = seq // IN_TM
    plans = _kv_window_plan(seq, batch, n_j)
    tab = jnp.asarray(np.stack([p["tab"] for p in plans]).reshape(-1))
    n_steps = n_i * n_j

    def kv_map(gi):
        def f(i, j, tab_ref):
            base = (gi * n_steps + i * n_j + j) * 3
            return tab_ref[base], tab_ref[base + 1], tab_ref[base + 2]
        return f

    def cm_map(gi):
        def f(i, j, tab_ref):
            plane = (j > gi).astype(jnp.int32) + (j > N_GROUPS + gi).astype(jnp.int32)
            return i // tiles_per_batch, plane, 0, i % tiles_per_batch, 0
        return f

    last = n_i - 1
    in_specs = [
        pl.BlockSpec((IN_TM, D_MODEL), lambda i, j, t: (i, 0)),
        pl.BlockSpec((D_MODEL, IN_TN), lambda i, j, t: (0, j)),
        pl.BlockSpec((IN_TM, HEAD_DIM), lambda i, j, t: (i % tiles_per_batch, 0)),
        pl.BlockSpec((IN_TM, HEAD_DIM), lambda i, j, t: (i % tiles_per_batch, 0)),
        pl.BlockSpec((ms, D_MODEL), lambda i, j, t: (0, 0)),
        pl.BlockSpec((ms, HEAD_DIM), lambda i, j, t: (0, 0)),
        pl.BlockSpec((ms, HEAD_DIM), lambda i, j, t: (0, 0)),
    ]
    out_shape, out_specs = [], []
    for gi, (_, d) in enumerate(DILATION_GROUPS):
        out_shape.append(jax.ShapeDtypeStruct((batch, 3, d, seq // d, GROUP_W), BF16))
        out_specs.append(pl.BlockSpec((1, 1, d, IN_TM // d, GROUP_W), cm_map(gi)))
    out_shape.append(jax.ShapeDtypeStruct((m, CONV_CH), F32))
    out_specs.append(pl.BlockSpec((IN_TM, CONV_CH), lambda i, j, t: (i, 0)))
    for gi, p in enumerate(plans):
        out_shape.append(jax.ShapeDtypeStruct((batch, p["keep"], 2 * GROUP_W), F32))
        out_specs.append(pl.BlockSpec((1, p["rb"], GROUP_W), kv_map(gi)))
    out_shape += [jax.ShapeDtypeStruct((ms, n_qkv, HEADS_PER_GROUP, HEAD_DIM), F32),
                  jax.ShapeDtypeStruct((ms, CONV_CH), F32)]
    out_specs += [
        pl.BlockSpec((ms, 1, HEADS_PER_GROUP, HEAD_DIM),
                     lambda i, j, t: (0, jnp.where(i == last, jnp.minimum(j, n_qkv - 1), 0), 0, 0)),
        pl.BlockSpec((ms, CONV_CH), lambda i, j, t: (0, 0)),
    ]
    grid_spec = pltpu.PrefetchScalarGridSpec(
        num_scalar_prefetch=1, grid=(n_i, n_j), in_specs=in_specs, out_specs=out_specs,
        scratch_shapes=[pltpu.VMEM((IN_TM, D_MODEL), BF16), pltpu.VMEM((IN_TM, CONV_CH), F32),
                        pltpu.VMEM((HEADS_PER_GROUP, IN_TM, HEAD_DIM), F32),
                        pltpu.VMEM((ms, D_MODEL), BF16), pltpu.VMEM((ms, CONV_CH), F32)])
    kern = functools.partial(_in_proj_kernel, n_i=n_i, tiles_per_batch=tiles_per_batch, plans=plans)
    return pl.pallas_call(kern, grid_spec=grid_spec, out_shape=out_shape,
                          compiler_params=_cparams(2), name="in_proj")(
        tab, x2, w_bf, cos_p, sin_p, xs2, cos_s, sin_s)


ATTN_TP = 2048


def _attn_kernel(q_ref, k_ref, v_ref, o_ref, lse_ref, k_scr, v_scr, o_scr, lse_scr, *, d, lc):
    n = pl.program_id(1)
    n_qb = lc // Q_BLOCK

    @pl.when(n == 0)
    def _():
        k_scr[:, 0:Q_BLOCK, :] = jnp.zeros((d, Q_BLOCK, GROUP_W), BF16)
        v_scr[:, 0:Q_BLOCK, :] = jnp.zeros((d, Q_BLOCK, GROUP_W), BF16)

    k_scr[:, Q_BLOCK:, :] = k_ref[0, 0]
    v_scr[:, Q_BLOCK:, :] = v_ref[0, 0]
    qq = lax.broadcasted_iota(jnp.int32, (Q_BLOCK, 2 * Q_BLOCK), 0)
    kk = lax.broadcasted_iota(jnp.int32, (Q_BLOCK, 2 * Q_BLOCK), 1)
    band = (kk >= qq) & (kk <= qq + Q_BLOCK)
    lane = lax.broadcasted_iota(jnp.int32, (Q_BLOCK, HEAD_DIM), 1)

    def unit(u, carry):
        r = u // n_qb
        qb = u % n_qb
        r0 = pl.multiple_of(qb * Q_BLOCK, Q_BLOCK)
        valid = band & ((kk >= Q_BLOCK) | (n > 0) | (qb > 0))
        if d == 1:
            rows = pl.ds(r0, Q_BLOCK)
        else:
            rows = pl.ds(r0 * d + r, Q_BLOCK, stride=d)
        lse_blk = jnp.zeros((Q_BLOCK, HEAD_DIM), F32)
        for h in range(HEADS_PER_GROUP):
            c0 = h * HEAD_DIM
            q = q_ref[0, 0, r, pl.ds(r0, Q_BLOCK), c0:c0 + HEAD_DIM]
            k = k_scr[r, pl.ds(r0, 2 * Q_BLOCK), c0:c0 + HEAD_DIM]
            v = v_scr[r, pl.ds(r0, 2 * Q_BLOCK), c0:c0 + HEAD_DIM]
            s = lax.dot_general(q, k, (((1,), (1,)), ((), ())), preferred_element_type=F32)
            s = jnp.where(valid, s * ATTN_SCALE, NEG_BIG)
            m = jnp.max(s, axis=1, keepdims=True)
            e = jnp.exp(s - m)
            den = jnp.sum(e, axis=1, keepdims=True)
            o_scr[h, rows, :] = jnp.dot(e.astype(BF16), v, preferred_element_type=F32) / den
            lse_blk = jnp.where(lane == h, m + jnp.log(den), lse_blk)
        lse_scr[rows, :] = lse_blk
        return carry

    lax.fori_loop(0, d * n_qb, unit, 0)
    k_scr[:, 0:Q_BLOCK, :] = k_scr[:, lc:lc + Q_BLOCK, :]
    v_scr[:, 0:Q_BLOCK, :] = v_scr[:, lc:lc + Q_BLOCK, :]
    for h in range(HEADS_PER_GROUP):
        o_ref[:, h * HEAD_DIM:(h + 1) * HEAD_DIM] = o_scr[h].astype(BF16)
    lse_ref[...] = lse_scr[...]


def _attn_group(cm, gi, batch, seq):
    _, d = DILATION_GROUPS[gi]
    lc = ATTN_TP // d
    n_t = seq // ATTN_TP

    def plane(p):
        return pl.BlockSpec((1, 1, d, lc, GROUP_W), lambda b, n: (b, p, 0, n, 0))

    return pl.pallas_call(
        functools.partial(_attn_kernel, d=d, lc=lc),
        grid=(batch, n_t),
        in_specs=[plane(0), plane(1), plane(2)],
        out_specs=[pl.BlockSpec((ATTN_TP, GROUP_W), lambda b, n: (b * n_t + n, 0)),
                   pl.BlockSpec((ATTN_TP, HEAD_DIM), lambda b, n: (b * n_t + n, 0))],
        out_shape=[jax.ShapeDtypeStruct((batch * seq, GROUP_W), BF16),
                   jax.ShapeDtypeStruct((batch * seq, HEAD_DIM), F32)],
        scratch_shapes=[pltpu.VMEM((d, lc + Q_BLOCK, GROUP_W), BF16),
                        pltpu.VMEM((d, lc + Q_BLOCK, GROUP_W), BF16),
                        pltpu.VMEM((HEADS_PER_GROUP, ATTN_TP, HEAD_DIM), F32),
                        pltpu.VMEM((ATTN_TP, HEAD_DIM), F32)],
        compiler_params=_cparams(2), name=f"attn_g{gi}")(cm, cm, cm)


CONV_TC = 512
CONV_HALO = 32
CONV_ROWS = 64


def _conv_kernel(u_ref, halo_ref, w_ref, b_ref, g_ref, beta_ref, c_ref, hist):
    t = pl.program_id(1)
    hist[0:CONV_HALO, :] = jnp.where(t > 0, halo_ref[0], 0.0)
    hist[CONV_HALO:, :] = u_ref[0]
    lead = CONV_HALO - (CONV_K - 1)
    for rc in range(CONV_TC // CONV_ROWS):
        r0 = rc * CONV_ROWS
        acc = jnp.broadcast_to(b_ref[...], (CONV_ROWS, CONV_CH))
        for k in range(CONV_K):
            acc = acc + hist[r0 + lead + k:r0 + lead + k + CONV_ROWS, :] * w_ref[k:k + 1, :]
        y = _layer_norm_rows(acc, g_ref[...], beta_ref[...])
        c_ref[0, r0:r0 + CONV_ROWS, :] = (y * jax.nn.sigmoid(y)).astype(BF16)


def _conv_prompt(u2, conv_w, conv_b, ln_g, ln_b, batch, seq):
    u3 = u2.reshape(batch, seq, CONV_CH)
    per = CONV_TC // CONV_HALO
    vec = pl.BlockSpec((1, CONV_CH), lambda b, t: (0, 0))
    c = pl.pallas_call(
        _conv_kernel,
        grid=(batch, seq // CONV_TC),
        in_specs=[pl.BlockSpec((1, CONV_TC, CONV_CH), lambda b, t: (b, t, 0)),
                  pl.BlockSpec((1, CONV_HALO, CONV_CH), lambda b, t: (b, jnp.maximum(t * per - 1, 0), 0)),
                  pl.BlockSpec((CONV_K, CONV_CH), lambda b, t: (0, 0)), vec, vec, vec],
        out_specs=pl.BlockSpec((1, CONV_TC, CONV_CH), lambda b, t: (b, t, 0)),
        out_shape=jax.ShapeDtypeStruct((batch, seq, CONV_CH), BF16),
        scratch_shapes=[pltpu.VMEM((CONV_TC + CONV_HALO, CONV_CH), F32)],
        compiler_params=_cparams(2), name="conv_prompt")(
        u3, u3, conv_w, conv_b[None, :], ln_g[None, :], ln_b[None, :])
    return c.reshape(batch * seq, CONV_CH)


KV_SLAB = 2 * HEADS_PER_GROUP


def _sample_kernel(qkv_ref, c0_ref, c1_ref, c2_ref, st_ref, u_ref, w_ref, b_ref, g_ref, beta_ref,
                   mix_ref, nconv_ref, kvn0_ref, kvn1_ref, kvn2_ref, hist, *, t_new):
    cache_refs = (c0_ref, c1_ref, c2_ref)
    kvn_refs = (kvn0_ref, kvn1_ref, kvn2_ref)
    hp = HEADS_PER_GROUP
    outs = [[None] * N_GROUPS for _ in range(t_new)]
    lses = [[None] * N_GROUPS for _ in range(t_new)]
    for gi, (_, d) in enumerate(DILATION_GROUPS):
        cref = cache_refs[gi]
        n_rows = cref.shape[1]
        row = lax.broadcasted_iota(jnp.int32, (n_rows, hp, 1), 0)
        for t in range(t_new):
            kvn_refs[gi][0, t, 0] = qkv_ref[0, t, N_GROUPS + gi]
            kvn_refs[gi][0, t, 1] = qkv_ref[0, t, 2 * N_GROUPS + gi]
        for t in range(t_new):
            s0 = (t % d) * KV_SLAB
            first_row = t // d
            q = qkv_ref[0, t, gi][None]
            kc = cref[0, :, s0:s0 + hp, :]
            vc = cref[0, :, s0 + hp:s0 + 2 * hp, :]
            s_c = jnp.sum(kc * q, axis=2, keepdims=True) * ATTN_SCALE
            if first_row > 0:
                s_c = jnp.where(row >= first_row, s_c, NEG_BIG)
            newest = [t - d * jj for jj in range(t // d + 1)]
            s_n, v_n = [], []
            for tn in newest:
                kn = qkv_ref[0, tn, N_GROUPS + gi][None]
                v_n.append(qkv_ref[0, tn, 2 * N_GROUPS + gi][None])
                s_n.append(jnp.sum(kn * q, axis=2, keepdims=True) * ATTN_SCALE)
            m = jnp.max(s_c, axis=0, keepdims=True)
            for sn in s_n:
                m = jnp.maximum(m, sn)
            e_c = jnp.exp(s_c - m)
            den = jnp.sum(e_c, axis=0, keepdims=True)
            acc = jnp.sum(e_c * vc, axis=0, keepdims=True)
            for sn, v1 in zip(s_n, v_n):
                e_n = jnp.exp(sn - m)
                den = den + e_n
                acc = acc + e_n * v1
            outs[t][gi] = acc / den
            lses[t][gi] = m + jnp.log(den)
    for t in range(t_new):
        ls = lses[t]
        m = functools.reduce(jnp.maximum, ls)
        es = [jnp.exp(l - m) for l in ls]
        tot = functools.reduce(lambda a, b: a + b, es)
        for gi in range(N_GROUPS):
            slab = (outs[t][gi] * (es[gi] / tot))[0]
            for h in range(hp):
                c0 = gi * GROUP_W + h * HEAD_DIM
                mix_ref[0, t:t + 1, c0:c0 + HEAD_DIM] = slab[h:h + 1, :]
    n_state = CONV_K - 1
    hist[0:n_state, :] = st_ref[0]
    hist[n_state:n_state + t_new, :] = u_ref[0]
    acc = jnp.broadcast_to(b_ref[...], (t_new, CONV_CH))
    for k in range(CONV_K):
        acc = acc + hist[k:k + t_new, :] * w_ref[k:k + 1, :]
    y = _layer_norm_rows(acc, g_ref[...], beta_ref[...])
    mix_ref[0, :, ATTN_WIDTH:] = y * jax.nn.sigmoid(y)
    nconv_ref[0] = hist[t_new:t_new + n_state, :]


def _sample_mixers(qkvs, us, caches, state, conv_w, conv_b, ln_g, ln_b, dec_batch, t_new):
    n_qkv = qkvs.shape[1]
    hp = HEADS_PER_GROUP
    qkv5 = qkvs.reshape(dec_batch, t_new, n_qkv, hp, HEAD_DIM)
    u3 = us.reshape(dec_batch, t_new, CONV_CH)
    cache_in, cache_specs = [], []
    for (window, d), cache in zip(DILATION_GROUPS, caches):
        buf = cache.shape[1]
        assert buf == window and buf % d == 0
        n_rows = buf // d
        cache_in.append(cache.reshape(dec_batch, n_rows, d * KV_SLAB, HEAD_DIM))
        used = min(d, t_new) * KV_SLAB
        cache_specs.append(pl.BlockSpec((1, n_rows, used, HEAD_DIM), lambda b: (b, 0, 0, 0)))
    vec = pl.BlockSpec((1, CONV_CH), lambda b: (0, 0))
    n_state = CONV_K - 1
    kvn_shape = jax.ShapeDtypeStruct((dec_batch, t_new, 2, hp, HEAD_DIM), F32)
    kvn_spec = pl.BlockSpec((1, t_new, 2, hp, HEAD_DIM), lambda b: (b, 0, 0, 0, 0))
    mix, nconv, kvn0, kvn1, kvn2 = pl.pallas_call(
        functools.partial(_sample_kernel, t_new=t_new),
        grid=(dec_batch,),
        in_specs=[pl.BlockSpec((1, t_new, n_qkv, hp, HEAD_DIM), lambda b: (b, 0, 0, 0, 0))] + cache_specs + [
            pl.BlockSpec((1, n_state, CONV_CH), lambda b: (b, 0, 0)),
            pl.BlockSpec((1, t_new, CONV_CH), lambda b: (b, 0, 0)),
            pl.BlockSpec((CONV_K, CONV_CH), lambda b: (0, 0)), vec, vec, vec],
        out_specs=[pl.BlockSpec((1, t_new, D_MODEL), lambda b: (b, 0, 0)),
                   pl.BlockSpec((1, n_state, CONV_CH), lambda b: (b, 0, 0)),
                   kvn_spec, kvn_spec, kvn_spec],
        out_shape=[jax.ShapeDtypeStruct((dec_batch, t_new, D_MODEL), F32),
                   jax.ShapeDtypeStruct((dec_batch, n_state, CONV_CH), F32),
                   kvn_shape, kvn_shape, kvn_shape],
        scratch_shapes=[pltpu.VMEM((n_state + t_new + 6, CONV_CH), F32)],
        compiler_params=_cparams(1), name="sample_mixers")(
        qkv5, *cache_in, state, u3, conv_w, conv_b[None, :], ln_g[None, :], ln_b[None, :])
    return mix.reshape(dec_batch * t_new, D_MODEL), nconv, (kvn0, kvn1, kvn2)


OUT_TM = 512


def _out_proj_kernel(o0_ref, o1_ref, o2_ref, l0_ref, l1_ref, l2_ref, c_ref, x_ref, w_ref, g_ref, b_ref,
                     mixs_ref, xs_ref, x1_ref, x1bf_ref, x1s_ref, mix_scr, *, n_i, alpha):
    i = pl.program_id(0)
    o_refs = (o0_ref, o1_ref, o2_ref)
    ls = [l0_ref[...], l1_ref[...], l2_ref[...]]
    m = jnp.maximum(jnp.maximum(ls[0], ls[1]), ls[2])
    es = [jnp.exp(l - m) for l in ls]
    inv = 1.0 / (es[0] + es[1] + es[2])
    for gi in range(N_GROUPS):
        a = es[gi] * inv
        for h in range(HEADS_PER_GROUP):
            c0 = h * HEAD_DIM
            og = o_refs[gi][:, c0:c0 + HEAD_DIM].astype(F32)
            mix_scr[:, gi * GROUP_W + c0:gi * GROUP_W + c0 + HEAD_DIM] = (og * a[:, h:h + 1]).astype(BF16)
    mix_scr[:, ATTN_WIDTH:] = c_ref[...]
    y = jnp.dot(mix_scr[...], w_ref[...], preferred_element_type=F32) + alpha * x_ref[...]
    x1 = _layer_norm_rows(y, g_ref[...], b_ref[...])
    x1_ref[...] = x1
    x1bf_ref[...] = x1.astype(BF16)

    @pl.when(i == n_i - 1)
    def _():
        ys = jnp.dot(mixs_ref[...].astype(BF16), w_ref[...], preferred_element_type=F32) + alpha * xs_ref[...]
        x1s_ref[...] = _layer_norm_rows(ys, g_ref[...], b_ref[...])


def _out_proj(os, lses, c, x2, w_bf, ln_g, ln_b, mixs, xs2, alpha):
    m = x2.shape[0]
    ms = xs2.shape[0]
    n_i = m // OUT_TM
    row = lambda w: pl.BlockSpec((OUT_TM, w), lambda i: (i, 0))
    whole = lambda a: pl.BlockSpec(a.shape, lambda i: (0,) * a.ndim)
    g2, b2 = ln_g[None, :], ln_b[None, :]
    return pl.pallas_call(
        functools.partial(_out_proj_kernel, n_i=n_i, alpha=alpha),
        grid=(n_i,),
        in_specs=[row(GROUP_W)] * 3 + [row(HEAD_DIM)] * 3 + [row(CONV_CH), row(D_MODEL),
                  whole(w_bf), whole(g2), whole(b2), whole(mixs), whole(xs2)],
        out_specs=[row(D_MODEL), row(D_MODEL), pl.BlockSpec((ms, D_MODEL), lambda i: (0, 0))],
        out_shape=[jax.ShapeDtypeStruct((m, D_MODEL), F32), jax.ShapeDtypeStruct((m, D_MODEL), BF16),
                   jax.ShapeDtypeStruct((ms, D_MODEL), F32)],
        scratch_shapes=[pltpu.VMEM((OUT_TM, D_MODEL), BF16)],
        compiler_params=_cparams(1), name="out_proj")(
        *os, *lses, c, x2, w_bf, g2, b2, mixs, xs2)


FFN_TM = 1024
FFN_TF = 512
FFN_RES_W = 256


def _ffn_kernel(xbf_ref, xres_ref, wg_ref, wu_ref, wd_ref, g_ref, b_ref, xs_ref,
                c0_ref, c1_ref, c2_ref, kvn0_ref, kvn1_ref, kvn2_ref,
                y_ref, ys_ref, nk0_ref, nk1_ref, nk2_ref, xsbf, sems, *, n_i, n_f, alpha, t_new):
    i = pl.program_id(0)
    f = pl.program_id(1)

    def cache_copies():
        copies = []
        for gi, (cache, new, out) in enumerate(((c0_ref, kvn0_ref, nk0_ref), (c1_ref, kvn1_ref, nk1_ref),
                                                (c2_ref, kvn2_ref, nk2_ref))):
            buf = cache.shape[1]
            copies.append(pltpu.make_async_copy(cache.at[:, pl.ds(t_new, buf - t_new)],
                                                out.at[:, pl.ds(0, buf - t_new)], sems.at[2 * gi]))
            copies.append(pltpu.make_async_copy(new, out.at[:, pl.ds(buf - t_new, t_new)],
                                                sems.at[2 * gi + 1]))
        return copies

    @pl.when((i == 0) & (f == 0))
    def _():
        for cp in cache_copies():
            cp.start()

    def swiglu_down(xb):
        gate = jnp.dot(xb, wg_ref[...], preferred_element_type=F32)
        up = jnp.dot(xb, wu_ref[...], preferred_element_type=F32)
        act = (gate * jax.nn.sigmoid(gate) * up).astype(BF16)
        return jnp.dot(act, wd_ref[...], preferred_element_type=F32)

    down = swiglu_down(xbf_ref[...])

    @pl.when(f == 0)
    def _():
        y_ref[...] = down

    @pl.when(f > 0)
    def _():
        y_ref[...] += down

    for c in range(D_MODEL // FFN_RES_W):
        @pl.when(f == c)
        def _(c=c):
            y_ref[:, c * FFN_RES_W:(c + 1) * FFN_RES_W] += alpha * xres_ref[...]

    @pl.when(f == n_f - 1)
    def _():
        y_ref[...] = _layer_norm_rows(y_ref[...], g_ref[...], b_ref[...])

    @pl.when(i == n_i - 1)
    def _():
        @pl.when(f == 0)
        def _():
            xsbf[...] = xs_ref[...].astype(BF16)

        downs = swiglu_down(xsbf[...])

        @pl.when(f == 0)
        def _():
            ys_ref[...] = downs + alpha * xs_ref[...]

        @pl.when(f > 0)
        def _():
            ys_ref[...] += downs

        @pl.when(f == n_f - 1)
        def _():
            ys_ref[...] = _layer_norm_rows(ys_ref[...], g_ref[...], b_ref[...])

    @pl.when((i == n_i - 1) & (f == n_f - 1))
    def _():
        for cp in cache_copies():
            cp.wait()


def _ffn(x1, x1bf, x1s, wg_bf, wu_bf, wd_bf, ln_g, ln_b, caches, kv_new, alpha, t_new):
    m = x1.shape[0]
    ms = x1s.shape[0]
    hidden = wg_bf.shape[1]
    n_i = m // FFN_TM
    n_f = hidden // FFN_TF
    n_res = D_MODEL // FFN_RES_W
    assert n_f >= n_res
    g2, b2 = ln_g[None, :], ln_b[None, :]
    any_spec = pl.BlockSpec(memory_space=pl.ANY)
    return pl.pallas_call(
        functools.partial(_ffn_kernel, n_i=n_i, n_f=n_f, alpha=alpha, t_new=t_new),
        grid=(n_i, n_f),
        in_specs=[pl.BlockSpec((FFN_TM, D_MODEL), lambda i, f: (i, 0)),
                  pl.BlockSpec((FFN_TM, FFN_RES_W), lambda i, f: (i, jnp.minimum(f, n_res - 1))),
                  pl.BlockSpec((D_MODEL, FFN_TF), lambda i, f: (0, f)),
                  pl.BlockSpec((D_MODEL, FFN_TF), lambda i, f: (0, f)),
                  pl.BlockSpec((FFN_TF, D_MODEL), lambda i, f: (f, 0)),
                  pl.BlockSpec((1, D_MODEL), lambda i, f: (0, 0)),
                  pl.BlockSpec((1, D_MODEL), lambda i, f: (0, 0)),
                  pl.BlockSpec((ms, D_MODEL), lambda i, f: (0, 0))] + [any_spec] * 6,
        out_specs=[pl.BlockSpec((FFN_TM, D_MODEL), lambda i, f: (i, 0)),
                   pl.BlockSpec((ms, D_MODEL), lambda i, f: (0, 0))] + [any_spec] * 3,
        out_shape=[jax.ShapeDtypeStruct((m, D_MODEL), F32), jax.ShapeDtypeStruct((ms, D_MODEL), F32)]
                  + [jax.ShapeDtypeStruct(c.shape, c.dtype) for c in caches],
        scratch_shapes=[pltpu.VMEM((ms, D_MODEL), BF16), pltpu.SemaphoreType.DMA((2 * N_GROUPS,))],
        compiler_params=_cparams(2), name="ffn")(
        x1bf, x1, wg_bf, wu_bf, wd_bf, g2, b2, x1s, *caches, *kv_new)


def kernel(x_prompt, x_sample, cache_kv_w128, cache_kv_w512, cache_kv_w2048, state_conv, w_in, w_out,
           conv_w, conv_b, conv_ln_g, conv_ln_b, ln1_g, ln1_b, w_gate, w_up, w_down, ln2_g, ln2_b):
    depth = w_in.shape[0]
    batch, seq, _ = x_prompt.shape
    dec_batch, t_new, _ = x_sample.shape
    caches = (cache_kv_w128, cache_kv_w512, cache_kv_w2048)
    alpha = (2.0 * depth) ** 0.25

    cos_p, sin_p = _rope_tables(jnp.arange(seq, dtype=jnp.int32))
    pos_s = PAST_LEN + jnp.arange(t_new, dtype=jnp.int32)
    cos_s, sin_s = _rope_tables(jnp.tile(pos_s, dec_batch))

    xp = x_prompt.reshape(batch * seq, D_MODEL)
    xs = x_sample.reshape(dec_batch * t_new, D_MODEL)
    kvp = [[] for _ in range(N_GROUPS)]
    kvs = [[] for _ in range(N_GROUPS)]
    convp, convs = [], []
    for l in range(depth):
        cm0, cm1, cm2, u, kv0, kv1, kv2, qkvs, us = _in_proj(
            xp, xs, w_in[l].astype(BF16), cos_p, sin_p, cos_s, sin_s, batch, seq)
        os, lses = zip(*[_attn_group(cm, gi, batch, seq) for gi, cm in enumerate((cm0, cm1, cm2))])
        c = _conv_prompt(u, conv_w[l], conv_b[l], conv_ln_g[l], conv_ln_b[l], batch, seq)
        layer_caches = [cc[l] for cc in caches]
        mixs, nconv_s, kv_new = _sample_mixers(qkvs, us, layer_caches, state_conv[l],
                                               conv_w[l], conv_b[l], conv_ln_g[l], conv_ln_b[l], dec_batch, t_new)
        x1, x1bf, x1s = _out_proj(os, lses, c, xp, w_out[l].astype(BF16), ln1_g[l], ln1_b[l], mixs, xs, alpha)
        xp, xs, nk0, nk1, nk2 = _ffn(x1, x1bf, x1s, w_gate[l].astype(BF16), w_up[l].astype(BF16),
                                     w_down[l].astype(BF16), ln2_g[l], ln2_b[l], layer_caches, kv_new,
                                     alpha, t_new)
        for gi, (kv, nk) in enumerate(zip((kv0, kv1, kv2), (nk0, nk1, nk2))):
            kvp[gi].append(kv.reshape(batch, kv.shape[1], 2, HEADS_PER_GROUP, HEAD_DIM))
            kvs[gi].append(nk)
        convp.append(u.reshape(batch, seq, CONV_CH)[:, seq - (CONV_K - 1):])
        convs.append(nconv_s)

    y_prompt = xp.reshape(batch, seq, D_MODEL)
    y_sample = xs.reshape(dec_batch, t_new, D_MODEL)
    return (y_prompt, y_sample, jnp.stack(kvp[0]), jnp.stack(kvp[1]), jnp.stack(kvp[2]), jnp.stack(convp),
            jnp.stack(kvs[0]), jnp.stack(kvs[1]), jnp.stack(kvs[2]), jnp.stack(convs))
```

```python
import functools

import numpy as np
import jax
import jax.numpy as jnp
from jax import lax
from jax.experimental import pallas as pl
from jax.experimental.pallas import tpu as pltpu

D_MODEL = 2048
HEAD_DIM = 128
CONV_CH = D_MODEL // 4
ATTN_WIDTH = D_MODEL - CONV_CH
DILATION_GROUPS = ((128, 1), (512, 4), (2048, 16))
N_GROUPS = len(DILATION_GROUPS)
HEADS_PER_GROUP = ATTN_WIDTH // HEAD_DIM // N_GROUPS
GROUP_W = HEADS_PER_GROUP * HEAD_DIM
CONV_K = 31
ROPE_THETA = 10000.0
LN_EPS = 1e-5
Q_BLOCK = 128
ATTN_SCALE = HEAD_DIM ** -0.5
NEG_BIG = -1e30
PAST_LEN = 16384

F32 = jnp.float32
BF16 = jnp.bfloat16

VMEM_LIMIT = 56 * 1024 * 1024


def _cparams(n_axes):
    return pltpu.CompilerParams(dimension_semantics=("arbitrary",) * n_axes,
                                vmem_limit_bytes=VMEM_LIMIT)


def _layer_norm_rows(y, g, b):
    mu = jnp.mean(y, axis=-1, keepdims=True)
    yc = y - mu
    var = jnp.mean(yc * yc, axis=-1, keepdims=True)
    return yc * lax.rsqrt(var + LN_EPS) * g + b


def _rope_tables(pos):
    half = HEAD_DIM // 2
    inv = ROPE_THETA ** (-jnp.arange(half, dtype=F32) / half)
    ang = pos.astype(F32)[:, None] * inv[None, :]
    cos, sin = jnp.cos(ang), jnp.sin(ang)
    return jnp.concatenate([cos, cos], axis=1), jnp.concatenate([-sin, sin], axis=1)


def _rope(h, cos, sin):
    parts = []
    for hh in range(HEADS_PER_GROUP):
        hs = h[:, hh * HEAD_DIM:(hh + 1) * HEAD_DIM]
        parts.append(hs * cos + pltpu.roll(hs, HEAD_DIM // 2, axis=1) * sin)
    return jnp.concatenate(parts, axis=1)


IN_TM = 1024
IN_TN = GROUP_W


def _kv_window_plan(seq, batch, n_j):
    tiles_per_batch = seq // IN_TM
    n_i = batch * tiles_per_batch
    plans = []
    for gi, (window, _) in enumerate(DILATION_GROUPS):
        keep = min(window, seq)
        rb = min(keep, IN_TM)
        first_tile = (seq - keep) // IN_TM
        row_lo = (seq - keep) - first_tile * IN_TM
        writes = []
        for i in range(n_i):
            b, it = divmod(i, tiles_per_batch)
            if it >= first_tile:
                for c, j in enumerate((N_GROUPS + gi, 2 * N_GROUPS + gi)):
                    writes.append((i * n_j + j, (b, it - first_tile, c)))
        writes.sort()
        tab = np.zeros((n_i * n_j, 3), np.int32)
        w = 0
        for step in range(n_i * n_j):
            while w < len(writes) - 1 and writes[w][0] < step:
                w += 1
            tab[step] = writes[w][1]
        plans.append(dict(keep=keep, rb=rb, first_tile=first_tile, row_lo=row_lo, tab=tab))
    return plans


def _in_proj_kernel(tab_ref, x_ref, w_ref, cos_ref, sin_ref, xs_ref, coss_ref, sins_ref,
                    cm0_ref, cm1_ref, cm2_ref, u_ref, kv0_ref, kv1_ref, kv2_ref, qkvs_ref, us_ref,
                    xbf, a_scr, de_scr, xsbf, as_scr, *, n_i, tiles_per_batch, plans):
    del tab_ref
    i = pl.program_id(0)
    j = pl.program_id(1)
    kv_refs = (kv0_ref, kv1_ref, kv2_ref)
    cm_refs = (cm0_ref, cm1_ref, cm2_ref)

    @pl.when(j == 0)
    def _():
        xbf[...] = x_ref[...].astype(BF16)

    acc = jnp.dot(xbf[...], w_ref[...], preferred_element_type=F32)

    def store_class_major(gi, val):
        d = DILATION_GROUPS[gi][1]
        ref = cm_refs[gi]
        if d == 1:
            ref[0, 0, 0] = val.astype(BF16)
            return
        for h in range(HEADS_PER_GROUP):
            de_scr[h] = val[:, h * HEAD_DIM:(h + 1) * HEAD_DIM]
        rows = IN_TM // d
        for r in range(d):
            for h in range(HEADS_PER_GROUP):
                ref[0, 0, r, :, h * HEAD_DIM:(h + 1) * HEAD_DIM] = (
                    de_scr[h, pl.ds(r, rows, stride=d), :].astype(BF16))

    it = i % tiles_per_batch
    for gi in range(N_GROUPS):
        plan = plans[gi]
        in_window = it >= plan["first_tile"]

        @pl.when(j == gi)
        def _(gi=gi):
            store_class_major(gi, _rope(acc, cos_ref[...], sin_ref[...]))

        @pl.when(j == N_GROUPS + gi)
        def _(plan=plan, in_window=in_window, gi=gi):
            r = _rope(acc, cos_ref[...], sin_ref[...])
            store_class_major(gi, r)

            @pl.when(in_window)
            def _():
                kv_refs[gi][0] = r[plan["row_lo"]:plan["row_lo"] + plan["rb"], :]

        @pl.when(j == 2 * N_GROUPS + gi)
        def _(plan=plan, in_window=in_window, gi=gi):
            store_class_major(gi, acc)

            @pl.when(in_window)
            def _():
                kv_refs[gi][0] = acc[plan["row_lo"]:plan["row_lo"] + plan["rb"], :]

    @pl.when(j == 3 * N_GROUPS)
    def _():
        a_scr[...] = acc

    @pl.when(j == 3 * N_GROUPS + 1)
    def _():
        u_ref[...] = a_scr[...] * jax.nn.sigmoid(acc)

    @pl.when(i == n_i - 1)
    def _():
        @pl.when(j == 0)
        def _():
            xsbf[...] = xs_ref[...].astype(BF16)

        accs = jnp.dot(xsbf[...], w_ref[...], preferred_element_type=F32)

        def store_heads(val):
            for h in range(HEADS_PER_GROUP):
                qkvs_ref[:, 0, h, :] = val[:, h * HEAD_DIM:(h + 1) * HEAD_DIM]

        @pl.when(j < 2 * N_GROUPS)
        def _():
            store_heads(_rope(accs, coss_ref[...], sins_ref[...]))

        @pl.when((j >= 2 * N_GROUPS) & (j < 3 * N_GROUPS))
        def _():
            store_heads(accs)

        @pl.when(j == 3 * N_GROUPS)
        def _():
            as_scr[...] = accs

        @pl.when(j == 3 * N_GROUPS + 1)
        def _():
            us_ref[...] = as_scr[...] * jax.nn.sigmoid(accs)


def _in_proj(x2, xs2, w_bf, cos_p, sin_p, cos_s, sin_s, batch, seq):
    m = x2.shape[0]
    ms = xs2.shape[0]
    n_i = m // IN_TM
    n_j = w_bf.shape[1] // IN_TN
    n_qkv = 3 * N_GROUPS
    tiles_per_batch = seq // IN_TM
    plans = _kv_window_plan(seq, batch, n_j)
    tab = jnp.asarray(np.stack([p["tab"] for p in plans]).reshape(-1))
    n_steps = n_i * n_j

    def kv_map(gi):
        def f(i, j, tab_ref):
            base = (gi * n_steps + i * n_j + j) * 3
            return tab_ref[base], tab_ref[base + 1], tab_ref[base + 2]
        return f

    def cm_map(gi):
        def f(i, j, tab_ref):
            plane = (j > gi).astype(jnp.int32) + (j > N_GROUPS + gi).astype(jnp.int32)
            return i // tiles_per_batch, plane, 0, i % tiles_per_batch, 0
        return f

    last = n_i - 1
    in_specs = [
        pl.BlockSpec((IN_TM, D_MODEL), lambda i, j, t: (i, 0)),
        pl.BlockSpec((D_MODEL, IN_TN), lambda i, j, t: (0, j)),
        pl.BlockSpec((IN_TM, HEAD_DIM), lambda i, j, t: (i % tiles_per_batch, 0)),
        pl.BlockSpec((IN_TM, HEAD_DIM), lambda i, j, t: (i % tiles_per_batch, 0)),
        pl.BlockSpec((ms, D_MODEL), lambda i, j, t: (0, 0)),
        pl.BlockSpec((ms, HEAD_DIM), lambda i, j, t: (0, 0)),
        pl.BlockSpec((ms, HEAD_DIM), lambda i, j, t: (0, 0)),
    ]
    out_shape, out_specs = [], []
    for gi, (_, d) in enumerate(DILATION_GROUPS):
        out_shape.append(jax.ShapeDtypeStruct((batch, 3, d, seq // d, GROUP_W), BF16))
        out_specs.append(pl.BlockSpec((1, 1, d, IN_TM // d, GROUP_W), cm_map(gi)))
    out_shape.append(jax.ShapeDtypeStruct((m, CONV_CH), F32))
    out_specs.append(pl.BlockSpec((IN_TM, CONV_CH), lambda i, j, t: (i, 0)))
    for gi, p in enumerate(plans):
        out_shape.append(jax.ShapeDtypeStruct((batch, p["keep"], 2 * GROUP_W), F32))
        out_specs.append(pl.BlockSpec((1, p["rb"], GROUP_W), kv_map(gi)))
    out_shape += [jax.ShapeDtypeStruct((ms, n_qkv, HEADS_PER_GROUP, HEAD_DIM), F32),
                  jax.ShapeDtypeStruct((ms, CONV_CH), F32)]
    out_specs += [
        pl.BlockSpec((ms, 1, HEADS_PER_GROUP, HEAD_DIM),
                     lambda i, j, t: (0, jnp.where(i == last, jnp.minimum(j, n_qkv - 1), 0), 0, 0)),
        pl.BlockSpec((ms, CONV_CH), lambda i, j, t: (0, 0)),
    ]
    grid_spec = pltpu.PrefetchScalarGridSpec(
        num_scalar_prefetch=1, grid=(n_i, n_j), in_specs=in_specs, out_specs=out_specs,
        scratch_shapes=[pltpu.VMEM((IN_TM, D_MODEL), BF16), pltpu.VMEM((IN_TM, CONV_CH), F32),
                        pltpu.VMEM((HEADS_PER_GROUP, IN_TM, HEAD_DIM), F32),
                        pltpu.VMEM((ms, D_MODEL), BF16), pltpu.VMEM((ms, CONV_CH), F32)])
    kern = functools.partial(_in_proj_kernel, n_i=n_i, tiles_per_batch=tiles_per_batch, plans=plans)
    return pl.pallas_call(kern, grid_spec=grid_spec, out_shape=out_shape,
                          compiler_params=_cparams(2), name="in_proj")(
        tab, x2, w_bf, cos_p, sin_p, xs2, cos_s, sin_s)


ATTN_TP = 2048


def _attn_kernel(q_ref, k_ref, v_ref, o_ref, lse_ref, k_scr, v_scr, o_scr, lse_scr, *, d, lc):
    n = pl.program_id(1)
    n_qb = lc // Q_BLOCK

    @pl.when(n == 0)
    def _():
        k_scr[:, 0:Q_BLOCK, :] = jnp.zeros((d, Q_BLOCK, GROUP_W), BF16)
        v_scr[:, 0:Q_BLOCK, :] = jnp.zeros((d, Q_BLOCK, GROUP_W), BF16)

    k_scr[:, Q_BLOCK:, :] = k_ref[0, 0]
    v_scr[:, Q_BLOCK:, :] = v_ref[0, 0]
    qq = lax.broadcasted_iota(jnp.int32, (Q_BLOCK, 2 * Q_BLOCK), 0)
    kk = lax.broadcasted_iota(jnp.int32, (Q_BLOCK, 2 * Q_BLOCK), 1)
    band = (kk >= qq) & (kk <= qq + Q_BLOCK)
    lane = lax.broadcasted_iota(jnp.int32, (Q_BLOCK, HEAD_DIM), 1)

    def unit(u, carry):
        r = u // n_qb
        qb = u % n_qb
        r0 = pl.multiple_of(qb * Q_BLOCK, Q_BLOCK)
        valid = band & ((kk >= Q_BLOCK) | (n > 0) | (qb > 0))
        if d == 1:
            rows = pl.ds(r0, Q_BLOCK)
        else:
            rows = pl.ds(r0 * d + r, Q_BLOCK, stride=d)
        lse_blk = jnp.zeros((Q_BLOCK, HEAD_DIM), F32)
        for h in range(HEADS_PER_GROUP):
            c0 = h * HEAD_DIM
            q = q_ref[0, 0, r, pl.ds(r0, Q_BLOCK), c0:c0 + HEAD_DIM]
            k = k_scr[r, pl.ds(r0, 2 * Q_BLOCK), c0:c0 + HEAD_DIM]
            v = v_scr[r, pl.ds(r0, 2 * Q_BLOCK), c0:c0 + HEAD_DIM]
            s = lax.dot_general(q, k, (((1,), (1,)), ((), ())), preferred_element_type=F32)
            s = jnp.where(valid, s * ATTN_SCALE, NEG_BIG)
            m = jnp.max(s, axis=1, keepdims=True)
            e = jnp.exp(s - m)
            den = jnp.sum(e, axis=1, keepdims=True)
            o_scr[h, rows, :] = jnp.dot(e.astype(BF16), v, preferred_element_type=F32) / den
            lse_blk = jnp.where(lane == h, m + jnp.log(den), lse_blk)
        lse_scr[rows, :] = lse_blk
        return carry

    lax.fori_loop(0, d * n_qb, unit, 0, unroll=4)
    k_scr[:, 0:Q_BLOCK, :] = k_scr[:, lc:lc + Q_BLOCK, :]
    v_scr[:, 0:Q_BLOCK, :] = v_scr[:, lc:lc + Q_BLOCK, :]
    for h in range(HEADS_PER_GROUP):
        o_ref[:, h * HEAD_DIM:(h + 1) * HEAD_DIM] = o_scr[h].astype(BF16)
    lse_ref[...] = lse_scr[...]


def _attn_group(cm, gi, batch, seq):
    _, d = DILATION_GROUPS[gi]
    lc = ATTN_TP // d
    n_t = seq // ATTN_TP

    def plane(p):
        return pl.BlockSpec((1, 1, d, lc, GROUP_W), lambda b, n: (b, p, 0, n, 0))

    return pl.pallas_call(
        functools.partial(_attn_kernel, d=d, lc=lc),
        grid=(batch, n_t),
        in_specs=[plane(0), plane(1), plane(2)],
        out_specs=[pl.BlockSpec((ATTN_TP, GROUP_W), lambda b, n: (b * n_t + n, 0)),
                   pl.BlockSpec((ATTN_TP, HEAD_DIM), lambda b, n: (b * n_t + n, 0))],
        out_shape=[jax.ShapeDtypeStruct((batch * seq, GROUP_W), BF16),
                   jax.ShapeDtypeStruct((batch * seq, HEAD_DIM), F32)],
        scratch_shapes=[pltpu.VMEM((d, lc + Q_BLOCK, GROUP_W), BF16),
                        pltpu.VMEM((d, lc + Q_BLOCK, GROUP_W), BF16),
                        pltpu.VMEM((HEADS_PER_GROUP, ATTN_TP, HEAD_DIM), F32),
                        pltpu.VMEM((ATTN_TP, HEAD_DIM), F32)],
        compiler_params=_cparams(2), name=f"attn_g{gi}")(cm, cm, cm)


CONV_TC = 512
CONV_HALO = 32
CONV_ROWS = 64


def _conv_kernel(u_ref, halo_ref, w_ref, b_ref, g_ref, beta_ref, c_ref, hist):
    t = pl.program_id(1)
    hist[0:CONV_HALO, :] = jnp.where(t > 0, halo_ref[0], 0.0)
    hist[CONV_HALO:, :] = u_ref[0]
    lead = CONV_HALO - (CONV_K - 1)
    for rc in range(CONV_TC // CONV_ROWS):
        r0 = rc * CONV_ROWS
        acc = jnp.broadcast_to(b_ref[...], (CONV_ROWS, CONV_CH))
        for k in range(CONV_K):
            acc = acc + hist[r0 + lead + k:r0 + lead + k + CONV_ROWS, :] * w_ref[k:k + 1, :]
        y = _layer_norm_rows(acc, g_ref[...], beta_ref[...])
        c_ref[0, r0:r0 + CONV_ROWS, :] = (y * jax.nn.sigmoid(y)).astype(BF16)


def _conv_prompt(u2, conv_w, conv_b, ln_g, ln_b, batch, seq):
    u3 = u2.reshape(batch, seq, CONV_CH)
    per = CONV_TC // CONV_HALO
    vec = pl.BlockSpec((1, CONV_CH), lambda b, t: (0, 0))
    c = pl.pallas_call(
        _conv_kernel,
        grid=(batch, seq // CONV_TC),
        in_specs=[pl.BlockSpec((1, CONV_TC, CONV_CH), lambda b, t: (b, t, 0)),
                  pl.BlockSpec((1, CONV_HALO, CONV_CH), lambda b, t: (b, jnp.maximum(t * per - 1, 0), 0)),
                  pl.BlockSpec((CONV_K, CONV_CH), lambda b, t: (0, 0)), vec, vec, vec],
        out_specs=pl.BlockSpec((1, CONV_TC, CONV_CH), lambda b, t: (b, t, 0)),
        out_shape=jax.ShapeDtypeStruct((batch, seq, CONV_CH), BF16),
        scratch_shapes=[pltpu.VMEM((CONV_TC + CONV_HALO, CONV_CH), F32)],
        compiler_params=_cparams(2), name="conv_prompt")(
        u3, u3, conv_w, conv_b[None, :], ln_g[None, :], ln_b[None, :])
    return c.reshape(batch * seq, CONV_CH)


KV_SLAB = 2 * HEADS_PER_GROUP


def _sample_kernel(qkv_ref, c0_ref, c1_ref, c2_ref, st_ref, u_ref, w_ref, b_ref, g_ref, beta_ref,
                   mix_ref, nconv_ref, kvn0_ref, kvn1_ref, kvn2_ref, hist, *, t_new):
    cache_refs = (c0_ref, c1_ref, c2_ref)
    kvn_refs = (kvn0_ref, kvn1_ref, kvn2_ref)
    hp = HEADS_PER_GROUP
    outs = [[None] * N_GROUPS for _ in range(t_new)]
    lses = [[None] * N_GROUPS for _ in range(t_new)]
    for gi, (_, d) in enumerate(DILATION_GROUPS):
        cref = cache_refs[gi]
        n_rows = cref.shape[1]
        row = lax.broadcasted_iota(jnp.int32, (n_rows, hp, 1), 0)
        for t in range(t_new):
            kvn_refs[gi][0, t, 0] = qkv_ref[0, t, N_GROUPS + gi]
            kvn_refs[gi][0, t, 1] = qkv_ref[0, t, 2 * N_GROUPS + gi]
        for t in range(t_new):
            s0 = (t % d) * KV_SLAB
            first_row = t // d
            q = qkv_ref[0, t, gi][None]
            kc = cref[0, :, s0:s0 + hp, :]
            vc = cref[0, :, s0 + hp:s0 + 2 * hp, :]
            s_c = jnp.sum(kc * q, axis=2, keepdims=True) * ATTN_SCALE
            if first_row > 0:
                s_c = jnp.where(row >= first_row, s_c, NEG_BIG)
            newest = [t - d * jj for jj in range(t // d + 1)]
            s_n, v_n = [], []
            for tn in newest:
                kn = qkv_ref[0, tn, N_GROUPS + gi][None]
                v_n.append(qkv_ref[0, tn, 2 * N_GROUPS + gi][None])
                s_n.append(jnp.sum(kn * q, axis=2, keepdims=True) * ATTN_SCALE)
            m = jnp.max(s_c, axis=0, keepdims=True)
            for sn in s_n:
                m = jnp.maximum(m, sn)
            e_c = jnp.exp(s_c - m)
            den = jnp.sum(e_c, axis=0, keepdims=True)
            acc = jnp.sum(e_c * vc, axis=0, keepdims=True)
            for sn, v1 in zip(s_n, v_n):
                e_n = jnp.exp(sn - m)
                den = den + e_n
                acc = acc + e_n * v1
            outs[t][gi] = acc / den
            lses[t][gi] = m + jnp.log(den)
    for t in range(t_new):
        ls = lses[t]
        m = functools.reduce(jnp.maximum, ls)
        es = [jnp.exp(l - m) for l in ls]
        tot = functools.reduce(lambda a, b: a + b, es)
        for gi in range(N_GROUPS):
            slab = (outs[t][gi] * (es[gi] / tot))[0]
            for h in range(hp):
                c0 = gi * GROUP_W + h * HEAD_DIM
                mix_ref[0, t:t + 1, c0:c0 + HEAD_DIM] = slab[h:h + 1, :]
    n_state = CONV_K - 1
    hist[0:n_state, :] = st_ref[0]
    hist[n_state:n_state + t_new, :] = u_ref[0]
    acc = jnp.broadcast_to(b_ref[...], (t_new, CONV_CH))
    for k in range(CONV_K):
        acc = acc + hist[k:k + t_new, :] * w_ref[k:k + 1, :]
    y = _layer_norm_rows(acc, g_ref[...], beta_ref[...])
    mix_ref[0, :, ATTN_WIDTH:] = y * jax.nn.sigmoid(y)
    nconv_ref[0] = hist[t_new:t_new + n_state, :]


def _sample_mixers(qkvs, us, caches, state, conv_w, conv_b, ln_g, ln_b, dec_batch, t_new):
    n_qkv = qkvs.shape[1]
    hp = HEADS_PER_GROUP
    qkv5 = qkvs.reshape(dec_batch, t_new, n_qkv, hp, HEAD_DIM)
    u3 = us.reshape(dec_batch, t_new, CONV_CH)
    cache_in, cache_specs = [], []
    for (window, d), cache in zip(DILATION_GROUPS, caches):
        buf = cache.shape[1]
        assert buf == window and buf % d == 0
        n_rows = buf // d
        cache_in.append(cache.reshape(dec_batch, n_rows, d * KV_SLAB, HEAD_DIM))
        used = min(d, t_new) * KV_SLAB
        cache_specs.append(pl.BlockSpec((1, n_rows, used, HEAD_DIM), lambda b: (b, 0, 0, 0)))
    vec = pl.BlockSpec((1, CONV_CH), lambda b: (0, 0))
    n_state = CONV_K - 1
    kvn_shape = jax.ShapeDtypeStruct((dec_batch, t_new, 2, hp, HEAD_DIM), F32)
    kvn_spec = pl.BlockSpec((1, t_new, 2, hp, HEAD_DIM), lambda b: (b, 0, 0, 0, 0))
    mix, nconv, kvn0, kvn1, kvn2 = pl.pallas_call(
        functools.partial(_sample_kernel, t_new=t_new),
        grid=(dec_batch,),
        in_specs=[pl.BlockSpec((1, t_new, n_qkv, hp, HEAD_DIM), lambda b: (b, 0, 0, 0, 0))] + cache_specs + [
            pl.BlockSpec((1, n_state, CONV_CH), lambda b: (b, 0, 0)),
            pl.BlockSpec((1, t_new, CONV_CH), lambda b: (b, 0, 0)),
            pl.BlockSpec((CONV_K, CONV_CH), lambda b: (0, 0)), vec, vec, vec],
        out_specs=[pl.BlockSpec((1, t_new, D_MODEL), lambda b: (b, 0, 0)),
                   pl.BlockSpec((1, n_state, CONV_CH), lambda b: (b, 0, 0)),
                   kvn_spec, kvn_spec, kvn_spec],
        out_shape=[jax.ShapeDtypeStruct((dec_batch, t_new, D_MODEL), F32),
                   jax.ShapeDtypeStruct((dec_batch, n_state, CONV_CH), F32),
                   kvn_shape, kvn_shape, kvn_shape],
        scratch_shapes=[pltpu.VMEM((n_state + t_new + 6, CONV_CH), F32)],
        compiler_params=_cparams(1), name="sample_mixers")(
        qkv5, *cache_in, state, u3, conv_w, conv_b[None, :], ln_g[None, :], ln_b[None, :])
    return mix.reshape(dec_batch * t_new, D_MODEL), nconv, (kvn0, kvn1, kvn2)


OUT_TM = 512


def _out_proj_kernel(o0_ref, o1_ref, o2_ref, l0_ref, l1_ref, l2_ref, c_ref, x_ref, w_ref, g_ref, b_ref,
                     mixs_ref, xs_ref, x1_ref, x1bf_ref, x1s_ref, mix_scr, *, n_i, alpha):
    i = pl.program_id(0)
    o_refs = (o0_ref, o1_ref, o2_ref)
    ls = [l0_ref[...], l1_ref[...], l2_ref[...]]
    m = jnp.maximum(jnp.maximum(ls[0], ls[1]), ls[2])
    es = [jnp.exp(l - m) for l in ls]
    inv = 1.0 / (es[0] + es[1] + es[2])
    for gi in range(N_GROUPS):
        a = es[gi] * inv
        for h in range(HEADS_PER_GROUP):
            c0 = h * HEAD_DIM
            og = o_refs[gi][:, c0:c0 + HEAD_DIM].astype(F32)
            mix_scr[:, gi * GROUP_W + c0:gi * GROUP_W + c0 + HEAD_DIM] = (og * a[:, h:h + 1]).astype(BF16)
    mix_scr[:, ATTN_WIDTH:] = c_ref[...]
    y = jnp.dot(mix_scr[...], w_ref[...], preferred_element_type=F32) + alpha * x_ref[...]
    x1 = _layer_norm_rows(y, g_ref[...], b_ref[...])
    x1_ref[...] = x1
    x1bf_ref[...] = x1.astype(BF16)

    @pl.when(i == n_i - 1)
    def _():
        ys = jnp.dot(mixs_ref[...].astype(BF16), w_ref[...], preferred_element_type=F32) + alpha * xs_ref[...]
        x1s_ref[...] = _layer_norm_rows(ys, g_ref[...], b_ref[...])


def _out_proj(os, lses, c, x2, w_bf, ln_g, ln_b, mixs, xs2, alpha):
    m = x2.shape[0]
    ms = xs2.shape[0]
    n_i = m // OUT_TM
    row = lambda w: pl.BlockSpec((OUT_TM, w), lambda i: (i, 0))
    whole = lambda a: pl.BlockSpec(a.shape, lambda i: (0,) * a.ndim)
    g2, b2 = ln_g[None, :], ln_b[None, :]
    return pl.pallas_call(
        functools.partial(_out_proj_kernel, n_i=n_i, alpha=alpha),
        grid=(n_i,),
        in_specs=[row(GROUP_W)] * 3 + [row(HEAD_DIM)] * 3 + [row(CONV_CH), row(D_MODEL),
                  whole(w_bf), whole(g2), whole(b2), whole(mixs), whole(xs2)],
        out_specs=[row(D_MODEL), row(D_MODEL), pl.BlockSpec((ms, D_MODEL), lambda i: (0, 0))],
        out_shape=[jax.ShapeDtypeStruct((m, D_MODEL), F32), jax.ShapeDtypeStruct((m, D_MODEL), BF16),
                   jax.ShapeDtypeStruct((ms, D_MODEL), F32)],
        scratch_shapes=[pltpu.VMEM((OUT_TM, D_MODEL), BF16)],
        compiler_params=_cparams(1), name="out_proj")(
        *os, *lses, c, x2, w_bf, g2, b2, mixs, xs2)


FFN_TM = 1024
FFN_TF = 512
FFN_RES_W = 256


def _ffn_kernel(xbf_ref, xres_ref, wg_ref, wu_ref, wd_ref, g_ref, b_ref, xs_ref,
                c0_ref, c1_ref, c2_ref, kvn0_ref, kvn1_ref, kvn2_ref,
                y_ref, ys_ref, nk0_ref, nk1_ref, nk2_ref, xsbf, sems, *, n_i, n_f, alpha, t_new):
    i = pl.program_id(0)
    f = pl.program_id(1)

    def cache_copies():
        copies = []
        for gi, (cache, new, out) in enumerate(((c0_ref, kvn0_ref, nk0_ref), (c1_ref, kvn1_ref, nk1_ref),
                                                (c2_ref, kvn2_ref, nk2_ref))):
            n_b, rows, _ = cache.shape
            keep = rows - t_new * KV_SLAB
            for b in range(n_b):
                copies.append(pltpu.make_async_copy(cache.at[b, pl.ds(t_new * KV_SLAB, keep)],
                                                    out.at[b, pl.ds(0, keep)], sems.at[gi, b]))
            copies.append(pltpu.make_async_copy(new, out.at[:, pl.ds(keep, t_new * KV_SLAB)],
                                                sems.at[gi, n_b]))
        return copies

    @pl.when((i == 0) & (f == 0))
    def _():
        for cp in cache_copies():
            cp.start()

    def swiglu_down(xb):
        gate = jnp.dot(xb, wg_ref[...], preferred_element_type=F32)
        up = jnp.dot(xb, wu_ref[...], preferred_element_type=F32)
        act = (gate * jax.nn.sigmoid(gate) * up).astype(BF16)
        return jnp.dot(act, wd_ref[...], preferred_element_type=F32)

    down = swiglu_down(xbf_ref[...])

    @pl.when(f == 0)
    def _():
        y_ref[...] = down

    @pl.when(f > 0)
    def _():
        y_ref[...] += down

    for c in range(D_MODEL // FFN_RES_W):
        @pl.when(f == c)
        def _(c=c):
            y_ref[:, c * FFN_RES_W:(c + 1) * FFN_RES_W] += alpha * xres_ref[...]

    @pl.when(f == n_f - 1)
    def _():
        y_ref[...] = _layer_norm_rows(y_ref[...], g_ref[...], b_ref[...])

    @pl.when(i == n_i - 1)
    def _():
        @pl.when(f == 0)
        def _():
            xsbf[...] = xs_ref[...].astype(BF16)

        downs = swiglu_down(xsbf[...])

        @pl.when(f == 0)
        def _():
            ys_ref[...] = downs + alpha * xs_ref[...]

        @pl.when(f > 0)
        def _():
            ys_ref[...] += downs

        @pl.when(f == n_f - 1)
        def _():
            ys_ref[...] = _layer_norm_rows(ys_ref[...], g_ref[...], b_ref[...])

    @pl.when((i == n_i - 1) & (f == n_f - 1))
    def _():
        for cp in cache_copies():
            cp.wait()


def _ffn(x1, x1bf, x1s, wg_bf, wu_bf, wd_bf, ln_g, ln_b, caches, kv_new, alpha, t_new):
    m = x1.shape[0]
    ms = x1s.shape[0]
    hidden = wg_bf.shape[1]
    n_i = m // FFN_TM
    n_f = hidden // FFN_TF
    n_res = D_MODEL // FFN_RES_W
    assert n_f >= n_res
    g2, b2 = ln_g[None, :], ln_b[None, :]
    any_spec = pl.BlockSpec(memory_space=pl.ANY)
    dec_batch = caches[0].shape[0]
    flat = lambda a: a.reshape(dec_batch, -1, HEAD_DIM)
    caches_flat = [flat(c) for c in caches]
    outs = pl.pallas_call(
        functools.partial(_ffn_kernel, n_i=n_i, n_f=n_f, alpha=alpha, t_new=t_new),
        grid=(n_i, n_f),
        in_specs=[pl.BlockSpec((FFN_TM, D_MODEL), lambda i, f: (i, 0)),
                  pl.BlockSpec((FFN_TM, FFN_RES_W), lambda i, f: (i, jnp.minimum(f, n_res - 1))),
                  pl.BlockSpec((D_MODEL, FFN_TF), lambda i, f: (0, f)),
                  pl.BlockSpec((D_MODEL, FFN_TF), lambda i, f: (0, f)),
                  pl.BlockSpec((FFN_TF, D_MODEL), lambda i, f: (f, 0)),
                  pl.BlockSpec((1, D_MODEL), lambda i, f: (0, 0)),
                  pl.BlockSpec((1, D_MODEL), lambda i, f: (0, 0)),
                  pl.BlockSpec((ms, D_MODEL), lambda i, f: (0, 0))] + [any_spec] * 6,
        out_specs=[pl.BlockSpec((FFN_TM, D_MODEL), lambda i, f: (i, 0)),
                   pl.BlockSpec((ms, D_MODEL), lambda i, f: (0, 0))] + [any_spec] * 3,
        out_shape=[jax.ShapeDtypeStruct((m, D_MODEL), F32), jax.ShapeDtypeStruct((ms, D_MODEL), F32)]
                  + [jax.ShapeDtypeStruct(c.shape, c.dtype) for c in caches_flat],
        scratch_shapes=[pltpu.VMEM((ms, D_MODEL), BF16),
                        pltpu.SemaphoreType.DMA((N_GROUPS, dec_batch + 1))],
        compiler_params=_cparams(2), name="ffn")(
        x1bf, x1, wg_bf, wu_bf, wd_bf, g2, b2, x1s, *caches_flat, *[flat(k) for k in kv_new])
    return list(outs[:2]) + [o.reshape(c.shape) for o, c in zip(outs[2:], caches)]


def kernel(x_prompt, x_sample, cache_kv_w128, cache_kv_w512, cache_kv_w2048, state_conv, w_in, w_out,
           conv_w, conv_b, conv_ln_g, conv_ln_b, ln1_g, ln1_b, w_gate, w_up, w_down, ln2_g, ln2_b):
    depth = w_in.shape[0]
    batch, seq, _ = x_prompt.shape
    dec_batch, t_new, _ = x_sample.shape
    caches = (cache_kv_w128, cache_kv_w512, cache_kv_w2048)
    alpha = (2.0 * depth) ** 0.25

    cos_p, sin_p = _rope_tables(jnp.arange(seq, dtype=jnp.int32))
    pos_s = PAST_LEN + jnp.arange(t_new, dtype=jnp.int32)
    cos_s, sin_s = _rope_tables(jnp.tile(pos_s, dec_batch))

    xp = x_prompt.reshape(batch * seq, D_MODEL)
    xs = x_sample.reshape(dec_batch * t_new, D_MODEL)
    kvp = [[] for _ in range(N_GROUPS)]
    kvs = [[] for _ in range(N_GROUPS)]
    convp, convs = [], []
    for l in range(depth):
        cm0, cm1, cm2, u, kv0, kv1, kv2, qkvs, us = _in_proj(
            xp, xs, w_in[l].astype(BF16), cos_p, sin_p, cos_s, sin_s, batch, seq)
        os, lses = zip(*[_attn_group(cm, gi, batch, seq) for gi, cm in enumerate((cm0, cm1, cm2))])
        c = _conv_prompt(u, conv_w[l], conv_b[l], conv_ln_g[l], conv_ln_b[l], batch, seq)
        layer_caches = [cc[l] for cc in caches]
        mixs, nconv_s, kv_new = _sample_mixers(qkvs, us, layer_caches, state_conv[l],
                                               conv_w[l], conv_b[l], conv_ln_g[l], conv_ln_b[l], dec_batch, t_new)
        x1, x1bf, x1s = _out_proj(os, lses, c, xp, w_out[l].astype(BF16), ln1_g[l], ln1_b[l], mixs, xs, alpha)
        xp, xs, nk0, nk1, nk2 = _ffn(x1, x1bf, x1s, w_gate[l].astype(BF16), w_up[l].astype(BF16),
                                     w_down[l].astype(BF16), ln2_g[l], ln2_b[l], layer_caches, kv_new,
                                     alpha, t_new)
        for gi, (kv, nk) in enumerate(zip((kv0, kv1, kv2), (nk0, nk1, nk2))):
            kvp[gi].append(kv.reshape(batch, kv.shape[1], 2, HEADS_PER_GROUP, HEAD_DIM))
            kvs[gi].append(nk)
        convp.append(u.reshape(batch, seq, CONV_CH)[:, seq - (CONV_K - 1):])
        convs.append(nconv_s)

    y_prompt = xp.reshape(batch, seq, D_MODEL)
    y_sample = xs.reshape(dec_batch, t_new, D_MODEL)
    return (y_prompt, y_sample, jnp.stack(kvp[0]), jnp.stack(kvp[1]), jnp.stack(kvp[2]), jnp.stack(convp),
            jnp.stack(kvs[0]), jnp.stack(kvs[1]), jnp.stack(kvs[2]), jnp.stack(convs))
```

```python
import functools

import numpy as np
import jax
import jax.numpy as jnp
from jax import lax
from jax.experimental import pallas as pl
from jax.experimental.pallas import tpu as pltpu

D_MODEL = 2048
HEAD_DIM = 128
CONV_CH = D_MODEL // 4
ATTN_WIDTH = D_MODEL - CONV_CH
DILATION_GROUPS = ((128, 1), (512, 4), (2048, 16))
N_GROUPS = len(DILATION_GROUPS)
HEADS_PER_GROUP = ATTN_WIDTH // HEAD_DIM // N_GROUPS
GROUP_W = HEADS_PER_GROUP * HEAD_DIM
CONV_K = 31
ROPE_THETA = 10000.0
LN_EPS = 1e-5
Q_BLOCK = 128
ATTN_SCALE = HEAD_DIM ** -0.5
NEG_BIG = -1e30
PAST_LEN = 16384

F32 = jnp.float32
BF16 = jnp.bfloat16

VMEM_LIMIT = 56 * 1024 * 1024


def _cparams(n_axes):
    return pltpu.CompilerParams(dimension_semantics=("arbitrary",) * n_axes,
                                vmem_limit_bytes=VMEM_LIMIT)


def _layer_norm_rows(y, g, b):
    mu = jnp.mean(y, axis=-1, keepdims=True)
    yc = y - mu
    var = jnp.mean(yc * yc, axis=-1, keepdims=True)
    return yc * lax.rsqrt(var + LN_EPS) * g + b


def _rope_tables(pos):
    half = HEAD_DIM // 2
    inv = ROPE_THETA ** (-jnp.arange(half, dtype=F32) / half)
    ang = pos.astype(F32)[:, None] * inv[None, :]
    cos, sin = jnp.cos(ang), jnp.sin(ang)
    return jnp.concatenate([cos, cos], axis=1), jnp.concatenate([-sin, sin], axis=1)


def _rope(h, cos, sin):
    parts = []
    for hh in range(HEADS_PER_GROUP):
        hs = h[:, hh * HEAD_DIM:(hh + 1) * HEAD_DIM]
        parts.append(hs * cos + pltpu.roll(hs, HEAD_DIM // 2, axis=1) * sin)
    return jnp.concatenate(parts, axis=1)


IN_TM = 1024
IN_TN = GROUP_W


def _kv_window_plan(seq, batch, n_j):
    tiles_per_batch = seq // IN_TM
    n_i = batch * tiles_per_batch
    plans = []
    for gi, (window, _) in enumerate(DILATION_GROUPS):
        keep = min(window, seq)
        rb = min(keep, IN_TM)
        first_tile = (seq - keep) // IN_TM
        row_lo = (seq - keep) - first_tile * IN_TM
        writes = []
        for i in range(n_i):
            b, it = divmod(i, tiles_per_batch)
            if it >= first_tile:
                for c, j in enumerate((N_GROUPS + gi, 2 * N_GROUPS + gi)):
                    writes.append((i * n_j + j, (b, it - first_tile, c)))
        writes.sort()
        tab = np.zeros((n_i * n_j, 3), np.int32)
        w = 0
        for step in range(n_i * n_j):
            while w < len(writes) - 1 and writes[w][0] < step:
                w += 1
            tab[step] = writes[w][1]
        plans.append(dict(keep=keep, rb=rb, first_tile=first_tile, row_lo=row_lo, tab=tab))
    return plans


def _in_proj_kernel(tab_ref, x_ref, w_ref, cos_ref, sin_ref, xs_ref, coss_ref, sins_ref,
                    cm0_ref, cm1_ref, cm2_ref, u_ref, kv0_ref, kv1_ref, kv2_ref, qkvs_ref, us_ref,
                    xbf, a_scr, de_scr, xsbf, as_scr, *, n_i, tiles_per_batch, plans):
    del tab_ref
    i = pl.program_id(0)
    j = pl.program_id(1)
    kv_refs = (kv0_ref, kv1_ref, kv2_ref)
    cm_refs = (cm0_ref, cm1_ref, cm2_ref)

    @pl.when(j == 0)
    def _():
        xbf[...] = x_ref[...].astype(BF16)

    acc = jnp.dot(xbf[...], w_ref[...], preferred_element_type=F32)

    def store_class_major(gi, val):
        d = DILATION_GROUPS[gi][1]
        ref = cm_refs[gi]
        if d == 1:
            ref[0, 0, 0] = val.astype(BF16)
            return
        for h in range(HEADS_PER_GROUP):
            de_scr[h] = val[:, h * HEAD_DIM:(h + 1) * HEAD_DIM]
        rows = IN_TM // d
        for r in range(d):
            for h in range(HEADS_PER_GROUP):
                ref[0, 0, r, :, h * HEAD_DIM:(h + 1) * HEAD_DIM] = (
                    de_scr[h, pl.ds(r, rows, stride=d), :].astype(BF16))

    it = i % tiles_per_batch
    for gi in range(N_GROUPS):
        plan = plans[gi]
        in_window = it >= plan["first_tile"]

        @pl.when(j == gi)
        def _(gi=gi):
            store_class_major(gi, _rope(acc, cos_ref[...], sin_ref[...]))

        @pl.when(j == N_GROUPS + gi)
        def _(plan=plan, in_window=in_window, gi=gi):
            r = _rope(acc, cos_ref[...], sin_ref[...])
            store_class_major(gi, r)

            @pl.when(in_window)
            def _():
                kv_refs[gi][0] = r[plan["row_lo"]:plan["row_lo"] + plan["rb"], :]

        @pl.when(j == 2 * N_GROUPS + gi)
        def _(plan=plan, in_window=in_window, gi=gi):
            store_class_major(gi, acc)

            @pl.when(in_window)
            def _():
                kv_refs[gi][0] = acc[plan["row_lo"]:plan["row_lo"] + plan["rb"], :]

    @pl.when(j == 3 * N_GROUPS)
    def _():
        a_scr[...] = acc

    @pl.when(j == 3 * N_GROUPS + 1)
    def _():
        u_ref[...] = a_scr[...] * jax.nn.sigmoid(acc)

    @pl.when(i == n_i - 1)
    def _():
        @pl.when(j == 0)
        def _():
            xsbf[...] = xs_ref[...].astype(BF16)

        accs = jnp.dot(xsbf[...], w_ref[...], preferred_element_type=F32)

        def store_heads(val):
            for h in range(HEADS_PER_GROUP):
                qkvs_ref[:, 0, h, :] = val[:, h * HEAD_DIM:(h + 1) * HEAD_DIM]

        @pl.when(j < 2 * N_GROUPS)
        def _():
            store_heads(_rope(accs, coss_ref[...], sins_ref[...]))

        @pl.when((j >= 2 * N_GROUPS) & (j < 3 * N_GROUPS))
        def _():
            store_heads(accs)

        @pl.when(j == 3 * N_GROUPS)
        def _():
            as_scr[...] = accs

        @pl.when(j == 3 * N_GROUPS + 1)
        def _():
            us_ref[...] = as_scr[...] * jax.nn.sigmoid(accs)


def _in_proj(x2, xs2, w_bf, cos_p, sin_p, cos_s, sin_s, batch, seq):
    m = x2.shape[0]
    ms = xs2.shape[0]
    n_i = m // IN_TM
    n_j = w_bf.shape[1] // IN_TN
    n_qkv = 3 * N_GROUPS
    tiles_per_batch = seq // IN_TM
    plans = _kv_window_plan(seq, batch, n_j)
    tab = jnp.asarray(np.stack([p["tab"] for p in plans]).reshape(-1))
    n_steps = n_i * n_j

    def kv_map(gi):
        def f(i, j, tab_ref):
            base = (gi * n_steps + i * n_j + j) * 3
            return tab_ref[base], tab_ref[base + 1], tab_ref[base + 2]
        return f

    def cm_map(gi):
        def f(i, j, tab_ref):
            plane = (j > gi).astype(jnp.int32) + (j > N_GROUPS + gi).astype(jnp.int32)
            return i // tiles_per_batch, plane, 0, i % tiles_per_batch, 0
        return f

    last = n_i - 1
    in_specs = [
        pl.BlockSpec((IN_TM, D_MODEL), lambda i, j, t: (i, 0)),
        pl.BlockSpec((D_MODEL, IN_TN), lambda i, j, t: (0, j)),
        pl.BlockSpec((IN_TM, HEAD_DIM), lambda i, j, t: (i % tiles_per_batch, 0)),
        pl.BlockSpec((IN_TM, HEAD_DIM), lambda i, j, t: (i % tiles_per_batch, 0)),
        pl.BlockSpec((ms, D_MODEL), lambda i, j, t: (0, 0)),
        pl.BlockSpec((ms, HEAD_DIM), lambda i, j, t: (0, 0)),
        pl.BlockSpec((ms, HEAD_DIM), lambda i, j, t: (0, 0)),
    ]
    out_shape, out_specs = [], []
    for gi, (_, d) in enumerate(DILATION_GROUPS):
        out_shape.append(jax.ShapeDtypeStruct((batch, 3, d, seq // d, GROUP_W), BF16))
        out_specs.append(pl.BlockSpec((1, 1, d, IN_TM // d, GROUP_W), cm_map(gi)))
    out_shape.append(jax.ShapeDtypeStruct((m, CONV_CH), F32))
    out_specs.append(pl.BlockSpec((IN_TM, CONV_CH), lambda i, j, t: (i, 0)))
    for gi, p in enumerate(plans):
        out_shape.append(jax.ShapeDtypeStruct((batch, p["keep"], 2 * GROUP_W), F32))
        out_specs.append(pl.BlockSpec((1, p["rb"], GROUP_W), kv_map(gi)))
    out_shape += [jax.ShapeDtypeStruct((ms, n_qkv, HEADS_PER_GROUP, HEAD_DIM), F32),
                  jax.ShapeDtypeStruct((ms, CONV_CH), F32)]
    out_specs += [
        pl.BlockSpec((ms, 1, HEADS_PER_GROUP, HEAD_DIM),
                     lambda i, j, t: (0, jnp.where(i == last, jnp.minimum(j, n_qkv - 1), 0), 0, 0)),
        pl.BlockSpec((ms, CONV_CH), lambda i, j, t: (0, 0)),
    ]
    grid_spec = pltpu.PrefetchScalarGridSpec(
        num_scalar_prefetch=1, grid=(n_i, n_j), in_specs=in_specs, out_specs=out_specs,
        scratch_shapes=[pltpu.VMEM((IN_TM, D_MODEL), BF16), pltpu.VMEM((IN_TM, CONV_CH), F32),
                        pltpu.VMEM((HEADS_PER_GROUP, IN_TM, HEAD_DIM), F32),
                        pltpu.VMEM((ms, D_MODEL), BF16), pltpu.VMEM((ms, CONV_CH), F32)])
    kern = functools.partial(_in_proj_kernel, n_i=n_i, tiles_per_batch=tiles_per_batch, plans=plans)
    return pl.pallas_call(kern, grid_spec=grid_spec, out_shape=out_shape,
                          compiler_params=_cparams(2), name="in_proj")(
        tab, x2, w_bf, cos_p, sin_p, xs2, cos_s, sin_s)


ATTN_TP = 2048


def _attn_kernel(q_ref, k_ref, v_ref, o_ref, lse_ref, k_scr, v_scr, o_scr, lse_scr, *, d, lc):
    n = pl.program_id(1)
    n_qb = lc // Q_BLOCK

    @pl.when(n == 0)
    def _():
        k_scr[:, 0:Q_BLOCK, :] = jnp.zeros((d, Q_BLOCK, GROUP_W), BF16)
        v_scr[:, 0:Q_BLOCK, :] = jnp.zeros((d, Q_BLOCK, GROUP_W), BF16)

    k_scr[:, Q_BLOCK:, :] = k_ref[0, 0]
    v_scr[:, Q_BLOCK:, :] = v_ref[0, 0]
    qq = lax.broadcasted_iota(jnp.int32, (Q_BLOCK, 2 * Q_BLOCK), 0)
    kk = lax.broadcasted_iota(jnp.int32, (Q_BLOCK, 2 * Q_BLOCK), 1)
    band = (kk >= qq) & (kk <= qq + Q_BLOCK)
    lane = lax.broadcasted_iota(jnp.int32, (Q_BLOCK, HEAD_DIM), 1)

    def unit(u, carry):
        r = u // n_qb
        qb = u % n_qb
        r0 = pl.multiple_of(qb * Q_BLOCK, Q_BLOCK)
        valid = band & ((kk >= Q_BLOCK) | (n > 0) | (qb > 0))
        if d == 1:
            rows = pl.ds(r0, Q_BLOCK)
        else:
            rows = pl.ds(r0 * d + r, Q_BLOCK, stride=d)
        lse_blk = jnp.zeros((Q_BLOCK, HEAD_DIM), F32)
        for h in range(HEADS_PER_GROUP):
            c0 = h * HEAD_DIM
            q = q_ref[0, 0, r, pl.ds(r0, Q_BLOCK), c0:c0 + HEAD_DIM]
            k = k_scr[r, pl.ds(r0, 2 * Q_BLOCK), c0:c0 + HEAD_DIM]
            v = v_scr[r, pl.ds(r0, 2 * Q_BLOCK), c0:c0 + HEAD_DIM]
            s = lax.dot_general(q, k, (((1,), (1,)), ((), ())), preferred_element_type=F32)
            s = jnp.where(valid, s * ATTN_SCALE, NEG_BIG)
            m = jnp.max(s, axis=1, keepdims=True)
            e = jnp.exp(s - m)
            den = jnp.sum(e, axis=1, keepdims=True)
            o_scr[h, rows, :] = jnp.dot(e.astype(BF16), v, preferred_element_type=F32) / den
            lse_blk = jnp.where(lane == h, m + jnp.log(den), lse_blk)
        lse_scr[rows, :] = lse_blk
        return carry

    lax.fori_loop(0, d * n_qb, unit, 0, unroll=4)
    k_scr[:, 0:Q_BLOCK, :] = k_scr[:, lc:lc + Q_BLOCK, :]
    v_scr[:, 0:Q_BLOCK, :] = v_scr[:, lc:lc + Q_BLOCK, :]
    for h in range(HEADS_PER_GROUP):
        o_ref[:, h * HEAD_DIM:(h + 1) * HEAD_DIM] = o_scr[h].astype(BF16)
    lse_ref[...] = lse_scr[...]


def _attn_group(cm, gi, batch, seq):
    _, d = DILATION_GROUPS[gi]
    lc = ATTN_TP // d
    n_t = seq // ATTN_TP

    def plane(p):
        return pl.BlockSpec((1, 1, d, lc, GROUP_W), lambda b, n: (b, p, 0, n, 0))

    return pl.pallas_call(
        functools.partial(_attn_kernel, d=d, lc=lc),
        grid=(batch, n_t),
        in_specs=[plane(0), plane(1), plane(2)],
        out_specs=[pl.BlockSpec((ATTN_TP, GROUP_W), lambda b, n: (b * n_t + n, 0)),
                   pl.BlockSpec((ATTN_TP, HEAD_DIM), lambda b, n: (b * n_t + n, 0))],
        out_shape=[jax.ShapeDtypeStruct((batch * seq, GROUP_W), BF16),
                   jax.ShapeDtypeStruct((batch * seq, HEAD_DIM), F32)],
        scratch_shapes=[pltpu.VMEM((d, lc + Q_BLOCK, GROUP_W), BF16),
                        pltpu.VMEM((d, lc + Q_BLOCK, GROUP_W), BF16),
                        pltpu.VMEM((HEADS_PER_GROUP, ATTN_TP, HEAD_DIM), F32),
                        pltpu.VMEM((ATTN_TP, HEAD_DIM), F32)],
        compiler_params=_cparams(2), name=f"attn_g{gi}")(cm, cm, cm)


CONV_TC = 512
CONV_HALO = 32
CONV_ROWS = 64


def _conv_kernel(u_ref, halo_ref, w_ref, b_ref, g_ref, beta_ref, c_ref, hist):
    t = pl.program_id(1)
    hist[0:CONV_HALO, :] = jnp.where(t > 0, halo_ref[0], 0.0)
    hist[CONV_HALO:, :] = u_ref[0]
    lead = CONV_HALO - (CONV_K - 1)
    for rc in range(CONV_TC // CONV_ROWS):
        r0 = rc * CONV_ROWS
        acc = jnp.broadcast_to(b_ref[...], (CONV_ROWS, CONV_CH))
        for k in range(CONV_K):
            acc = acc + hist[r0 + lead + k:r0 + lead + k + CONV_ROWS, :] * w_ref[k:k + 1, :]
        y = _layer_norm_rows(acc, g_ref[...], beta_ref[...])
        c_ref[0, r0:r0 + CONV_ROWS, :] = (y * jax.nn.sigmoid(y)).astype(BF16)


def _conv_prompt(u2, conv_w, conv_b, ln_g, ln_b, batch, seq):
    u3 = u2.reshape(batch, seq, CONV_CH)
    per = CONV_TC // CONV_HALO
    vec = pl.BlockSpec((1, CONV_CH), lambda b, t: (0, 0))
    c = pl.pallas_call(
        _conv_kernel,
        grid=(batch, seq // CONV_TC),
        in_specs=[pl.BlockSpec((1, CONV_TC, CONV_CH), lambda b, t: (b, t, 0)),
                  pl.BlockSpec((1, CONV_HALO, CONV_CH), lambda b, t: (b, jnp.maximum(t * per - 1, 0), 0)),
                  pl.BlockSpec((CONV_K, CONV_CH), lambda b, t: (0, 0)), vec, vec, vec],
        out_specs=pl.BlockSpec((1, CONV_TC, CONV_CH), lambda b, t: (b, t, 0)),
        out_shape=jax.ShapeDtypeStruct((batch, seq, CONV_CH), BF16),
        scratch_shapes=[pltpu.VMEM((CONV_TC + CONV_HALO, CONV_CH), F32)],
        compiler_params=_cparams(2), name="conv_prompt")(
        u3, u3, conv_w, conv_b[None, :], ln_g[None, :], ln_b[None, :])
    return c.reshape(batch * seq, CONV_CH)


KV_SLAB = 2 * HEADS_PER_GROUP


def _shift_cache(cref, out_ref, d, t_new):
    slab = d * KV_SLAB
    shift = t_new * KV_SLAB
    n_rows = cref.shape[1]
    if shift % slab == 0:
        s = shift // slab
        out_ref[0, 0:(n_rows - s) * slab] = cref[0, s:n_rows].reshape((n_rows - s) * slab, HEAD_DIM)
    else:
        assert shift < slab and shift % 8 == 0
        out_ref[0, 0:slab - shift] = cref[0, 0, shift:slab]

        def body(ci, carry):
            out_ref[0, pl.ds(pl.multiple_of(ci * slab - shift, 8), slab)] = cref[0, ci]
            return carry

        lax.fori_loop(1, n_rows, body, 0)


def _sample_kernel(qkv_ref, c0_ref, c1_ref, c2_ref, st_ref, u_ref, w_ref, b_ref, g_ref, beta_ref,
                   mix_ref, nconv_ref, nk0_ref, nk1_ref, nk2_ref, hist, *, t_new):
    cache_refs = (c0_ref, c1_ref, c2_ref)
    nk_refs = (nk0_ref, nk1_ref, nk2_ref)
    hp = HEADS_PER_GROUP
    outs = [[None] * N_GROUPS for _ in range(t_new)]
    lses = [[None] * N_GROUPS for _ in range(t_new)]
    for gi, (_, d) in enumerate(DILATION_GROUPS):
        cref = cache_refs[gi]
        n_rows = cref.shape[1]
        row = lax.broadcasted_iota(jnp.int32, (n_rows, hp, 1), 0)
        _shift_cache(cref, nk_refs[gi], d, t_new)
        keep = nk_refs[gi].shape[1] - t_new * KV_SLAB
        for t in range(t_new):
            nk_refs[gi][0, keep + t * KV_SLAB:keep + t * KV_SLAB + hp] = qkv_ref[0, t, N_GROUPS + gi]
            nk_refs[gi][0, keep + t * KV_SLAB + hp:keep + (t + 1) * KV_SLAB] = qkv_ref[0, t, 2 * N_GROUPS + gi]
        for t in range(t_new):
            s0 = (t % d) * KV_SLAB
            first_row = t // d
            q = qkv_ref[0, t, gi][None]
            kc = cref[0, :, s0:s0 + hp, :]
            vc = cref[0, :, s0 + hp:s0 + 2 * hp, :]
            s_c = jnp.sum(kc * q, axis=2, keepdims=True) * ATTN_SCALE
            if first_row > 0:
                s_c = jnp.where(row >= first_row, s_c, NEG_BIG)
            newest = [t - d * jj for jj in range(t // d + 1)]
            s_n, v_n = [], []
            for tn in newest:
                kn = qkv_ref[0, tn, N_GROUPS + gi][None]
                v_n.append(qkv_ref[0, tn, 2 * N_GROUPS + gi][None])
                s_n.append(jnp.sum(kn * q, axis=2, keepdims=True) * ATTN_SCALE)
            m = jnp.max(s_c, axis=0, keepdims=True)
            for sn in s_n:
                m = jnp.maximum(m, sn)
            e_c = jnp.exp(s_c - m)
            den = jnp.sum(e_c, axis=0, keepdims=True)
            acc = jnp.sum(e_c * vc, axis=0, keepdims=True)
            for sn, v1 in zip(s_n, v_n):
                e_n = jnp.exp(sn - m)
                den = den + e_n
                acc = acc + e_n * v1
            outs[t][gi] = acc / den
            lses[t][gi] = m + jnp.log(den)
    for t in range(t_new):
        ls = lses[t]
        m = functools.reduce(jnp.maximum, ls)
        es = [jnp.exp(l - m) for l in ls]
        tot = functools.reduce(lambda a, b: a + b, es)
        for gi in range(N_GROUPS):
            slab = (outs[t][gi] * (es[gi] / tot))[0]
            for h in range(hp):
                c0 = gi * GROUP_W + h * HEAD_DIM
                mix_ref[0, t:t + 1, c0:c0 + HEAD_DIM] = slab[h:h + 1, :]
    n_state = CONV_K - 1
    hist[0:n_state, :] = st_ref[0]
    hist[n_state:n_state + t_new, :] = u_ref[0]
    acc = jnp.broadcast_to(b_ref[...], (t_new, CONV_CH))
    for k in range(CONV_K):
        acc = acc + hist[k:k + t_new, :] * w_ref[k:k + 1, :]
    y = _layer_norm_rows(acc, g_ref[...], beta_ref[...])
    mix_ref[0, :, ATTN_WIDTH:] = y * jax.nn.sigmoid(y)
    nconv_ref[0] = hist[t_new:t_new + n_state, :]


def _sample_mixers(qkvs, us, caches, state, conv_w, conv_b, ln_g, ln_b, dec_batch, t_new):
    n_qkv = qkvs.shape[1]
    hp = HEADS_PER_GROUP
    qkv5 = qkvs.reshape(dec_batch, t_new, n_qkv, hp, HEAD_DIM)
    u3 = us.reshape(dec_batch, t_new, CONV_CH)
    cache_in, cache_specs, nk_shapes, nk_specs = [], [], [], []
    for (window, d), cache in zip(DILATION_GROUPS, caches):
        buf = cache.shape[1]
        assert buf == window and buf % d == 0
        n_rows = buf // d
        cache_in.append(cache.reshape(dec_batch, n_rows, d * KV_SLAB, HEAD_DIM))
        cache_specs.append(pl.BlockSpec((1, n_rows, d * KV_SLAB, HEAD_DIM), lambda b: (b, 0, 0, 0)))
        nk_shapes.append(jax.ShapeDtypeStruct((dec_batch, buf * KV_SLAB, HEAD_DIM), F32))
        nk_specs.append(pl.BlockSpec((1, buf * KV_SLAB, HEAD_DIM), lambda b: (b, 0, 0)))
    vec = pl.BlockSpec((1, CONV_CH), lambda b: (0, 0))
    n_state = CONV_K - 1
    mix, nconv, nk0, nk1, nk2 = pl.pallas_call(
        functools.partial(_sample_kernel, t_new=t_new),
        grid=(dec_batch,),
        in_specs=[pl.BlockSpec((1, t_new, n_qkv, hp, HEAD_DIM), lambda b: (b, 0, 0, 0, 0))] + cache_specs + [
            pl.BlockSpec((1, n_state, CONV_CH), lambda b: (b, 0, 0)),
            pl.BlockSpec((1, t_new, CONV_CH), lambda b: (b, 0, 0)),
            pl.BlockSpec((CONV_K, CONV_CH), lambda b: (0, 0)), vec, vec, vec],
        out_specs=[pl.BlockSpec((1, t_new, D_MODEL), lambda b: (b, 0, 0)),
                   pl.BlockSpec((1, n_state, CONV_CH), lambda b: (b, 0, 0))] + nk_specs,
        out_shape=[jax.ShapeDtypeStruct((dec_batch, t_new, D_MODEL), F32),
                   jax.ShapeDtypeStruct((dec_batch, n_state, CONV_CH), F32)] + nk_shapes,
        scratch_shapes=[pltpu.VMEM((n_state + t_new + 6, CONV_CH), F32)],
        compiler_params=_cparams(1), name="sample_mixers")(
        qkv5, *cache_in, state, u3, conv_w, conv_b[None, :], ln_g[None, :], ln_b[None, :])
    new_caches = [nk.reshape(c.shape) for nk, c in zip((nk0, nk1, nk2), caches)]
    return mix.reshape(dec_batch * t_new, D_MODEL), nconv, new_caches


OUT_TM = 512


def _out_proj_kernel(o0_ref, o1_ref, o2_ref, l0_ref, l1_ref, l2_ref, c_ref, x_ref, w_ref, g_ref, b_ref,
                     mixs_ref, xs_ref, x1_ref, x1bf_ref, x1s_ref, mix_scr, *, n_i, alpha):
    i = pl.program_id(0)
    o_refs = (o0_ref, o1_ref, o2_ref)
    ls = [l0_ref[...], l1_ref[...], l2_ref[...]]
    m = jnp.maximum(jnp.maximum(ls[0], ls[1]), ls[2])
    es = [jnp.exp(l - m) for l in ls]
    inv = 1.0 / (es[0] + es[1] + es[2])
    for gi in range(N_GROUPS):
        a = es[gi] * inv
        for h in range(HEADS_PER_GROUP):
            c0 = h * HEAD_DIM
            og = o_refs[gi][:, c0:c0 + HEAD_DIM].astype(F32)
            mix_scr[:, gi * GROUP_W + c0:gi * GROUP_W + c0 + HEAD_DIM] = (og * a[:, h:h + 1]).astype(BF16)
    mix_scr[:, ATTN_WIDTH:] = c_ref[...]
    y = jnp.dot(mix_scr[...], w_ref[...], preferred_element_type=F32) + alpha * x_ref[...]
    x1 = _layer_norm_rows(y, g_ref[...], b_ref[...])
    x1_ref[...] = x1
    x1bf_ref[...] = x1.astype(BF16)

    @pl.when(i == n_i - 1)
    def _():
        ys = jnp.dot(mixs_ref[...].astype(BF16), w_ref[...], preferred_element_type=F32) + alpha * xs_ref[...]
        x1s_ref[...] = _layer_norm_rows(ys, g_ref[...], b_ref[...])


def _out_proj(os, lses, c, x2, w_bf, ln_g, ln_b, mixs, xs2, alpha):
    m = x2.shape[0]
    ms = xs2.shape[0]
    n_i = m // OUT_TM
    row = lambda w: pl.BlockSpec((OUT_TM, w), lambda i: (i, 0))
    whole = lambda a: pl.BlockSpec(a.shape, lambda i: (0,) * a.ndim)
    g2, b2 = ln_g[None, :], ln_b[None, :]
    return pl.pallas_call(
        functools.partial(_out_proj_kernel, n_i=n_i, alpha=alpha),
        grid=(n_i,),
        in_specs=[row(GROUP_W)] * 3 + [row(HEAD_DIM)] * 3 + [row(CONV_CH), row(D_MODEL),
                  whole(w_bf), whole(g2), whole(b2), whole(mixs), whole(xs2)],
        out_specs=[row(D_MODEL), row(D_MODEL), pl.BlockSpec((ms, D_MODEL), lambda i: (0, 0))],
        out_shape=[jax.ShapeDtypeStruct((m, D_MODEL), F32), jax.ShapeDtypeStruct((m, D_MODEL), BF16),
                   jax.ShapeDtypeStruct((ms, D_MODEL), F32)],
        scratch_shapes=[pltpu.VMEM((OUT_TM, D_MODEL), BF16)],
        compiler_params=_cparams(1), name="out_proj")(
        *os, *lses, c, x2, w_bf, g2, b2, mixs, xs2)


FFN_TM = 1024
FFN_TF = 512
FFN_RES_W = 256


def _ffn_kernel(xbf_ref, xres_ref, wg_ref, wu_ref, wd_ref, g_ref, b_ref, xs_ref,
                y_ref, ys_ref, xsbf, *, n_i, n_f, alpha):
    i = pl.program_id(0)
    f = pl.program_id(1)

    def swiglu_down(xb):
        gate = jnp.dot(xb, wg_ref[...], preferred_element_type=F32)
        up = jnp.dot(xb, wu_ref[...], preferred_element_type=F32)
        act = (gate * jax.nn.sigmoid(gate) * up).astype(BF16)
        return jnp.dot(act, wd_ref[...], preferred_element_type=F32)

    down = swiglu_down(xbf_ref[...])

    @pl.when(f == 0)
    def _():
        y_ref[...] = down

    @pl.when(f > 0)
    def _():
        y_ref[...] += down

    for c in range(D_MODEL // FFN_RES_W):
        @pl.when(f == c)
        def _(c=c):
            y_ref[:, c * FFN_RES_W:(c + 1) * FFN_RES_W] += alpha * xres_ref[...]

    @pl.when(f == n_f - 1)
    def _():
        y_ref[...] = _layer_norm_rows(y_ref[...], g_ref[...], b_ref[...])

    @pl.when(i == n_i - 1)
    def _():
        @pl.when(f == 0)
        def _():
            xsbf[...] = xs_ref[...].astype(BF16)

        downs = swiglu_down(xsbf[...])

        @pl.when(f == 0)
        def _():
            ys_ref[...] = downs + alpha * xs_ref[...]

        @pl.when(f > 0)
        def _():
            ys_ref[...] += downs

        @pl.when(f == n_f - 1)
        def _():
            ys_ref[...] = _layer_norm_rows(ys_ref[...], g_ref[...], b_ref[...])


def _ffn(x1, x1bf, x1s, wg_bf, wu_bf, wd_bf, ln_g, ln_b, alpha):
    m = x1.shape[0]
    ms = x1s.shape[0]
    hidden = wg_bf.shape[1]
    n_i = m // FFN_TM
    n_f = hidden // FFN_TF
    n_res = D_MODEL // FFN_RES_W
    assert n_f >= n_res
    g2, b2 = ln_g[None, :], ln_b[None, :]
    return pl.pallas_call(
        functools.partial(_ffn_kernel, n_i=n_i, n_f=n_f, alpha=alpha),
        grid=(n_i, n_f),
        in_specs=[pl.BlockSpec((FFN_TM, D_MODEL), lambda i, f: (i, 0)),
                  pl.BlockSpec((FFN_TM, FFN_RES_W), lambda i, f: (i, jnp.minimum(f, n_res - 1))),
                  pl.BlockSpec((D_MODEL, FFN_TF), lambda i, f: (0, f)),
                  pl.BlockSpec((D_MODEL, FFN_TF), lambda i, f: (0, f)),
                  pl.BlockSpec((FFN_TF, D_MODEL), lambda i, f: (f, 0)),
                  pl.BlockSpec((1, D_MODEL), lambda i, f: (0, 0)),
                  pl.BlockSpec((1, D_MODEL), lambda i, f: (0, 0)),
                  pl.BlockSpec((ms, D_MODEL), lambda i, f: (0, 0))],
        out_specs=[pl.BlockSpec((FFN_TM, D_MODEL), lambda i, f: (i, 0)),
                   pl.BlockSpec((ms, D_MODEL), lambda i, f: (0, 0))],
        out_shape=[jax.ShapeDtypeStruct((m, D_MODEL), F32), jax.ShapeDtypeStruct((ms, D_MODEL), F32)],
        scratch_shapes=[pltpu.VMEM((ms, D_MODEL), BF16)],
        compiler_params=_cparams(2), name="ffn")(
        x1bf, x1, wg_bf, wu_bf, wd_bf, g2, b2, x1s)


def kernel(x_prompt, x_sample, cache_kv_w128, cache_kv_w512, cache_kv_w2048, state_conv, w_in, w_out,
           conv_w, conv_b, conv_ln_g, conv_ln_b, ln1_g, ln1_b, w_gate, w_up, w_down, ln2_g, ln2_b):
    depth = w_in.shape[0]
    batch, seq, _ = x_prompt.shape
    dec_batch, t_new, _ = x_sample.shape
    caches = (cache_kv_w128, cache_kv_w512, cache_kv_w2048)
    alpha = (2.0 * depth) ** 0.25

    cos_p, sin_p = _rope_tables(jnp.arange(seq, dtype=jnp.int32))
    pos_s = PAST_LEN + jnp.arange(t_new, dtype=jnp.int32)
    cos_s, sin_s = _rope_tables(jnp.tile(pos_s, dec_batch))

    xp = x_prompt.reshape(batch * seq, D_MODEL)
    xs = x_sample.reshape(dec_batch * t_new, D_MODEL)
    kvp = [[] for _ in range(N_GROUPS)]
    kvs = [[] for _ in range(N_GROUPS)]
    convp, convs = [], []
    for l in range(depth):
        cm0, cm1, cm2, u, kv0, kv1, kv2, qkvs, us = _in_proj(
            xp, xs, w_in[l].astype(BF16), cos_p, sin_p, cos_s, sin_s, batch, seq)
        os, lses = zip(*[_attn_group(cm, gi, batch, seq) for gi, cm in enumerate((cm0, cm1, cm2))])
        c = _conv_prompt(u, conv_w[l], conv_b[l], conv_ln_g[l], conv_ln_b[l], batch, seq)
        layer_caches = [cc[l] for cc in caches]
        mixs, nconv_s, new_caches = _sample_mixers(qkvs, us, layer_caches, state_conv[l], conv_w[l], conv_b[l],
                                                   conv_ln_g[l], conv_ln_b[l], dec_batch, t_new)
        x1, x1bf, x1s = _out_proj(os, lses, c, xp, w_out[l].astype(BF16), ln1_g[l], ln1_b[l], mixs, xs, alpha)
        xp, xs = _ffn(x1, x1bf, x1s, w_gate[l].astype(BF16), w_up[l].astype(BF16), w_down[l].astype(BF16),
                      ln2_g[l], ln2_b[l], alpha)
        for gi, (kv, nk) in enumerate(zip((kv0, kv1, kv2), new_caches)):
            kvp[gi].append(kv.reshape(batch, kv.shape[1], 2, HEADS_PER_GROUP, HEAD_DIM))
            kvs[gi].append(nk)
        convp.append(u.reshape(batch, seq, CONV_CH)[:, seq - (CONV_K - 1):])
        convs.append(nconv_s)

    y_prompt = xp.reshape(batch, seq, D_MODEL)
    y_sample = xs.reshape(dec_batch, t_new, D_MODEL)
    return (y_prompt, y_sample, jnp.stack(kvp[0]), jnp.stack(kvp[1]), jnp.stack(kvp[2]), jnp.stack(convp),
            jnp.stack(kvs[0]), jnp.stack(kvs[1]), jnp.stack(kvs[2]), jnp.stack(convs))
```

```python
import functools

import numpy as np
import jax
import jax.numpy as jnp
from jax import lax
from jax.experimental import pallas as pl
from jax.experimental.pallas import tpu as pltpu

D_MODEL = 2048
HEAD_DIM = 128
CONV_CH = D_MODEL // 4
ATTN_WIDTH = D_MODEL - CONV_CH
DILATION_GROUPS = ((128, 1), (512, 4), (2048, 16))
N_GROUPS = len(DILATION_GROUPS)
HEADS_PER_GROUP = ATTN_WIDTH // HEAD_DIM // N_GROUPS
GROUP_W = HEADS_PER_GROUP * HEAD_DIM
CONV_K = 31
ROPE_THETA = 10000.0
LN_EPS = 1e-5
Q_BLOCK = 128
ATTN_SCALE = HEAD_DIM ** -0.5
NEG_BIG = -1e30
PAST_LEN = 16384

F32 = jnp.float32
BF16 = jnp.bfloat16

VMEM_LIMIT = 56 * 1024 * 1024
LN_STRIP = 16


def _cparams(n_axes):
    return pltpu.CompilerParams(dimension_semantics=("arbitrary",) * n_axes,
                                vmem_limit_bytes=VMEM_LIMIT)


def _layer_norm_rows(y, g, b):
    mu = jnp.mean(y, axis=-1, keepdims=True)
    yc = y - mu
    var = jnp.mean(yc * yc, axis=-1, keepdims=True)
    return yc * lax.rsqrt(var + LN_EPS) * g + b


def _rope_tables(pos):
    half = HEAD_DIM // 2
    inv = ROPE_THETA ** (-jnp.arange(half, dtype=F32) / half)
    ang = pos.astype(F32)[:, None] * inv[None, :]
    cos, sin = jnp.cos(ang), jnp.sin(ang)
    return jnp.concatenate([cos, cos], axis=1), jnp.concatenate([-sin, sin], axis=1)


def _rope(h, cos, sin):
    parts = []
    for hh in range(HEADS_PER_GROUP):
        hs = h[:, hh * HEAD_DIM:(hh + 1) * HEAD_DIM]
        parts.append(hs * cos + pltpu.roll(hs, HEAD_DIM // 2, axis=1) * sin)
    return jnp.concatenate(parts, axis=1)


IN_TM = 1024
IN_TN = GROUP_W
IN_ROWS = 256


def _kv_window_plan(seq, batch, n_j):
    tiles_per_batch = seq // IN_TM
    n_i = batch * tiles_per_batch
    plans = []
    for gi, (window, _) in enumerate(DILATION_GROUPS):
        keep = min(window, seq)
        rb = min(keep, IN_TM)
        first_tile = (seq - keep) // IN_TM
        row_lo = (seq - keep) - first_tile * IN_TM
        writes = []
        for i in range(n_i):
            b, it = divmod(i, tiles_per_batch)
            if it >= first_tile:
                for c, j in enumerate((N_GROUPS + gi, 2 * N_GROUPS + gi)):
                    writes.append((i * n_j + j, (b, it - first_tile, c)))
        writes.sort()
        tab = np.zeros((n_i * n_j, 3), np.int32)
        w = 0
        for step in range(n_i * n_j):
            while w < len(writes) - 1 and writes[w][0] < step:
                w += 1
            tab[step] = writes[w][1]
        plans.append(dict(keep=keep, rb=rb, first_tile=first_tile, row_lo=row_lo, tab=tab))
    return plans


def _in_proj_kernel(tab_ref, x_ref, w_ref, cos_ref, sin_ref, xs_ref, coss_ref, sins_ref,
                    cm0_ref, cm1_ref, cm2_ref, u_ref, kv0_ref, kv1_ref, kv2_ref, qkvs_ref, us_ref,
                    xbf, a_scr, de_scr, xsbf, as_scr, *, n_i, tiles_per_batch, plans):
    del tab_ref
    i = pl.program_id(0)
    j = pl.program_id(1)
    kv_refs = (kv0_ref, kv1_ref, kv2_ref)
    cm_refs = (cm0_ref, cm1_ref, cm2_ref)

    chunks = [slice(c * IN_ROWS, (c + 1) * IN_ROWS) for c in range(IN_TM // IN_ROWS)]

    def chunk_dot(rows, cast=False):
        if cast:
            xbf[rows, :] = x_ref[rows, :].astype(BF16)
        return jnp.dot(xbf[rows, :], w_ref[...], preferred_element_type=F32)

    def rope_rows(val, rows):
        return _rope(val, cos_ref[rows, :], sin_ref[rows, :])

    def store_class_major(gi, c, val):
        d = DILATION_GROUPS[gi][1]
        ref = cm_refs[gi]
        per = IN_ROWS // d
        if d == 1:
            ref[0, 0, 0, chunks[c], :] = val.astype(BF16)
            return
        for h in range(HEADS_PER_GROUP):
            de_scr[c, h] = val[:, h * HEAD_DIM:(h + 1) * HEAD_DIM]
        for r in range(d):
            for h in range(HEADS_PER_GROUP):
                ref[0, 0, r, c * per:(c + 1) * per, h * HEAD_DIM:(h + 1) * HEAD_DIM] = (
                    de_scr[c, h, pl.ds(r, per, stride=d), :].astype(BF16))

    it = i % tiles_per_batch
    for gi in range(N_GROUPS):
        plan = plans[gi]
        in_window = it >= plan["first_tile"]
        window_rows = slice(plan["row_lo"], plan["row_lo"] + plan["rb"])

        @pl.when(j == gi)
        def _(gi=gi):
            for c, rows in enumerate(chunks):
                store_class_major(gi, c, rope_rows(chunk_dot(rows, cast=(gi == 0)), rows))

        @pl.when(j == N_GROUPS + gi)
        def _(window_rows=window_rows, in_window=in_window, gi=gi):
            for c, rows in enumerate(chunks):
                r = rope_rows(chunk_dot(rows), rows)
                store_class_major(gi, c, r)
                a_scr[rows, :] = r

            @pl.when(in_window)
            def _():
                kv_refs[gi][0] = a_scr[window_rows, :]

        @pl.when(j == 2 * N_GROUPS + gi)
        def _(window_rows=window_rows, in_window=in_window, gi=gi):
            for c, rows in enumerate(chunks):
                acc = chunk_dot(rows)
                store_class_major(gi, c, acc)
                a_scr[rows, :] = acc

            @pl.when(in_window)
            def _():
                kv_refs[gi][0] = a_scr[window_rows, :]

    @pl.when(j == 3 * N_GROUPS)
    def _():
        for rows in chunks:
            a_scr[rows, :] = chunk_dot(rows)

    @pl.when(j == 3 * N_GROUPS + 1)
    def _():
        for rows in chunks:
            u_ref[rows, :] = a_scr[rows, :] * jax.nn.sigmoid(chunk_dot(rows))

    @pl.when(i == n_i - 1)
    def _():
        @pl.when(j == 0)
        def _():
            xsbf[...] = xs_ref[...].astype(BF16)

        accs = jnp.dot(xsbf[...], w_ref[...], preferred_element_type=F32)

        def store_heads(val):
            for h in range(HEADS_PER_GROUP):
                qkvs_ref[:, 0, h, :] = val[:, h * HEAD_DIM:(h + 1) * HEAD_DIM]

        @pl.when(j < 2 * N_GROUPS)
        def _():
            store_heads(_rope(accs, coss_ref[...], sins_ref[...]))

        @pl.when((j >= 2 * N_GROUPS) & (j < 3 * N_GROUPS))
        def _():
            store_heads(accs)

        @pl.when(j == 3 * N_GROUPS)
        def _():
            as_scr[...] = accs

        @pl.when(j == 3 * N_GROUPS + 1)
        def _():
            us_ref[...] = as_scr[...] * jax.nn.sigmoid(accs)


def _in_proj(x2, xs2, w_bf, cos_p, sin_p, cos_s, sin_s, batch, seq):
    m = x2.shape[0]
    ms = xs2.shape[0]
    n_i = m // IN_TM
    n_j = w_bf.shape[1] // IN_TN
    n_qkv = 3 * N_GROUPS
    tiles_per_batch = seq // IN_TM
    plans = _kv_window_plan(seq, batch, n_j)
    tab = jnp.asarray(np.stack([p["tab"] for p in plans]).reshape(-1))
    n_steps = n_i * n_j

    def kv_map(gi):
        def f(i, j, tab_ref):
            base = (gi * n_steps + i * n_j + j) * 3
            return tab_ref[base], tab_ref[base + 1], tab_ref[base + 2]
        return f

    def cm_map(gi):
        def f(i, j, tab_ref):
            plane = (j > gi).astype(jnp.int32) + (j > N_GROUPS + gi).astype(jnp.int32)
            return i // tiles_per_batch, plane, 0, i % tiles_per_batch, 0
        return f

    last = n_i - 1
    in_specs = [
        pl.BlockSpec((IN_TM, D_MODEL), lambda i, j, t: (i, 0)),
        pl.BlockSpec((D_MODEL, IN_TN), lambda i, j, t: (0, j)),
        pl.BlockSpec((IN_TM, HEAD_DIM), lambda i, j, t: (i % tiles_per_batch, 0)),
        pl.BlockSpec((IN_TM, HEAD_DIM), lambda i, j, t: (i % tiles_per_batch, 0)),
        pl.BlockSpec((ms, D_MODEL), lambda i, j, t: (0, 0)),
        pl.BlockSpec((ms, HEAD_DIM), lambda i, j, t: (0, 0)),
        pl.BlockSpec((ms, HEAD_DIM), lambda i, j, t: (0, 0)),
    ]
    out_shape, out_specs = [], []
    for gi, (_, d) in enumerate(DILATION_GROUPS):
        out_shape.append(jax.ShapeDtypeStruct((batch, 3, d, seq // d, GROUP_W), BF16))
        out_specs.append(pl.BlockSpec((1, 1, d, IN_TM // d, GROUP_W), cm_map(gi)))
    out_shape.append(jax.ShapeDtypeStruct((m, CONV_CH), F32))
    out_specs.append(pl.BlockSpec((IN_TM, CONV_CH), lambda i, j, t: (i, 0)))
    for gi, p in enumerate(plans):
        out_shape.append(jax.ShapeDtypeStruct((batch, p["keep"], 2 * GROUP_W), F32))
        out_specs.append(pl.BlockSpec((1, p["rb"], GROUP_W), kv_map(gi)))
    out_shape += [jax.ShapeDtypeStruct((ms, n_qkv, HEADS_PER_GROUP, HEAD_DIM), F32),
                  jax.ShapeDtypeStruct((ms, CONV_CH), F32)]
    out_specs += [
        pl.BlockSpec((ms, 1, HEADS_PER_GROUP, HEAD_DIM),
                     lambda i, j, t: (0, jnp.where(i == last, jnp.minimum(j, n_qkv - 1), 0), 0, 0)),
        pl.BlockSpec((ms, CONV_CH), lambda i, j, t: (0, 0)),
    ]
    grid_spec = pltpu.PrefetchScalarGridSpec(
        num_scalar_prefetch=1, grid=(n_i, n_j), in_specs=in_specs, out_specs=out_specs,
        scratch_shapes=[pltpu.VMEM((IN_TM, D_MODEL), BF16), pltpu.VMEM((IN_TM, CONV_CH), F32),
                        pltpu.VMEM((IN_TM // IN_ROWS, HEADS_PER_GROUP, IN_ROWS, HEAD_DIM), F32),
                        pltpu.VMEM((ms, D_MODEL), BF16), pltpu.VMEM((ms, CONV_CH), F32)])
    kern = functools.partial(_in_proj_kernel, n_i=n_i, tiles_per_batch=tiles_per_batch, plans=plans)
    return pl.pallas_call(kern, grid_spec=grid_spec, out_shape=out_shape,
                          compiler_params=_cparams(2), name="in_proj")(
        tab, x2, w_bf, cos_p, sin_p, xs2, cos_s, sin_s)


ATTN_TP = 2048


def _attn_kernel(q_ref, k_ref, v_ref, o_ref, lse_ref, k_scr, v_scr, o_scr, lse_scr, *, d, lc):
    n = pl.program_id(1)
    n_qb = lc // Q_BLOCK

    @pl.when(n == 0)
    def _():
        k_scr[:, 0:Q_BLOCK, :] = jnp.zeros((d, Q_BLOCK, GROUP_W), BF16)
        v_scr[:, 0:Q_BLOCK, :] = jnp.zeros((d, Q_BLOCK, GROUP_W), BF16)

    k_scr[:, Q_BLOCK:, :] = k_ref[0, 0]
    v_scr[:, Q_BLOCK:, :] = v_ref[0, 0]
    qq = lax.broadcasted_iota(jnp.int32, (Q_BLOCK, 2 * Q_BLOCK), 0)
    kk = lax.broadcasted_iota(jnp.int32, (Q_BLOCK, 2 * Q_BLOCK), 1)
    band = (kk >= qq) & (kk <= qq + Q_BLOCK)
    lane = lax.broadcasted_iota(jnp.int32, (Q_BLOCK, HEAD_DIM), 1)

    def unit(u, carry):
        r = u // n_qb
        qb = u % n_qb
        r0 = pl.multiple_of(qb * Q_BLOCK, Q_BLOCK)
        valid = band & ((kk >= Q_BLOCK) | (n > 0) | (qb > 0))
        if d == 1:
            rows = pl.ds(r0, Q_BLOCK)
        else:
            rows = pl.ds(r0 * d + r, Q_BLOCK, stride=d)
        lse_blk = jnp.zeros((Q_BLOCK, HEAD_DIM), F32)
        for h in range(HEADS_PER_GROUP):
            c0 = h * HEAD_DIM
            q = q_ref[0, 0, r, pl.ds(r0, Q_BLOCK), c0:c0 + HEAD_DIM]
            k = k_scr[r, pl.ds(r0, 2 * Q_BLOCK), c0:c0 + HEAD_DIM]
            v = v_scr[r, pl.ds(r0, 2 * Q_BLOCK), c0:c0 + HEAD_DIM]
            s = lax.dot_general(q, k, (((1,), (1,)), ((), ())), preferred_element_type=F32)
            s = jnp.where(valid, s * ATTN_SCALE, NEG_BIG)
            m = jnp.max(s, axis=1, keepdims=True)
            e = jnp.exp(s - m)
            den = jnp.sum(e, axis=1, keepdims=True)
            o_scr[h, rows, :] = jnp.dot(e.astype(BF16), v, preferred_element_type=F32) / den
            lse_blk = jnp.where(lane == h, m + jnp.log(den), lse_blk)
        lse_scr[rows, :] = lse_blk
        return carry

    lax.fori_loop(0, d * n_qb, unit, 0, unroll=4)
    k_scr[:, 0:Q_BLOCK, :] = k_scr[:, lc:lc + Q_BLOCK, :]
    v_scr[:, 0:Q_BLOCK, :] = v_scr[:, lc:lc + Q_BLOCK, :]
    for h in range(HEADS_PER_GROUP):
        o_ref[:, h * HEAD_DIM:(h + 1) * HEAD_DIM] = o_scr[h].astype(BF16)
    lse_ref[...] = lse_scr[...]


def _attn_group(cm, gi, batch, seq):
    _, d = DILATION_GROUPS[gi]
    lc = ATTN_TP // d
    n_t = seq // ATTN_TP

    def plane(p):
        return pl.BlockSpec((1, 1, d, lc, GROUP_W), lambda b, n: (b, p, 0, n, 0))

    return pl.pallas_call(
        functools.partial(_attn_kernel, d=d, lc=lc),
        grid=(batch, n_t),
        in_specs=[plane(0), plane(1), plane(2)],
        out_specs=[pl.BlockSpec((ATTN_TP, GROUP_W), lambda b, n: (b * n_t + n, 0)),
                   pl.BlockSpec((ATTN_TP, HEAD_DIM), lambda b, n: (b * n_t + n, 0))],
        out_shape=[jax.ShapeDtypeStruct((batch * seq, GROUP_W), BF16),
                   jax.ShapeDtypeStruct((batch * seq, HEAD_DIM), F32)],
        scratch_shapes=[pltpu.VMEM((d, lc + Q_BLOCK, GROUP_W), BF16),
                        pltpu.VMEM((d, lc + Q_BLOCK, GROUP_W), BF16),
                        pltpu.VMEM((HEADS_PER_GROUP, ATTN_TP, HEAD_DIM), F32),
                        pltpu.VMEM((ATTN_TP, HEAD_DIM), F32)],
        compiler_params=_cparams(2), name=f"attn_g{gi}")(cm, cm, cm)


CONV_TC = 512
CONV_HALO = 32
CONV_ROWS = 64


def _conv_kernel(u_ref, halo_ref, w_ref, b_ref, g_ref, beta_ref, c_ref, hist):
    t = pl.program_id(1)
    hist[0:CONV_HALO, :] = jnp.where(t > 0, halo_ref[0], 0.0)
    hist[CONV_HALO:, :] = u_ref[0]
    lead = CONV_HALO - (CONV_K - 1)
    for rc in range(CONV_TC // CONV_ROWS):
        r0 = rc * CONV_ROWS
        acc = jnp.broadcast_to(b_ref[...], (CONV_ROWS, CONV_CH))
        for k in range(CONV_K):
            acc = acc + hist[r0 + lead + k:r0 + lead + k + CONV_ROWS, :] * w_ref[k:k + 1, :]
        y = _layer_norm_rows(acc, g_ref[...], beta_ref[...])
        c_ref[0, r0:r0 + CONV_ROWS, :] = (y * jax.nn.sigmoid(y)).astype(BF16)


def _conv_prompt(u2, conv_w, conv_b, ln_g, ln_b, batch, seq):
    u3 = u2.reshape(batch, seq, CONV_CH)
    per = CONV_TC // CONV_HALO
    vec = pl.BlockSpec((1, CONV_CH), lambda b, t: (0, 0))
    c = pl.pallas_call(
        _conv_kernel,
        grid=(batch, seq // CONV_TC),
        in_specs=[pl.BlockSpec((1, CONV_TC, CONV_CH), lambda b, t: (b, t, 0)),
                  pl.BlockSpec((1, CONV_HALO, CONV_CH), lambda b, t: (b, jnp.maximum(t * per - 1, 0), 0)),
                  pl.BlockSpec((CONV_K, CONV_CH), lambda b, t: (0, 0)), vec, vec, vec],
        out_specs=pl.BlockSpec((1, CONV_TC, CONV_CH), lambda b, t: (b, t, 0)),
        out_shape=jax.ShapeDtypeStruct((batch, seq, CONV_CH), BF16),
        scratch_shapes=[pltpu.VMEM((CONV_TC + CONV_HALO, CONV_CH), F32)],
        compiler_params=_cparams(2), name="conv_prompt")(
        u3, u3, conv_w, conv_b[None, :], ln_g[None, :], ln_b[None, :])
    return c.reshape(batch * seq, CONV_CH)


KV_SLAB = 2 * HEADS_PER_GROUP


def _shift_cache(cref, out_ref, d, t_new):
    slab = d * KV_SLAB
    shift = t_new * KV_SLAB
    n_rows = cref.shape[1]
    if shift % slab == 0:
        s = shift // slab
        out_ref[0, 0:(n_rows - s) * slab] = cref[0, s:n_rows].reshape((n_rows - s) * slab, HEAD_DIM)
    else:
        assert shift < slab and shift % 8 == 0
        out_ref[0, 0:slab - shift] = cref[0, 0, shift:slab]

        def body(ci, carry):
            out_ref[0, pl.ds(pl.multiple_of(ci * slab - shift, 8), slab)] = cref[0, ci]
            return carry

        lax.fori_loop(1, n_rows, body, 0)


def _sample_kernel(qkv_ref, c0_ref, c1_ref, c2_ref, st_ref, u_ref, w_ref, b_ref, g_ref, beta_ref,
                   mix_ref, nconv_ref, nk0_ref, nk1_ref, nk2_ref, hist, *, t_new):
    cache_refs = (c0_ref, c1_ref, c2_ref)
    nk_refs = (nk0_ref, nk1_ref, nk2_ref)
    hp = HEADS_PER_GROUP
    outs = [[None] * N_GROUPS for _ in range(t_new)]
    lses = [[None] * N_GROUPS for _ in range(t_new)]
    for gi, (_, d) in enumerate(DILATION_GROUPS):
        cref = cache_refs[gi]
        n_rows = cref.shape[1]
        row = lax.broadcasted_iota(jnp.int32, (n_rows, hp, 1), 0)
        _shift_cache(cref, nk_refs[gi], d, t_new)
        keep = nk_refs[gi].shape[1] - t_new * KV_SLAB
        for t in range(t_new):
            nk_refs[gi][0, keep + t * KV_SLAB:keep + t * KV_SLAB + hp] = qkv_ref[0, t, N_GROUPS + gi]
            nk_refs[gi][0, keep + t * KV_SLAB + hp:keep + (t + 1) * KV_SLAB] = qkv_ref[0, t, 2 * N_GROUPS + gi]
        for t in range(t_new):
            s0 = (t % d) * KV_SLAB
            first_row = t // d
            q = qkv_ref[0, t, gi][None]
            kc = cref[0, :, s0:s0 + hp, :]
            vc = cref[0, :, s0 + hp:s0 + 2 * hp, :]
            s_c = jnp.sum(kc * q, axis=2, keepdims=True) * ATTN_SCALE
            if first_row > 0:
                s_c = jnp.where(row >= first_row, s_c, NEG_BIG)
            newest = [t - d * jj for jj in range(t // d + 1)]
            s_n, v_n = [], []
            for tn in newest:
                kn = qkv_ref[0, tn, N_GROUPS + gi][None]
                v_n.append(qkv_ref[0, tn, 2 * N_GROUPS + gi][None])
                s_n.append(jnp.sum(kn * q, axis=2, keepdims=True) * ATTN_SCALE)
            m = jnp.max(s_c, axis=0, keepdims=True)
            for sn in s_n:
                m = jnp.maximum(m, sn)
            e_c = jnp.exp(s_c - m)
            den = jnp.sum(e_c, axis=0, keepdims=True)
            acc = jnp.sum(e_c * vc, axis=0, keepdims=True)
            for sn, v1 in zip(s_n, v_n):
                e_n = jnp.exp(sn - m)
                den = den + e_n
                acc = acc + e_n * v1
            outs[t][gi] = acc / den
            lses[t][gi] = m + jnp.log(den)
    for t in range(t_new):
        ls = lses[t]
        m = functools.reduce(jnp.maximum, ls)
        es = [jnp.exp(l - m) for l in ls]
        tot = functools.reduce(lambda a, b: a + b, es)
        for gi in range(N_GROUPS):
            slab = (outs[t][gi] * (es[gi] / tot))[0]
            for h in range(hp):
                c0 = gi * GROUP_W + h * HEAD_DIM
                mix_ref[0, t:t + 1, c0:c0 + HEAD_DIM] = slab[h:h + 1, :]
    n_state = CONV_K - 1
    hist[0:n_state, :] = st_ref[0]
    hist[n_state:n_state + t_new, :] = u_ref[0]
    acc = jnp.broadcast_to(b_ref[...], (t_new, CONV_CH))
    for k in range(CONV_K):
        acc = acc + hist[k:k + t_new, :] * w_ref[k:k + 1, :]
    y = _layer_norm_rows(acc, g_ref[...], beta_ref[...])
    mix_ref[0, :, ATTN_WIDTH:] = y * jax.nn.sigmoid(y)
    nconv_ref[0] = hist[t_new:t_new + n_state, :]


def _sample_mixers(qkvs, us, caches, state, conv_w, conv_b, ln_g, ln_b, dec_batch, t_new):
    n_qkv = qkvs.shape[1]
    hp = HEADS_PER_GROUP
    qkv5 = qkvs.reshape(dec_batch, t_new, n_qkv, hp, HEAD_DIM)
    u3 = us.reshape(dec_batch, t_new, CONV_CH)
    cache_in, cache_specs, nk_shapes, nk_specs = [], [], [], []
    for (window, d), cache in zip(DILATION_GROUPS, caches):
        buf = cache.shape[1]
        assert buf == window and buf % d == 0
        n_rows = buf // d
        cache_in.append(cache.reshape(dec_batch, n_rows, d * KV_SLAB, HEAD_DIM))
        cache_specs.append(pl.BlockSpec((1, n_rows, d * KV_SLAB, HEAD_DIM), lambda b: (b, 0, 0, 0)))
        nk_shapes.append(jax.ShapeDtypeStruct((dec_batch, buf * KV_SLAB, HEAD_DIM), F32))
        nk_specs.append(pl.BlockSpec((1, buf * KV_SLAB, HEAD_DIM), lambda b: (b, 0, 0)))
    vec = pl.BlockSpec((1, CONV_CH), lambda b: (0, 0))
    n_state = CONV_K - 1
    mix, nconv, nk0, nk1, nk2 = pl.pallas_call(
        functools.partial(_sample_kernel, t_new=t_new),
        grid=(dec_batch,),
        in_specs=[pl.BlockSpec((1, t_new, n_qkv, hp, HEAD_DIM), lambda b: (b, 0, 0, 0, 0))] + cache_specs + [
            pl.BlockSpec((1, n_state, CONV_CH), lambda b: (b, 0, 0)),
            pl.BlockSpec((1, t_new, CONV_CH), lambda b: (b, 0, 0)),
            pl.BlockSpec((CONV_K, CONV_CH), lambda b: (0, 0)), vec, vec, vec],
        out_specs=[pl.BlockSpec((1, t_new, D_MODEL), lambda b: (b, 0, 0)),
                   pl.BlockSpec((1, n_state, CONV_CH), lambda b: (b, 0, 0))] + nk_specs,
        out_shape=[jax.ShapeDtypeStruct((dec_batch, t_new, D_MODEL), F32),
                   jax.ShapeDtypeStruct((dec_batch, n_state, CONV_CH), F32)] + nk_shapes,
        scratch_shapes=[pltpu.VMEM((n_state + t_new + 6, CONV_CH), F32)],
        compiler_params=_cparams(1), name="sample_mixers")(
        qkv5, *cache_in, state, u3, conv_w, conv_b[None, :], ln_g[None, :], ln_b[None, :])
    new_caches = [nk.reshape(c.shape) for nk, c in zip((nk0, nk1, nk2), caches)]
    return mix.reshape(dec_batch * t_new, D_MODEL), nconv, new_caches


OUT_TM = 512
OUT_ROWS = 128


def _out_proj_kernel(o0_ref, o1_ref, o2_ref, l0_ref, l1_ref, l2_ref, c_ref, x_ref, w_ref, g_ref, b_ref,
                     mixs_ref, xs_ref, x1_ref, x1bf_ref, x1s_ref, mix_scr, y_scr, *, n_i, alpha):
    i = pl.program_id(0)
    o_refs = (o0_ref, o1_ref, o2_ref)
    chunks = [slice(r0, r0 + OUT_ROWS) for r0 in range(0, OUT_TM, OUT_ROWS)]

    def project(rows):
        ls = [l0_ref[rows, :], l1_ref[rows, :], l2_ref[rows, :]]
        m = jnp.maximum(jnp.maximum(ls[0], ls[1]), ls[2])
        es = [jnp.exp(l - m) for l in ls]
        inv = 1.0 / (es[0] + es[1] + es[2])
        for gi in range(N_GROUPS):
            a = es[gi] * inv
            for h in range(HEADS_PER_GROUP):
                c0 = h * HEAD_DIM
                og = o_refs[gi][rows, c0:c0 + HEAD_DIM].astype(F32)
                mix_scr[rows, gi * GROUP_W + c0:gi * GROUP_W + c0 + HEAD_DIM] = (og * a[:, h:h + 1]).astype(BF16)
        mix_scr[rows, ATTN_WIDTH:] = c_ref[rows, :]
        y_scr[rows, :] = (jnp.dot(mix_scr[rows, :], w_ref[...], preferred_element_type=F32)
                          + alpha * x_ref[rows, :])

    def normalize(rows):
        for r0 in range(rows.start, rows.stop, LN_STRIP):
            strip = slice(r0, r0 + LN_STRIP)
            x1 = _layer_norm_rows(y_scr[strip, :], g_ref[...], b_ref[...])
            x1_ref[strip, :] = x1
            x1bf_ref[strip, :] = x1.astype(BF16)

    project(chunks[0])
    for c in range(1, len(chunks)):
        project(chunks[c])
        normalize(chunks[c - 1])
    normalize(chunks[-1])

    @pl.when(i == n_i - 1)
    def _():
        ys = jnp.dot(mixs_ref[...].astype(BF16), w_ref[...], preferred_element_type=F32) + alpha * xs_ref[...]
        x1s_ref[...] = _layer_norm_rows(ys, g_ref[...], b_ref[...])


def _out_proj(os, lses, c, x2, w_bf, ln_g, ln_b, mixs, xs2, alpha):
    m = x2.shape[0]
    ms = xs2.shape[0]
    n_i = m // OUT_TM
    row = lambda w: pl.BlockSpec((OUT_TM, w), lambda i: (i, 0))
    whole = lambda a: pl.BlockSpec(a.shape, lambda i: (0,) * a.ndim)
    g2, b2 = ln_g[None, :], ln_b[None, :]
    return pl.pallas_call(
        functools.partial(_out_proj_kernel, n_i=n_i, alpha=alpha),
        grid=(n_i,),
        in_specs=[row(GROUP_W)] * 3 + [row(HEAD_DIM)] * 3 + [row(CONV_CH), row(D_MODEL),
                  whole(w_bf), whole(g2), whole(b2), whole(mixs), whole(xs2)],
        out_specs=[row(D_MODEL), row(D_MODEL), pl.BlockSpec((ms, D_MODEL), lambda i: (0, 0))],
        out_shape=[jax.ShapeDtypeStruct((m, D_MODEL), F32), jax.ShapeDtypeStruct((m, D_MODEL), BF16),
                   jax.ShapeDtypeStruct((ms, D_MODEL), F32)],
        scratch_shapes=[pltpu.VMEM((OUT_TM, D_MODEL), BF16), pltpu.VMEM((OUT_TM, D_MODEL), F32)],
        compiler_params=_cparams(1), name="out_proj")(
        *os, *lses, c, x2, w_bf, g2, b2, mixs, xs2)


FFN_TM = 1024
FFN_TF = 512
FFN_ROWS = 256
FFN_RES_W = 256


def _ffn_kernel(xbf_ref, xres_ref, wg_ref, wu_ref, wd_ref, g_ref, b_ref, xs_ref,
                y_ref, ys_ref, xsbf, *, n_i, n_f, alpha):
    i = pl.program_id(0)
    f = pl.program_id(1)

    def swiglu_down(xb):
        gate = jnp.dot(xb, wg_ref[...], preferred_element_type=F32)
        up = jnp.dot(xb, wu_ref[...], preferred_element_type=F32)
        act = (gate * jax.nn.sigmoid(gate) * up).astype(BF16)
        return jnp.dot(act, wd_ref[...], preferred_element_type=F32)

    @pl.when(f == 0)
    def _():
        y_ref[...] = jnp.zeros_like(y_ref)

    for r0 in range(0, FFN_TM, FFN_ROWS):
        y_ref[r0:r0 + FFN_ROWS, :] += swiglu_down(xbf_ref[r0:r0 + FFN_ROWS, :])

    for c in range(D_MODEL // FFN_RES_W):
        @pl.when(f == c)
        def _(c=c):
            y_ref[:, c * FFN_RES_W:(c + 1) * FFN_RES_W] += alpha * xres_ref[...]

    @pl.when(f == n_f - 1)
    def _():
        y_ref[...] = _layer_norm_rows(y_ref[...], g_ref[...], b_ref[...])

    @pl.when(i == n_i - 1)
    def _():
        @pl.when(f == 0)
        def _():
            xsbf[...] = xs_ref[...].astype(BF16)

        downs = swiglu_down(xsbf[...])

        @pl.when(f == 0)
        def _():
            ys_ref[...] = downs + alpha * xs_ref[...]

        @pl.when(f > 0)
        def _():
            ys_ref[...] += downs

        @pl.when(f == n_f - 1)
        def _():
            ys_ref[...] = _layer_norm_rows(ys_ref[...], g_ref[...], b_ref[...])


def _ffn(x1, x1bf, x1s, wg_bf, wu_bf, wd_bf, ln_g, ln_b, alpha):
    m = x1.shape[0]
    ms = x1s.shape[0]
    hidden = wg_bf.shape[1]
    n_i = m // FFN_TM
    n_f = hidden // FFN_TF
    n_res = D_MODEL // FFN_RES_W
    assert n_f >= n_res
    g2, b2 = ln_g[None, :], ln_b[None, :]
    return pl.pallas_call(
        functools.partial(_ffn_kernel, n_i=n_i, n_f=n_f, alpha=alpha),
        grid=(n_i, n_f),
        in_specs=[pl.BlockSpec((FFN_TM, D_MODEL), lambda i, f: (i, 0)),
                  pl.BlockSpec((FFN_TM, FFN_RES_W), lambda i, f: (i, jnp.minimum(f, n_res - 1))),
                  pl.BlockSpec((D_MODEL, FFN_TF), lambda i, f: (0, f)),
                  pl.BlockSpec((D_MODEL, FFN_TF), lambda i, f: (0, f)),
                  pl.BlockSpec((FFN_TF, D_MODEL), lambda i, f: (f, 0)),
                  pl.BlockSpec((1, D_MODEL), lambda i, f: (0, 0)),
                  pl.BlockSpec((1, D_MODEL), lambda i, f: (0, 0)),
                  pl.BlockSpec((ms, D_MODEL), lambda i, f: (0, 0))],
        out_specs=[pl.BlockSpec((FFN_TM, D_MODEL), lambda i, f: (i, 0)),
                   pl.BlockSpec((ms, D_MODEL), lambda i, f: (0, 0))],
        out_shape=[jax.ShapeDtypeStruct((m, D_MODEL), F32), jax.ShapeDtypeStruct((ms, D_MODEL), F32)],
        scratch_shapes=[pltpu.VMEM((ms, D_MODEL), BF16)],
        compiler_params=_cparams(2), name="ffn")(
        x1bf, x1, wg_bf, wu_bf, wd_bf, g2, b2, x1s)


def kernel(x_prompt, x_sample, cache_kv_w128, cache_kv_w512, cache_kv_w2048, state_conv, w_in, w_out,
           conv_w, conv_b, conv_ln_g, conv_ln_b, ln1_g, ln1_b, w_gate, w_up, w_down, ln2_g, ln2_b):
    depth = w_in.shape[0]
    batch, seq, _ = x_prompt.shape
    dec_batch, t_new, _ = x_sample.shape
    caches = (cache_kv_w128, cache_kv_w512, cache_kv_w2048)
    alpha = (2.0 * depth) ** 0.25

    cos_p, sin_p = _rope_tables(jnp.arange(seq, dtype=jnp.int32))
    pos_s = PAST_LEN + jnp.arange(t_new, dtype=jnp.int32)
    cos_s, sin_s = _rope_tables(jnp.tile(pos_s, dec_batch))

    xp = x_prompt.reshape(batch * seq, D_MODEL)
    xs = x_sample.reshape(dec_batch * t_new, D_MODEL)
    kvp = [[] for _ in range(N_GROUPS)]
    kvs = [[] for _ in range(N_GROUPS)]
    convp, convs = [], []
    for l in range(depth):
        cm0, cm1, cm2, u, kv0, kv1, kv2, qkvs, us = _in_proj(
            xp, xs, w_in[l].astype(BF16), cos_p, sin_p, cos_s, sin_s, batch, seq)
        os, lses = zip(*[_attn_group(cm, gi, batch, seq) for gi, cm in enumerate((cm0, cm1, cm2))])
        c = _conv_prompt(u, conv_w[l], conv_b[l], conv_ln_g[l], conv_ln_b[l], batch, seq)
        layer_caches = [cc[l] for cc in caches]
        mixs, nconv_s, new_caches = _sample_mixers(qkvs, us, layer_caches, state_conv[l], conv_w[l], conv_b[l],
                                                   conv_ln_g[l], conv_ln_b[l], dec_batch, t_new)
        x1, x1bf, x1s = _out_proj(os, lses, c, xp, w_out[l].astype(BF16), ln1_g[l], ln1_b[l], mixs, xs, alpha)
        xp, xs = _ffn(x1, x1bf, x1s, w_gate[l].astype(BF16), w_up[l].astype(BF16), w_down[l].astype(BF16),
                      ln2_g[l], ln2_b[l], alpha)
        for gi, (kv, nk) in enumerate(zip((kv0, kv1, kv2), new_caches)):
            kvp[gi].append(kv.reshape(batch, kv.shape[1], 2, HEADS_PER_GROUP, HEAD_DIM))
            kvs[gi].append(nk)
        convp.append(u.reshape(batch, seq, CONV_CH)[:, seq - (CONV_K - 1):])
        convs.append(nconv_s)

    y_prompt = xp.reshape(batch, seq, D_MODEL)
    y_sample = xs.reshape(dec_batch, t_new, D_MODEL)
    return (y_prompt, y_sample, jnp.stack(kvp[0]), jnp.stack(kvp[1]), jnp.stack(kvp[2]), jnp.stack(convp),
            jnp.stack(kvs[0]), jnp.stack(kvs[1]), jnp.stack(kvs[2]), jnp.stack(convs))
```

```python
import functools

import numpy as np
import jax
import jax.numpy as jnp
from jax import lax
from jax.experimental import pallas as pl
from jax.experimental.pallas import tpu as pltpu

D_MODEL = 2048
HEAD_DIM = 128
CONV_CH = D_MODEL // 4
ATTN_WIDTH = D_MODEL - CONV_CH
DILATION_GROUPS = ((128, 1), (512, 4), (2048, 16))
N_GROUPS = len(DILATION_GROUPS)
HEADS_PER_GROUP = ATTN_WIDTH // HEAD_DIM // N_GROUPS
GROUP_W = HEADS_PER_GROUP * HEAD_DIM
CONV_K = 31
ROPE_THETA = 10000.0
LN_EPS = 1e-5
Q_BLOCK = 128
ATTN_SCALE = HEAD_DIM ** -0.5
NEG_BIG = -1e30
PAST_LEN = 16384

F32 = jnp.float32
BF16 = jnp.bfloat16

VMEM_LIMIT = 56 * 1024 * 1024
LN_STRIP = 16
SUBLANES = 8


def _cparams(n_axes):
    return pltpu.CompilerParams(dimension_semantics=("arbitrary",) * n_axes,
                                vmem_limit_bytes=VMEM_LIMIT)


def _layer_norm_rows(y, g, b):
    mu = jnp.mean(y, axis=-1, keepdims=True)
    yc = y - mu
    var = jnp.mean(yc * yc, axis=-1, keepdims=True)
    return yc * lax.rsqrt(var + LN_EPS) * g + b


def _rope_tables(pos):
    half = HEAD_DIM // 2
    inv = ROPE_THETA ** (-jnp.arange(half, dtype=F32) / half)
    ang = pos.astype(F32)[:, None] * inv[None, :]
    cos, sin = jnp.cos(ang), jnp.sin(ang)
    return jnp.concatenate([cos, cos], axis=1), jnp.concatenate([-sin, sin], axis=1)


def _rope(h, cos, sin):
    parts = []
    for hh in range(HEADS_PER_GROUP):
        hs = h[:, hh * HEAD_DIM:(hh + 1) * HEAD_DIM]
        parts.append(hs * cos + pltpu.roll(hs, HEAD_DIM // 2, axis=1) * sin)
    return jnp.concatenate(parts, axis=1)


IN_TM = 1024
IN_TN = GROUP_W
IN_ROWS = 512


def _kv_window_plan(seq, batch, n_j):
    tiles_per_batch = seq // IN_TM
    n_i = batch * tiles_per_batch
    plans = []
    for gi, (window, _) in enumerate(DILATION_GROUPS):
        keep = min(window, seq)
        rb = min(keep, IN_TM)
        first_tile = (seq - keep) // IN_TM
        row_lo = (seq - keep) - first_tile * IN_TM
        writes = []
        for i in range(n_i):
            b, it = divmod(i, tiles_per_batch)
            if it >= first_tile:
                for c, j in enumerate((N_GROUPS + gi, 2 * N_GROUPS + gi)):
                    writes.append((i * n_j + j, (b, it - first_tile, c)))
        writes.sort()
        tab = np.zeros((n_i * n_j, 3), np.int32)
        w = 0
        for step in range(n_i * n_j):
            while w < len(writes) - 1 and writes[w][0] < step:
                w += 1
            tab[step] = writes[w][1]
        plans.append(dict(keep=keep, rb=rb, first_tile=first_tile, row_lo=row_lo, tab=tab))
    return plans


def _in_proj_kernel(tab_ref, x_ref, w_ref, cos_ref, sin_ref, xs_ref, coss_ref, sins_ref,
                    cm0_ref, cm1_ref, cm2_ref, u_ref, kv0_ref, kv1_ref, kv2_ref, qkvs_ref, us_ref,
                    xbf, a_scr, de_scr, xsbf, as_scr, *, n_i, tiles_per_batch, plans):
    del tab_ref
    i = pl.program_id(0)
    j = pl.program_id(1)
    kv_refs = (kv0_ref, kv1_ref, kv2_ref)
    cm_refs = (cm0_ref, cm1_ref, cm2_ref)

    chunks = [slice(c * IN_ROWS, (c + 1) * IN_ROWS) for c in range(IN_TM // IN_ROWS)]

    def chunk_dot(rows, cast=False):
        if cast:
            xbf[rows, :] = x_ref[rows, :].astype(BF16)
        return jnp.dot(xbf[rows, :], w_ref[...], preferred_element_type=F32)

    def rope_rows(val, rows):
        return _rope(val, cos_ref[rows, :], sin_ref[rows, :])

    def store_class_major(gi, c, val):
        d = DILATION_GROUPS[gi][1]
        ref = cm_refs[gi]
        per = IN_ROWS // d
        if d == 1:
            ref[0, 0, 0, chunks[c], :] = val.astype(BF16)
            return
        for h in range(HEADS_PER_GROUP):
            de_scr[c, h] = val[:, h * HEAD_DIM:(h + 1) * HEAD_DIM]
        for r in range(d):
            for h in range(HEADS_PER_GROUP):
                ref[0, 0, r, c * per:(c + 1) * per, h * HEAD_DIM:(h + 1) * HEAD_DIM] = (
                    de_scr[c, h, pl.ds(r, per, stride=d), :].astype(BF16))

    it = i % tiles_per_batch
    for gi in range(N_GROUPS):
        plan = plans[gi]
        in_window = it >= plan["first_tile"]
        window_rows = slice(plan["row_lo"], plan["row_lo"] + plan["rb"])

        @pl.when(j == gi)
        def _(gi=gi):
            for c, rows in enumerate(chunks):
                store_class_major(gi, c, rope_rows(chunk_dot(rows, cast=(gi == 0)), rows))

        @pl.when(j == N_GROUPS + gi)
        def _(window_rows=window_rows, in_window=in_window, gi=gi):
            for c, rows in enumerate(chunks):
                r = rope_rows(chunk_dot(rows), rows)
                store_class_major(gi, c, r)
                a_scr[rows, :] = r

            @pl.when(in_window)
            def _():
                kv_refs[gi][0] = a_scr[window_rows, :]

        @pl.when(j == 2 * N_GROUPS + gi)
        def _(window_rows=window_rows, in_window=in_window, gi=gi):
            for c, rows in enumerate(chunks):
                acc = chunk_dot(rows)
                store_class_major(gi, c, acc)
                a_scr[rows, :] = acc

            @pl.when(in_window)
            def _():
                kv_refs[gi][0] = a_scr[window_rows, :]

    @pl.when(j == 3 * N_GROUPS)
    def _():
        for rows in chunks:
            a_scr[rows, :] = chunk_dot(rows)

    @pl.when(j == 3 * N_GROUPS + 1)
    def _():
        for rows in chunks:
            u_ref[rows, :] = a_scr[rows, :] * jax.nn.sigmoid(chunk_dot(rows))

    @pl.when(i == n_i - 1)
    def _():
        @pl.when(j == 0)
        def _():
            xsbf[...] = xs_ref[...].astype(BF16)

        accs = jnp.dot(xsbf[...], w_ref[...], preferred_element_type=F32)

        def store_heads(val):
            for h in range(HEADS_PER_GROUP):
                qkvs_ref[:, 0, h, :] = val[:, h * HEAD_DIM:(h + 1) * HEAD_DIM]

        @pl.when(j < 2 * N_GROUPS)
        def _():
            store_heads(_rope(accs, coss_ref[...], sins_ref[...]))

        @pl.when((j >= 2 * N_GROUPS) & (j < 3 * N_GROUPS))
        def _():
            store_heads(accs)

        @pl.when(j == 3 * N_GROUPS)
        def _():
            as_scr[...] = accs

        @pl.when(j == 3 * N_GROUPS + 1)
        def _():
            us_ref[...] = as_scr[...] * jax.nn.sigmoid(accs)


def _in_proj(x2, xs2, w_bf, cos_p, sin_p, cos_s, sin_s, batch, seq):
    m = x2.shape[0]
    ms = xs2.shape[0]
    n_i = m // IN_TM
    n_j = w_bf.shape[1] // IN_TN
    n_qkv = 3 * N_GROUPS
    tiles_per_batch = seq // IN_TM
    plans = _kv_window_plan(seq, batch, n_j)
    tab = jnp.asarray(np.stack([p["tab"] for p in plans]).reshape(-1))
    n_steps = n_i * n_j

    def kv_map(gi):
        def f(i, j, tab_ref):
            base = (gi * n_steps + i * n_j + j) * 3
            return tab_ref[base], tab_ref[base + 1], tab_ref[base + 2]
        return f

    def cm_map(gi):
        def f(i, j, tab_ref):
            plane = (j > gi).astype(jnp.int32) + (j > N_GROUPS + gi).astype(jnp.int32)
            return i // tiles_per_batch, plane, 0, i % tiles_per_batch, 0
        return f

    last = n_i - 1
    in_specs = [
        pl.BlockSpec((IN_TM, D_MODEL), lambda i, j, t: (i, 0)),
        pl.BlockSpec((D_MODEL, IN_TN), lambda i, j, t: (0, j)),
        pl.BlockSpec((IN_TM, HEAD_DIM), lambda i, j, t: (i % tiles_per_batch, 0)),
        pl.BlockSpec((IN_TM, HEAD_DIM), lambda i, j, t: (i % tiles_per_batch, 0)),
        pl.BlockSpec((ms, D_MODEL), lambda i, j, t: (0, 0)),
        pl.BlockSpec((ms, HEAD_DIM), lambda i, j, t: (0, 0)),
        pl.BlockSpec((ms, HEAD_DIM), lambda i, j, t: (0, 0)),
    ]
    out_shape, out_specs = [], []
    for gi, (_, d) in enumerate(DILATION_GROUPS):
        out_shape.append(jax.ShapeDtypeStruct((batch, 3, d, seq // d, GROUP_W), BF16))
        out_specs.append(pl.BlockSpec((1, 1, d, IN_TM // d, GROUP_W), cm_map(gi)))
    out_shape.append(jax.ShapeDtypeStruct((m, CONV_CH), F32))
    out_specs.append(pl.BlockSpec((IN_TM, CONV_CH), lambda i, j, t: (i, 0)))
    for gi, p in enumerate(plans):
        out_shape.append(jax.ShapeDtypeStruct((batch, p["keep"], 2 * GROUP_W), F32))
        out_specs.append(pl.BlockSpec((1, p["rb"], GROUP_W), kv_map(gi)))
    out_shape += [jax.ShapeDtypeStruct((ms, n_qkv, HEADS_PER_GROUP, HEAD_DIM), F32),
                  jax.ShapeDtypeStruct((ms, CONV_CH), F32)]
    out_specs += [
        pl.BlockSpec((ms, 1, HEADS_PER_GROUP, HEAD_DIM),
                     lambda i, j, t: (0, jnp.where(i == last, jnp.minimum(j, n_qkv - 1), 0), 0, 0)),
        pl.BlockSpec((ms, CONV_CH), lambda i, j, t: (0, 0)),
    ]
    grid_spec = pltpu.PrefetchScalarGridSpec(
        num_scalar_prefetch=1, grid=(n_i, n_j), in_specs=in_specs, out_specs=out_specs,
        scratch_shapes=[pltpu.VMEM((IN_TM, D_MODEL), BF16), pltpu.VMEM((IN_TM, CONV_CH), F32),
                        pltpu.VMEM((IN_TM // IN_ROWS, HEADS_PER_GROUP, IN_ROWS, HEAD_DIM), F32),
                        pltpu.VMEM((ms, D_MODEL), BF16), pltpu.VMEM((ms, CONV_CH), F32)])
    kern = functools.partial(_in_proj_kernel, n_i=n_i, tiles_per_batch=tiles_per_batch, plans=plans)
    return pl.pallas_call(kern, grid_spec=grid_spec, out_shape=out_shape,
                          compiler_params=_cparams(2), name="in_proj")(
        tab, x2, w_bf, cos_p, sin_p, xs2, cos_s, sin_s)


ATTN_TP = 2048


def _attn_kernel(q_ref, k_ref, v_ref, o_ref, lse_ref, k_scr, v_scr, o_scr, lse_scr, *, d, lc):
    n = pl.program_id(1)
    n_qb = lc // Q_BLOCK

    @pl.when(n == 0)
    def _():
        k_scr[:, 0:Q_BLOCK, :] = jnp.zeros((d, Q_BLOCK, GROUP_W), BF16)
        v_scr[:, 0:Q_BLOCK, :] = jnp.zeros((d, Q_BLOCK, GROUP_W), BF16)

    k_scr[:, Q_BLOCK:, :] = k_ref[0, 0]
    v_scr[:, Q_BLOCK:, :] = v_ref[0, 0]
    qq = lax.broadcasted_iota(jnp.int32, (Q_BLOCK, 2 * Q_BLOCK), 0)
    kk = lax.broadcasted_iota(jnp.int32, (Q_BLOCK, 2 * Q_BLOCK), 1)
    band = (kk >= qq) & (kk <= qq + Q_BLOCK)
    lane = lax.broadcasted_iota(jnp.int32, (Q_BLOCK, HEAD_DIM), 1)

    def unit(u, carry):
        r = u // n_qb
        qb = u % n_qb
        r0 = pl.multiple_of(qb * Q_BLOCK, Q_BLOCK)
        valid = band & ((kk >= Q_BLOCK) | (n > 0) | (qb > 0))
        if d == 1:
            rows = pl.ds(r0, Q_BLOCK)
        else:
            rows = pl.ds(r0 * d + r, Q_BLOCK, stride=d)
        lse_blk = jnp.zeros((Q_BLOCK, HEAD_DIM), F32)
        for h in range(HEADS_PER_GROUP):
            c0 = h * HEAD_DIM
            q = q_ref[0, 0, r, pl.ds(r0, Q_BLOCK), c0:c0 + HEAD_DIM]
            k = k_scr[r, pl.ds(r0, 2 * Q_BLOCK), c0:c0 + HEAD_DIM]
            v = v_scr[r, pl.ds(r0, 2 * Q_BLOCK), c0:c0 + HEAD_DIM]
            s = lax.dot_general(q, k, (((1,), (1,)), ((), ())), preferred_element_type=F32)
            s = jnp.where(valid, s * ATTN_SCALE, NEG_BIG)
            m = jnp.max(s, axis=1, keepdims=True)
            e = jnp.exp(s - m)
            den = jnp.sum(e, axis=1, keepdims=True)
            o_scr[h, rows, :] = jnp.dot(e.astype(BF16), v, preferred_element_type=F32) / den
            lse_blk = jnp.where(lane == h, m + jnp.log(den), lse_blk)
        lse_scr[rows, :] = lse_blk
        return carry

    lax.fori_loop(0, d * n_qb, unit, 0, unroll=4)
    k_scr[:, 0:Q_BLOCK, :] = k_scr[:, lc:lc + Q_BLOCK, :]
    v_scr[:, 0:Q_BLOCK, :] = v_scr[:, lc:lc + Q_BLOCK, :]
    for h in range(HEADS_PER_GROUP):
        o_ref[:, h * HEAD_DIM:(h + 1) * HEAD_DIM] = o_scr[h].astype(BF16)
    lse_ref[...] = lse_scr[...]


def _attn_group(cm, gi, batch, seq):
    _, d = DILATION_GROUPS[gi]
    lc = ATTN_TP // d
    n_t = seq // ATTN_TP

    def plane(p):
        return pl.BlockSpec((1, 1, d, lc, GROUP_W), lambda b, n: (b, p, 0, n, 0))

    return pl.pallas_call(
        functools.partial(_attn_kernel, d=d, lc=lc),
        grid=(batch, n_t),
        in_specs=[plane(0), plane(1), plane(2)],
        out_specs=[pl.BlockSpec((ATTN_TP, GROUP_W), lambda b, n: (b * n_t + n, 0)),
                   pl.BlockSpec((ATTN_TP, HEAD_DIM), lambda b, n: (b * n_t + n, 0))],
        out_shape=[jax.ShapeDtypeStruct((batch * seq, GROUP_W), BF16),
                   jax.ShapeDtypeStruct((batch * seq, HEAD_DIM), F32)],
        scratch_shapes=[pltpu.VMEM((d, lc + Q_BLOCK, GROUP_W), BF16),
                        pltpu.VMEM((d, lc + Q_BLOCK, GROUP_W), BF16),
                        pltpu.VMEM((HEADS_PER_GROUP, ATTN_TP, HEAD_DIM), F32),
                        pltpu.VMEM((ATTN_TP, HEAD_DIM), F32)],
        compiler_params=_cparams(2), name=f"attn_g{gi}")(cm, cm, cm)


CONV_TC = 512
CONV_HALO = 32
CONV_ROWS = 64


def _conv_kernel(u_ref, halo_ref, w_ref, b_ref, g_ref, beta_ref, c_ref, hist):
    t = pl.program_id(1)
    hist[0:CONV_HALO, :] = jnp.where(t > 0, halo_ref[0], 0.0)
    hist[CONV_HALO:CONV_HALO + CONV_TC, :] = u_ref[0]
    hist[CONV_HALO + CONV_TC:, :] = jnp.zeros((SUBLANES, CONV_CH), F32)
    lead = CONV_HALO - (CONV_K - 1)
    for rc in range(CONV_TC // CONV_ROWS):
        r0 = rc * CONV_ROWS
        acc = jnp.broadcast_to(b_ref[...], (CONV_ROWS, CONV_CH))
        for s in range(SUBLANES):
            grp = None
            for c in range(s, lead + CONV_K, SUBLANES):
                if c < lead:
                    continue
                term = hist[r0 + c - s:r0 + c - s + CONV_ROWS + SUBLANES, :] * w_ref[c - lead:c - lead + 1, :]
                grp = term if grp is None else grp + term
            acc = acc + grp[s:s + CONV_ROWS, :]
        y = _layer_norm_rows(acc, g_ref[...], beta_ref[...])
        c_ref[0, r0:r0 + CONV_ROWS, :] = (y * jax.nn.sigmoid(y)).astype(BF16)


def _conv_prompt(u2, conv_w, conv_b, ln_g, ln_b, batch, seq):
    u3 = u2.reshape(batch, seq, CONV_CH)
    per = CONV_TC // CONV_HALO
    vec = pl.BlockSpec((1, CONV_CH), lambda b, t: (0, 0))
    c = pl.pallas_call(
        _conv_kernel,
        grid=(batch, seq // CONV_TC),
        in_specs=[pl.BlockSpec((1, CONV_TC, CONV_CH), lambda b, t: (b, t, 0)),
                  pl.BlockSpec((1, CONV_HALO, CONV_CH), lambda b, t: (b, jnp.maximum(t * per - 1, 0), 0)),
                  pl.BlockSpec((CONV_K, CONV_CH), lambda b, t: (0, 0)), vec, vec, vec],
        out_specs=pl.BlockSpec((1, CONV_TC, CONV_CH), lambda b, t: (b, t, 0)),
        out_shape=jax.ShapeDtypeStruct((batch, seq, CONV_CH), BF16),
        scratch_shapes=[pltpu.VMEM((CONV_HALO + CONV_TC + SUBLANES, CONV_CH), F32)],
        compiler_params=_cparams(2), name="conv_prompt")(
        u3, u3, conv_w, conv_b[None, :], ln_g[None, :], ln_b[None, :])
    return c.reshape(batch * seq, CONV_CH)


KV_SLAB = 2 * HEADS_PER_GROUP


def _shift_cache(cref, out_ref, d, t_new):
    slab = d * KV_SLAB
    shift = t_new * KV_SLAB
    n_rows = cref.shape[1]
    if shift % slab == 0:
        s = shift // slab
        out_ref[0, 0:(n_rows - s) * slab] = cref[0, s:n_rows].reshape((n_rows - s) * slab, HEAD_DIM)
    else:
        assert shift < slab and shift % 8 == 0
        out_ref[0, 0:slab - shift] = cref[0, 0, shift:slab]

        def body(ci, carry):
            out_ref[0, pl.ds(pl.multiple_of(ci * slab - shift, 8), slab)] = cref[0, ci]
            return carry

        lax.fori_loop(1, n_rows, body, 0)


def _sample_kernel(qkv_ref, c0_ref, c1_ref, c2_ref, st_ref, u_ref, w_ref, b_ref, g_ref, beta_ref,
                   mix_ref, nconv_ref, nk0_ref, nk1_ref, nk2_ref, hist, *, t_new):
    cache_refs = (c0_ref, c1_ref, c2_ref)
    nk_refs = (nk0_ref, nk1_ref, nk2_ref)
    hp = HEADS_PER_GROUP
    outs = [[None] * N_GROUPS for _ in range(t_new)]
    lses = [[None] * N_GROUPS for _ in range(t_new)]
    for gi, (_, d) in enumerate(DILATION_GROUPS):
        cref = cache_refs[gi]
        n_rows = cref.shape[1]
        row = lax.broadcasted_iota(jnp.int32, (n_rows, hp, 1), 0)
        _shift_cache(cref, nk_refs[gi], d, t_new)
        keep = nk_refs[gi].shape[1] - t_new * KV_SLAB
        for t in range(t_new):
            nk_refs[gi][0, keep + t * KV_SLAB:keep + t * KV_SLAB + hp] = qkv_ref[0, t, N_GROUPS + gi]
            nk_refs[gi][0, keep + t * KV_SLAB + hp:keep + (t + 1) * KV_SLAB] = qkv_ref[0, t, 2 * N_GROUPS + gi]
        for t in range(t_new):
            s0 = (t % d) * KV_SLAB
            first_row = t // d
            q = qkv_ref[0, t, gi][None]
            kc = cref[0, :, s0:s0 + hp, :]
            vc = cref[0, :, s0 + hp:s0 + 2 * hp, :]
            s_c = jnp.sum(kc * q, axis=2, keepdims=True) * ATTN_SCALE
            if first_row > 0:
                s_c = jnp.where(row >= first_row, s_c, NEG_BIG)
            newest = [t - d * jj for jj in range(t // d + 1)]
            s_n, v_n = [], []
            for tn in newest:
                kn = qkv_ref[0, tn, N_GROUPS + gi][None]
                v_n.append(qkv_ref[0, tn, 2 * N_GROUPS + gi][None])
                s_n.append(jnp.sum(kn * q, axis=2, keepdims=True) * ATTN_SCALE)
            m = jnp.max(s_c, axis=0, keepdims=True)
            for sn in s_n:
                m = jnp.maximum(m, sn)
            e_c = jnp.exp(s_c - m)
            den = jnp.sum(e_c, axis=0, keepdims=True)
            acc = jnp.sum(e_c * vc, axis=0, keepdims=True)
            for sn, v1 in zip(s_n, v_n):
                e_n = jnp.exp(sn - m)
                den = den + e_n
                acc = acc + e_n * v1
            outs[t][gi] = acc / den
            lses[t][gi] = m + jnp.log(den)
    for t in range(t_new):
        ls = lses[t]
        m = functools.reduce(jnp.maximum, ls)
        es = [jnp.exp(l - m) for l in ls]
        tot = functools.reduce(lambda a, b: a + b, es)
        for gi in range(N_GROUPS):
            slab = (outs[t][gi] * (es[gi] / tot))[0]
            for h in range(hp):
                c0 = gi * GROUP_W + h * HEAD_DIM
                mix_ref[0, t:t + 1, c0:c0 + HEAD_DIM] = slab[h:h + 1, :]
    n_state = CONV_K - 1
    hist[0:n_state, :] = st_ref[0]
    hist[n_state:n_state + t_new, :] = u_ref[0]
    acc = jnp.broadcast_to(b_ref[...], (t_new, CONV_CH))
    for k in range(CONV_K):
        acc = acc + hist[k:k + t_new, :] * w_ref[k:k + 1, :]
    y = _layer_norm_rows(acc, g_ref[...], beta_ref[...])
    mix_ref[0, :, ATTN_WIDTH:] = y * jax.nn.sigmoid(y)
    nconv_ref[0] = hist[t_new:t_new + n_state, :]


def _sample_mixers(qkvs, us, caches, state, conv_w, conv_b, ln_g, ln_b, dec_batch, t_new):
    n_qkv = qkvs.shape[1]
    hp = HEADS_PER_GROUP
    qkv5 = qkvs.reshape(dec_batch, t_new, n_qkv, hp, HEAD_DIM)
    u3 = us.reshape(dec_batch, t_new, CONV_CH)
    cache_in, cache_specs, nk_shapes, nk_specs = [], [], [], []
    for (window, d), cache in zip(DILATION_GROUPS, caches):
        buf = cache.shape[1]
        assert buf == window and buf % d == 0
        n_rows = buf // d
        cache_in.append(cache.reshape(dec_batch, n_rows, d * KV_SLAB, HEAD_DIM))
        cache_specs.append(pl.BlockSpec((1, n_rows, d * KV_SLAB, HEAD_DIM), lambda b: (b, 0, 0, 0)))
        nk_shapes.append(jax.ShapeDtypeStruct((dec_batch, buf * KV_SLAB, HEAD_DIM), F32))
        nk_specs.append(pl.BlockSpec((1, buf * KV_SLAB, HEAD_DIM), lambda b: (b, 0, 0)))
    vec = pl.BlockSpec((1, CONV_CH), lambda b: (0, 0))
    n_state = CONV_K - 1
    mix, nconv, nk0, nk1, nk2 = pl.pallas_call(
        functools.partial(_sample_kernel, t_new=t_new),
        grid=(dec_batch,),
        in_specs=[pl.BlockSpec((1, t_new, n_qkv, hp, HEAD_DIM), lambda b: (b, 0, 0, 0, 0))] + cache_specs + [
            pl.BlockSpec((1, n_state, CONV_CH), lambda b: (b, 0, 0)),
            pl.BlockSpec((1, t_new, CONV_CH), lambda b: (b, 0, 0)),
            pl.BlockSpec((CONV_K, CONV_CH), lambda b: (0, 0)), vec, vec, vec],
        out_specs=[pl.BlockSpec((1, t_new, D_MODEL), lambda b: (b, 0, 0)),
                   pl.BlockSpec((1, n_state, CONV_CH), lambda b: (b, 0, 0))] + nk_specs,
        out_shape=[jax.ShapeDtypeStruct((dec_batch, t_new, D_MODEL), F32),
                   jax.ShapeDtypeStruct((dec_batch, n_state, CONV_CH), F32)] + nk_shapes,
        scratch_shapes=[pltpu.VMEM((n_state + t_new + 6, CONV_CH), F32)],
        compiler_params=_cparams(1), name="sample_mixers")(
        qkv5, *cache_in, state, u3, conv_w, conv_b[None, :], ln_g[None, :], ln_b[None, :])
    new_caches = [nk.reshape(c.shape) for nk, c in zip((nk0, nk1, nk2), caches)]
    return mix.reshape(dec_batch * t_new, D_MODEL), nconv, new_caches


OUT_TM = 512
OUT_ROWS = 128


def _out_proj_kernel(o0_ref, o1_ref, o2_ref, l0_ref, l1_ref, l2_ref, c_ref, x_ref, w_ref, g_ref, b_ref,
                     mixs_ref, xs_ref, x1_ref, x1bf_ref, x1s_ref, mix_scr, y_scr, *, n_i, alpha):
    i = pl.program_id(0)
    o_refs = (o0_ref, o1_ref, o2_ref)
    chunks = [slice(r0, r0 + OUT_ROWS) for r0 in range(0, OUT_TM, OUT_ROWS)]

    def project(rows):
        ls = [l0_ref[rows, :], l1_ref[rows, :], l2_ref[rows, :]]
        m = jnp.maximum(jnp.maximum(ls[0], ls[1]), ls[2])
        es = [jnp.exp(l - m) for l in ls]
        inv = 1.0 / (es[0] + es[1] + es[2])
        for gi in range(N_GROUPS):
            a = es[gi] * inv
            for h in range(HEADS_PER_GROUP):
                c0 = h * HEAD_DIM
                og = o_refs[gi][rows, c0:c0 + HEAD_DIM].astype(F32)
                mix_scr[rows, gi * GROUP_W + c0:gi * GROUP_W + c0 + HEAD_DIM] = (og * a[:, h:h + 1]).astype(BF16)
        mix_scr[rows, ATTN_WIDTH:] = c_ref[rows, :]
        y_scr[rows, :] = (jnp.dot(mix_scr[rows, :], w_ref[...], preferred_element_type=F32)
                          + alpha * x_ref[rows, :])

    def normalize(rows):
        for r0 in range(rows.start, rows.stop, LN_STRIP):
            strip = slice(r0, r0 + LN_STRIP)
            x1 = _layer_norm_rows(y_scr[strip, :], g_ref[...], b_ref[...])
            x1_ref[strip, :] = x1
            x1bf_ref[strip, :] = x1.astype(BF16)

    project(chunks[0])
    for c in range(1, len(chunks)):
        project(chunks[c])
        normalize(chunks[c - 1])
    normalize(chunks[-1])

    @pl.when(i == n_i - 1)
    def _():
        ys = jnp.dot(mixs_ref[...].astype(BF16), w_ref[...], preferred_element_type=F32) + alpha * xs_ref[...]
        x1s_ref[...] = _layer_norm_rows(ys, g_ref[...], b_ref[...])


def _out_proj(os, lses, c, x2, w_bf, ln_g, ln_b, mixs, xs2, alpha):
    m = x2.shape[0]
    ms = xs2.shape[0]
    n_i = m // OUT_TM
    row = lambda w: pl.BlockSpec((OUT_TM, w), lambda i: (i, 0))
    whole = lambda a: pl.BlockSpec(a.shape, lambda i: (0,) * a.ndim)
    g2, b2 = ln_g[None, :], ln_b[None, :]
    return pl.pallas_call(
        functools.partial(_out_proj_kernel, n_i=n_i, alpha=alpha),
        grid=(n_i,),
        in_specs=[row(GROUP_W)] * 3 + [row(HEAD_DIM)] * 3 + [row(CONV_CH), row(D_MODEL),
                  whole(w_bf), whole(g2), whole(b2), whole(mixs), whole(xs2)],
        out_specs=[row(D_MODEL), row(D_MODEL), pl.BlockSpec((ms, D_MODEL), lambda i: (0, 0))],
        out_shape=[jax.ShapeDtypeStruct((m, D_MODEL), F32), jax.ShapeDtypeStruct((m, D_MODEL), BF16),
                   jax.ShapeDtypeStruct((ms, D_MODEL), F32)],
        scratch_shapes=[pltpu.VMEM((OUT_TM, D_MODEL), BF16), pltpu.VMEM((OUT_TM, D_MODEL), F32)],
        compiler_params=_cparams(1), name="out_proj")(
        *os, *lses, c, x2, w_bf, g2, b2, mixs, xs2)


FFN_TM = 1024
FFN_TF = 512
FFN_ROWS = 512
FFN_RES_W = 256


def _ffn_kernel(xbf_ref, xres_ref, wg_ref, wu_ref, wd_ref, g_ref, b_ref, xs_ref,
                y_ref, ys_ref, xsbf, *, n_i, n_f, alpha):
    i = pl.program_id(0)
    f = pl.program_id(1)

    def swiglu_down(xb):
        gate = jnp.dot(xb, wg_ref[...], preferred_element_type=F32)
        up = jnp.dot(xb, wu_ref[...], preferred_element_type=F32)
        act = (gate * jax.nn.sigmoid(gate) * up).astype(BF16)
        return jnp.dot(act, wd_ref[...], preferred_element_type=F32)

    @pl.when(f == 0)
    def _():
        y_ref[...] = jnp.zeros_like(y_ref)

    for r0 in range(0, FFN_TM, FFN_ROWS):
        y_ref[r0:r0 + FFN_ROWS, :] += swiglu_down(xbf_ref[r0:r0 + FFN_ROWS, :])

    for c in range(D_MODEL // FFN_RES_W):
        @pl.when(f == c)
        def _(c=c):
            y_ref[:, c * FFN_RES_W:(c + 1) * FFN_RES_W] += alpha * xres_ref[...]

    @pl.when(f == n_f - 1)
    def _():
        y_ref[...] = _layer_norm_rows(y_ref[...], g_ref[...], b_ref[...])

    @pl.when(i == n_i - 1)
    def _():
        @pl.when(f == 0)
        def _():
            xsbf[...] = xs_ref[...].astype(BF16)

        downs = swiglu_down(xsbf[...])

        @pl.when(f == 0)
        def _():
            ys_ref[...] = downs + alpha * xs_ref[...]

        @pl.when(f > 0)
        def _():
            ys_ref[...] += downs

        @pl.when(f == n_f - 1)
        def _():
            ys_ref[...] = _layer_norm_rows(ys_ref[...], g_ref[...], b_ref[...])


def _ffn(x1, x1bf, x1s, wg_bf, wu_bf, wd_bf, ln_g, ln_b, alpha):
    m = x1.shape[0]
    ms = x1s.shape[0]
    hidden = wg_bf.shape[1]
    n_i = m // FFN_TM
    n_f = hidden // FFN_TF
    n_res = D_MODEL // FFN_RES_W
    assert n_f >= n_res
    g2, b2 = ln_g[None, :], ln_b[None, :]
    return pl.pallas_call(
        functools.partial(_ffn_kernel, n_i=n_i, n_f=n_f, alpha=alpha),
        grid=(n_i, n_f),
        in_specs=[pl.BlockSpec((FFN_TM, D_MODEL), lambda i, f: (i, 0)),
                  pl.BlockSpec((FFN_TM, FFN_RES_W), lambda i, f: (i, jnp.minimum(f, n_res - 1))),
                  pl.BlockSpec((D_MODEL, FFN_TF), lambda i, f: (0, f)),
                  pl.BlockSpec((D_MODEL, FFN_TF), lambda i, f: (0, f)),
                  pl.BlockSpec((FFN_TF, D_MODEL), lambda i, f: (f, 0)),
                  pl.BlockSpec((1, D_MODEL), lambda i, f: (0, 0)),
                  pl.BlockSpec((1, D_MODEL), lambda i, f: (0, 0)),
                  pl.BlockSpec((ms, D_MODEL), lambda i, f: (0, 0))],
        out_specs=[pl.BlockSpec((FFN_TM, D_MODEL), lambda i, f: (i, 0)),
                   pl.BlockSpec((ms, D_MODEL), lambda i, f: (0, 0))],
        out_shape=[jax.ShapeDtypeStruct((m, D_MODEL), F32), jax.ShapeDtypeStruct((ms, D_MODEL), F32)],
        scratch_shapes=[pltpu.VMEM((ms, D_MODEL), BF16)],
        compiler_params=_cparams(2), name="ffn")(
        x1bf, x1, wg_bf, wu_bf, wd_bf, g2, b2, x1s)


def kernel(x_prompt, x_sample, cache_kv_w128, cache_kv_w512, cache_kv_w2048, state_conv, w_in, w_out,
           conv_w, conv_b, conv_ln_g, conv_ln_b, ln1_g, ln1_b, w_gate, w_up, w_down, ln2_g, ln2_b):
    depth = w_in.shape[0]
    batch, seq, _ = x_prompt.shape
    dec_batch, t_new, _ = x_sample.shape
    caches = (cache_kv_w128, cache_kv_w512, cache_kv_w2048)
    alpha = (2.0 * depth) ** 0.25

    cos_p, sin_p = _rope_tables(jnp.arange(seq, dtype=jnp.int32))
    pos_s = PAST_LEN + jnp.arange(t_new, dtype=jnp.int32)
    cos_s, sin_s = _rope_tables(jnp.tile(pos_s, dec_batch))

    xp = x_prompt.reshape(batch * seq, D_MODEL)
    xs = x_sample.reshape(dec_batch * t_new, D_MODEL)
    kvp = [[] for _ in range(N_GROUPS)]
    kvs = [[] for _ in range(N_GROUPS)]
    convp, convs = [], []
    for l in range(depth):
        cm0, cm1, cm2, u, kv0, kv1, kv2, qkvs, us = _in_proj(
            xp, xs, w_in[l].astype(BF16), cos_p, sin_p, cos_s, sin_s, batch, seq)
        os, lses = zip(*[_attn_group(cm, gi, batch, seq) for gi, cm in enumerate((cm0, cm1, cm2))])
        c = _conv_prompt(u, conv_w[l], conv_b[l], conv_ln_g[l], conv_ln_b[l], batch, seq)
        layer_caches = [cc[l] for cc in caches]
        mixs, nconv_s, new_caches = _sample_mixers(qkvs, us, layer_caches, state_conv[l], conv_w[l], conv_b[l],
                                                   conv_ln_g[l], conv_ln_b[l], dec_batch, t_new)
        x1, x1bf, x1s = _out_proj(os, lses, c, xp, w_out[l].astype(BF16), ln1_g[l], ln1_b[l], mixs, xs, alpha)
        xp, xs = _ffn(x1, x1bf, x1s, w_gate[l].astype(BF16), w_up[l].astype(BF16), w_down[l].astype(BF16),
                      ln2_g[l], ln2_b[l], alpha)
        for gi, (kv, nk) in enumerate(zip((kv0, kv1, kv2), new_caches)):
            kvp[gi].append(kv.reshape(batch, kv.shape[1], 2, HEADS_PER_GROUP, HEAD_DIM))
            kvs[gi].append(nk)
        convp.append(u.reshape(batch, seq, CONV_CH)[:, seq - (CONV_K - 1):])
        convs.append(nconv_s)

    y_prompt = xp.reshape(batch, seq, D_MODEL)
    y_sample = xs.reshape(dec_batch, t_new, D_MODEL)
    return (y_prompt, y_sample, jnp.stack(kvp[0]), jnp.stack(kvp[1]), jnp.stack(kvp[2]), jnp.stack(convp),
            jnp.stack(kvs[0]), jnp.stack(kvs[1]), jnp.stack(kvs[2]), jnp.stack(convs))
```

```python
import functools

import numpy as np
import jax
import jax.numpy as jnp
from jax import lax
from jax.experimental import pallas as pl
from jax.experimental.pallas import tpu as pltpu

D_MODEL = 2048
HEAD_DIM = 128
CONV_CH = D_MODEL // 4
ATTN_WIDTH = D_MODEL - CONV_CH
DILATION_GROUPS = ((128, 1), (512, 4), (2048, 16))
N_GROUPS = len(DILATION_GROUPS)
HEADS_PER_GROUP = ATTN_WIDTH // HEAD_DIM // N_GROUPS
GROUP_W = HEADS_PER_GROUP * HEAD_DIM
CONV_K = 31
ROPE_THETA = 10000.0
LN_EPS = 1e-5
Q_BLOCK = 128
ATTN_SCALE = HEAD_DIM ** -0.5
NEG_BIG = -1e30
PAST_LEN = 16384

F32 = jnp.float32
BF16 = jnp.bfloat16

VMEM_LIMIT = 56 * 1024 * 1024
LN_STRIP = 16
SUBLANES = 8


def _cparams(n_axes):
    return pltpu.CompilerParams(dimension_semantics=("arbitrary",) * n_axes,
                                vmem_limit_bytes=VMEM_LIMIT)


def _layer_norm_rows(y, g, b):
    mu = jnp.mean(y, axis=-1, keepdims=True)
    yc = y - mu
    var = jnp.mean(yc * yc, axis=-1, keepdims=True)
    return yc * lax.rsqrt(var + LN_EPS) * g + b


def _rope_tables(pos):
    half = HEAD_DIM // 2
    inv = ROPE_THETA ** (-jnp.arange(half, dtype=F32) / half)
    ang = pos.astype(F32)[:, None] * inv[None, :]
    cos, sin = jnp.cos(ang), jnp.sin(ang)
    return jnp.concatenate([cos, cos], axis=1), jnp.concatenate([-sin, sin], axis=1)


def _rope(h, cos, sin):
    parts = []
    for hh in range(HEADS_PER_GROUP):
        hs = h[:, hh * HEAD_DIM:(hh + 1) * HEAD_DIM]
        parts.append(hs * cos + pltpu.roll(hs, HEAD_DIM // 2, axis=1) * sin)
    return jnp.concatenate(parts, axis=1)


IN_TM = 1024
IN_TN = GROUP_W
IN_ROWS = 256


def _kv_window_plan(seq, batch, n_j):
    tiles_per_batch = seq // IN_TM
    n_i = batch * tiles_per_batch
    plans = []
    for gi, (window, _) in enumerate(DILATION_GROUPS):
        keep = min(window, seq)
        rb = min(keep, IN_TM)
        first_tile = (seq - keep) // IN_TM
        row_lo = (seq - keep) - first_tile * IN_TM
        writes = []
        for i in range(n_i):
            b, it = divmod(i, tiles_per_batch)
            if it >= first_tile:
                for c, j in enumerate((N_GROUPS + gi, 2 * N_GROUPS + gi)):
                    writes.append((i * n_j + j, (b, it - first_tile, c)))
        writes.sort()
        tab = np.zeros((n_i * n_j, 3), np.int32)
        w = 0
        for step in range(n_i * n_j):
            while w < len(writes) - 1 and writes[w][0] < step:
                w += 1
            tab[step] = writes[w][1]
        plans.append(dict(keep=keep, rb=rb, first_tile=first_tile, row_lo=row_lo, tab=tab))
    return plans


def _in_proj_kernel(tab_ref, x_ref, w_ref, cos_ref, sin_ref, xs_ref, coss_ref, sins_ref,
                    cm0_ref, cm1_ref, cm2_ref, u_ref, kv0_ref, kv1_ref, kv2_ref, qkvs_ref, us_ref,
                    xbf, a_scr, de_scr, xsbf, as_scr, *, n_i, tiles_per_batch, plans):
    del tab_ref
    i = pl.program_id(0)
    j = pl.program_id(1)
    kv_refs = (kv0_ref, kv1_ref, kv2_ref)
    cm_refs = (cm0_ref, cm1_ref, cm2_ref)

    chunks = [slice(c * IN_ROWS, (c + 1) * IN_ROWS) for c in range(IN_TM // IN_ROWS)]

    def chunk_dot(rows, cast=False):
        if cast:
            xbf[rows, :] = x_ref[rows, :].astype(BF16)
        return jnp.dot(xbf[rows, :], w_ref[...], preferred_element_type=F32)

    def rope_rows(val, rows):
        return _rope(val, cos_ref[rows, :], sin_ref[rows, :])

    def store_class_major(gi, c, val):
        d = DILATION_GROUPS[gi][1]
        ref = cm_refs[gi]
        per = IN_ROWS // d
        if d == 1:
            ref[0, 0, 0, chunks[c], :] = val.astype(BF16)
            return
        for h in range(HEADS_PER_GROUP):
            de_scr[c, h] = val[:, h * HEAD_DIM:(h + 1) * HEAD_DIM]
        for r in range(d):
            for h in range(HEADS_PER_GROUP):
                ref[0, 0, r, c * per:(c + 1) * per, h * HEAD_DIM:(h + 1) * HEAD_DIM] = (
                    de_scr[c, h, pl.ds(r, per, stride=d), :].astype(BF16))

    it = i % tiles_per_batch
    for gi in range(N_GROUPS):
        plan = plans[gi]
        in_window = it >= plan["first_tile"]
        window_rows = slice(plan["row_lo"], plan["row_lo"] + plan["rb"])

        @pl.when(j == gi)
        def _(gi=gi):
            for c, rows in enumerate(chunks):
                store_class_major(gi, c, rope_rows(chunk_dot(rows, cast=(gi == 0)), rows))

        @pl.when(j == N_GROUPS + gi)
        def _(window_rows=window_rows, in_window=in_window, gi=gi):
            for c, rows in enumerate(chunks):
                r = rope_rows(chunk_dot(rows), rows)
                store_class_major(gi, c, r)
                a_scr[rows, :] = r

            @pl.when(in_window)
            def _():
                kv_refs[gi][0] = a_scr[window_rows, :]

        @pl.when(j == 2 * N_GROUPS + gi)
        def _(window_rows=window_rows, in_window=in_window, gi=gi):
            for c, rows in enumerate(chunks):
                acc = chunk_dot(rows)
                store_class_major(gi, c, acc)
                a_scr[rows, :] = acc

            @pl.when(in_window)
            def _():
                kv_refs[gi][0] = a_scr[window_rows, :]

    @pl.when(j == 3 * N_GROUPS)
    def _():
        for rows in chunks:
            a_scr[rows, :] = chunk_dot(rows)

    @pl.when(j == 3 * N_GROUPS + 1)
    def _():
        for rows in chunks:
            u_ref[rows, :] = a_scr[rows, :] * jax.nn.sigmoid(chunk_dot(rows))

    @pl.when(i == n_i - 1)
    def _():
        @pl.when(j == 0)
        def _():
            xsbf[...] = xs_ref[...].astype(BF16)

        accs = jnp.dot(xsbf[...], w_ref[...], preferred_element_type=F32)

        def store_heads(val):
            for h in range(HEADS_PER_GROUP):
                qkvs_ref[:, 0, h, :] = val[:, h * HEAD_DIM:(h + 1) * HEAD_DIM]

        @pl.when(j < 2 * N_GROUPS)
        def _():
            store_heads(_rope(accs, coss_ref[...], sins_ref[...]))

        @pl.when((j >= 2 * N_GROUPS) & (j < 3 * N_GROUPS))
        def _():
            store_heads(accs)

        @pl.when(j == 3 * N_GROUPS)
        def _():
            as_scr[...] = accs

        @pl.when(j == 3 * N_GROUPS + 1)
        def _():
            us_ref[...] = as_scr[...] * jax.nn.sigmoid(accs)


def _in_proj(x2, xs2, w_bf, cos_p, sin_p, cos_s, sin_s, batch, seq):
    m = x2.shape[0]
    ms = xs2.shape[0]
    n_i = m // IN_TM
    n_j = w_bf.shape[1] // IN_TN
    n_qkv = 3 * N_GROUPS
    tiles_per_batch = seq // IN_TM
    plans = _kv_window_plan(seq, batch, n_j)
    tab = jnp.asarray(np.stack([p["tab"] for p in plans]).reshape(-1))
    n_steps = n_i * n_j

    def kv_map(gi):
        def f(i, j, tab_ref):
            base = (gi * n_steps + i * n_j + j) * 3
            return tab_ref[base], tab_ref[base + 1], tab_ref[base + 2]
        return f

    def cm_map(gi):
        def f(i, j, tab_ref):
            plane = (j > gi).astype(jnp.int32) + (j > N_GROUPS + gi).astype(jnp.int32)
            return i // tiles_per_batch, plane, 0, i % tiles_per_batch, 0
        return f

    last = n_i - 1
    in_specs = [
        pl.BlockSpec((IN_TM, D_MODEL), lambda i, j, t: (i, 0)),
        pl.BlockSpec((D_MODEL, IN_TN), lambda i, j, t: (0, j)),
        pl.BlockSpec((IN_TM, HEAD_DIM), lambda i, j, t: (i % tiles_per_batch, 0)),
        pl.BlockSpec((IN_TM, HEAD_DIM), lambda i, j, t: (i % tiles_per_batch, 0)),
        pl.BlockSpec((ms, D_MODEL), lambda i, j, t: (0, 0)),
        pl.BlockSpec((ms, HEAD_DIM), lambda i, j, t: (0, 0)),
        pl.BlockSpec((ms, HEAD_DIM), lambda i, j, t: (0, 0)),
    ]
    out_shape, out_specs = [], []
    for gi, (_, d) in enumerate(DILATION_GROUPS):
        out_shape.append(jax.ShapeDtypeStruct((batch, 3, d, seq // d, GROUP_W), BF16))
        out_specs.append(pl.BlockSpec((1, 1, d, IN_TM // d, GROUP_W), cm_map(gi)))
    out_shape.append(jax.ShapeDtypeStruct((m, CONV_CH), F32))
    out_specs.append(pl.BlockSpec((IN_TM, CONV_CH), lambda i, j, t: (i, 0)))
    for gi, p in enumerate(plans):
        out_shape.append(jax.ShapeDtypeStruct((batch, p["keep"], 2 * GROUP_W), F32))
        out_specs.append(pl.BlockSpec((1, p["rb"], GROUP_W), kv_map(gi)))
    out_shape += [jax.ShapeDtypeStruct((ms, n_qkv, HEADS_PER_GROUP, HEAD_DIM), F32),
                  jax.ShapeDtypeStruct((ms, CONV_CH), F32)]
    out_specs += [
        pl.BlockSpec((ms, 1, HEADS_PER_GROUP, HEAD_DIM),
                     lambda i, j, t: (0, jnp.where(i == last, jnp.minimum(j, n_qkv - 1), 0), 0, 0)),
        pl.BlockSpec((ms, CONV_CH), lambda i, j, t: (0, 0)),
    ]
    grid_spec = pltpu.PrefetchScalarGridSpec(
        num_scalar_prefetch=1, grid=(n_i, n_j), in_specs=in_specs, out_specs=out_specs,
        scratch_shapes=[pltpu.VMEM((IN_TM, D_MODEL), BF16), pltpu.VMEM((IN_TM, CONV_CH), F32),
                        pltpu.VMEM((IN_TM // IN_ROWS, HEADS_PER_GROUP, IN_ROWS, HEAD_DIM), F32),
                        pltpu.VMEM((ms, D_MODEL), BF16), pltpu.VMEM((ms, CONV_CH), F32)])
    kern = functools.partial(_in_proj_kernel, n_i=n_i, tiles_per_batch=tiles_per_batch, plans=plans)
    return pl.pallas_call(kern, grid_spec=grid_spec, out_shape=out_shape,
                          compiler_params=_cparams(2), name="in_proj")(
        tab, x2, w_bf, cos_p, sin_p, xs2, cos_s, sin_s)


ATTN_TP = 2048


def _attn_kernel(q_ref, k_ref, v_ref, o_ref, lse_ref, k_scr, v_scr, o_scr, lse_scr, *, d, lc):
    n = pl.program_id(1)
    n_qb = lc // Q_BLOCK

    @pl.when(n == 0)
    def _():
        k_scr[:, 0:Q_BLOCK, :] = jnp.zeros((d, Q_BLOCK, GROUP_W), BF16)
        v_scr[:, 0:Q_BLOCK, :] = jnp.zeros((d, Q_BLOCK, GROUP_W), BF16)

    k_scr[:, Q_BLOCK:, :] = k_ref[0, 0]
    v_scr[:, Q_BLOCK:, :] = v_ref[0, 0]
    qq = lax.broadcasted_iota(jnp.int32, (Q_BLOCK, 2 * Q_BLOCK), 0)
    kk = lax.broadcasted_iota(jnp.int32, (Q_BLOCK, 2 * Q_BLOCK), 1)
    band = (kk >= qq) & (kk <= qq + Q_BLOCK)
    lane = lax.broadcasted_iota(jnp.int32, (Q_BLOCK, HEAD_DIM), 1)

    def unit(u, carry):
        r = u // n_qb
        qb = u % n_qb
        r0 = pl.multiple_of(qb * Q_BLOCK, Q_BLOCK)
        valid = band & ((kk >= Q_BLOCK) | (n > 0) | (qb > 0))
        if d == 1:
            rows = pl.ds(r0, Q_BLOCK)
        else:
            rows = pl.ds(r0 * d + r, Q_BLOCK, stride=d)
        lse_blk = jnp.zeros((Q_BLOCK, HEAD_DIM), F32)
        for h in range(HEADS_PER_GROUP):
            c0 = h * HEAD_DIM
            q = q_ref[0, 0, r, pl.ds(r0, Q_BLOCK), c0:c0 + HEAD_DIM]
            k = k_scr[r, pl.ds(r0, 2 * Q_BLOCK), c0:c0 + HEAD_DIM]
            v = v_scr[r, pl.ds(r0, 2 * Q_BLOCK), c0:c0 + HEAD_DIM]
            s = lax.dot_general(q, k, (((1,), (1,)), ((), ())), preferred_element_type=F32)
            s = jnp.where(valid, s * ATTN_SCALE, NEG_BIG)
            m = jnp.max(s, axis=1, keepdims=True)
            e = jnp.exp(s - m)
            den = jnp.sum(e, axis=1, keepdims=True)
            o_scr[h, rows, :] = jnp.dot(e.astype(BF16), v, preferred_element_type=F32) / den
            lse_blk = jnp.where(lane == h, m + jnp.log(den), lse_blk)
        lse_scr[rows, :] = lse_blk
        return carry

    lax.fori_loop(0, d * n_qb, unit, 0, unroll=4)
    k_scr[:, 0:Q_BLOCK, :] = k_scr[:, lc:lc + Q_BLOCK, :]
    v_scr[:, 0:Q_BLOCK, :] = v_scr[:, lc:lc + Q_BLOCK, :]
    for h in range(HEADS_PER_GROUP):
        o_ref[:, h * HEAD_DIM:(h + 1) * HEAD_DIM] = o_scr[h].astype(BF16)
    lse_ref[...] = lse_scr[...]


def _attn_group(cm, gi, batch, seq):
    _, d = DILATION_GROUPS[gi]
    lc = ATTN_TP // d
    n_t = seq // ATTN_TP

    def plane(p):
        return pl.BlockSpec((1, 1, d, lc, GROUP_W), lambda b, n: (b, p, 0, n, 0))

    return pl.pallas_call(
        functools.partial(_attn_kernel, d=d, lc=lc),
        grid=(batch, n_t),
        in_specs=[plane(0), plane(1), plane(2)],
        out_specs=[pl.BlockSpec((ATTN_TP, GROUP_W), lambda b, n: (b * n_t + n, 0)),
                   pl.BlockSpec((ATTN_TP, HEAD_DIM), lambda b, n: (b * n_t + n, 0))],
        out_shape=[jax.ShapeDtypeStruct((batch * seq, GROUP_W), BF16),
                   jax.ShapeDtypeStruct((batch * seq, HEAD_DIM), F32)],
        scratch_shapes=[pltpu.VMEM((d, lc + Q_BLOCK, GROUP_W), BF16),
                        pltpu.VMEM((d, lc + Q_BLOCK, GROUP_W), BF16),
                        pltpu.VMEM((HEADS_PER_GROUP, ATTN_TP, HEAD_DIM), F32),
                        pltpu.VMEM((ATTN_TP, HEAD_DIM), F32)],
        compiler_params=_cparams(2), name=f"attn_g{gi}")(cm, cm, cm)


CONV_TC = 256
CONV_HALO = 32
CONV_ROWS = 64
CAST_ROWS_OUT = 64
CAST_CHUNK_FFN = 256


def _conv_kernel(u_ref, halo_ref, w_ref, b_ref, g_ref, beta_ref, *rest, n_t, stream_chunks):
    n_s = len(stream_chunks)
    src_refs, c_ref, dst_refs, hist = rest[:n_s], rest[n_s], rest[n_s + 1:2 * n_s + 1], rest[2 * n_s + 1]
    t = pl.program_id(1)
    step = pl.program_id(0) * n_t + t

    for src, dst, n_chunks in zip(src_refs, dst_refs, stream_chunks):
        @pl.when(step < n_chunks)
        def _(src=src, dst=dst):
            dst[...] = src[...].astype(BF16)

    hist[0:CONV_HALO, :] = jnp.where(t > 0, halo_ref[0], 0.0)
    hist[CONV_HALO:CONV_HALO + CONV_TC, :] = u_ref[0]
    hist[CONV_HALO + CONV_TC:, :] = jnp.zeros((SUBLANES, CONV_CH), F32)
    lead = CONV_HALO - (CONV_K - 1)
    for rc in range(CONV_TC // CONV_ROWS):
        r0 = rc * CONV_ROWS
        acc = jnp.broadcast_to(b_ref[...], (CONV_ROWS, CONV_CH))
        for s in range(SUBLANES):
            grp = None
            for c in range(s, lead + CONV_K, SUBLANES):
                if c < lead:
                    continue
                term = hist[r0 + c - s:r0 + c - s + CONV_ROWS + SUBLANES, :] * w_ref[c - lead:c - lead + 1, :]
                grp = term if grp is None else grp + term
            acc = acc + grp[s:s + CONV_ROWS, :]
        y = _layer_norm_rows(acc, g_ref[...], beta_ref[...])
        c_ref[0, r0:r0 + CONV_ROWS, :] = (y * jax.nn.sigmoid(y)).astype(BF16)


def _conv_prompt(u2, conv_w, conv_b, ln_g, ln_b, batch, seq, cast_streams):
    u3 = u2.reshape(batch, seq, CONV_CH)
    per = CONV_TC // CONV_HALO
    n_t = seq // CONV_TC
    vec = pl.BlockSpec((1, CONV_CH), lambda b, t: (0, 0))
    stream_specs, stream_chunks = [], []
    for w, axis, chunk in cast_streams:
        n_chunks = w.shape[axis] // chunk
        assert w.ndim == 2 and n_chunks * chunk == w.shape[axis] and n_chunks <= batch * n_t
        block = (chunk, w.shape[1]) if axis == 0 else (w.shape[0], chunk)

        def index_map(b, t, axis=axis, n_chunks=n_chunks):
            k = jnp.minimum(b * n_t + t, n_chunks - 1)
            return (k, 0) if axis == 0 else (0, k)

        stream_specs.append(pl.BlockSpec(block, index_map))
        stream_chunks.append(n_chunks)
    outs = pl.pallas_call(
        functools.partial(_conv_kernel, n_t=n_t, stream_chunks=tuple(stream_chunks)),
        grid=(batch, n_t),
        in_specs=[pl.BlockSpec((1, CONV_TC, CONV_CH), lambda b, t: (b, t, 0)),
                  pl.BlockSpec((1, CONV_HALO, CONV_CH), lambda b, t: (b, jnp.maximum(t * per - 1, 0), 0)),
                  pl.BlockSpec((CONV_K, CONV_CH), lambda b, t: (0, 0)), vec, vec, vec] + stream_specs,
        out_specs=[pl.BlockSpec((1, CONV_TC, CONV_CH), lambda b, t: (b, t, 0))] + stream_specs,
        out_shape=[jax.ShapeDtypeStruct((batch, seq, CONV_CH), BF16)]
                  + [jax.ShapeDtypeStruct(w.shape, BF16) for w, _, _ in cast_streams],
        scratch_shapes=[pltpu.VMEM((CONV_HALO + CONV_TC + SUBLANES, CONV_CH), F32)],
        compiler_params=_cparams(2), name="conv_prompt")(
        u3, u3, conv_w, conv_b[None, :], ln_g[None, :], ln_b[None, :], *[w for w, _, _ in cast_streams])
    return outs[0].reshape(batch * seq, CONV_CH), outs[1:]


KV_SLAB = 2 * HEADS_PER_GROUP


def _shift_cache(cref, out_ref, d, t_new):
    slab = d * KV_SLAB
    shift = t_new * KV_SLAB
    n_rows = cref.shape[1]
    if shift % slab == 0:
        s = shift // slab
        out_ref[0, 0:(n_rows - s) * slab] = cref[0, s:n_rows].reshape((n_rows - s) * slab, HEAD_DIM)
    else:
        assert shift < slab and shift % 8 == 0
        out_ref[0, 0:slab - shift] = cref[0, 0, shift:slab]

        def body(ci, carry):
            out_ref[0, pl.ds(pl.multiple_of(ci * slab - shift, 8), slab)] = cref[0, ci]
            return carry

        lax.fori_loop(1, n_rows, body, 0)


def _sample_kernel(qkv_ref, c0_ref, c1_ref, c2_ref, st_ref, u_ref, w_ref, b_ref, g_ref, beta_ref,
                   mix_ref, nconv_ref, nk0_ref, nk1_ref, nk2_ref, hist, *, t_new):
    cache_refs = (c0_ref, c1_ref, c2_ref)
    nk_refs = (nk0_ref, nk1_ref, nk2_ref)
    hp = HEADS_PER_GROUP
    outs = [[None] * N_GROUPS for _ in range(t_new)]
    lses = [[None] * N_GROUPS for _ in range(t_new)]
    for gi, (_, d) in enumerate(DILATION_GROUPS):
        cref = cache_refs[gi]
        n_rows = cref.shape[1]
        row = lax.broadcasted_iota(jnp.int32, (n_rows, hp, 1), 0)
        _shift_cache(cref, nk_refs[gi], d, t_new)
        keep = nk_refs[gi].shape[1] - t_new * KV_SLAB
        for t in range(t_new):
            nk_refs[gi][0, keep + t * KV_SLAB:keep + t * KV_SLAB + hp] = qkv_ref[0, t, N_GROUPS + gi]
            nk_refs[gi][0, keep + t * KV_SLAB + hp:keep + (t + 1) * KV_SLAB] = qkv_ref[0, t, 2 * N_GROUPS + gi]
        for t in range(t_new):
            s0 = (t % d) * KV_SLAB
            first_row = t // d
            q = qkv_ref[0, t, gi][None]
            kc = cref[0, :, s0:s0 + hp, :]
            vc = cref[0, :, s0 + hp:s0 + 2 * hp, :]
            s_c = jnp.sum(kc * q, axis=2, keepdims=True) * ATTN_SCALE
            if first_row > 0:
                s_c = jnp.where(row >= first_row, s_c, NEG_BIG)
            newest = [t - d * jj for jj in range(t // d + 1)]
            s_n, v_n = [], []
            for tn in newest:
                kn = qkv_ref[0, tn, N_GROUPS + gi][None]
                v_n.append(qkv_ref[0, tn, 2 * N_GROUPS + gi][None])
                s_n.append(jnp.sum(kn * q, axis=2, keepdims=True) * ATTN_SCALE)
            m = jnp.max(s_c, axis=0, keepdims=True)
            for sn in s_n:
                m = jnp.maximum(m, sn)
            e_c = jnp.exp(s_c - m)
            den = jnp.sum(e_c, axis=0, keepdims=True)
            acc = jnp.sum(e_c * vc, axis=0, keepdims=True)
            for sn, v1 in zip(s_n, v_n):
                e_n = jnp.exp(sn - m)
                den = den + e_n
                acc = acc + e_n * v1
            outs[t][gi] = acc / den
            lses[t][gi] = m + jnp.log(den)
    for t in range(t_new):
        ls = lses[t]
        m = functools.reduce(jnp.maximum, ls)
        es = [jnp.exp(l - m) for l in ls]
        tot = functools.reduce(lambda a, b: a + b, es)
        for gi in range(N_GROUPS):
            slab = (outs[t][gi] * (es[gi] / tot))[0]
            for h in range(hp):
                c0 = gi * GROUP_W + h * HEAD_DIM
                mix_ref[0, t:t + 1, c0:c0 + HEAD_DIM] = slab[h:h + 1, :]
    n_state = CONV_K - 1
    hist[0:n_state, :] = st_ref[0]
    hist[n_state:n_state + t_new, :] = u_ref[0]
    acc = jnp.broadcast_to(b_ref[...], (t_new, CONV_CH))
    for k in range(CONV_K):
        acc = acc + hist[k:k + t_new, :] * w_ref[k:k + 1, :]
    y = _layer_norm_rows(acc, g_ref[...], beta_ref[...])
    mix_ref[0, :, ATTN_WIDTH:] = y * jax.nn.sigmoid(y)
    nconv_ref[0] = hist[t_new:t_new + n_state, :]


def _sample_mixers(qkvs, us, caches, state, conv_w, conv_b, ln_g, ln_b, dec_batch, t_new):
    n_qkv = qkvs.shape[1]
    hp = HEADS_PER_GROUP
    qkv5 = qkvs.reshape(dec_batch, t_new, n_qkv, hp, HEAD_DIM)
    u3 = us.reshape(dec_batch, t_new, CONV_CH)
    cache_in, cache_specs, nk_shapes, nk_specs = [], [], [], []
    for (window, d), cache in zip(DILATION_GROUPS, caches):
        buf = cache.shape[1]
        assert buf == window and buf % d == 0
        n_rows = buf // d
        cache_in.append(cache.reshape(dec_batch, n_rows, d * KV_SLAB, HEAD_DIM))
        cache_specs.append(pl.BlockSpec((1, n_rows, d * KV_SLAB, HEAD_DIM), lambda b: (b, 0, 0, 0)))
        nk_shapes.append(jax.ShapeDtypeStruct((dec_batch, buf * KV_SLAB, HEAD_DIM), F32))
        nk_specs.append(pl.BlockSpec((1, buf * KV_SLAB, HEAD_DIM), lambda b: (b, 0, 0)))
    vec = pl.BlockSpec((1, CONV_CH), lambda b: (0, 0))
    n_state = CONV_K - 1
    mix, nconv, nk0, nk1, nk2 = pl.pallas_call(
        functools.partial(_sample_kernel, t_new=t_new),
        grid=(dec_batch,),
        in_specs=[pl.BlockSpec((1, t_new, n_qkv, hp, HEAD_DIM), lambda b: (b, 0, 0, 0, 0))] + cache_specs + [
            pl.BlockSpec((1, n_state, CONV_CH), lambda b: (b, 0, 0)),
            pl.BlockSpec((1, t_new, CONV_CH), lambda b: (b, 0, 0)),
            pl.BlockSpec((CONV_K, CONV_CH), lambda b: (0, 0)), vec, vec, vec],
        out_specs=[pl.BlockSpec((1, t_new, D_MODEL), lambda b: (b, 0, 0)),
                   pl.BlockSpec((1, n_state, CONV_CH), lambda b: (b, 0, 0))] + nk_specs,
        out_shape=[jax.ShapeDtypeStruct((dec_batch, t_new, D_MODEL), F32),
                   jax.ShapeDtypeStruct((dec_batch, n_state, CONV_CH), F32)] + nk_shapes,
        scratch_shapes=[pltpu.VMEM((n_state + t_new + 6, CONV_CH), F32)],
        compiler_params=_cparams(1), name="sample_mixers")(
        qkv5, *cache_in, state, u3, conv_w, conv_b[None, :], ln_g[None, :], ln_b[None, :])
    new_caches = [nk.reshape(c.shape) for nk, c in zip((nk0, nk1, nk2), caches)]
    return mix.reshape(dec_batch * t_new, D_MODEL), nconv, new_caches


OUT_TM = 512
OUT_ROWS = 128


def _out_proj_kernel(o0_ref, o1_ref, o2_ref, l0_ref, l1_ref, l2_ref, c_ref, x_ref, w_ref, g_ref, b_ref,
                     mixs_ref, xs_ref, x1_ref, x1bf_ref, x1s_ref, mix_scr, y_scr, *, n_i, alpha):
    i = pl.program_id(0)
    o_refs = (o0_ref, o1_ref, o2_ref)
    chunks = [slice(r0, r0 + OUT_ROWS) for r0 in range(0, OUT_TM, OUT_ROWS)]

    def project(rows):
        ls = [l0_ref[rows, :], l1_ref[rows, :], l2_ref[rows, :]]
        m = jnp.maximum(jnp.maximum(ls[0], ls[1]), ls[2])
        es = [jnp.exp(l - m) for l in ls]
        inv = 1.0 / (es[0] + es[1] + es[2])
        for gi in range(N_GROUPS):
            a = es[gi] * inv
            for h in range(HEADS_PER_GROUP):
                c0 = h * HEAD_DIM
                og = o_refs[gi][rows, c0:c0 + HEAD_DIM].astype(F32)
                mix_scr[rows, gi * GROUP_W + c0:gi * GROUP_W + c0 + HEAD_DIM] = (og * a[:, h:h + 1]).astype(BF16)
        mix_scr[rows, ATTN_WIDTH:] = c_ref[rows, :]
        y_scr[rows, :] = (jnp.dot(mix_scr[rows, :], w_ref[...], preferred_element_type=F32)
                          + alpha * x_ref[rows, :])

    def normalize(rows):
        for r0 in range(rows.start, rows.stop, LN_STRIP):
            strip = slice(r0, r0 + LN_STRIP)
            x1 = _layer_norm_rows(y_scr[strip, :], g_ref[...], b_ref[...])
            x1_ref[strip, :] = x1
            x1bf_ref[strip, :] = x1.astype(BF16)

    project(chunks[0])
    for c in range(1, len(chunks)):
        project(chunks[c])
        normalize(chunks[c - 1])
    normalize(chunks[-1])

    @pl.when(i == n_i - 1)
    def _():
        ys = jnp.dot(mixs_ref[...].astype(BF16), w_ref[...], preferred_element_type=F32) + alpha * xs_ref[...]
        x1s_ref[...] = _layer_norm_rows(ys, g_ref[...], b_ref[...])


def _out_proj(os, lses, c, x2, w_bf, ln_g, ln_b, mixs, xs2, alpha):
    m = x2.shape[0]
    ms = xs2.shape[0]
    n_i = m // OUT_TM
    row = lambda w: pl.BlockSpec((OUT_TM, w), lambda i: (i, 0))
    whole = lambda a: pl.BlockSpec(a.shape, lambda i: (0,) * a.ndim)
    g2, b2 = ln_g[None, :], ln_b[None, :]
    return pl.pallas_call(
        functools.partial(_out_proj_kernel, n_i=n_i, alpha=alpha),
        grid=(n_i,),
        in_specs=[row(GROUP_W)] * 3 + [row(HEAD_DIM)] * 3 + [row(CONV_CH), row(D_MODEL),
                  whole(w_bf), whole(g2), whole(b2), whole(mixs), whole(xs2)],
        out_specs=[row(D_MODEL), row(D_MODEL), pl.BlockSpec((ms, D_MODEL), lambda i: (0, 0))],
        out_shape=[jax.ShapeDtypeStruct((m, D_MODEL), F32), jax.ShapeDtypeStruct((m, D_MODEL), BF16),
                   jax.ShapeDtypeStruct((ms, D_MODEL), F32)],
        scratch_shapes=[pltpu.VMEM((OUT_TM, D_MODEL), BF16), pltpu.VMEM((OUT_TM, D_MODEL), F32)],
        compiler_params=_cparams(1), name="out_proj")(
        *os, *lses, c, x2, w_bf, g2, b2, mixs, xs2)


FFN_TM = 1024
FFN_TF = 512
FFN_ROWS = 1024
FFN_RES_W = 256


def _ffn_kernel(xbf_ref, xres_ref, wg_ref, wu_ref, wd_ref, g_ref, b_ref, xs_ref,
                y_ref, ys_ref, xsbf, *, n_i, n_f, alpha):
    i = pl.program_id(0)
    f = pl.program_id(1)

    def swiglu_down(xb):
        gate = jnp.dot(xb, wg_ref[...], preferred_element_type=F32)
        up = jnp.dot(xb, wu_ref[...], preferred_element_type=F32)
        act = (gate * jax.nn.sigmoid(gate) * up).astype(BF16)
        return jnp.dot(act, wd_ref[...], preferred_element_type=F32)

    @pl.when(f == 0)
    def _():
        y_ref[...] = jnp.zeros_like(y_ref)

    for r0 in range(0, FFN_TM, FFN_ROWS):
        y_ref[r0:r0 + FFN_ROWS, :] += swiglu_down(xbf_ref[r0:r0 + FFN_ROWS, :])

    for c in range(D_MODEL // FFN_RES_W):
        @pl.when(f == c)
        def _(c=c):
            y_ref[:, c * FFN_RES_W:(c + 1) * FFN_RES_W] += alpha * xres_ref[...]

    @pl.when(f == n_f - 1)
    def _():
        y_ref[...] = _layer_norm_rows(y_ref[...], g_ref[...], b_ref[...])

    @pl.when(i == n_i - 1)
    def _():
        @pl.when(f == 0)
        def _():
            xsbf[...] = xs_ref[...].astype(BF16)

        downs = swiglu_down(xsbf[...])

        @pl.when(f == 0)
        def _():
            ys_ref[...] = downs + alpha * xs_ref[...]

        @pl.when(f > 0)
        def _():
            ys_ref[...] += downs

        @pl.when(f == n_f - 1)
        def _():
            ys_ref[...] = _layer_norm_rows(ys_ref[...], g_ref[...], b_ref[...])


def _ffn(x1, x1bf, x1s, wg_bf, wu_bf, wd_bf, ln_g, ln_b, alpha):
    m = x1.shape[0]
    ms = x1s.shape[0]
    hidden = wg_bf.shape[1]
    n_i = m // FFN_TM
    n_f = hidden // FFN_TF
    n_res = D_MODEL // FFN_RES_W
    assert n_f >= n_res
    g2, b2 = ln_g[None, :], ln_b[None, :]
    return pl.pallas_call(
        functools.partial(_ffn_kernel, n_i=n_i, n_f=n_f, alpha=alpha),
        grid=(n_i, n_f),
        in_specs=[pl.BlockSpec((FFN_TM, D_MODEL), lambda i, f: (i, 0)),
                  pl.BlockSpec((FFN_TM, FFN_RES_W), lambda i, f: (i, jnp.minimum(f, n_res - 1))),
                  pl.BlockSpec((D_MODEL, FFN_TF), lambda i, f: (0, f)),
                  pl.BlockSpec((D_MODEL, FFN_TF), lambda i, f: (0, f)),
                  pl.BlockSpec((FFN_TF, D_MODEL), lambda i, f: (f, 0)),
                  pl.BlockSpec((1, D_MODEL), lambda i, f: (0, 0)),
                  pl.BlockSpec((1, D_MODEL), lambda i, f: (0, 0)),
                  pl.BlockSpec((ms, D_MODEL), lambda i, f: (0, 0))],
        out_specs=[pl.BlockSpec((FFN_TM, D_MODEL), lambda i, f: (i, 0)),
                   pl.BlockSpec((ms, D_MODEL), lambda i, f: (0, 0))],
        out_shape=[jax.ShapeDtypeStruct((m, D_MODEL), F32), jax.ShapeDtypeStruct((ms, D_MODEL), F32)],
        scratch_shapes=[pltpu.VMEM((ms, D_MODEL), BF16)],
        compiler_params=_cparams(2), name="ffn")(
        x1bf, x1, wg_bf, wu_bf, wd_bf, g2, b2, x1s)


def kernel(x_prompt, x_sample, cache_kv_w128, cache_kv_w512, cache_kv_w2048, state_conv, w_in, w_out,
           conv_w, conv_b, conv_ln_g, conv_ln_b, ln1_g, ln1_b, w_gate, w_up, w_down, ln2_g, ln2_b):
    depth = w_in.shape[0]
    batch, seq, _ = x_prompt.shape
    dec_batch, t_new, _ = x_sample.shape
    caches = (cache_kv_w128, cache_kv_w512, cache_kv_w2048)
    alpha = (2.0 * depth) ** 0.25

    cos_p, sin_p = _rope_tables(jnp.arange(seq, dtype=jnp.int32))
    pos_s = PAST_LEN + jnp.arange(t_new, dtype=jnp.int32)
    cos_s, sin_s = _rope_tables(jnp.tile(pos_s, dec_batch))

    xp = x_prompt.reshape(batch * seq, D_MODEL)
    xs = x_sample.reshape(dec_batch * t_new, D_MODEL)
    kvp = [[] for _ in range(N_GROUPS)]
    kvs = [[] for _ in range(N_GROUPS)]
    convp, convs = [], []
    for l in range(depth):
        cm0, cm1, cm2, u, kv0, kv1, kv2, qkvs, us = _in_proj(
            xp, xs, w_in[l].astype(BF16), cos_p, sin_p, cos_s, sin_s, batch, seq)
        os, lses = zip(*[_attn_group(cm, gi, batch, seq) for gi, cm in enumerate((cm0, cm1, cm2))])
        c, (wo_bf, wg_bf, wu_bf, wd_bf) = _conv_prompt(
            u, conv_w[l], conv_b[l], conv_ln_g[l], conv_ln_b[l], batch, seq,
            [(w_out[l], 0, CAST_ROWS_OUT), (w_gate[l], 1, CAST_CHUNK_FFN), (w_up[l], 1, CAST_CHUNK_FFN),
             (w_down[l], 0, CAST_CHUNK_FFN)])
        layer_caches = [cc[l] for cc in caches]
        mixs, nconv_s, new_caches = _sample_mixers(qkvs, us, layer_caches, state_conv[l], conv_w[l], conv_b[l],
                                                   conv_ln_g[l], conv_ln_b[l], dec_batch, t_new)
        x1, x1bf, x1s = _out_proj(os, lses, c, xp, wo_bf, ln1_g[l], ln1_b[l], mixs, xs, alpha)
        xp, xs = _ffn(x1, x1bf, x1s, wg_bf, wu_bf, wd_bf, ln2_g[l], ln2_b[l], alpha)
        for gi, (kv, nk) in enumerate(zip((kv0, kv1, kv2), new_caches)):
            kvp[gi].append(kv.reshape(batch, kv.shape[1], 2, HEADS_PER_GROUP, HEAD_DIM))
            kvs[gi].append(nk)
        convp.append(u.reshape(batch, seq, CONV_CH)[:, seq - (CONV_K - 1):])
        convs.append(nconv_s)

    y_prompt = xp.reshape(batch, seq, D_MODEL)
    y_sample = xs.reshape(dec_batch, t_new, D_MODEL)
    return (y_prompt, y_sample, jnp.stack(kvp[0]), jnp.stack(kvp[1]), jnp.stack(kvp[2]), jnp.stack(convp),
            jnp.stack(kvs[0]), jnp.stack(kvs[1]), jnp.stack(kvs[2]), jnp.stack(convs))
```

```python
import functools

import numpy as np
import jax
import jax.numpy as jnp
from jax import lax
from jax.experimental import pallas as pl
from jax.experimental.pallas import tpu as pltpu

D_MODEL = 2048
HEAD_DIM = 128
CONV_CH = D_MODEL // 4
ATTN_WIDTH = D_MODEL - CONV_CH
DILATION_GROUPS = ((128, 1), (512, 4), (2048, 16))
N_GROUPS = len(DILATION_GROUPS)
HEADS_PER_GROUP = ATTN_WIDTH // HEAD_DIM // N_GROUPS
GROUP_W = HEADS_PER_GROUP * HEAD_DIM
CONV_K = 31
ROPE_THETA = 10000.0
LN_EPS = 1e-5
Q_BLOCK = 128
ATTN_SCALE = HEAD_DIM ** -0.5
NEG_BIG = -1e30
PAST_LEN = 16384

F32 = jnp.float32
BF16 = jnp.bfloat16

VMEM_LIMIT = 56 * 1024 * 1024
LN_STRIP = 16
SUBLANES = 8


def _cparams(n_axes):
    return pltpu.CompilerParams(dimension_semantics=("arbitrary",) * n_axes,
                                vmem_limit_bytes=VMEM_LIMIT)


def _layer_norm_rows(y, g, b):
    mu = jnp.mean(y, axis=-1, keepdims=True)
    yc = y - mu
    var = jnp.mean(yc * yc, axis=-1, keepdims=True)
    return yc * lax.rsqrt(var + LN_EPS) * g + b


def _rope_tables(pos):
    half = HEAD_DIM // 2
    inv = ROPE_THETA ** (-jnp.arange(half, dtype=F32) / half)
    ang = pos.astype(F32)[:, None] * inv[None, :]
    cos, sin = jnp.cos(ang), jnp.sin(ang)
    return jnp.concatenate([cos, cos], axis=1), jnp.concatenate([-sin, sin], axis=1)


def _rope(h, cos, sin):
    parts = []
    for hh in range(HEADS_PER_GROUP):
        hs = h[:, hh * HEAD_DIM:(hh + 1) * HEAD_DIM]
        parts.append(hs * cos + pltpu.roll(hs, HEAD_DIM // 2, axis=1) * sin)
    return jnp.concatenate(parts, axis=1)


IN_TM = 1024
IN_TN = GROUP_W
IN_ROWS = 256


def _kv_window_plan(seq, batch, n_j):
    tiles_per_batch = seq // IN_TM
    n_i = batch * tiles_per_batch
    plans = []
    for gi, (window, _) in enumerate(DILATION_GROUPS):
        keep = min(window, seq)
        rb = min(keep, IN_TM)
        first_tile = (seq - keep) // IN_TM
        row_lo = (seq - keep) - first_tile * IN_TM
        writes = []
        for i in range(n_i):
            b, it = divmod(i, tiles_per_batch)
            if it >= first_tile:
                for c, j in enumerate((N_GROUPS + gi, 2 * N_GROUPS + gi)):
                    writes.append((i * n_j + j, (b, it - first_tile, c)))
        writes.sort()
        tab = np.zeros((n_i * n_j, 3), np.int32)
        w = 0
        for step in range(n_i * n_j):
            while w < len(writes) - 1 and writes[w][0] < step:
                w += 1
            tab[step] = writes[w][1]
        plans.append(dict(keep=keep, rb=rb, first_tile=first_tile, row_lo=row_lo, tab=tab))
    return plans


def _in_proj_kernel(tab_ref, x_ref, w_ref, cos_ref, sin_ref, xs_ref, coss_ref, sins_ref,
                    cm0_ref, cm1_ref, cm2_ref, u_ref, kv0_ref, kv1_ref, kv2_ref, qkvs_ref, us_ref,
                    xbf, a_scr, de_scr, xsbf, as_scr, *, n_i, tiles_per_batch, plans):
    del tab_ref
    i = pl.program_id(0)
    j = pl.program_id(1)
    kv_refs = (kv0_ref, kv1_ref, kv2_ref)
    cm_refs = (cm0_ref, cm1_ref, cm2_ref)

    chunks = [slice(c * IN_ROWS, (c + 1) * IN_ROWS) for c in range(IN_TM // IN_ROWS)]

    def chunk_dot(rows, cast=False):
        if cast:
            xbf[rows, :] = x_ref[rows, :].astype(BF16)
        return jnp.dot(xbf[rows, :], w_ref[...], preferred_element_type=F32)

    def rope_rows(val, rows):
        return _rope(val, cos_ref[rows, :], sin_ref[rows, :])

    def store_class_major(gi, c, val):
        d = DILATION_GROUPS[gi][1]
        ref = cm_refs[gi]
        per = IN_ROWS // d
        if d == 1:
            ref[0, 0, 0, chunks[c], :] = val.astype(BF16)
            return
        for h in range(HEADS_PER_GROUP):
            de_scr[c, h] = val[:, h * HEAD_DIM:(h + 1) * HEAD_DIM]
        for r in range(d):
            for h in range(HEADS_PER_GROUP):
                ref[0, 0, r, c * per:(c + 1) * per, h * HEAD_DIM:(h + 1) * HEAD_DIM] = (
                    de_scr[c, h, pl.ds(r, per, stride=d), :].astype(BF16))

    it = i % tiles_per_batch
    for gi in range(N_GROUPS):
        plan = plans[gi]
        in_window = it >= plan["first_tile"]
        window_rows = slice(plan["row_lo"], plan["row_lo"] + plan["rb"])

        @pl.when(j == gi)
        def _(gi=gi):
            for c, rows in enumerate(chunks):
                store_class_major(gi, c, rope_rows(chunk_dot(rows, cast=(gi == 0)), rows))

        @pl.when(j == N_GROUPS + gi)
        def _(window_rows=window_rows, in_window=in_window, gi=gi):
            for c, rows in enumerate(chunks):
                r = rope_rows(chunk_dot(rows), rows)
                store_class_major(gi, c, r)
                a_scr[rows, :] = r

            @pl.when(in_window)
            def _():
                kv_refs[gi][0] = a_scr[window_rows, :]

        @pl.when(j == 2 * N_GROUPS + gi)
        def _(window_rows=window_rows, in_window=in_window, gi=gi):
            for c, rows in enumerate(chunks):
                acc = chunk_dot(rows)
                store_class_major(gi, c, acc)
                a_scr[rows, :] = acc

            @pl.when(in_window)
            def _():
                kv_refs[gi][0] = a_scr[window_rows, :]

    @pl.when(j == 3 * N_GROUPS)
    def _():
        for rows in chunks:
            a_scr[rows, :] = chunk_dot(rows)

    @pl.when(j == 3 * N_GROUPS + 1)
    def _():
        for rows in chunks:
            u_ref[rows, :] = a_scr[rows, :] * jax.nn.sigmoid(chunk_dot(rows))

    @pl.when(i == n_i - 1)
    def _():
        @pl.when(j == 0)
        def _():
            xsbf[...] = xs_ref[...].astype(BF16)

        accs = jnp.dot(xsbf[...], w_ref[...], preferred_element_type=F32)

        def store_heads(val):
            for h in range(HEADS_PER_GROUP):
                qkvs_ref[:, 0, h, :] = val[:, h * HEAD_DIM:(h + 1) * HEAD_DIM]

        @pl.when(j < 2 * N_GROUPS)
        def _():
            store_heads(_rope(accs, coss_ref[...], sins_ref[...]))

        @pl.when((j >= 2 * N_GROUPS) & (j < 3 * N_GROUPS))
        def _():
            store_heads(accs)

        @pl.when(j == 3 * N_GROUPS)
        def _():
            as_scr[...] = accs

        @pl.when(j == 3 * N_GROUPS + 1)
        def _():
            us_ref[...] = as_scr[...] * jax.nn.sigmoid(accs)


def _in_proj(x2, xs2, w_bf, cos_p, sin_p, cos_s, sin_s, batch, seq):
    m = x2.shape[0]
    ms = xs2.shape[0]
    n_i = m // IN_TM
    n_j = w_bf.shape[1] // IN_TN
    n_qkv = 3 * N_GROUPS
    tiles_per_batch = seq // IN_TM
    plans = _kv_window_plan(seq, batch, n_j)
    tab = jnp.asarray(np.stack([p["tab"] for p in plans]).reshape(-1))
    n_steps = n_i * n_j

    def kv_map(gi):
        def f(i, j, tab_ref):
            base = (gi * n_steps + i * n_j + j) * 3
            return tab_ref[base], tab_ref[base + 1], tab_ref[base + 2]
        return f

    def cm_map(gi):
        def f(i, j, tab_ref):
            plane = (j > gi).astype(jnp.int32) + (j > N_GROUPS + gi).astype(jnp.int32)
            return i // tiles_per_batch, plane, 0, i % tiles_per_batch, 0
        return f

    last = n_i - 1
    in_specs = [
        pl.BlockSpec((IN_TM, D_MODEL), lambda i, j, t: (i, 0)),
        pl.BlockSpec((D_MODEL, IN_TN), lambda i, j, t: (0, j)),
        pl.BlockSpec((IN_TM, HEAD_DIM), lambda i, j, t: (i % tiles_per_batch, 0)),
        pl.BlockSpec((IN_TM, HEAD_DIM), lambda i, j, t: (i % tiles_per_batch, 0)),
        pl.BlockSpec((ms, D_MODEL), lambda i, j, t: (0, 0)),
        pl.BlockSpec((ms, HEAD_DIM), lambda i, j, t: (0, 0)),
        pl.BlockSpec((ms, HEAD_DIM), lambda i, j, t: (0, 0)),
    ]
    out_shape, out_specs = [], []
    for gi, (_, d) in enumerate(DILATION_GROUPS):
        out_shape.append(jax.ShapeDtypeStruct((batch, 3, d, seq // d, GROUP_W), BF16))
        out_specs.append(pl.BlockSpec((1, 1, d, IN_TM // d, GROUP_W), cm_map(gi)))
    out_shape.append(jax.ShapeDtypeStruct((m, CONV_CH), F32))
    out_specs.append(pl.BlockSpec((IN_TM, CONV_CH), lambda i, j, t: (i, 0)))
    for gi, p in enumerate(plans):
        out_shape.append(jax.ShapeDtypeStruct((batch, p["keep"], 2 * GROUP_W), F32))
        out_specs.append(pl.BlockSpec((1, p["rb"], GROUP_W), kv_map(gi)))
    out_shape += [jax.ShapeDtypeStruct((ms, n_qkv, HEADS_PER_GROUP, HEAD_DIM), F32),
                  jax.ShapeDtypeStruct((ms, CONV_CH), F32)]
    out_specs += [
        pl.BlockSpec((ms, 1, HEADS_PER_GROUP, HEAD_DIM),
                     lambda i, j, t: (0, jnp.where(i == last, jnp.minimum(j, n_qkv - 1), 0), 0, 0)),
        pl.BlockSpec((ms, CONV_CH), lambda i, j, t: (0, 0)),
    ]
    grid_spec = pltpu.PrefetchScalarGridSpec(
        num_scalar_prefetch=1, grid=(n_i, n_j), in_specs=in_specs, out_specs=out_specs,
        scratch_shapes=[pltpu.VMEM((IN_TM, D_MODEL), BF16), pltpu.VMEM((IN_TM, CONV_CH), F32),
                        pltpu.VMEM((IN_TM // IN_ROWS, HEADS_PER_GROUP, IN_ROWS, HEAD_DIM), F32),
                        pltpu.VMEM((ms, D_MODEL), BF16), pltpu.VMEM((ms, CONV_CH), F32)])
    kern = functools.partial(_in_proj_kernel, n_i=n_i, tiles_per_batch=tiles_per_batch, plans=plans)
    return pl.pallas_call(kern, grid_spec=grid_spec, out_shape=out_shape,
                          compiler_params=_cparams(2), name="in_proj")(
        tab, x2, w_bf, cos_p, sin_p, xs2, cos_s, sin_s)


ATTN_TP = 2048


def _attn_kernel(q_ref, k_ref, v_ref, o_ref, lse_ref, k_scr, v_scr, o_scr, lse_scr, *, d, lc):
    n = pl.program_id(1)
    n_qb = lc // Q_BLOCK

    @pl.when(n == 0)
    def _():
        k_scr[:, 0:Q_BLOCK, :] = jnp.zeros((d, Q_BLOCK, GROUP_W), BF16)
        v_scr[:, 0:Q_BLOCK, :] = jnp.zeros((d, Q_BLOCK, GROUP_W), BF16)

    k_scr[:, Q_BLOCK:, :] = k_ref[0, 0]
    v_scr[:, Q_BLOCK:, :] = v_ref[0, 0]
    qq = lax.broadcasted_iota(jnp.int32, (Q_BLOCK, 2 * Q_BLOCK), 0)
    kk = lax.broadcasted_iota(jnp.int32, (Q_BLOCK, 2 * Q_BLOCK), 1)
    band = (kk >= qq) & (kk <= qq + Q_BLOCK)
    lane = lax.broadcasted_iota(jnp.int32, (Q_BLOCK, HEAD_DIM), 1)

    def unit(u, carry):
        r = u // n_qb
        qb = u % n_qb
        r0 = pl.multiple_of(qb * Q_BLOCK, Q_BLOCK)
        valid = band & ((kk >= Q_BLOCK) | (n > 0) | (qb > 0))
        if d == 1:
            rows = pl.ds(r0, Q_BLOCK)
        else:
            rows = pl.ds(r0 * d + r, Q_BLOCK, stride=d)
        lse_blk = jnp.zeros((Q_BLOCK, HEAD_DIM), F32)
        for h in range(HEADS_PER_GROUP):
            c0 = h * HEAD_DIM
            q = q_ref[0, 0, r, pl.ds(r0, Q_BLOCK), c0:c0 + HEAD_DIM]
            k = k_scr[r, pl.ds(r0, 2 * Q_BLOCK), c0:c0 + HEAD_DIM]
            v = v_scr[r, pl.ds(r0, 2 * Q_BLOCK), c0:c0 + HEAD_DIM]
            s = lax.dot_general(q, k, (((1,), (1,)), ((), ())), preferred_element_type=F32)
            s = jnp.where(valid, s * ATTN_SCALE, NEG_BIG)
            m = jnp.max(s, axis=1, keepdims=True)
            e = jnp.exp(s - m)
            den = jnp.sum(e, axis=1, keepdims=True)
            o_scr[h, rows, :] = jnp.dot(e.astype(BF16), v, preferred_element_type=F32) / den
            lse_blk = jnp.where(lane == h, m + jnp.log(den), lse_blk)
        lse_scr[rows, :] = lse_blk
        return carry

    lax.fori_loop(0, d * n_qb, unit, 0, unroll=4)
    k_scr[:, 0:Q_BLOCK, :] = k_scr[:, lc:lc + Q_BLOCK, :]
    v_scr[:, 0:Q_BLOCK, :] = v_scr[:, lc:lc + Q_BLOCK, :]
    for h in range(HEADS_PER_GROUP):
        o_ref[:, h * HEAD_DIM:(h + 1) * HEAD_DIM] = o_scr[h].astype(BF16)
    lse_ref[...] = lse_scr[...]


def _attn_group(cm, gi, batch, seq):
    _, d = DILATION_GROUPS[gi]
    lc = ATTN_TP // d
    n_t = seq // ATTN_TP

    def plane(p):
        return pl.BlockSpec((1, 1, d, lc, GROUP_W), lambda b, n: (b, p, 0, n, 0))

    return pl.pallas_call(
        functools.partial(_attn_kernel, d=d, lc=lc),
        grid=(batch, n_t),
        in_specs=[plane(0), plane(1), plane(2)],
        out_specs=[pl.BlockSpec((ATTN_TP, GROUP_W), lambda b, n: (b * n_t + n, 0)),
                   pl.BlockSpec((ATTN_TP, HEAD_DIM), lambda b, n: (b * n_t + n, 0))],
        out_shape=[jax.ShapeDtypeStruct((batch * seq, GROUP_W), BF16),
                   jax.ShapeDtypeStruct((batch * seq, HEAD_DIM), F32)],
        scratch_shapes=[pltpu.VMEM((d, lc + Q_BLOCK, GROUP_W), BF16),
                        pltpu.VMEM((d, lc + Q_BLOCK, GROUP_W), BF16),
                        pltpu.VMEM((HEADS_PER_GROUP, ATTN_TP, HEAD_DIM), F32),
                        pltpu.VMEM((ATTN_TP, HEAD_DIM), F32)],
        compiler_params=_cparams(2), name=f"attn_g{gi}")(cm, cm, cm)


CONV_TC = 256
CONV_HALO = 32
CONV_ROWS = 64


def _conv_kernel(u_ref, halo_ref, w_ref, b_ref, g_ref, beta_ref, *rest, n_t, stream_chunks):
    n_s = len(stream_chunks)
    src_refs, c_ref, dst_refs, hist = rest[:n_s], rest[n_s], rest[n_s + 1:2 * n_s + 1], rest[2 * n_s + 1]
    t = pl.program_id(1)
    step = pl.program_id(0) * n_t + t

    for src, dst, n_chunks in zip(src_refs, dst_refs, stream_chunks):
        @pl.when(step < n_chunks)
        def _(src=src, dst=dst):
            dst[...] = src[...].astype(BF16)

    hist[0:CONV_HALO, :] = jnp.where(t > 0, halo_ref[0], 0.0)
    hist[CONV_HALO:CONV_HALO + CONV_TC, :] = u_ref[0]
    hist[CONV_HALO + CONV_TC:, :] = jnp.zeros((SUBLANES, CONV_CH), F32)
    lead = CONV_HALO - (CONV_K - 1)
    for rc in range(CONV_TC // CONV_ROWS):
        r0 = rc * CONV_ROWS
        acc = jnp.broadcast_to(b_ref[...], (CONV_ROWS, CONV_CH))
        for s in range(SUBLANES):
            grp = None
            for c in range(s, lead + CONV_K, SUBLANES):
                if c < lead:
                    continue
                term = hist[r0 + c - s:r0 + c - s + CONV_ROWS + SUBLANES, :] * w_ref[c - lead:c - lead + 1, :]
                grp = term if grp is None else grp + term
            acc = acc + grp[s:s + CONV_ROWS, :]
        y = _layer_norm_rows(acc, g_ref[...], beta_ref[...])
        c_ref[0, r0:r0 + CONV_ROWS, :] = (y * jax.nn.sigmoid(y)).astype(BF16)


def _conv_prompt(u2, conv_w, conv_b, ln_g, ln_b, batch, seq, cast_streams):
    u3 = u2.reshape(batch, seq, CONV_CH)
    per = CONV_TC // CONV_HALO
    n_t = seq // CONV_TC
    vec = pl.BlockSpec((1, CONV_CH), lambda b, t: (0, 0))
    stream_specs, stream_chunks = [], []
    for w, axis, chunk in cast_streams:
        n_chunks = w.shape[axis] // chunk
        assert w.ndim == 2 and n_chunks * chunk == w.shape[axis] and n_chunks <= batch * n_t
        block = (chunk, w.shape[1]) if axis == 0 else (w.shape[0], chunk)

        def index_map(b, t, axis=axis, n_chunks=n_chunks):
            k = jnp.minimum(b * n_t + t, n_chunks - 1)
            return (k, 0) if axis == 0 else (0, k)

        stream_specs.append(pl.BlockSpec(block, index_map))
        stream_chunks.append(n_chunks)
    outs = pl.pallas_call(
        functools.partial(_conv_kernel, n_t=n_t, stream_chunks=tuple(stream_chunks)),
        grid=(batch, n_t),
        in_specs=[pl.BlockSpec((1, CONV_TC, CONV_CH), lambda b, t: (b, t, 0)),
                  pl.BlockSpec((1, CONV_HALO, CONV_CH), lambda b, t: (b, jnp.maximum(t * per - 1, 0), 0)),
                  pl.BlockSpec((CONV_K, CONV_CH), lambda b, t: (0, 0)), vec, vec, vec] + stream_specs,
        out_specs=[pl.BlockSpec((1, CONV_TC, CONV_CH), lambda b, t: (b, t, 0))] + stream_specs,
        out_shape=[jax.ShapeDtypeStruct((batch, seq, CONV_CH), BF16)]
                  + [jax.ShapeDtypeStruct(w.shape, BF16) for w, _, _ in cast_streams],
        scratch_shapes=[pltpu.VMEM((CONV_HALO + CONV_TC + SUBLANES, CONV_CH), F32)],
        compiler_params=_cparams(2), name="conv_prompt")(
        u3, u3, conv_w, conv_b[None, :], ln_g[None, :], ln_b[None, :], *[w for w, _, _ in cast_streams])
    return outs[0].reshape(batch * seq, CONV_CH), outs[1:]


KV_SLAB = 2 * HEADS_PER_GROUP


def _shift_cache(cref, out_ref, d, t_new):
    slab = d * KV_SLAB
    shift = t_new * KV_SLAB
    n_rows = cref.shape[1]
    if shift % slab == 0:
        s = shift // slab
        out_ref[0, 0:(n_rows - s) * slab] = cref[0, s:n_rows].reshape((n_rows - s) * slab, HEAD_DIM)
    else:
        assert shift < slab and shift % 8 == 0
        out_ref[0, 0:slab - shift] = cref[0, 0, shift:slab]

        def body(ci, carry):
            out_ref[0, pl.ds(pl.multiple_of(ci * slab - shift, 8), slab)] = cref[0, ci]
            return carry

        lax.fori_loop(1, n_rows, body, 0)


def _sample_kernel(qkv_ref, c0_ref, c1_ref, c2_ref, st_ref, u_ref, w_ref, b_ref, g_ref, beta_ref,
                   mix_ref, nconv_ref, nk0_ref, nk1_ref, nk2_ref, hist, *, t_new):
    cache_refs = (c0_ref, c1_ref, c2_ref)
    nk_refs = (nk0_ref, nk1_ref, nk2_ref)
    hp = HEADS_PER_GROUP
    outs = [[None] * N_GROUPS for _ in range(t_new)]
    lses = [[None] * N_GROUPS for _ in range(t_new)]
    for gi, (_, d) in enumerate(DILATION_GROUPS):
        cref = cache_refs[gi]
        n_rows = cref.shape[1]
        row = lax.broadcasted_iota(jnp.int32, (n_rows, hp, 1), 0)
        _shift_cache(cref, nk_refs[gi], d, t_new)
        keep = nk_refs[gi].shape[1] - t_new * KV_SLAB
        for t in range(t_new):
            nk_refs[gi][0, keep + t * KV_SLAB:keep + t * KV_SLAB + hp] = qkv_ref[0, t, N_GROUPS + gi]
            nk_refs[gi][0, keep + t * KV_SLAB + hp:keep + (t + 1) * KV_SLAB] = qkv_ref[0, t, 2 * N_GROUPS + gi]
        for t in range(t_new):
            s0 = (t % d) * KV_SLAB
            first_row = t // d
            q = qkv_ref[0, t, gi][None]
            kc = cref[0, :, s0:s0 + hp, :]
            vc = cref[0, :, s0 + hp:s0 + 2 * hp, :]
            s_c = jnp.sum(kc * q, axis=2, keepdims=True) * ATTN_SCALE
            if first_row > 0:
                s_c = jnp.where(row >= first_row, s_c, NEG_BIG)
            newest = [t - d * jj for jj in range(t // d + 1)]
            s_n, v_n = [], []
            for tn in newest:
                kn = qkv_ref[0, tn, N_GROUPS + gi][None]
                v_n.append(qkv_ref[0, tn, 2 * N_GROUPS + gi][None])
                s_n.append(jnp.sum(kn * q, axis=2, keepdims=True) * ATTN_SCALE)
            m = jnp.max(s_c, axis=0, keepdims=True)
            for sn in s_n:
                m = jnp.maximum(m, sn)
            e_c = jnp.exp(s_c - m)
            den = jnp.sum(e_c, axis=0, keepdims=True)
            acc = jnp.sum(e_c * vc, axis=0, keepdims=True)
            for sn, v1 in zip(s_n, v_n):
                e_n = jnp.exp(sn - m)
                den = den + e_n
                acc = acc + e_n * v1
            outs[t][gi] = acc / den
            lses[t][gi] = m + jnp.log(den)
    for t in range(t_new):
        ls = lses[t]
        m = functools.reduce(jnp.maximum, ls)
        es = [jnp.exp(l - m) for l in ls]
        tot = functools.reduce(lambda a, b: a + b, es)
        for gi in range(N_GROUPS):
            slab = (outs[t][gi] * (es[gi] / tot))[0]
            for h in range(hp):
                c0 = gi * GROUP_W + h * HEAD_DIM
                mix_ref[0, t:t + 1, c0:c0 + HEAD_DIM] = slab[h:h + 1, :]
    n_state = CONV_K - 1
    hist[0:n_state, :] = st_ref[0]
    hist[n_state:n_state + t_new, :] = u_ref[0]
    acc = jnp.broadcast_to(b_ref[...], (t_new, CONV_CH))
    for k in range(CONV_K):
        acc = acc + hist[k:k + t_new, :] * w_ref[k:k + 1, :]
    y = _layer_norm_rows(acc, g_ref[...], beta_ref[...])
    mix_ref[0, :, ATTN_WIDTH:] = y * jax.nn.sigmoid(y)
    nconv_ref[0] = hist[t_new:t_new + n_state, :]


def _sample_mixers(qkvs, us, caches, state, conv_w, conv_b, ln_g, ln_b, dec_batch, t_new):
    n_qkv = qkvs.shape[1]
    hp = HEADS_PER_GROUP
    qkv5 = qkvs.reshape(dec_batch, t_new, n_qkv, hp, HEAD_DIM)
    u3 = us.reshape(dec_batch, t_new, CONV_CH)
    cache_in, cache_specs, nk_shapes, nk_specs = [], [], [], []
    for (window, d), cache in zip(DILATION_GROUPS, caches):
        buf = cache.shape[1]
        assert buf == window and buf % d == 0
        n_rows = buf // d
        cache_in.append(cache.reshape(dec_batch, n_rows, d * KV_SLAB, HEAD_DIM))
        cache_specs.append(pl.BlockSpec((1, n_rows, d * KV_SLAB, HEAD_DIM), lambda b: (b, 0, 0, 0)))
        nk_shapes.append(jax.ShapeDtypeStruct((dec_batch, buf * KV_SLAB, HEAD_DIM), F32))
        nk_specs.append(pl.BlockSpec((1, buf * KV_SLAB, HEAD_DIM), lambda b: (b, 0, 0)))
    vec = pl.BlockSpec((1, CONV_CH), lambda b: (0, 0))
    n_state = CONV_K - 1
    mix, nconv, nk0, nk1, nk2 = pl.pallas_call(
        functools.partial(_sample_kernel, t_new=t_new),
        grid=(dec_batch,),
        in_specs=[pl.BlockSpec((1, t_new, n_qkv, hp, HEAD_DIM), lambda b: (b, 0, 0, 0, 0))] + cache_specs + [
            pl.BlockSpec((1, n_state, CONV_CH), lambda b: (b, 0, 0)),
            pl.BlockSpec((1, t_new, CONV_CH), lambda b: (b, 0, 0)),
            pl.BlockSpec((CONV_K, CONV_CH), lambda b: (0, 0)), vec, vec, vec],
        out_specs=[pl.BlockSpec((1, t_new, D_MODEL), lambda b: (b, 0, 0)),
                   pl.BlockSpec((1, n_state, CONV_CH), lambda b: (b, 0, 0))] + nk_specs,
        out_shape=[jax.ShapeDtypeStruct((dec_batch, t_new, D_MODEL), F32),
                   jax.ShapeDtypeStruct((dec_batch, n_state, CONV_CH), F32)] + nk_shapes,
        scratch_shapes=[pltpu.VMEM((n_state + t_new + 6, CONV_CH), F32)],
        compiler_params=_cparams(1), name="sample_mixers")(
        qkv5, *cache_in, state, u3, conv_w, conv_b[None, :], ln_g[None, :], ln_b[None, :])
    new_caches = [nk.reshape(c.shape) for nk, c in zip((nk0, nk1, nk2), caches)]
    return mix.reshape(dec_batch * t_new, D_MODEL), nconv, new_caches


OUT_TM = 512
OUT_ROWS = 128


def _out_proj_kernel(o0_ref, o1_ref, o2_ref, l0_ref, l1_ref, l2_ref, c_ref, x_ref, w_ref, g_ref, b_ref,
                     mixs_ref, xs_ref, x1_ref, x1bf_ref, x1s_ref, mix_scr, y_scr, *, n_i, alpha):
    i = pl.program_id(0)
    o_refs = (o0_ref, o1_ref, o2_ref)
    chunks = [slice(r0, r0 + OUT_ROWS) for r0 in range(0, OUT_TM, OUT_ROWS)]

    def project(rows):
        ls = [l0_ref[rows, :], l1_ref[rows, :], l2_ref[rows, :]]
        m = jnp.maximum(jnp.maximum(ls[0], ls[1]), ls[2])
        es = [jnp.exp(l - m) for l in ls]
        inv = 1.0 / (es[0] + es[1] + es[2])
        for gi in range(N_GROUPS):
            a = es[gi] * inv
            for h in range(HEADS_PER_GROUP):
                c0 = h * HEAD_DIM
                og = o_refs[gi][rows, c0:c0 + HEAD_DIM].astype(F32)
                mix_scr[rows, gi * GROUP_W + c0:gi * GROUP_W + c0 + HEAD_DIM] = (og * a[:, h:h + 1]).astype(BF16)
        mix_scr[rows, ATTN_WIDTH:] = c_ref[rows, :]
        y_scr[rows, :] = (jnp.dot(mix_scr[rows, :], w_ref[...], preferred_element_type=F32)
                          + alpha * x_ref[rows, :])

    def normalize(rows):
        for r0 in range(rows.start, rows.stop, LN_STRIP):
            strip = slice(r0, r0 + LN_STRIP)
            x1 = _layer_norm_rows(y_scr[strip, :], g_ref[...], b_ref[...])
            x1_ref[strip, :] = x1
            x1bf_ref[strip, :] = x1.astype(BF16)

    project(chunks[0])
    for c in range(1, len(chunks)):
        project(chunks[c])
        normalize(chunks[c - 1])
    normalize(chunks[-1])

    @pl.when(i == n_i - 1)
    def _():
        ys = jnp.dot(mixs_ref[...].astype(BF16), w_ref[...], preferred_element_type=F32) + alpha * xs_ref[...]
        x1s_ref[...] = _layer_norm_rows(ys, g_ref[...], b_ref[...])


def _out_proj(os, lses, c, x2, w_bf, ln_g, ln_b, mixs, xs2, alpha):
    m = x2.shape[0]
    ms = xs2.shape[0]
    n_i = m // OUT_TM
    row = lambda w: pl.BlockSpec((OUT_TM, w), lambda i: (i, 0))
    whole = lambda a: pl.BlockSpec(a.shape, lambda i: (0,) * a.ndim)
    g2, b2 = ln_g[None, :], ln_b[None, :]
    return pl.pallas_call(
        functools.partial(_out_proj_kernel, n_i=n_i, alpha=alpha),
        grid=(n_i,),
        in_specs=[row(GROUP_W)] * 3 + [row(HEAD_DIM)] * 3 + [row(CONV_CH), row(D_MODEL),
                  whole(w_bf), whole(g2), whole(b2), whole(mixs), whole(xs2)],
        out_specs=[row(D_MODEL), row(D_MODEL), pl.BlockSpec((ms, D_MODEL), lambda i: (0, 0))],
        out_shape=[jax.ShapeDtypeStruct((m, D_MODEL), F32), jax.ShapeDtypeStruct((m, D_MODEL), BF16),
                   jax.ShapeDtypeStruct((ms, D_MODEL), F32)],
        scratch_shapes=[pltpu.VMEM((OUT_TM, D_MODEL), BF16), pltpu.VMEM((OUT_TM, D_MODEL), F32)],
        compiler_params=_cparams(1), name="out_proj")(
        *os, *lses, c, x2, w_bf, g2, b2, mixs, xs2)


FFN_TM = 1024
FFN_TF = 512
FFN_ROWS = 1024
FFN_RES_W = 256


def _ffn_kernel(xbf_ref, xres_ref, wg_ref, wu_ref, wd_ref, g_ref, b_ref, xs_ref,
                y_ref, ys_ref, xsbf, *, n_i, n_f, alpha):
    i = pl.program_id(0)
    f = pl.program_id(1)

    def swiglu_down(xb):
        gate = jnp.dot(xb, wg_ref[...], preferred_element_type=F32)
        up = jnp.dot(xb, wu_ref[...], preferred_element_type=F32)
        act = (gate * jax.nn.sigmoid(gate) * up).astype(BF16)
        return jnp.dot(act, wd_ref[...], preferred_element_type=F32)

    @pl.when(f == 0)
    def _():
        y_ref[...] = jnp.zeros_like(y_ref)

    for r0 in range(0, FFN_TM, FFN_ROWS):
        y_ref[r0:r0 + FFN_ROWS, :] += swiglu_down(xbf_ref[r0:r0 + FFN_ROWS, :])

    for c in range(D_MODEL // FFN_RES_W):
        @pl.when(f == c)
        def _(c=c):
            y_ref[:, c * FFN_RES_W:(c + 1) * FFN_RES_W] += alpha * xres_ref[...]

    @pl.when(f == n_f - 1)
    def _():
        y_ref[...] = _layer_norm_rows(y_ref[...], g_ref[...], b_ref[...])

    @pl.when(i == n_i - 1)
    def _():
        @pl.when(f == 0)
        def _():
            xsbf[...] = xs_ref[...].astype(BF16)

        downs = swiglu_down(xsbf[...])

        @pl.when(f == 0)
        def _():
            ys_ref[...] = downs + alpha * xs_ref[...]

        @pl.when(f > 0)
        def _():
            ys_ref[...] += downs

        @pl.when(f == n_f - 1)
        def _():
            ys_ref[...] = _layer_norm_rows(ys_ref[...], g_ref[...], b_ref[...])


def _ffn(x1, x1bf, x1s, wg_bf, wu_bf, wd_bf, ln_g, ln_b, alpha):
    m = x1.shape[0]
    ms = x1s.shape[0]
    hidden = wg_bf.shape[1]
    n_i = m // FFN_TM
    n_f = hidden // FFN_TF
    n_res = D_MODEL // FFN_RES_W
    assert n_f >= n_res
    g2, b2 = ln_g[None, :], ln_b[None, :]
    return pl.pallas_call(
        functools.partial(_ffn_kernel, n_i=n_i, n_f=n_f, alpha=alpha),
        grid=(n_i, n_f),
        in_specs=[pl.BlockSpec((FFN_TM, D_MODEL), lambda i, f: (i, 0)),
                  pl.BlockSpec((FFN_TM, FFN_RES_W), lambda i, f: (i, jnp.minimum(f, n_res - 1))),
                  pl.BlockSpec((D_MODEL, FFN_TF), lambda i, f: (0, f)),
                  pl.BlockSpec((D_MODEL, FFN_TF), lambda i, f: (0, f)),
                  pl.BlockSpec((FFN_TF, D_MODEL), lambda i, f: (f, 0)),
                  pl.BlockSpec((1, D_MODEL), lambda i, f: (0, 0)),
                  pl.BlockSpec((1, D_MODEL), lambda i, f: (0, 0)),
                  pl.BlockSpec((ms, D_MODEL), lambda i, f: (0, 0))],
        out_specs=[pl.BlockSpec((FFN_TM, D_MODEL), lambda i, f: (i, 0)),
                   pl.BlockSpec((ms, D_MODEL), lambda i, f: (0, 0))],
        out_shape=[jax.ShapeDtypeStruct((m, D_MODEL), F32), jax.ShapeDtypeStruct((ms, D_MODEL), F32)],
        scratch_shapes=[pltpu.VMEM((ms, D_MODEL), BF16)],
        compiler_params=_cparams(2), name="ffn")(
        x1bf, x1, wg_bf, wu_bf, wd_bf, g2, b2, x1s)


def kernel(x_prompt, x_sample, cache_kv_w128, cache_kv_w512, cache_kv_w2048, state_conv, w_in, w_out,
           conv_w, conv_b, conv_ln_g, conv_ln_b, ln1_g, ln1_b, w_gate, w_up, w_down, ln2_g, ln2_b):
    depth = w_in.shape[0]
    batch, seq, _ = x_prompt.shape
    dec_batch, t_new, _ = x_sample.shape
    caches = (cache_kv_w128, cache_kv_w512, cache_kv_w2048)
    alpha = (2.0 * depth) ** 0.25

    cos_p, sin_p = _rope_tables(jnp.arange(seq, dtype=jnp.int32))
    pos_s = PAST_LEN + jnp.arange(t_new, dtype=jnp.int32)
    cos_s, sin_s = _rope_tables(jnp.tile(pos_s, dec_batch))

    xp = x_prompt.reshape(batch * seq, D_MODEL)
    xs = x_sample.reshape(dec_batch * t_new, D_MODEL)
    kvp = [[] for _ in range(N_GROUPS)]
    kvs = [[] for _ in range(N_GROUPS)]
    convp, convs = [], []
    for l in range(depth):
        cm0, cm1, cm2, u, kv0, kv1, kv2, qkvs, us = _in_proj(
            xp, xs, w_in[l].astype(BF16), cos_p, sin_p, cos_s, sin_s, batch, seq)
        os, lses = zip(*[_attn_group(cm, gi, batch, seq) for gi, cm in enumerate((cm0, cm1, cm2))])
        conv_steps = batch * seq // CONV_TC
        c, (wo_bf, wg_bf, wu_bf, wd_bf) = _conv_prompt(
            u, conv_w[l], conv_b[l], conv_ln_g[l], conv_ln_b[l], batch, seq,
            [(w, 0, w.shape[0] // conv_steps) for w in (w_out[l], w_gate[l], w_up[l], w_down[l])])
        layer_caches = [cc[l] for cc in caches]
        mixs, nconv_s, new_caches = _sample_mixers(qkvs, us, layer_caches, state_conv[l], conv_w[l], conv_b[l],
                                                   conv_ln_g[l], conv_ln_b[l], dec_batch, t_new)
        x1, x1bf, x1s = _out_proj(os, lses, c, xp, wo_bf, ln1_g[l], ln1_b[l], mixs, xs, alpha)
        xp, xs = _ffn(x1, x1bf, x1s, wg_bf, wu_bf, wd_bf, ln2_g[l], ln2_b[l], alpha)
        for gi, (kv, nk) in enumerate(zip((kv0, kv1, kv2), new_caches)):
            kvp[gi].append(kv.reshape(batch, kv.shape[1], 2, HEADS_PER_GROUP, HEAD_DIM))
            kvs[gi].append(nk)
        convp.append(u.reshape(batch, seq, CONV_CH)[:, seq - (CONV_K - 1):])
        convs.append(nconv_s)

    y_prompt = xp.reshape(batch, seq, D_MODEL)
    y_sample = xs.reshape(dec_batch, t_new, D_MODEL)
    return (y_prompt, y_sample, jnp.stack(kvp[0]), jnp.stack(kvp[1]), jnp.stack(kvp[2]), jnp.stack(convp),
            jnp.stack(kvs[0]), jnp.stack(kvs[1]), jnp.stack(kvs[2]), jnp.stack(convs))
```

```python
import functools

import numpy as np
import jax
import jax.numpy as jnp
from jax import lax
from jax.experimental import pallas as pl
from jax.experimental.pallas import tpu as pltpu

D_MODEL = 2048
HEAD_DIM = 128
CONV_CH = D_MODEL // 4
ATTN_WIDTH = D_MODEL - CONV_CH
DILATION_GROUPS = ((128, 1), (512, 4), (2048, 16))
N_GROUPS = len(DILATION_GROUPS)
HEADS_PER_GROUP = ATTN_WIDTH // HEAD_DIM // N_GROUPS
GROUP_W = HEADS_PER_GROUP * HEAD_DIM
CONV_K = 31
ROPE_THETA = 10000.0
LN_EPS = 1e-5
Q_BLOCK = 128
ATTN_SCALE = HEAD_DIM ** -0.5
NEG_BIG = -1e30
PAST_LEN = 16384

F32 = jnp.float32
BF16 = jnp.bfloat16

VMEM_LIMIT = 56 * 1024 * 1024
LN_STRIP = 16
SUBLANES = 8


def _cparams(n_axes):
    return pltpu.CompilerParams(dimension_semantics=("arbitrary",) * n_axes,
                                vmem_limit_bytes=VMEM_LIMIT)


def _layer_norm_rows(y, g, b):
    mu = jnp.mean(y, axis=-1, keepdims=True)
    yc = y - mu
    var = jnp.mean(yc * yc, axis=-1, keepdims=True)
    return yc * lax.rsqrt(var + LN_EPS) * g + b


def _rope_tables(pos):
    half = HEAD_DIM // 2
    inv = ROPE_THETA ** (-jnp.arange(half, dtype=F32) / half)
    ang = pos.astype(F32)[:, None] * inv[None, :]
    cos, sin = jnp.cos(ang), jnp.sin(ang)
    return jnp.concatenate([cos, cos], axis=1), jnp.concatenate([-sin, sin], axis=1)


def _rope(h, cos, sin):
    parts = []
    for hh in range(HEADS_PER_GROUP):
        hs = h[:, hh * HEAD_DIM:(hh + 1) * HEAD_DIM]
        parts.append(hs * cos + pltpu.roll(hs, HEAD_DIM // 2, axis=1) * sin)
    return jnp.concatenate(parts, axis=1)


IN_TM = 1024
IN_TN = GROUP_W
IN_ROWS = 256


def _kv_window_plan(seq, batch, n_j):
    tiles_per_batch = seq // IN_TM
    n_i = batch * tiles_per_batch
    plans = []
    for gi, (window, _) in enumerate(DILATION_GROUPS):
        keep = min(window, seq)
        rb = min(keep, IN_TM)
        first_tile = (seq - keep) // IN_TM
        row_lo = (seq - keep) - first_tile * IN_TM
        writes = []
        for i in range(n_i):
            b, it = divmod(i, tiles_per_batch)
            if it >= first_tile:
                for c, j in enumerate((N_GROUPS + gi, 2 * N_GROUPS + gi)):
                    writes.append((i * n_j + j, (b, it - first_tile, c)))
        writes.sort()
        tab = np.zeros((n_i * n_j, 3), np.int32)
        w = 0
        for step in range(n_i * n_j):
            while w < len(writes) - 1 and writes[w][0] < step:
                w += 1
            tab[step] = writes[w][1]
        plans.append(dict(keep=keep, rb=rb, first_tile=first_tile, row_lo=row_lo, tab=tab))
    return plans


def _in_proj_kernel(tab_ref, x_ref, w_ref, cos_ref, sin_ref, xs_ref, coss_ref, sins_ref,
                    cm0_ref, cm1_ref, cm2_ref, u_ref, kv0_ref, kv1_ref, kv2_ref, qkvs_ref, us_ref,
                    xbf, a_scr, de_scr, xsbf, as_scr, *, n_i, tiles_per_batch, plans):
    del tab_ref
    i = pl.program_id(0)
    j = pl.program_id(1)
    kv_refs = (kv0_ref, kv1_ref, kv2_ref)
    cm_refs = (cm0_ref, cm1_ref, cm2_ref)

    chunks = [slice(c * IN_ROWS, (c + 1) * IN_ROWS) for c in range(IN_TM // IN_ROWS)]

    def chunk_dot(rows, cast=False):
        if cast:
            xbf[rows, :] = x_ref[rows, :].astype(BF16)
        return jnp.dot(xbf[rows, :], w_ref[...], preferred_element_type=F32)

    def rope_rows(val, rows):
        return _rope(val, cos_ref[rows, :], sin_ref[rows, :])

    def store_class_major(gi, c, val):
        d = DILATION_GROUPS[gi][1]
        ref = cm_refs[gi]
        per = IN_ROWS // d
        if d == 1:
            ref[0, 0, 0, chunks[c], :] = val.astype(BF16)
            return
        for h in range(HEADS_PER_GROUP):
            de_scr[c, h] = val[:, h * HEAD_DIM:(h + 1) * HEAD_DIM]
        for r in range(d):
            for h in range(HEADS_PER_GROUP):
                ref[0, 0, r, c * per:(c + 1) * per, h * HEAD_DIM:(h + 1) * HEAD_DIM] = (
                    de_scr[c, h, pl.ds(r, per, stride=d), :].astype(BF16))

    it = i % tiles_per_batch
    for gi in range(N_GROUPS):
        plan = plans[gi]
        in_window = it >= plan["first_tile"]
        window_rows = slice(plan["row_lo"], plan["row_lo"] + plan["rb"])

        @pl.when(j == gi)
        def _(gi=gi):
            for c, rows in enumerate(chunks):
                store_class_major(gi, c, rope_rows(chunk_dot(rows, cast=(gi == 0)), rows))

        @pl.when(j == N_GROUPS + gi)
        def _(window_rows=window_rows, in_window=in_window, gi=gi):
            for c, rows in enumerate(chunks):
                r = rope_rows(chunk_dot(rows), rows)
                store_class_major(gi, c, r)
                a_scr[rows, :] = r

            @pl.when(in_window)
            def _():
                kv_refs[gi][0] = a_scr[window_rows, :]

        @pl.when(j == 2 * N_GROUPS + gi)
        def _(window_rows=window_rows, in_window=in_window, gi=gi):
            for c, rows in enumerate(chunks):
                acc = chunk_dot(rows)
                store_class_major(gi, c, acc)
                a_scr[rows, :] = acc

            @pl.when(in_window)
            def _():
                kv_refs[gi][0] = a_scr[window_rows, :]

    @pl.when(j == 3 * N_GROUPS)
    def _():
        for rows in chunks:
            a_scr[rows, :] = chunk_dot(rows)

    @pl.when(j == 3 * N_GROUPS + 1)
    def _():
        for rows in chunks:
            u_ref[rows, :] = a_scr[rows, :] * jax.nn.sigmoid(chunk_dot(rows))

    @pl.when(i == n_i - 1)
    def _():
        @pl.when(j == 0)
        def _():
            xsbf[...] = xs_ref[...].astype(BF16)

        accs = jnp.dot(xsbf[...], w_ref[...], preferred_element_type=F32)

        def store_heads(val):
            for h in range(HEADS_PER_GROUP):
                qkvs_ref[:, 0, h, :] = val[:, h * HEAD_DIM:(h + 1) * HEAD_DIM]

        @pl.when(j < 2 * N_GROUPS)
        def _():
            store_heads(_rope(accs, coss_ref[...], sins_ref[...]))

        @pl.when((j >= 2 * N_GROUPS) & (j < 3 * N_GROUPS))
        def _():
            store_heads(accs)

        @pl.when(j == 3 * N_GROUPS)
        def _():
            as_scr[...] = accs

        @pl.when(j == 3 * N_GROUPS + 1)
        def _():
            us_ref[...] = as_scr[...] * jax.nn.sigmoid(accs)


def _in_proj(x2, xs2, w_bf, cos_p, sin_p, cos_s, sin_s, batch, seq):
    m = x2.shape[0]
    ms = xs2.shape[0]
    n_i = m // IN_TM
    n_j = w_bf.shape[1] // IN_TN
    n_qkv = 3 * N_GROUPS
    tiles_per_batch = seq // IN_TM
    plans = _kv_window_plan(seq, batch, n_j)
    tab = jnp.asarray(np.stack([p["tab"] for p in plans]).reshape(-1))
    n_steps = n_i * n_j

    def kv_map(gi):
        def f(i, j, tab_ref):
            base = (gi * n_steps + i * n_j + j) * 3
            return tab_ref[base], tab_ref[base + 1], tab_ref[base + 2]
        return f

    def cm_map(gi):
        def f(i, j, tab_ref):
            plane = (j > gi).astype(jnp.int32) + (j > N_GROUPS + gi).astype(jnp.int32)
            return i // tiles_per_batch, plane, 0, i % tiles_per_batch, 0
        return f

    last = n_i - 1
    in_specs = [
        pl.BlockSpec((IN_TM, D_MODEL), lambda i, j, t: (i, 0)),
        pl.BlockSpec((D_MODEL, IN_TN), lambda i, j, t: (0, j)),
        pl.BlockSpec((IN_TM, HEAD_DIM), lambda i, j, t: (i % tiles_per_batch, 0)),
        pl.BlockSpec((IN_TM, HEAD_DIM), lambda i, j, t: (i % tiles_per_batch, 0)),
        pl.BlockSpec((ms, D_MODEL), lambda i, j, t: (0, 0)),
        pl.BlockSpec((ms, HEAD_DIM), lambda i, j, t: (0, 0)),
        pl.BlockSpec((ms, HEAD_DIM), lambda i, j, t: (0, 0)),
    ]
    out_shape, out_specs = [], []
    for gi, (_, d) in enumerate(DILATION_GROUPS):
        out_shape.append(jax.ShapeDtypeStruct((batch, 3, d, seq // d, GROUP_W), BF16))
        out_specs.append(pl.BlockSpec((1, 1, d, IN_TM // d, GROUP_W), cm_map(gi)))
    out_shape.append(jax.ShapeDtypeStruct((m, CONV_CH), F32))
    out_specs.append(pl.BlockSpec((IN_TM, CONV_CH), lambda i, j, t: (i, 0)))
    for gi, p in enumerate(plans):
        out_shape.append(jax.ShapeDtypeStruct((batch, p["keep"], 2 * GROUP_W), F32))
        out_specs.append(pl.BlockSpec((1, p["rb"], GROUP_W), kv_map(gi)))
    out_shape += [jax.ShapeDtypeStruct((ms, n_qkv, HEADS_PER_GROUP, HEAD_DIM), F32),
                  jax.ShapeDtypeStruct((ms, CONV_CH), F32)]
    out_specs += [
        pl.BlockSpec((ms, 1, HEADS_PER_GROUP, HEAD_DIM),
                     lambda i, j, t: (0, jnp.where(i == last, jnp.minimum(j, n_qkv - 1), 0), 0, 0)),
        pl.BlockSpec((ms, CONV_CH), lambda i, j, t: (0, 0)),
    ]
    grid_spec = pltpu.PrefetchScalarGridSpec(
        num_scalar_prefetch=1, grid=(n_i, n_j), in_specs=in_specs, out_specs=out_specs,
        scratch_shapes=[pltpu.VMEM((IN_TM, D_MODEL), BF16), pltpu.VMEM((IN_TM, CONV_CH), F32),
                        pltpu.VMEM((IN_TM // IN_ROWS, HEADS_PER_GROUP, IN_ROWS, HEAD_DIM), F32),
                        pltpu.VMEM((ms, D_MODEL), BF16), pltpu.VMEM((ms, CONV_CH), F32)])
    kern = functools.partial(_in_proj_kernel, n_i=n_i, tiles_per_batch=tiles_per_batch, plans=plans)
    return pl.pallas_call(kern, grid_spec=grid_spec, out_shape=out_shape,
                          compiler_params=_cparams(2), name="in_proj")(
        tab, x2, w_bf, cos_p, sin_p, xs2, cos_s, sin_s)


ATTN_TP = 2048


def _attn_kernel(q_ref, k_ref, v_ref, o_ref, lse_ref, k_scr, v_scr, o_scr, lse_scr, *, d, lc):
    n = pl.program_id(1)
    n_qb = lc // Q_BLOCK

    @pl.when(n == 0)
    def _():
        k_scr[:, 0:Q_BLOCK, :] = jnp.zeros((d, Q_BLOCK, GROUP_W), BF16)
        v_scr[:, 0:Q_BLOCK, :] = jnp.zeros((d, Q_BLOCK, GROUP_W), BF16)

    k_scr[:, Q_BLOCK:, :] = k_ref[0, 0]
    v_scr[:, Q_BLOCK:, :] = v_ref[0, 0]
    qq = lax.broadcasted_iota(jnp.int32, (Q_BLOCK, 2 * Q_BLOCK), 0)
    kk = lax.broadcasted_iota(jnp.int32, (Q_BLOCK, 2 * Q_BLOCK), 1)
    band = (kk >= qq) & (kk <= qq + Q_BLOCK)
    lane = lax.broadcasted_iota(jnp.int32, (Q_BLOCK, HEAD_DIM), 1)

    def unit(u, carry):
        r = u // n_qb
        qb = u % n_qb
        r0 = pl.multiple_of(qb * Q_BLOCK, Q_BLOCK)
        valid = band & ((kk >= Q_BLOCK) | (n > 0) | (qb > 0))
        if d == 1:
            rows = pl.ds(r0, Q_BLOCK)
        else:
            rows = pl.ds(r0 * d + r, Q_BLOCK, stride=d)
        lse_blk = jnp.zeros((Q_BLOCK, HEAD_DIM), F32)
        for h in range(HEADS_PER_GROUP):
            c0 = h * HEAD_DIM
            q = q_ref[0, 0, r, pl.ds(r0, Q_BLOCK), c0:c0 + HEAD_DIM]
            k = k_scr[r, pl.ds(r0, 2 * Q_BLOCK), c0:c0 + HEAD_DIM]
            v = v_scr[r, pl.ds(r0, 2 * Q_BLOCK), c0:c0 + HEAD_DIM]
            s = lax.dot_general(q, k, (((1,), (1,)), ((), ())), preferred_element_type=F32)
            s = jnp.where(valid, s * ATTN_SCALE, NEG_BIG)
            m = jnp.max(s, axis=1, keepdims=True)
            e = jnp.exp(s - m)
            den = jnp.sum(e, axis=1, keepdims=True)
            o_scr[h, rows, :] = jnp.dot(e.astype(BF16), v, preferred_element_type=F32) / den
            lse_blk = jnp.where(lane == h, m + jnp.log(den), lse_blk)
        lse_scr[rows, :] = lse_blk
        return carry

    lax.fori_loop(0, d * n_qb, unit, 0, unroll=4)
    k_scr[:, 0:Q_BLOCK, :] = k_scr[:, lc:lc + Q_BLOCK, :]
    v_scr[:, 0:Q_BLOCK, :] = v_scr[:, lc:lc + Q_BLOCK, :]
    for h in range(HEADS_PER_GROUP):
        o_ref[:, h * HEAD_DIM:(h + 1) * HEAD_DIM] = o_scr[h].astype(BF16)
    lse_ref[...] = lse_scr[...]


def _attn_group(cm, gi, batch, seq):
    _, d = DILATION_GROUPS[gi]
    lc = ATTN_TP // d
    n_t = seq // ATTN_TP

    def plane(p):
        return pl.BlockSpec((1, 1, d, lc, GROUP_W), lambda b, n: (b, p, 0, n, 0))

    return pl.pallas_call(
        functools.partial(_attn_kernel, d=d, lc=lc),
        grid=(batch, n_t),
        in_specs=[plane(0), plane(1), plane(2)],
        out_specs=[pl.BlockSpec((ATTN_TP, GROUP_W), lambda b, n: (b * n_t + n, 0)),
                   pl.BlockSpec((ATTN_TP, HEAD_DIM), lambda b, n: (b * n_t + n, 0))],
        out_shape=[jax.ShapeDtypeStruct((batch * seq, GROUP_W), BF16),
                   jax.ShapeDtypeStruct((batch * seq, HEAD_DIM), F32)],
        scratch_shapes=[pltpu.VMEM((d, lc + Q_BLOCK, GROUP_W), BF16),
                        pltpu.VMEM((d, lc + Q_BLOCK, GROUP_W), BF16),
                        pltpu.VMEM((HEADS_PER_GROUP, ATTN_TP, HEAD_DIM), F32),
                        pltpu.VMEM((ATTN_TP, HEAD_DIM), F32)],
        compiler_params=_cparams(2), name=f"attn_g{gi}")(cm, cm, cm)


CONV_TC = 256
CONV_HALO = 32
CONV_ROWS = 64


def _conv_kernel(u_ref, halo_ref, w_ref, b_ref, g_ref, beta_ref, *rest, n_t, stream_chunks):
    n_s = len(stream_chunks)
    src_refs, c_ref, dst_refs, hist = rest[:n_s], rest[n_s], rest[n_s + 1:2 * n_s + 1], rest[2 * n_s + 1]
    t = pl.program_id(1)
    step = pl.program_id(0) * n_t + t

    for src, dst, n_chunks in zip(src_refs, dst_refs, stream_chunks):
        @pl.when(step < n_chunks)
        def _(src=src, dst=dst):
            dst[...] = src[...].astype(BF16)

    hist[0:CONV_HALO, :] = jnp.where(t > 0, halo_ref[0], 0.0)
    hist[CONV_HALO:CONV_HALO + CONV_TC, :] = u_ref[0]
    hist[CONV_HALO + CONV_TC:, :] = jnp.zeros((SUBLANES, CONV_CH), F32)
    lead = CONV_HALO - (CONV_K - 1)
    for rc in range(CONV_TC // CONV_ROWS):
        r0 = rc * CONV_ROWS
        acc = jnp.broadcast_to(b_ref[...], (CONV_ROWS, CONV_CH))
        for s in range(SUBLANES):
            grp = None
            for c in range(s, lead + CONV_K, SUBLANES):
                if c < lead:
                    continue
                term = hist[r0 + c - s:r0 + c - s + CONV_ROWS + SUBLANES, :] * w_ref[c - lead:c - lead + 1, :]
                grp = term if grp is None else grp + term
            acc = acc + grp[s:s + CONV_ROWS, :]
        y = _layer_norm_rows(acc, g_ref[...], beta_ref[...])
        c_ref[0, r0:r0 + CONV_ROWS, :] = (y * jax.nn.sigmoid(y)).astype(BF16)


def _conv_prompt(u2, conv_w, conv_b, ln_g, ln_b, batch, seq, cast_streams):
    u3 = u2.reshape(batch, seq, CONV_CH)
    per = CONV_TC // CONV_HALO
    n_t = seq // CONV_TC
    vec = pl.BlockSpec((1, CONV_CH), lambda b, t: (0, 0))
    stream_specs, stream_chunks = [], []
    for w, axis, chunk in cast_streams:
        n_chunks = w.shape[axis] // chunk
        assert w.ndim == 2 and n_chunks * chunk == w.shape[axis] and n_chunks <= batch * n_t
        block = (chunk, w.shape[1]) if axis == 0 else (w.shape[0], chunk)

        def index_map(b, t, axis=axis, n_chunks=n_chunks):
            k = jnp.minimum(b * n_t + t, n_chunks - 1)
            return (k, 0) if axis == 0 else (0, k)

        stream_specs.append(pl.BlockSpec(block, index_map))
        stream_chunks.append(n_chunks)
    outs = pl.pallas_call(
        functools.partial(_conv_kernel, n_t=n_t, stream_chunks=tuple(stream_chunks)),
        grid=(batch, n_t),
        in_specs=[pl.BlockSpec((1, CONV_TC, CONV_CH), lambda b, t: (b, t, 0)),
                  pl.BlockSpec((1, CONV_HALO, CONV_CH), lambda b, t: (b, jnp.maximum(t * per - 1, 0), 0)),
                  pl.BlockSpec((CONV_K, CONV_CH), lambda b, t: (0, 0)), vec, vec, vec] + stream_specs,
        out_specs=[pl.BlockSpec((1, CONV_TC, CONV_CH), lambda b, t: (b, t, 0))] + stream_specs,
        out_shape=[jax.ShapeDtypeStruct((batch, seq, CONV_CH), BF16)]
                  + [jax.ShapeDtypeStruct(w.shape, BF16) for w, _, _ in cast_streams],
        scratch_shapes=[pltpu.VMEM((CONV_HALO + CONV_TC + SUBLANES, CONV_CH), F32)],
        compiler_params=_cparams(2), name="conv_prompt")(
        u3, u3, conv_w, conv_b[None, :], ln_g[None, :], ln_b[None, :], *[w for w, _, _ in cast_streams])
    return outs[0].reshape(batch * seq, CONV_CH), outs[1:]


KV_SLAB = 2 * HEADS_PER_GROUP
SAMPLE_SHIFT_GROUPS = (0, 1)
FFN_SHIFT_GROUP = 2
FFN_SHIFT_ROWS = 2048


def _shift_cache(cref, out_ref, d, t_new):
    slab = d * KV_SLAB
    shift = t_new * KV_SLAB
    n_rows = cref.shape[1]
    if shift % slab == 0:
        s = shift // slab
        out_ref[0, 0:(n_rows - s) * slab] = cref[0, s:n_rows].reshape((n_rows - s) * slab, HEAD_DIM)
    else:
        assert shift < slab and shift % 8 == 0
        out_ref[0, 0:slab - shift] = cref[0, 0, shift:slab]

        def body(ci, carry):
            out_ref[0, pl.ds(pl.multiple_of(ci * slab - shift, 8), slab)] = cref[0, ci]
            return carry

        lax.fori_loop(1, n_rows, body, 0)


def _new_kv_rows(qkv_ref, gi, t_new):
    slabs = []
    for t in range(t_new):
        slabs += [qkv_ref[0, t, N_GROUPS + gi], qkv_ref[0, t, 2 * N_GROUPS + gi]]
    return jnp.concatenate(slabs, axis=0)


def _sample_kernel(qkv_ref, c0_ref, c1_ref, c2_ref, st_ref, u_ref, w_ref, b_ref, g_ref, beta_ref,
                   mix_ref, nconv_ref, *rest, t_new):
    cache_refs = (c0_ref, c1_ref, c2_ref)
    nk_refs = dict(zip(SAMPLE_SHIFT_GROUPS, rest[:-1]))
    hist = rest[-1]
    hp = HEADS_PER_GROUP
    outs = [[None] * N_GROUPS for _ in range(t_new)]
    lses = [[None] * N_GROUPS for _ in range(t_new)]
    for gi, (_, d) in enumerate(DILATION_GROUPS):
        cref = cache_refs[gi]
        n_rows = cref.shape[1]
        row = lax.broadcasted_iota(jnp.int32, (n_rows, hp, 1), 0)
        if gi in nk_refs:
            _shift_cache(cref, nk_refs[gi], d, t_new)
            keep = nk_refs[gi].shape[1] - t_new * KV_SLAB
            nk_refs[gi][0, keep:, :] = _new_kv_rows(qkv_ref, gi, t_new)
        for t in range(t_new):
            s0 = (t % d) * KV_SLAB
            first_row = t // d
            q = qkv_ref[0, t, gi][None]
            kc = cref[0, :, s0:s0 + hp, :]
            vc = cref[0, :, s0 + hp:s0 + 2 * hp, :]
            s_c = jnp.sum(kc * q, axis=2, keepdims=True) * ATTN_SCALE
            if first_row > 0:
                s_c = jnp.where(row >= first_row, s_c, NEG_BIG)
            newest = [t - d * jj for jj in range(t // d + 1)]
            s_n, v_n = [], []
            for tn in newest:
                kn = qkv_ref[0, tn, N_GROUPS + gi][None]
                v_n.append(qkv_ref[0, tn, 2 * N_GROUPS + gi][None])
                s_n.append(jnp.sum(kn * q, axis=2, keepdims=True) * ATTN_SCALE)
            m = jnp.max(s_c, axis=0, keepdims=True)
            for sn in s_n:
                m = jnp.maximum(m, sn)
            e_c = jnp.exp(s_c - m)
            den = jnp.sum(e_c, axis=0, keepdims=True)
            acc = jnp.sum(e_c * vc, axis=0, keepdims=True)
            for sn, v1 in zip(s_n, v_n):
                e_n = jnp.exp(sn - m)
                den = den + e_n
                acc = acc + e_n * v1
            outs[t][gi] = acc / den
            lses[t][gi] = m + jnp.log(den)
    for t in range(t_new):
        ls = lses[t]
        m = functools.reduce(jnp.maximum, ls)
        es = [jnp.exp(l - m) for l in ls]
        tot = functools.reduce(lambda a, b: a + b, es)
        for gi in range(N_GROUPS):
            slab = (outs[t][gi] * (es[gi] / tot))[0]
            for h in range(hp):
                c0 = gi * GROUP_W + h * HEAD_DIM
                mix_ref[0, t:t + 1, c0:c0 + HEAD_DIM] = slab[h:h + 1, :]
    n_state = CONV_K - 1
    hist[0:n_state, :] = st_ref[0]
    hist[n_state:n_state + t_new, :] = u_ref[0]
    acc = jnp.broadcast_to(b_ref[...], (t_new, CONV_CH))
    for k in range(CONV_K):
        acc = acc + hist[k:k + t_new, :] * w_ref[k:k + 1, :]
    y = _layer_norm_rows(acc, g_ref[...], beta_ref[...])
    mix_ref[0, :, ATTN_WIDTH:] = y * jax.nn.sigmoid(y)
    nconv_ref[0] = hist[t_new:t_new + n_state, :]


def _sample_mixers(qkvs, us, caches, state, conv_w, conv_b, ln_g, ln_b, dec_batch, t_new):
    n_qkv = qkvs.shape[1]
    hp = HEADS_PER_GROUP
    qkv5 = qkvs.reshape(dec_batch, t_new, n_qkv, hp, HEAD_DIM)
    u3 = us.reshape(dec_batch, t_new, CONV_CH)
    cache_in, cache_specs, nk_shapes, nk_specs = [], [], [], []
    for gi, ((window, d), cache) in enumerate(zip(DILATION_GROUPS, caches)):
        buf = cache.shape[1]
        assert buf == window and buf % d == 0
        n_rows = buf // d
        cache_in.append(cache.reshape(dec_batch, n_rows, d * KV_SLAB, HEAD_DIM))
        if gi in SAMPLE_SHIFT_GROUPS:
            cache_specs.append(pl.BlockSpec((1, n_rows, d * KV_SLAB, HEAD_DIM), lambda b: (b, 0, 0, 0)))
            nk_shapes.append(jax.ShapeDtypeStruct((dec_batch, buf * KV_SLAB, HEAD_DIM), F32))
            nk_specs.append(pl.BlockSpec((1, buf * KV_SLAB, HEAD_DIM), lambda b: (b, 0, 0)))
        else:
            used = min(d, t_new) * KV_SLAB
            cache_specs.append(pl.BlockSpec((1, n_rows, used, HEAD_DIM), lambda b: (b, 0, 0, 0)))
    vec = pl.BlockSpec((1, CONV_CH), lambda b: (0, 0))
    n_state = CONV_K - 1
    mix, nconv, *nks = pl.pallas_call(
        functools.partial(_sample_kernel, t_new=t_new),
        grid=(dec_batch,),
        in_specs=[pl.BlockSpec((1, t_new, n_qkv, hp, HEAD_DIM), lambda b: (b, 0, 0, 0, 0))] + cache_specs + [
            pl.BlockSpec((1, n_state, CONV_CH), lambda b: (b, 0, 0)),
            pl.BlockSpec((1, t_new, CONV_CH), lambda b: (b, 0, 0)),
            pl.BlockSpec((CONV_K, CONV_CH), lambda b: (0, 0)), vec, vec, vec],
        out_specs=[pl.BlockSpec((1, t_new, D_MODEL), lambda b: (b, 0, 0)),
                   pl.BlockSpec((1, n_state, CONV_CH), lambda b: (b, 0, 0))] + nk_specs,
        out_shape=[jax.ShapeDtypeStruct((dec_batch, t_new, D_MODEL), F32),
                   jax.ShapeDtypeStruct((dec_batch, n_state, CONV_CH), F32)] + nk_shapes,
        scratch_shapes=[pltpu.VMEM((n_state + t_new + 6, CONV_CH), F32)],
        compiler_params=_cparams(1), name="sample_mixers")(
        qkv5, *cache_in, state, u3, conv_w, conv_b[None, :], ln_g[None, :], ln_b[None, :])
    new_caches = {gi: nk.reshape(caches[gi].shape) for gi, nk in zip(SAMPLE_SHIFT_GROUPS, nks)}
    return mix.reshape(dec_batch * t_new, D_MODEL), nconv, new_caches


OUT_TM = 512
OUT_ROWS = 128


def _out_proj_kernel(o0_ref, o1_ref, o2_ref, l0_ref, l1_ref, l2_ref, c_ref, x_ref, w_ref, g_ref, b_ref,
                     mixs_ref, xs_ref, x1_ref, x1bf_ref, x1s_ref, mix_scr, y_scr, *, n_i, alpha):
    i = pl.program_id(0)
    o_refs = (o0_ref, o1_ref, o2_ref)
    chunks = [slice(r0, r0 + OUT_ROWS) for r0 in range(0, OUT_TM, OUT_ROWS)]

    def project(rows):
        ls = [l0_ref[rows, :], l1_ref[rows, :], l2_ref[rows, :]]
        m = jnp.maximum(jnp.maximum(ls[0], ls[1]), ls[2])
        es = [jnp.exp(l - m) for l in ls]
        inv = 1.0 / (es[0] + es[1] + es[2])
        for gi in range(N_GROUPS):
            a = es[gi] * inv
            for h in range(HEADS_PER_GROUP):
                c0 = h * HEAD_DIM
                og = o_refs[gi][rows, c0:c0 + HEAD_DIM].astype(F32)
                mix_scr[rows, gi * GROUP_W + c0:gi * GROUP_W + c0 + HEAD_DIM] = (og * a[:, h:h + 1]).astype(BF16)
        mix_scr[rows, ATTN_WIDTH:] = c_ref[rows, :]
        y_scr[rows, :] = (jnp.dot(mix_scr[rows, :], w_ref[...], preferred_element_type=F32)
                          + alpha * x_ref[rows, :])

    def normalize(rows):
        for r0 in range(rows.start, rows.stop, LN_STRIP):
            strip = slice(r0, r0 + LN_STRIP)
            x1 = _layer_norm_rows(y_scr[strip, :], g_ref[...], b_ref[...])
            x1_ref[strip, :] = x1
            x1bf_ref[strip, :] = x1.astype(BF16)

    project(chunks[0])
    for c in range(1, len(chunks)):
        project(chunks[c])
        normalize(chunks[c - 1])
    normalize(chunks[-1])

    @pl.when(i == n_i - 1)
    def _():
        ys = jnp.dot(mixs_ref[...].astype(BF16), w_ref[...], preferred_element_type=F32) + alpha * xs_ref[...]
        x1s_ref[...] = _layer_norm_rows(ys, g_ref[...], b_ref[...])


def _out_proj(os, lses, c, x2, w_bf, ln_g, ln_b, mixs, xs2, alpha):
    m = x2.shape[0]
    ms = xs2.shape[0]
    n_i = m // OUT_TM
    row = lambda w: pl.BlockSpec((OUT_TM, w), lambda i: (i, 0))
    whole = lambda a: pl.BlockSpec(a.shape, lambda i: (0,) * a.ndim)
    g2, b2 = ln_g[None, :], ln_b[None, :]
    return pl.pallas_call(
        functools.partial(_out_proj_kernel, n_i=n_i, alpha=alpha),
        grid=(n_i,),
        in_specs=[row(GROUP_W)] * 3 + [row(HEAD_DIM)] * 3 + [row(CONV_CH), row(D_MODEL),
                  whole(w_bf), whole(g2), whole(b2), whole(mixs), whole(xs2)],
        out_specs=[row(D_MODEL), row(D_MODEL), pl.BlockSpec((ms, D_MODEL), lambda i: (0, 0))],
        out_shape=[jax.ShapeDtypeStruct((m, D_MODEL), F32), jax.ShapeDtypeStruct((m, D_MODEL), BF16),
                   jax.ShapeDtypeStruct((ms, D_MODEL), F32)],
        scratch_shapes=[pltpu.VMEM((OUT_TM, D_MODEL), BF16), pltpu.VMEM((OUT_TM, D_MODEL), F32)],
        compiler_params=_cparams(1), name="out_proj")(
        *os, *lses, c, x2, w_bf, g2, b2, mixs, xs2)


FFN_TM = 1024
FFN_TF = 512
FFN_ROWS = 1024
FFN_RES_W = 256


def _ffn_kernel(xbf_ref, xres_ref, wg_ref, wu_ref, wd_ref, g_ref, b_ref, xs_ref, ca_ref, cb_ref, qkv_ref,
                y_ref, ys_ref, nk_ref, xsbf, *, n_i, n_f, alpha, t_new, n_chunks, chunks_per_entry):
    i = pl.program_id(0)
    f = pl.program_id(1)

    shift = t_new * KV_SLAB
    rows = ca_ref.shape[1]
    chunk = jnp.minimum(i * n_f + f, n_chunks - 1) % chunks_per_entry
    nk_ref[0, 0:rows - shift, :] = ca_ref[0, shift:rows, :]
    nk_ref[0, rows - shift:rows, :] = jnp.where(chunk == chunks_per_entry - 1,
                                                _new_kv_rows(qkv_ref, FFN_SHIFT_GROUP, t_new), cb_ref[0])

    def swiglu_down(xb):
        gate = jnp.dot(xb, wg_ref[...], preferred_element_type=F32)
        up = jnp.dot(xb, wu_ref[...], preferred_element_type=F32)
        act = (gate * jax.nn.sigmoid(gate) * up).astype(BF16)
        return jnp.dot(act, wd_ref[...], preferred_element_type=F32)

    @pl.when(f == 0)
    def _():
        y_ref[...] = jnp.zeros_like(y_ref)

    for r0 in range(0, FFN_TM, FFN_ROWS):
        y_ref[r0:r0 + FFN_ROWS, :] += swiglu_down(xbf_ref[r0:r0 + FFN_ROWS, :])

    for c in range(D_MODEL // FFN_RES_W):
        @pl.when(f == c)
        def _(c=c):
            y_ref[:, c * FFN_RES_W:(c + 1) * FFN_RES_W] += alpha * xres_ref[...]

    @pl.when(f == n_f - 1)
    def _():
        y_ref[...] = _layer_norm_rows(y_ref[...], g_ref[...], b_ref[...])

    @pl.when(i == n_i - 1)
    def _():
        @pl.when(f == 0)
        def _():
            xsbf[...] = xs_ref[...].astype(BF16)

        downs = swiglu_down(xsbf[...])

        @pl.when(f == 0)
        def _():
            ys_ref[...] = downs + alpha * xs_ref[...]

        @pl.when(f > 0)
        def _():
            ys_ref[...] += downs

        @pl.when(f == n_f - 1)
        def _():
            ys_ref[...] = _layer_norm_rows(ys_ref[...], g_ref[...], b_ref[...])


def _ffn(x1, x1bf, x1s, wg_bf, wu_bf, wd_bf, ln_g, ln_b, alpha, cache, qkvs, t_new):
    m = x1.shape[0]
    ms = x1s.shape[0]
    hidden = wg_bf.shape[1]
    n_i = m // FFN_TM
    n_f = hidden // FFN_TF
    n_res = D_MODEL // FFN_RES_W
    assert n_f >= n_res
    g2, b2 = ln_g[None, :], ln_b[None, :]
    dec_batch = cache.shape[0]
    cache_flat = cache.reshape(dec_batch, -1, HEAD_DIM)
    entry_rows = cache_flat.shape[1]
    shift = t_new * KV_SLAB
    cpe = entry_rows // FFN_SHIFT_ROWS
    n_chunks = dec_batch * cpe
    assert cpe * FFN_SHIFT_ROWS == entry_rows and n_chunks <= n_i * n_f and FFN_SHIFT_ROWS % shift == 0
    qkv5 = qkvs.reshape(dec_batch, t_new, qkvs.shape[1], HEADS_PER_GROUP, HEAD_DIM)

    def chunk_of(i, f):
        c = jnp.minimum(i * n_f + f, n_chunks - 1)
        return c // cpe, c % cpe

    def chunk_map(i, f):
        b, k = chunk_of(i, f)
        return b, k, 0

    def follow_map(i, f):
        b, k = chunk_of(i, f)
        return b, jnp.minimum((k + 1) * (FFN_SHIFT_ROWS // shift), entry_rows // shift - 1), 0

    y, ys, nk = pl.pallas_call(
        functools.partial(_ffn_kernel, n_i=n_i, n_f=n_f, alpha=alpha, t_new=t_new, n_chunks=n_chunks,
                          chunks_per_entry=cpe),
        grid=(n_i, n_f),
        in_specs=[pl.BlockSpec((FFN_TM, D_MODEL), lambda i, f: (i, 0)),
                  pl.BlockSpec((FFN_TM, FFN_RES_W), lambda i, f: (i, jnp.minimum(f, n_res - 1))),
                  pl.BlockSpec((D_MODEL, FFN_TF), lambda i, f: (0, f)),
                  pl.BlockSpec((D_MODEL, FFN_TF), lambda i, f: (0, f)),
                  pl.BlockSpec((FFN_TF, D_MODEL), lambda i, f: (f, 0)),
                  pl.BlockSpec((1, D_MODEL), lambda i, f: (0, 0)),
                  pl.BlockSpec((1, D_MODEL), lambda i, f: (0, 0)),
                  pl.BlockSpec((ms, D_MODEL), lambda i, f: (0, 0)),
                  pl.BlockSpec((1, FFN_SHIFT_ROWS, HEAD_DIM), chunk_map),
                  pl.BlockSpec((1, shift, HEAD_DIM), follow_map),
                  pl.BlockSpec((1,) + qkv5.shape[1:], lambda i, f: (chunk_of(i, f)[0], 0, 0, 0, 0))],
        out_specs=[pl.BlockSpec((FFN_TM, D_MODEL), lambda i, f: (i, 0)),
                   pl.BlockSpec((ms, D_MODEL), lambda i, f: (0, 0)),
                   pl.BlockSpec((1, FFN_SHIFT_ROWS, HEAD_DIM), chunk_map)],
        out_shape=[jax.ShapeDtypeStruct((m, D_MODEL), F32), jax.ShapeDtypeStruct((ms, D_MODEL), F32),
                   jax.ShapeDtypeStruct(cache_flat.shape, F32)],
        scratch_shapes=[pltpu.VMEM((ms, D_MODEL), BF16)],
        compiler_params=_cparams(2), name="ffn")(
        x1bf, x1, wg_bf, wu_bf, wd_bf, g2, b2, x1s, cache_flat, cache_flat, qkv5)
    return y, ys, nk.reshape(cache.shape)


def kernel(x_prompt, x_sample, cache_kv_w128, cache_kv_w512, cache_kv_w2048, state_conv, w_in, w_out,
           conv_w, conv_b, conv_ln_g, conv_ln_b, ln1_g, ln1_b, w_gate, w_up, w_down, ln2_g, ln2_b):
    depth = w_in.shape[0]
    batch, seq, _ = x_prompt.shape
    dec_batch, t_new, _ = x_sample.shape
    caches = (cache_kv_w128, cache_kv_w512, cache_kv_w2048)
    alpha = (2.0 * depth) ** 0.25

    cos_p, sin_p = _rope_tables(jnp.arange(seq, dtype=jnp.int32))
    pos_s = PAST_LEN + jnp.arange(t_new, dtype=jnp.int32)
    cos_s, sin_s = _rope_tables(jnp.tile(pos_s, dec_batch))

    xp = x_prompt.reshape(batch * seq, D_MODEL)
    xs = x_sample.reshape(dec_batch * t_new, D_MODEL)
    kvp = [[] for _ in range(N_GROUPS)]
    kvs = [[] for _ in range(N_GROUPS)]
    convp, convs = [], []
    for l in range(depth):
        cm0, cm1, cm2, u, kv0, kv1, kv2, qkvs, us = _in_proj(
            xp, xs, w_in[l].astype(BF16), cos_p, sin_p, cos_s, sin_s, batch, seq)
        os, lses = zip(*[_attn_group(cm, gi, batch, seq) for gi, cm in enumerate((cm0, cm1, cm2))])
        conv_steps = batch * seq // CONV_TC
        c, (wo_bf, wg_bf, wu_bf, wd_bf) = _conv_prompt(
            u, conv_w[l], conv_b[l], conv_ln_g[l], conv_ln_b[l], batch, seq,
            [(w, 0, w.shape[0] // conv_steps) for w in (w_out[l], w_gate[l], w_up[l], w_down[l])])
        layer_caches = [cc[l] for cc in caches]
        mixs, nconv_s, new_caches = _sample_mixers(qkvs, us, layer_caches, state_conv[l], conv_w[l], conv_b[l],
                                                   conv_ln_g[l], conv_ln_b[l], dec_batch, t_new)
        x1, x1bf, x1s = _out_proj(os, lses, c, xp, wo_bf, ln1_g[l], ln1_b[l], mixs, xs, alpha)
        xp, xs, new_caches[FFN_SHIFT_GROUP] = _ffn(x1, x1bf, x1s, wg_bf, wu_bf, wd_bf, ln2_g[l], ln2_b[l], alpha,
                                                   layer_caches[FFN_SHIFT_GROUP], qkvs, t_new)
        for gi, kv in enumerate((kv0, kv1, kv2)):
            kvp[gi].append(kv.reshape(batch, kv.shape[1], 2, HEADS_PER_GROUP, HEAD_DIM))
            kvs[gi].append(new_caches[gi])
        convp.append(u.reshape(batch, seq, CONV_CH)[:, seq - (CONV_K - 1):])
        convs.append(nconv_s)

    y_prompt = xp.reshape(batch, seq, D_MODEL)
    y_sample = xs.reshape(dec_batch, t_new, D_MODEL)
    return (y_prompt, y_sample, jnp.stack(kvp[0]), jnp.stack(kvp[1]), jnp.stack(kvp[2]), jnp.stack(convp),
            jnp.stack(kvs[0]), jnp.stack(kvs[1]), jnp.stack(kvs[2]), jnp.stack(convs))
```

```python
import functools

import numpy as np
import jax
import jax.numpy as jnp
from jax import lax
from jax.experimental import pallas as pl
from jax.experimental.pallas import tpu as pltpu

D_MODEL = 2048
HEAD_DIM = 128
CONV_CH = D_MODEL // 4
ATTN_WIDTH = D_MODEL - CONV_CH
DILATION_GROUPS = ((128, 1), (512, 4), (2048, 16))
N_GROUPS = len(DILATION_GROUPS)
HEADS_PER_GROUP = ATTN_WIDTH // HEAD_DIM // N_GROUPS
GROUP_W = HEADS_PER_GROUP * HEAD_DIM
CONV_K = 31
ROPE_THETA = 10000.0
LN_EPS = 1e-5
Q_BLOCK = 128
ATTN_SCALE = HEAD_DIM ** -0.5
NEG_BIG = -1e30
PAST_LEN = 16384

F32 = jnp.float32
BF16 = jnp.bfloat16

VMEM_LIMIT = 56 * 1024 * 1024
LN_STRIP = 16
SUBLANES = 8


def _cparams(n_axes):
    return pltpu.CompilerParams(dimension_semantics=("arbitrary",) * n_axes,
                                vmem_limit_bytes=VMEM_LIMIT)


def _layer_norm_rows(y, g, b):
    mu = jnp.mean(y, axis=-1, keepdims=True)
    yc = y - mu
    var = jnp.mean(yc * yc, axis=-1, keepdims=True)
    return yc * lax.rsqrt(var + LN_EPS) * g + b


def _rope_tables(pos):
    half = HEAD_DIM // 2
    inv = ROPE_THETA ** (-jnp.arange(half, dtype=F32) / half)
    ang = pos.astype(F32)[:, None] * inv[None, :]
    cos, sin = jnp.cos(ang), jnp.sin(ang)
    return jnp.concatenate([cos, cos], axis=1), jnp.concatenate([-sin, sin], axis=1)


def _rope(h, cos, sin):
    parts = []
    for hh in range(HEADS_PER_GROUP):
        hs = h[:, hh * HEAD_DIM:(hh + 1) * HEAD_DIM]
        parts.append(hs * cos + pltpu.roll(hs, HEAD_DIM // 2, axis=1) * sin)
    return jnp.concatenate(parts, axis=1)


IN_TM = 512
IN_TN = GROUP_W
IN_ROWS = 256
PH_Q, PH_K, PH_V, PH_GLU = range(4)
N_PHASES = 4


def _kv_window_plan(seq, batch):
    tiles_per_batch = seq // IN_TM
    n_i = batch * tiles_per_batch
    plans = []
    for window, _ in DILATION_GROUPS:
        keep = min(window, seq)
        rb = min(keep, IN_TM)
        first_tile = (seq - keep) // IN_TM
        row_lo = (seq - keep) - first_tile * IN_TM
        writes = []
        for i in range(n_i):
            b, it = divmod(i, tiles_per_batch)
            if it >= first_tile:
                for c, phase in enumerate((PH_K, PH_V)):
                    writes.append((i * N_PHASES + phase, (b, it - first_tile, c)))
        writes.sort()
        tab = np.zeros((n_i * N_PHASES, 3), np.int32)
        w = 0
        for step in range(n_i * N_PHASES):
            while w < len(writes) - 1 and writes[w][0] < step:
                w += 1
            tab[step] = writes[w][1]
        plans.append(dict(keep=keep, rb=rb, row_lo=row_lo, tab=tab))
    return plans


def _in_proj_kernel(tab_ref, x_ref, w_ref, cos_ref, sin_ref, xs_ref, coss_ref, sins_ref,
                    cm0_ref, cm1_ref, cm2_ref, u_ref, kv0_ref, kv1_ref, kv2_ref, qkvs_ref, us_ref,
                    xbf, de_scr, xsbf, *, n_i, plans):
    del tab_ref
    i = pl.program_id(0)
    phase = pl.program_id(1)
    kv_refs = (kv0_ref, kv1_ref, kv2_ref)
    cm_refs = (cm0_ref, cm1_ref, cm2_ref)
    glu_col = 3 * N_GROUPS

    chunks = [slice(c * IN_ROWS, (c + 1) * IN_ROWS) for c in range(IN_TM // IN_ROWS)]

    def col_dot(lhs, col):
        return jnp.dot(lhs, w_ref[:, col * IN_TN:(col + 1) * IN_TN], preferred_element_type=F32)

    def rope_rows(val, rows):
        return _rope(val, cos_ref[rows, :], sin_ref[rows, :])

    def store_class_major(gi, c, val):
        d = DILATION_GROUPS[gi][1]
        ref = cm_refs[gi]
        per = IN_ROWS // d
        if d == 1:
            ref[0, 0, 0, chunks[c], :] = val.astype(BF16)
            return
        for h in range(HEADS_PER_GROUP):
            de_scr[gi, c, h] = val[:, h * HEAD_DIM:(h + 1) * HEAD_DIM]
        for r in range(d):
            for h in range(HEADS_PER_GROUP):
                ref[0, 0, r, c * per:(c + 1) * per, h * HEAD_DIM:(h + 1) * HEAD_DIM] = (
                    de_scr[gi, c, h, pl.ds(r, per, stride=d), :].astype(BF16))

    def store_window_rows(gi, rows, val):
        lo = max(rows.start, plans[gi]["row_lo"])
        hi = min(rows.stop, plans[gi]["row_lo"] + plans[gi]["rb"])
        if lo < hi:
            kv_refs[gi][0, lo - plans[gi]["row_lo"]:hi - plans[gi]["row_lo"], :] = (
                val[lo - rows.start:hi - rows.start, :])

    def store_heads(slot, val):
        for h in range(HEADS_PER_GROUP):
            qkvs_ref[:, slot, h, :] = val[:, h * HEAD_DIM:(h + 1) * HEAD_DIM]

    is_last = i == n_i - 1

    @pl.when(phase == PH_Q)
    def _():
        for c, rows in enumerate(chunks):
            xbf[rows, :] = x_ref[rows, :].astype(BF16)
            for gi in range(N_GROUPS):
                store_class_major(gi, c, rope_rows(col_dot(xbf[rows, :], gi), rows))

        @pl.when(is_last)
        def _():
            xsbf[...] = xs_ref[...].astype(BF16)
            for gi in range(N_GROUPS):
                store_heads(gi, _rope(col_dot(xsbf[...], gi), coss_ref[...], sins_ref[...]))

    @pl.when(phase == PH_K)
    def _():
        for c, rows in enumerate(chunks):
            for gi in range(N_GROUPS):
                r = rope_rows(col_dot(xbf[rows, :], N_GROUPS + gi), rows)
                store_class_major(gi, c, r)
                store_window_rows(gi, rows, r)

        @pl.when(is_last)
        def _():
            for gi in range(N_GROUPS):
                store_heads(gi, _rope(col_dot(xsbf[...], N_GROUPS + gi), coss_ref[...], sins_ref[...]))

    @pl.when(phase == PH_V)
    def _():
        for c, rows in enumerate(chunks):
            for gi in range(N_GROUPS):
                acc = col_dot(xbf[rows, :], 2 * N_GROUPS + gi)
                store_class_major(gi, c, acc)
                store_window_rows(gi, rows, acc)

        @pl.when(is_last)
        def _():
            for gi in range(N_GROUPS):
                store_heads(gi, col_dot(xsbf[...], 2 * N_GROUPS + gi))

    @pl.when(phase == PH_GLU)
    def _():
        for rows in chunks:
            u_ref[rows, :] = col_dot(xbf[rows, :], glu_col) * jax.nn.sigmoid(col_dot(xbf[rows, :], glu_col + 1))

        @pl.when(is_last)
        def _():
            us_ref[...] = col_dot(xsbf[...], glu_col) * jax.nn.sigmoid(col_dot(xsbf[...], glu_col + 1))


def _in_proj(x2, xs2, w_bf, cos_p, sin_p, cos_s, sin_s, batch, seq):
    m = x2.shape[0]
    ms = xs2.shape[0]
    n_i = m // IN_TM
    n_qkv = 3 * N_GROUPS
    assert w_bf.shape[1] == (n_qkv + 2) * IN_TN
    tiles_per_batch = seq // IN_TM
    plans = _kv_window_plan(seq, batch)
    tab = jnp.asarray(np.stack([p["tab"] for p in plans]).reshape(-1))
    n_steps = n_i * N_PHASES

    def kv_map(gi):
        def f(i, p, tab_ref):
            base = (gi * n_steps + i * N_PHASES + p) * 3
            return tab_ref[base], tab_ref[base + 1], tab_ref[base + 2]
        return f

    def cm_map(i, p, tab_ref):
        return i // tiles_per_batch, jnp.minimum(p, PH_V), 0, i % tiles_per_batch, 0

    last = n_i - 1
    in_specs = [
        pl.BlockSpec((IN_TM, D_MODEL), lambda i, j, t: (i, 0)),
        pl.BlockSpec(w_bf.shape, lambda i, j, t: (0, 0)),
        pl.BlockSpec((IN_TM, HEAD_DIM), lambda i, j, t: (i % tiles_per_batch, 0)),
        pl.BlockSpec((IN_TM, HEAD_DIM), lambda i, j, t: (i % tiles_per_batch, 0)),
        pl.BlockSpec((ms, D_MODEL), lambda i, j, t: (0, 0)),
        pl.BlockSpec((ms, HEAD_DIM), lambda i, j, t: (0, 0)),
        pl.BlockSpec((ms, HEAD_DIM), lambda i, j, t: (0, 0)),
    ]
    out_shape, out_specs = [], []
    for gi, (_, d) in enumerate(DILATION_GROUPS):
        out_shape.append(jax.ShapeDtypeStruct((batch, 3, d, seq // d, GROUP_W), BF16))
        out_specs.append(pl.BlockSpec((1, 1, d, IN_TM // d, GROUP_W), cm_map))
    out_shape.append(jax.ShapeDtypeStruct((m, CONV_CH), F32))
    out_specs.append(pl.BlockSpec((IN_TM, CONV_CH), lambda i, j, t: (i, 0)))
    for gi, p in enumerate(plans):
        out_shape.append(jax.ShapeDtypeStruct((batch, p["keep"], 2 * GROUP_W), F32))
        out_specs.append(pl.BlockSpec((1, p["rb"], GROUP_W), kv_map(gi)))
    out_shape += [jax.ShapeDtypeStruct((ms, n_qkv, HEADS_PER_GROUP, HEAD_DIM), F32),
                  jax.ShapeDtypeStruct((ms, CONV_CH), F32)]
    out_specs += [
        pl.BlockSpec((ms, N_GROUPS, HEADS_PER_GROUP, HEAD_DIM),
                     lambda i, p, t: (0, jnp.where(i == last, jnp.minimum(p, PH_V), 0), 0, 0)),
        pl.BlockSpec((ms, CONV_CH), lambda i, j, t: (0, 0)),
    ]
    grid_spec = pltpu.PrefetchScalarGridSpec(
        num_scalar_prefetch=1, grid=(n_i, N_PHASES), in_specs=in_specs, out_specs=out_specs,
        scratch_shapes=[pltpu.VMEM((IN_TM, D_MODEL), BF16),
                        pltpu.VMEM((N_GROUPS, IN_TM // IN_ROWS, HEADS_PER_GROUP, IN_ROWS, HEAD_DIM), F32),
                        pltpu.VMEM((ms, D_MODEL), BF16)])
    kern = functools.partial(_in_proj_kernel, n_i=n_i, plans=plans)
    return pl.pallas_call(kern, grid_spec=grid_spec, out_shape=out_shape,
                          compiler_params=_cparams(2), name="in_proj")(
        tab, x2, w_bf, cos_p, sin_p, xs2, cos_s, sin_s)


ATTN_TP = 2048


def _attn_kernel(q_ref, k_ref, v_ref, o_ref, lse_ref, k_scr, v_scr, o_scr, lse_scr, *, d, lc):
    n = pl.program_id(1)
    n_qb = lc // Q_BLOCK

    @pl.when(n == 0)
    def _():
        k_scr[:, 0:Q_BLOCK, :] = jnp.zeros((d, Q_BLOCK, GROUP_W), BF16)
        v_scr[:, 0:Q_BLOCK, :] = jnp.zeros((d, Q_BLOCK, GROUP_W), BF16)

    k_scr[:, Q_BLOCK:, :] = k_ref[0, 0]
    v_scr[:, Q_BLOCK:, :] = v_ref[0, 0]
    qq = lax.broadcasted_iota(jnp.int32, (Q_BLOCK, 2 * Q_BLOCK), 0)
    kk = lax.broadcasted_iota(jnp.int32, (Q_BLOCK, 2 * Q_BLOCK), 1)
    band = (kk >= qq) & (kk <= qq + Q_BLOCK)
    lane = lax.broadcasted_iota(jnp.int32, (Q_BLOCK, HEAD_DIM), 1)

    def unit(u, carry):
        r = u // n_qb
        qb = u % n_qb
        r0 = pl.multiple_of(qb * Q_BLOCK, Q_BLOCK)
        valid = band & ((kk >= Q_BLOCK) | (n > 0) | (qb > 0))
        if d == 1:
            rows = pl.ds(r0, Q_BLOCK)
        else:
            rows = pl.ds(r0 * d + r, Q_BLOCK, stride=d)
        lse_blk = jnp.zeros((Q_BLOCK, HEAD_DIM), F32)
        for h in range(HEADS_PER_GROUP):
            c0 = h * HEAD_DIM
            q = q_ref[0, 0, r, pl.ds(r0, Q_BLOCK), c0:c0 + HEAD_DIM]
            k = k_scr[r, pl.ds(r0, 2 * Q_BLOCK), c0:c0 + HEAD_DIM]
            v = v_scr[r, pl.ds(r0, 2 * Q_BLOCK), c0:c0 + HEAD_DIM]
            s = lax.dot_general(q, k, (((1,), (1,)), ((), ())), preferred_element_type=F32)
            s = jnp.where(valid, s * ATTN_SCALE, NEG_BIG)
            m = jnp.max(s, axis=1, keepdims=True)
            e = jnp.exp(s - m)
            den = jnp.sum(e, axis=1, keepdims=True)
            o_scr[h, rows, :] = jnp.dot(e.astype(BF16), v, preferred_element_type=F32) / den
            lse_blk = jnp.where(lane == h, m + jnp.log(den), lse_blk)
        lse_scr[rows, :] = lse_blk
        return carry

    lax.fori_loop(0, d * n_qb, unit, 0, unroll=4)
    k_scr[:, 0:Q_BLOCK, :] = k_scr[:, lc:lc + Q_BLOCK, :]
    v_scr[:, 0:Q_BLOCK, :] = v_scr[:, lc:lc + Q_BLOCK, :]
    for h in range(HEADS_PER_GROUP):
        o_ref[:, h * HEAD_DIM:(h + 1) * HEAD_DIM] = o_scr[h].astype(BF16)
    lse_ref[...] = lse_scr[...]


def _attn_group(cm, gi, batch, seq):
    _, d = DILATION_GROUPS[gi]
    lc = ATTN_TP // d
    n_t = seq // ATTN_TP

    def plane(p):
        return pl.BlockSpec((1, 1, d, lc, GROUP_W), lambda b, n: (b, p, 0, n, 0))

    return pl.pallas_call(
        functools.partial(_attn_kernel, d=d, lc=lc),
        grid=(batch, n_t),
        in_specs=[plane(0), plane(1), plane(2)],
        out_specs=[pl.BlockSpec((ATTN_TP, GROUP_W), lambda b, n: (b * n_t + n, 0)),
                   pl.BlockSpec((ATTN_TP, HEAD_DIM), lambda b, n: (b * n_t + n, 0))],
        out_shape=[jax.ShapeDtypeStruct((batch * seq, GROUP_W), BF16),
                   jax.ShapeDtypeStruct((batch * seq, HEAD_DIM), F32)],
        scratch_shapes=[pltpu.VMEM((d, lc + Q_BLOCK, GROUP_W), BF16),
                        pltpu.VMEM((d, lc + Q_BLOCK, GROUP_W), BF16),
                        pltpu.VMEM((HEADS_PER_GROUP, ATTN_TP, HEAD_DIM), F32),
                        pltpu.VMEM((ATTN_TP, HEAD_DIM), F32)],
        compiler_params=_cparams(2), name=f"attn_g{gi}")(cm, cm, cm)


CONV_TC = 256
CONV_HALO = 32
CONV_ROWS = 64


def _conv_kernel(u_ref, halo_ref, w_ref, b_ref, g_ref, beta_ref, *rest, n_t, stream_chunks):
    n_s = len(stream_chunks)
    src_refs, c_ref, dst_refs, hist = rest[:n_s], rest[n_s], rest[n_s + 1:2 * n_s + 1], rest[2 * n_s + 1]
    t = pl.program_id(1)
    step = pl.program_id(0) * n_t + t

    for src, dst, n_chunks in zip(src_refs, dst_refs, stream_chunks):
        @pl.when(step < n_chunks)
        def _(src=src, dst=dst):
            dst[...] = src[...].astype(BF16)

    hist[0:CONV_HALO, :] = jnp.where(t > 0, halo_ref[0], 0.0)
    hist[CONV_HALO:CONV_HALO + CONV_TC, :] = u_ref[0]
    hist[CONV_HALO + CONV_TC:, :] = jnp.zeros((SUBLANES, CONV_CH), F32)
    lead = CONV_HALO - (CONV_K - 1)
    for rc in range(CONV_TC // CONV_ROWS):
        r0 = rc * CONV_ROWS
        acc = jnp.broadcast_to(b_ref[...], (CONV_ROWS, CONV_CH))
        for s in range(SUBLANES):
            grp = None
            for c in range(s, lead + CONV_K, SUBLANES):
                if c < lead:
                    continue
                term = hist[r0 + c - s:r0 + c - s + CONV_ROWS + SUBLANES, :] * w_ref[c - lead:c - lead + 1, :]
                grp = term if grp is None else grp + term
            acc = acc + grp[s:s + CONV_ROWS, :]
        y = _layer_norm_rows(acc, g_ref[...], beta_ref[...])
        c_ref[0, r0:r0 + CONV_ROWS, :] = (y * jax.nn.sigmoid(y)).astype(BF16)


def _conv_prompt(u2, conv_w, conv_b, ln_g, ln_b, batch, seq, cast_streams):
    u3 = u2.reshape(batch, seq, CONV_CH)
    per = CONV_TC // CONV_HALO
    n_t = seq // CONV_TC
    vec = pl.BlockSpec((1, CONV_CH), lambda b, t: (0, 0))
    stream_specs, stream_chunks = [], []
    for w, axis, chunk in cast_streams:
        n_chunks = w.shape[axis] // chunk
        assert w.ndim == 2 and n_chunks * chunk == w.shape[axis] and n_chunks <= batch * n_t
        block = (chunk, w.shape[1]) if axis == 0 else (w.shape[0], chunk)

        def index_map(b, t, axis=axis, n_chunks=n_chunks):
            k = jnp.minimum(b * n_t + t, n_chunks - 1)
            return (k, 0) if axis == 0 else (0, k)

        stream_specs.append(pl.BlockSpec(block, index_map))
        stream_chunks.append(n_chunks)
    outs = pl.pallas_call(
        functools.partial(_conv_kernel, n_t=n_t, stream_chunks=tuple(stream_chunks)),
        grid=(batch, n_t),
        in_specs=[pl.BlockSpec((1, CONV_TC, CONV_CH), lambda b, t: (b, t, 0)),
                  pl.BlockSpec((1, CONV_HALO, CONV_CH), lambda b, t: (b, jnp.maximum(t * per - 1, 0), 0)),
                  pl.BlockSpec((CONV_K, CONV_CH), lambda b, t: (0, 0)), vec, vec, vec] + stream_specs,
        out_specs=[pl.BlockSpec((1, CONV_TC, CONV_CH), lambda b, t: (b, t, 0))] + stream_specs,
        out_shape=[jax.ShapeDtypeStruct((batch, seq, CONV_CH), BF16)]
                  + [jax.ShapeDtypeStruct(w.shape, BF16) for w, _, _ in cast_streams],
        scratch_shapes=[pltpu.VMEM((CONV_HALO + CONV_TC + SUBLANES, CONV_CH), F32)],
        compiler_params=_cparams(2), name="conv_prompt")(
        u3, u3, conv_w, conv_b[None, :], ln_g[None, :], ln_b[None, :], *[w for w, _, _ in cast_streams])
    return outs[0].reshape(batch * seq, CONV_CH), outs[1:]


KV_SLAB = 2 * HEADS_PER_GROUP
SAMPLE_SHIFT_GROUPS = (0, 1)
FFN_SHIFT_GROUP = 2
FFN_SHIFT_ROWS = 2048


def _shift_cache(cref, out_ref, d, t_new):
    slab = d * KV_SLAB
    shift = t_new * KV_SLAB
    n_rows = cref.shape[1]
    if shift % slab == 0:
        s = shift // slab
        out_ref[0, 0:(n_rows - s) * slab] = cref[0, s:n_rows].reshape((n_rows - s) * slab, HEAD_DIM)
    else:
        assert shift < slab and shift % 8 == 0
        out_ref[0, 0:slab - shift] = cref[0, 0, shift:slab]

        def body(ci, carry):
            out_ref[0, pl.ds(pl.multiple_of(ci * slab - shift, 8), slab)] = cref[0, ci]
            return carry

        lax.fori_loop(1, n_rows, body, 0)


def _new_kv_rows(qkv_ref, gi, t_new):
    slabs = []
    for t in range(t_new):
        slabs += [qkv_ref[0, t, N_GROUPS + gi], qkv_ref[0, t, 2 * N_GROUPS + gi]]
    return jnp.concatenate(slabs, axis=0)


def _sample_kernel(qkv_ref, c0_ref, c1_ref, c2_ref, st_ref, u_ref, w_ref, b_ref, g_ref, beta_ref,
                   mix_ref, nconv_ref, *rest, t_new):
    cache_refs = (c0_ref, c1_ref, c2_ref)
    nk_refs = dict(zip(SAMPLE_SHIFT_GROUPS, rest[:-1]))
    hist = rest[-1]
    hp = HEADS_PER_GROUP
    outs = [[None] * N_GROUPS for _ in range(t_new)]
    lses = [[None] * N_GROUPS for _ in range(t_new)]
    for gi, (_, d) in enumerate(DILATION_GROUPS):
        cref = cache_refs[gi]
        n_rows = cref.shape[1]
        row = lax.broadcasted_iota(jnp.int32, (n_rows, hp, 1), 0)
        if gi in nk_refs:
            _shift_cache(cref, nk_refs[gi], d, t_new)
            keep = nk_refs[gi].shape[1] - t_new * KV_SLAB
            nk_refs[gi][0, keep:, :] = _new_kv_rows(qkv_ref, gi, t_new)
        for t in range(t_new):
            s0 = (t % d) * KV_SLAB
            first_row = t // d
            q = qkv_ref[0, t, gi][None]
            kc = cref[0, :, s0:s0 + hp, :]
            vc = cref[0, :, s0 + hp:s0 + 2 * hp, :]
            s_c = jnp.sum(kc * q, axis=2, keepdims=True) * ATTN_SCALE
            if first_row > 0:
                s_c = jnp.where(row >= first_row, s_c, NEG_BIG)
            newest = [t - d * jj for jj in range(t // d + 1)]
            s_n, v_n = [], []
            for tn in newest:
                kn = qkv_ref[0, tn, N_GROUPS + gi][None]
                v_n.append(qkv_ref[0, tn, 2 * N_GROUPS + gi][None])
                s_n.append(jnp.sum(kn * q, axis=2, keepdims=True) * ATTN_SCALE)
            m = jnp.max(s_c, axis=0, keepdims=True)
            for sn in s_n:
                m = jnp.maximum(m, sn)
            e_c = jnp.exp(s_c - m)
            den = jnp.sum(e_c, axis=0, keepdims=True)
            acc = jnp.sum(e_c * vc, axis=0, keepdims=True)
            for sn, v1 in zip(s_n, v_n):
                e_n = jnp.exp(sn - m)
                den = den + e_n
                acc = acc + e_n * v1
            outs[t][gi] = acc / den
            lses[t][gi] = m + jnp.log(den)
    for t in range(t_new):
        ls = lses[t]
        m = functools.reduce(jnp.maximum, ls)
        es = [jnp.exp(l - m) for l in ls]
        tot = functools.reduce(lambda a, b: a + b, es)
        for gi in range(N_GROUPS):
            slab = (outs[t][gi] * (es[gi] / tot))[0]
            for h in range(hp):
                c0 = gi * GROUP_W + h * HEAD_DIM
                mix_ref[0, t:t + 1, c0:c0 + HEAD_DIM] = slab[h:h + 1, :]
    n_state = CONV_K - 1
    hist[0:n_state, :] = st_ref[0]
    hist[n_state:n_state + t_new, :] = u_ref[0]
    acc = jnp.broadcast_to(b_ref[...], (t_new, CONV_CH))
    for k in range(CONV_K):
        acc = acc + hist[k:k + t_new, :] * w_ref[k:k + 1, :]
    y = _layer_norm_rows(acc, g_ref[...], beta_ref[...])
    mix_ref[0, :, ATTN_WIDTH:] = y * jax.nn.sigmoid(y)
    nconv_ref[0] = hist[t_new:t_new + n_state, :]


def _sample_mixers(qkvs, us, caches, state, conv_w, conv_b, ln_g, ln_b, dec_batch, t_new):
    n_qkv = qkvs.shape[1]
    hp = HEADS_PER_GROUP
    qkv5 = qkvs.reshape(dec_batch, t_new, n_qkv, hp, HEAD_DIM)
    u3 = us.reshape(dec_batch, t_new, CONV_CH)
    cache_in, cache_specs, nk_shapes, nk_specs = [], [], [], []
    for gi, ((window, d), cache) in enumerate(zip(DILATION_GROUPS, caches)):
        buf = cache.shape[1]
        assert buf == window and buf % d == 0
        n_rows = buf // d
        cache_in.append(cache.reshape(dec_batch, n_rows, d * KV_SLAB, HEAD_DIM))
        if gi in SAMPLE_SHIFT_GROUPS:
            cache_specs.append(pl.BlockSpec((1, n_rows, d * KV_SLAB, HEAD_DIM), lambda b: (b, 0, 0, 0)))
            nk_shapes.append(jax.ShapeDtypeStruct((dec_batch, buf * KV_SLAB, HEAD_DIM), F32))
            nk_specs.append(pl.BlockSpec((1, buf * KV_SLAB, HEAD_DIM), lambda b: (b, 0, 0)))
        else:
            used = min(d, t_new) * KV_SLAB
            cache_specs.append(pl.BlockSpec((1, n_rows, used, HEAD_DIM), lambda b: (b, 0, 0, 0)))
    vec = pl.BlockSpec((1, CONV_CH), lambda b: (0, 0))
    n_state = CONV_K - 1
    mix, nconv, *nks = pl.pallas_call(
        functools.partial(_sample_kernel, t_new=t_new),
        grid=(dec_batch,),
        in_specs=[pl.BlockSpec((1, t_new, n_qkv, hp, HEAD_DIM), lambda b: (b, 0, 0, 0, 0))] + cache_specs + [
            pl.BlockSpec((1, n_state, CONV_CH), lambda b: (b, 0, 0)),
            pl.BlockSpec((1, t_new, CONV_CH), lambda b: (b, 0, 0)),
            pl.BlockSpec((CONV_K, CONV_CH), lambda b: (0, 0)), vec, vec, vec],
        out_specs=[pl.BlockSpec((1, t_new, D_MODEL), lambda b: (b, 0, 0)),
                   pl.BlockSpec((1, n_state, CONV_CH), lambda b: (b, 0, 0))] + nk_specs,
        out_shape=[jax.ShapeDtypeStruct((dec_batch, t_new, D_MODEL), F32),
                   jax.ShapeDtypeStruct((dec_batch, n_state, CONV_CH), F32)] + nk_shapes,
        scratch_shapes=[pltpu.VMEM((n_state + t_new + 6, CONV_CH), F32)],
        compiler_params=_cparams(1), name="sample_mixers")(
        qkv5, *cache_in, state, u3, conv_w, conv_b[None, :], ln_g[None, :], ln_b[None, :])
    new_caches = {gi: nk.reshape(caches[gi].shape) for gi, nk in zip(SAMPLE_SHIFT_GROUPS, nks)}
    return mix.reshape(dec_batch * t_new, D_MODEL), nconv, new_caches


OUT_TM = 512
OUT_ROWS = 128


def _out_proj_kernel(o0_ref, o1_ref, o2_ref, l0_ref, l1_ref, l2_ref, c_ref, x_ref, w_ref, g_ref, b_ref,
                     mixs_ref, xs_ref, x1_ref, x1bf_ref, x1s_ref, mix_scr, y_scr, *, n_i, alpha):
    i = pl.program_id(0)
    o_refs = (o0_ref, o1_ref, o2_ref)
    chunks = [slice(r0, r0 + OUT_ROWS) for r0 in range(0, OUT_TM, OUT_ROWS)]

    def project(rows):
        ls = [l0_ref[rows, :], l1_ref[rows, :], l2_ref[rows, :]]
        m = jnp.maximum(jnp.maximum(ls[0], ls[1]), ls[2])
        es = [jnp.exp(l - m) for l in ls]
        inv = 1.0 / (es[0] + es[1] + es[2])
        for gi in range(N_GROUPS):
            a = es[gi] * inv
            for h in range(HEADS_PER_GROUP):
                c0 = h * HEAD_DIM
                og = o_refs[gi][rows, c0:c0 + HEAD_DIM].astype(F32)
                mix_scr[rows, gi * GROUP_W + c0:gi * GROUP_W + c0 + HEAD_DIM] = (og * a[:, h:h + 1]).astype(BF16)
        mix_scr[rows, ATTN_WIDTH:] = c_ref[rows, :]
        y_scr[rows, :] = (jnp.dot(mix_scr[rows, :], w_ref[...], preferred_element_type=F32)
                          + alpha * x_ref[rows, :])

    def normalize(rows):
        for r0 in range(rows.start, rows.stop, LN_STRIP):
            strip = slice(r0, r0 + LN_STRIP)
            x1 = _layer_norm_rows(y_scr[strip, :], g_ref[...], b_ref[...])
            x1_ref[strip, :] = x1
            x1bf_ref[strip, :] = x1.astype(BF16)

    project(chunks[0])
    for c in range(1, len(chunks)):
        project(chunks[c])
        normalize(chunks[c - 1])
    normalize(chunks[-1])

    @pl.when(i == n_i - 1)
    def _():
        ys = jnp.dot(mixs_ref[...].astype(BF16), w_ref[...], preferred_element_type=F32) + alpha * xs_ref[...]
        x1s_ref[...] = _layer_norm_rows(ys, g_ref[...], b_ref[...])


def _out_proj(os, lses, c, x2, w_bf, ln_g, ln_b, mixs, xs2, alpha):
    m = x2.shape[0]
    ms = xs2.shape[0]
    n_i = m // OUT_TM
    row = lambda w: pl.BlockSpec((OUT_TM, w), lambda i: (i, 0))
    whole = lambda a: pl.BlockSpec(a.shape, lambda i: (0,) * a.ndim)
    g2, b2 = ln_g[None, :], ln_b[None, :]
    return pl.pallas_call(
        functools.partial(_out_proj_kernel, n_i=n_i, alpha=alpha),
        grid=(n_i,),
        in_specs=[row(GROUP_W)] * 3 + [row(HEAD_DIM)] * 3 + [row(CONV_CH), row(D_MODEL),
                  whole(w_bf), whole(g2), whole(b2), whole(mixs), whole(xs2)],
        out_specs=[row(D_MODEL), row(D_MODEL), pl.BlockSpec((ms, D_MODEL), lambda i: (0, 0))],
        out_shape=[jax.ShapeDtypeStruct((m, D_MODEL), F32), jax.ShapeDtypeStruct((m, D_MODEL), BF16),
                   jax.ShapeDtypeStruct((ms, D_MODEL), F32)],
        scratch_shapes=[pltpu.VMEM((OUT_TM, D_MODEL), BF16), pltpu.VMEM((OUT_TM, D_MODEL), F32)],
        compiler_params=_cparams(1), name="out_proj")(
        *os, *lses, c, x2, w_bf, g2, b2, mixs, xs2)


FFN_TM = 1024
FFN_TF = 512
FFN_ROWS = 1024
FFN_RES_W = 256


def _ffn_kernel(xbf_ref, xres_ref, wg_ref, wu_ref, wd_ref, g_ref, b_ref, xs_ref, ca_ref, cb_ref, qkv_ref,
                y_ref, ys_ref, nk_ref, xsbf, *, n_i, n_f, alpha, t_new, n_chunks, chunks_per_entry):
    i = pl.program_id(0)
    f = pl.program_id(1)

    shift = t_new * KV_SLAB
    rows = ca_ref.shape[1]
    chunk = jnp.minimum(i * n_f + f, n_chunks - 1) % chunks_per_entry
    nk_ref[0, 0:rows - shift, :] = ca_ref[0, shift:rows, :]
    nk_ref[0, rows - shift:rows, :] = jnp.where(chunk == chunks_per_entry - 1,
                                                _new_kv_rows(qkv_ref, FFN_SHIFT_GROUP, t_new), cb_ref[0])

    def swiglu_down(xb):
        gate = jnp.dot(xb, wg_ref[...], preferred_element_type=F32)
        up = jnp.dot(xb, wu_ref[...], preferred_element_type=F32)
        act = (gate * jax.nn.sigmoid(gate) * up).astype(BF16)
        return jnp.dot(act, wd_ref[...], preferred_element_type=F32)

    @pl.when(f == 0)
    def _():
        y_ref[...] = jnp.zeros_like(y_ref)

    for r0 in range(0, FFN_TM, FFN_ROWS):
        y_ref[r0:r0 + FFN_ROWS, :] += swiglu_down(xbf_ref[r0:r0 + FFN_ROWS, :])

    for c in range(D_MODEL // FFN_RES_W):
        @pl.when(f == c)
        def _(c=c):
            y_ref[:, c * FFN_RES_W:(c + 1) * FFN_RES_W] += alpha * xres_ref[...]

    @pl.when(f == n_f - 1)
    def _():
        y_ref[...] = _layer_norm_rows(y_ref[...], g_ref[...], b_ref[...])

    @pl.when(i == n_i - 1)
    def _():
        @pl.when(f == 0)
        def _():
            xsbf[...] = xs_ref[...].astype(BF16)

        downs = swiglu_down(xsbf[...])

        @pl.when(f == 0)
        def _():
            ys_ref[...] = downs + alpha * xs_ref[...]

        @pl.when(f > 0)
        def _():
            ys_ref[...] += downs

        @pl.when(f == n_f - 1)
        def _():
            ys_ref[...] = _layer_norm_rows(ys_ref[...], g_ref[...], b_ref[...])


def _ffn(x1, x1bf, x1s, wg_bf, wu_bf, wd_bf, ln_g, ln_b, alpha, cache, qkvs, t_new):
    m = x1.shape[0]
    ms = x1s.shape[0]
    hidden = wg_bf.shape[1]
    n_i = m // FFN_TM
    n_f = hidden // FFN_TF
    n_res = D_MODEL // FFN_RES_W
    assert n_f >= n_res
    g2, b2 = ln_g[None, :], ln_b[None, :]
    dec_batch = cache.shape[0]
    cache_flat = cache.reshape(dec_batch, -1, HEAD_DIM)
    entry_rows = cache_flat.shape[1]
    shift = t_new * KV_SLAB
    cpe = entry_rows // FFN_SHIFT_ROWS
    n_chunks = dec_batch * cpe
    assert cpe * FFN_SHIFT_ROWS == entry_rows and n_chunks <= n_i * n_f and FFN_SHIFT_ROWS % shift == 0
    qkv5 = qkvs.reshape(dec_batch, t_new, qkvs.shape[1], HEADS_PER_GROUP, HEAD_DIM)

    def chunk_of(i, f):
        c = jnp.minimum(i * n_f + f, n_chunks - 1)
        return c // cpe, c % cpe

    def chunk_map(i, f):
        b, k = chunk_of(i, f)
        return b, k, 0

    def follow_map(i, f):
        b, k = chunk_of(i, f)
        return b, jnp.minimum((k + 1) * (FFN_SHIFT_ROWS // shift), entry_rows // shift - 1), 0

    y, ys, nk = pl.pallas_call(
        functools.partial(_ffn_kernel, n_i=n_i, n_f=n_f, alpha=alpha, t_new=t_new, n_chunks=n_chunks,
                          chunks_per_entry=cpe),
        grid=(n_i, n_f),
        in_specs=[pl.BlockSpec((FFN_TM, D_MODEL), lambda i, f: (i, 0)),
                  pl.BlockSpec((FFN_TM, FFN_RES_W), lambda i, f: (i, jnp.minimum(f, n_res - 1))),
                  pl.BlockSpec((D_MODEL, FFN_TF), lambda i, f: (0, f)),
                  pl.BlockSpec((D_MODEL, FFN_TF), lambda i, f: (0, f)),
                  pl.BlockSpec((FFN_TF, D_MODEL), lambda i, f: (f, 0)),
                  pl.BlockSpec((1, D_MODEL), lambda i, f: (0, 0)),
                  pl.BlockSpec((1, D_MODEL), lambda i, f: (0, 0)),
                  pl.BlockSpec((ms, D_MODEL), lambda i, f: (0, 0)),
                  pl.BlockSpec((1, FFN_SHIFT_ROWS, HEAD_DIM), chunk_map),
                  pl.BlockSpec((1, shift, HEAD_DIM), follow_map),
                  pl.BlockSpec((1,) + qkv5.shape[1:], lambda i, f: (chunk_of(i, f)[0], 0, 0, 0, 0))],
        out_specs=[pl.BlockSpec((FFN_TM, D_MODEL), lambda i, f: (i, 0)),
                   pl.BlockSpec((ms, D_MODEL), lambda i, f: (0, 0)),
                   pl.BlockSpec((1, FFN_SHIFT_ROWS, HEAD_DIM), chunk_map)],
        out_shape=[jax.ShapeDtypeStruct((m, D_MODEL), F32), jax.ShapeDtypeStruct((ms, D_MODEL), F32),
                   jax.ShapeDtypeStruct(cache_flat.shape, F32)],
        scratch_shapes=[pltpu.VMEM((ms, D_MODEL), BF16)],
        compiler_params=_cparams(2), name="ffn")(
        x1bf, x1, wg_bf, wu_bf, wd_bf, g2, b2, x1s, cache_flat, cache_flat, qkv5)
    return y, ys, nk.reshape(cache.shape)


def kernel(x_prompt, x_sample, cache_kv_w128, cache_kv_w512, cache_kv_w2048, state_conv, w_in, w_out,
           conv_w, conv_b, conv_ln_g, conv_ln_b, ln1_g, ln1_b, w_gate, w_up, w_down, ln2_g, ln2_b):
    depth = w_in.shape[0]
    batch, seq, _ = x_prompt.shape
    dec_batch, t_new, _ = x_sample.shape
    caches = (cache_kv_w128, cache_kv_w512, cache_kv_w2048)
    alpha = (2.0 * depth) ** 0.25

    cos_p, sin_p = _rope_tables(jnp.arange(seq, dtype=jnp.int32))
    pos_s = PAST_LEN + jnp.arange(t_new, dtype=jnp.int32)
    cos_s, sin_s = _rope_tables(jnp.tile(pos_s, dec_batch))

    xp = x_prompt.reshape(batch * seq, D_MODEL)
    xs = x_sample.reshape(dec_batch * t_new, D_MODEL)
    kvp = [[] for _ in range(N_GROUPS)]
    kvs = [[] for _ in range(N_GROUPS)]
    convp, convs = [], []
    for l in range(depth):
        cm0, cm1, cm2, u, kv0, kv1, kv2, qkvs, us = _in_proj(
            xp, xs, w_in[l].astype(BF16), cos_p, sin_p, cos_s, sin_s, batch, seq)
        os, lses = zip(*[_attn_group(cm, gi, batch, seq) for gi, cm in enumerate((cm0, cm1, cm2))])
        conv_steps = batch * seq // CONV_TC
        c, (wo_bf, wg_bf, wu_bf, wd_bf) = _conv_prompt(
            u, conv_w[l], conv_b[l], conv_ln_g[l], conv_ln_b[l], batch, seq,
            [(w, 0, w.shape[0] // conv_steps) for w in (w_out[l], w_gate[l], w_up[l], w_down[l])])
        layer_caches = [cc[l] for cc in caches]
        mixs, nconv_s, new_caches = _sample_mixers(qkvs, us, layer_caches, state_conv[l], conv_w[l], conv_b[l],
                                                   conv_ln_g[l], conv_ln_b[l], dec_batch, t_new)
        x1, x1bf, x1s = _out_proj(os, lses, c, xp, wo_bf, ln1_g[l], ln1_b[l], mixs, xs, alpha)
        xp, xs, new_caches[FFN_SHIFT_GROUP] = _ffn(x1, x1bf, x1s, wg_bf, wu_bf, wd_bf, ln2_g[l], ln2_b[l], alpha,
                                                   layer_caches[FFN_SHIFT_GROUP], qkvs, t_new)
        for gi, kv in enumerate((kv0, kv1, kv2)):
            kvp[gi].append(kv.reshape(batch, kv.shape[1], 2, HEADS_PER_GROUP, HEAD_DIM))
            kvs[gi].append(new_caches[gi])
        convp.append(u.reshape(batch, seq, CONV_CH)[:, seq - (CONV_K - 1):])
        convs.append(nconv_s)

    y_prompt = xp.reshape(batch, seq, D_MODEL)
    y_sample = xs.reshape(dec_batch, t_new, D_MODEL)
    return (y_prompt, y_sample, jnp.stack(kvp[0]), jnp.stack(kvp[1]), jnp.stack(kvp[2]), jnp.stack(convp),
            jnp.stack(kvs[0]), jnp.stack(kvs[1]), jnp.stack(kvs[2]), jnp.stack(convs))
```

```python
import functools

import numpy as np
import jax
import jax.numpy as jnp
from jax import lax
from jax.experimental import pallas as pl
from jax.experimental.pallas import tpu as pltpu

D_MODEL = 2048
HEAD_DIM = 128
CONV_CH = D_MODEL // 4
ATTN_WIDTH = D_MODEL - CONV_CH
DILATION_GROUPS = ((128, 1), (512, 4), (2048, 16))
N_GROUPS = len(DILATION_GROUPS)
HEADS_PER_GROUP = ATTN_WIDTH // HEAD_DIM // N_GROUPS
GROUP_W = HEADS_PER_GROUP * HEAD_DIM
CONV_K = 31
ROPE_THETA = 10000.0
LN_EPS = 1e-5
Q_BLOCK = 128
ATTN_SCALE = HEAD_DIM ** -0.5
NEG_BIG = -1e30
PAST_LEN = 16384

F32 = jnp.float32
BF16 = jnp.bfloat16

VMEM_LIMIT = 56 * 1024 * 1024
LN_STRIP = 16
SUBLANES = 8


def _cparams(n_axes):
    return pltpu.CompilerParams(dimension_semantics=("arbitrary",) * n_axes,
                                vmem_limit_bytes=VMEM_LIMIT)


def _layer_norm_rows(y, g, b):
    mu = jnp.mean(y, axis=-1, keepdims=True)
    yc = y - mu
    var = jnp.mean(yc * yc, axis=-1, keepdims=True)
    return yc * lax.rsqrt(var + LN_EPS) * g + b


def _rope_tables(pos):
    half = HEAD_DIM // 2
    inv = ROPE_THETA ** (-np.arange(half, dtype=np.float64) / half)
    ang = np.asarray(pos, np.float64)[:, None] * inv[None, :]
    cos, sin = np.cos(ang), np.sin(ang)
    return (jnp.asarray(np.concatenate([cos, cos], axis=1), F32),
            jnp.asarray(np.concatenate([-sin, sin], axis=1), F32))


def _rope(h, cos, sin):
    parts = []
    for hh in range(HEADS_PER_GROUP):
        hs = h[:, hh * HEAD_DIM:(hh + 1) * HEAD_DIM]
        parts.append(hs * cos + pltpu.roll(hs, HEAD_DIM // 2, axis=1) * sin)
    return jnp.concatenate(parts, axis=1)


IN_TM = 512
IN_TN = GROUP_W
IN_ROWS = 256
PH_Q, PH_K, PH_V, PH_GLU = range(4)
N_PHASES = 4


def _kv_window_plan(seq, batch):
    tiles_per_batch = seq // IN_TM
    n_i = batch * tiles_per_batch
    plans = []
    for window, _ in DILATION_GROUPS:
        keep = min(window, seq)
        rb = min(keep, IN_TM)
        first_tile = (seq - keep) // IN_TM
        row_lo = (seq - keep) - first_tile * IN_TM
        writes = []
        for i in range(n_i):
            b, it = divmod(i, tiles_per_batch)
            if it >= first_tile:
                for c, phase in enumerate((PH_K, PH_V)):
                    writes.append((i * N_PHASES + phase, (b, it - first_tile, c)))
        writes.sort()
        tab = np.zeros((n_i * N_PHASES, 3), np.int32)
        w = 0
        for step in range(n_i * N_PHASES):
            while w < len(writes) - 1 and writes[w][0] < step:
                w += 1
            tab[step] = writes[w][1]
        plans.append(dict(keep=keep, rb=rb, row_lo=row_lo, tab=tab))
    return plans


def _in_proj_kernel(tab_ref, x_ref, w_ref, cos_ref, sin_ref, xs_ref, coss_ref, sins_ref,
                    cm0_ref, cm1_ref, cm2_ref, u_ref, kv0_ref, kv1_ref, kv2_ref, qkvs_ref, us_ref,
                    xbf, de_scr, xsbf, *, n_i, plans):
    del tab_ref
    i = pl.program_id(0)
    phase = pl.program_id(1)
    kv_refs = (kv0_ref, kv1_ref, kv2_ref)
    cm_refs = (cm0_ref, cm1_ref, cm2_ref)
    glu_col = 3 * N_GROUPS

    chunks = [slice(c * IN_ROWS, (c + 1) * IN_ROWS) for c in range(IN_TM // IN_ROWS)]

    def col_dot(lhs, col):
        return jnp.dot(lhs, w_ref[:, col * IN_TN:(col + 1) * IN_TN], preferred_element_type=F32)

    def rope_rows(val, rows):
        return _rope(val, cos_ref[rows, :], sin_ref[rows, :])

    def store_class_major(gi, c, val):
        d = DILATION_GROUPS[gi][1]
        ref = cm_refs[gi]
        per = IN_ROWS // d
        if d == 1:
            ref[0, 0, 0, chunks[c], :] = val.astype(BF16)
            return
        for h in range(HEADS_PER_GROUP):
            de_scr[gi, c, h] = val[:, h * HEAD_DIM:(h + 1) * HEAD_DIM]
        for r in range(d):
            for h in range(HEADS_PER_GROUP):
                ref[0, 0, r, c * per:(c + 1) * per, h * HEAD_DIM:(h + 1) * HEAD_DIM] = (
                    de_scr[gi, c, h, pl.ds(r, per, stride=d), :].astype(BF16))

    def store_window_rows(gi, rows, val):
        lo = max(rows.start, plans[gi]["row_lo"])
        hi = min(rows.stop, plans[gi]["row_lo"] + plans[gi]["rb"])
        if lo < hi:
            kv_refs[gi][0, lo - plans[gi]["row_lo"]:hi - plans[gi]["row_lo"], :] = (
                val[lo - rows.start:hi - rows.start, :])

    def store_heads(slot, val):
        for h in range(HEADS_PER_GROUP):
            qkvs_ref[:, slot, h, :] = val[:, h * HEAD_DIM:(h + 1) * HEAD_DIM]

    is_last = i == n_i - 1

    @pl.when(phase == PH_Q)
    def _():
        for c, rows in enumerate(chunks):
            xbf[rows, :] = x_ref[rows, :].astype(BF16)
            for gi in range(N_GROUPS):
                store_class_major(gi, c, rope_rows(col_dot(xbf[rows, :], gi), rows))

        @pl.when(is_last)
        def _():
            xsbf[...] = xs_ref[...].astype(BF16)
            for gi in range(N_GROUPS):
                store_heads(gi, _rope(col_dot(xsbf[...], gi), coss_ref[...], sins_ref[...]))

    @pl.when(phase == PH_K)
    def _():
        for c, rows in enumerate(chunks):
            for gi in range(N_GROUPS):
                r = rope_rows(col_dot(xbf[rows, :], N_GROUPS + gi), rows)
                store_class_major(gi, c, r)
                store_window_rows(gi, rows, r)

        @pl.when(is_last)
        def _():
            for gi in range(N_GROUPS):
                store_heads(gi, _rope(col_dot(xsbf[...], N_GROUPS + gi), coss_ref[...], sins_ref[...]))

    @pl.when(phase == PH_V)
    def _():
        for c, rows in enumerate(chunks):
            for gi in range(N_GROUPS):
                acc = col_dot(xbf[rows, :], 2 * N_GROUPS + gi)
                store_class_major(gi, c, acc)
                store_window_rows(gi, rows, acc)

        @pl.when(is_last)
        def _():
            for gi in range(N_GROUPS):
                store_heads(gi, col_dot(xsbf[...], 2 * N_GROUPS + gi))

    @pl.when(phase == PH_GLU)
    def _():
        for rows in chunks:
            u_ref[rows, :] = col_dot(xbf[rows, :], glu_col) * jax.nn.sigmoid(col_dot(xbf[rows, :], glu_col + 1))

        @pl.when(is_last)
        def _():
            us_ref[...] = col_dot(xsbf[...], glu_col) * jax.nn.sigmoid(col_dot(xsbf[...], glu_col + 1))


def _in_proj(x2, xs2, w_bf, cos_p, sin_p, cos_s, sin_s, batch, seq):
    m = x2.shape[0]
    ms = xs2.shape[0]
    n_i = m // IN_TM
    n_qkv = 3 * N_GROUPS
    assert w_bf.shape[1] == (n_qkv + 2) * IN_TN
    tiles_per_batch = seq // IN_TM
    plans = _kv_window_plan(seq, batch)
    tab = jnp.asarray(np.stack([p["tab"] for p in plans]).reshape(-1))
    n_steps = n_i * N_PHASES

    def kv_map(gi):
        def f(i, p, tab_ref):
            base = (gi * n_steps + i * N_PHASES + p) * 3
            return tab_ref[base], tab_ref[base + 1], tab_ref[base + 2]
        return f

    def cm_map(i, p, tab_ref):
        return i // tiles_per_batch, jnp.minimum(p, PH_V), 0, i % tiles_per_batch, 0

    last = n_i - 1
    in_specs = [
        pl.BlockSpec((IN_TM, D_MODEL), lambda i, j, t: (i, 0)),
        pl.BlockSpec(w_bf.shape, lambda i, j, t: (0, 0)),
        pl.BlockSpec((IN_TM, HEAD_DIM), lambda i, j, t: (i % tiles_per_batch, 0)),
        pl.BlockSpec((IN_TM, HEAD_DIM), lambda i, j, t: (i % tiles_per_batch, 0)),
        pl.BlockSpec((ms, D_MODEL), lambda i, j, t: (0, 0)),
        pl.BlockSpec((ms, HEAD_DIM), lambda i, j, t: (0, 0)),
        pl.BlockSpec((ms, HEAD_DIM), lambda i, j, t: (0, 0)),
    ]
    out_shape, out_specs = [], []
    for gi, (_, d) in enumerate(DILATION_GROUPS):
        out_shape.append(jax.ShapeDtypeStruct((batch, 3, d, seq // d, GROUP_W), BF16))
        out_specs.append(pl.BlockSpec((1, 1, d, IN_TM // d, GROUP_W), cm_map))
    out_shape.append(jax.ShapeDtypeStruct((m, CONV_CH), F32))
    out_specs.append(pl.BlockSpec((IN_TM, CONV_CH), lambda i, j, t: (i, 0)))
    for gi, p in enumerate(plans):
        out_shape.append(jax.ShapeDtypeStruct((batch, p["keep"], 2 * GROUP_W), F32))
        out_specs.append(pl.BlockSpec((1, p["rb"], GROUP_W), kv_map(gi)))
    out_shape += [jax.ShapeDtypeStruct((ms, n_qkv, HEADS_PER_GROUP, HEAD_DIM), F32),
                  jax.ShapeDtypeStruct((ms, CONV_CH), F32)]
    out_specs += [
        pl.BlockSpec((ms, N_GROUPS, HEADS_PER_GROUP, HEAD_DIM),
                     lambda i, p, t: (0, jnp.where(i == last, jnp.minimum(p, PH_V), 0), 0, 0)),
        pl.BlockSpec((ms, CONV_CH), lambda i, j, t: (0, 0)),
    ]
    grid_spec = pltpu.PrefetchScalarGridSpec(
        num_scalar_prefetch=1, grid=(n_i, N_PHASES), in_specs=in_specs, out_specs=out_specs,
        scratch_shapes=[pltpu.VMEM((IN_TM, D_MODEL), BF16),
                        pltpu.VMEM((N_GROUPS, IN_TM // IN_ROWS, HEADS_PER_GROUP, IN_ROWS, HEAD_DIM), F32),
                        pltpu.VMEM((ms, D_MODEL), BF16)])
    kern = functools.partial(_in_proj_kernel, n_i=n_i, plans=plans)
    return pl.pallas_call(kern, grid_spec=grid_spec, out_shape=out_shape,
                          compiler_params=_cparams(2), name="in_proj")(
        tab, x2, w_bf, cos_p, sin_p, xs2, cos_s, sin_s)


ATTN_TP = 2048


def _attn_kernel(q_ref, k_ref, v_ref, o_ref, lse_ref, k_scr, v_scr, o_scr, lse_scr, *, d, lc):
    n = pl.program_id(1)
    n_qb = lc // Q_BLOCK

    @pl.when(n == 0)
    def _():
        k_scr[:, 0:Q_BLOCK, :] = jnp.zeros((d, Q_BLOCK, GROUP_W), BF16)
        v_scr[:, 0:Q_BLOCK, :] = jnp.zeros((d, Q_BLOCK, GROUP_W), BF16)

    k_scr[:, Q_BLOCK:, :] = k_ref[0, 0]
    v_scr[:, Q_BLOCK:, :] = v_ref[0, 0]
    qq = lax.broadcasted_iota(jnp.int32, (Q_BLOCK, 2 * Q_BLOCK), 0)
    kk = lax.broadcasted_iota(jnp.int32, (Q_BLOCK, 2 * Q_BLOCK), 1)
    band = (kk >= qq) & (kk <= qq + Q_BLOCK)
    lane = lax.broadcasted_iota(jnp.int32, (Q_BLOCK, HEAD_DIM), 1)

    def unit(u, carry):
        r = u // n_qb
        qb = u % n_qb
        r0 = pl.multiple_of(qb * Q_BLOCK, Q_BLOCK)
        valid = band & ((kk >= Q_BLOCK) | (n > 0) | (qb > 0))
        if d == 1:
            rows = pl.ds(r0, Q_BLOCK)
        else:
            rows = pl.ds(r0 * d + r, Q_BLOCK, stride=d)
        lse_blk = jnp.zeros((Q_BLOCK, HEAD_DIM), F32)
        for h in range(HEADS_PER_GROUP):
            c0 = h * HEAD_DIM
            q = q_ref[0, 0, r, pl.ds(r0, Q_BLOCK), c0:c0 + HEAD_DIM]
            k = k_scr[r, pl.ds(r0, 2 * Q_BLOCK), c0:c0 + HEAD_DIM]
            v = v_scr[r, pl.ds(r0, 2 * Q_BLOCK), c0:c0 + HEAD_DIM]
            s = lax.dot_general(q, k, (((1,), (1,)), ((), ())), preferred_element_type=F32)
            s = jnp.where(valid, s * ATTN_SCALE, NEG_BIG)
            m = jnp.max(s, axis=1, keepdims=True)
            e = jnp.exp(s - m)
            den = jnp.sum(e, axis=1, keepdims=True)
            o_scr[h, rows, :] = jnp.dot(e.astype(BF16), v, preferred_element_type=F32) / den
            lse_blk = jnp.where(lane == h, m + jnp.log(den), lse_blk)
        lse_scr[rows, :] = lse_blk
        return carry

    lax.fori_loop(0, d * n_qb, unit, 0, unroll=4)
    k_scr[:, 0:Q_BLOCK, :] = k_scr[:, lc:lc + Q_BLOCK, :]
    v_scr[:, 0:Q_BLOCK, :] = v_scr[:, lc:lc + Q_BLOCK, :]
    for h in range(HEADS_PER_GROUP):
        o_ref[:, h * HEAD_DIM:(h + 1) * HEAD_DIM] = o_scr[h].astype(BF16)
    lse_ref[...] = lse_scr[...]


def _attn_group(cm, gi, batch, seq):
    _, d = DILATION_GROUPS[gi]
    lc = ATTN_TP // d
    n_t = seq // ATTN_TP

    def plane(p):
        return pl.BlockSpec((1, 1, d, lc, GROUP_W), lambda b, n: (b, p, 0, n, 0))

    return pl.pallas_call(
        functools.partial(_attn_kernel, d=d, lc=lc),
        grid=(batch, n_t),
        in_specs=[plane(0), plane(1), plane(2)],
        out_specs=[pl.BlockSpec((ATTN_TP, GROUP_W), lambda b, n: (b * n_t + n, 0)),
                   pl.BlockSpec((ATTN_TP, HEAD_DIM), lambda b, n: (b * n_t + n, 0))],
        out_shape=[jax.ShapeDtypeStruct((batch * seq, GROUP_W), BF16),
                   jax.ShapeDtypeStruct((batch * seq, HEAD_DIM), F32)],
        scratch_shapes=[pltpu.VMEM((d, lc + Q_BLOCK, GROUP_W), BF16),
                        pltpu.VMEM((d, lc + Q_BLOCK, GROUP_W), BF16),
                        pltpu.VMEM((HEADS_PER_GROUP, ATTN_TP, HEAD_DIM), F32),
                        pltpu.VMEM((ATTN_TP, HEAD_DIM), F32)],
        compiler_params=_cparams(2), name=f"attn_g{gi}")(cm, cm, cm)


CONV_TC = 512
CONV_HALO = 32
CONV_ROWS = 64


def _conv_kernel(u_ref, halo_ref, w_ref, b_ref, g_ref, beta_ref, *rest, n_t, stream_chunks):
    n_s = len(stream_chunks)
    src_refs, c_ref, dst_refs, hist = rest[:n_s], rest[n_s], rest[n_s + 1:2 * n_s + 1], rest[2 * n_s + 1]
    t = pl.program_id(1)
    step = pl.program_id(0) * n_t + t

    for src, dst, n_chunks in zip(src_refs, dst_refs, stream_chunks):
        @pl.when(step < n_chunks)
        def _(src=src, dst=dst):
            dst[...] = src[...].astype(BF16)

    hist[0:CONV_HALO, :] = jnp.where(t > 0, halo_ref[0], 0.0)
    hist[CONV_HALO:CONV_HALO + CONV_TC, :] = u_ref[0]
    hist[CONV_HALO + CONV_TC:, :] = jnp.zeros((SUBLANES, CONV_CH), F32)
    lead = CONV_HALO - (CONV_K - 1)
    for rc in range(CONV_TC // CONV_ROWS):
        r0 = rc * CONV_ROWS
        acc = jnp.broadcast_to(b_ref[...], (CONV_ROWS, CONV_CH))
        for s in range(SUBLANES):
            grp = None
            for c in range(s, lead + CONV_K, SUBLANES):
                if c < lead:
                    continue
                term = hist[r0 + c - s:r0 + c - s + CONV_ROWS + SUBLANES, :] * w_ref[c - lead:c - lead + 1, :]
                grp = term if grp is None else grp + term
            acc = acc + grp[s:s + CONV_ROWS, :]
        y = _layer_norm_rows(acc, g_ref[...], beta_ref[...])
        c_ref[0, r0:r0 + CONV_ROWS, :] = (y * jax.nn.sigmoid(y)).astype(BF16)


def _conv_prompt(u2, conv_w, conv_b, ln_g, ln_b, batch, seq, cast_streams):
    u3 = u2.reshape(batch, seq, CONV_CH)
    per = CONV_TC // CONV_HALO
    n_t = seq // CONV_TC
    vec = pl.BlockSpec((1, CONV_CH), lambda b, t: (0, 0))
    stream_specs, stream_chunks = [], []
    for w, axis, chunk in cast_streams:
        n_chunks = w.shape[axis] // chunk
        assert w.ndim == 2 and n_chunks * chunk == w.shape[axis] and n_chunks <= batch * n_t
        block = (chunk, w.shape[1]) if axis == 0 else (w.shape[0], chunk)

        def index_map(b, t, axis=axis, n_chunks=n_chunks):
            k = jnp.minimum(b * n_t + t, n_chunks - 1)
            return (k, 0) if axis == 0 else (0, k)

        stream_specs.append(pl.BlockSpec(block, index_map))
        stream_chunks.append(n_chunks)
    outs = pl.pallas_call(
        functools.partial(_conv_kernel, n_t=n_t, stream_chunks=tuple(stream_chunks)),
        grid=(batch, n_t),
        in_specs=[pl.BlockSpec((1, CONV_TC, CONV_CH), lambda b, t: (b, t, 0)),
                  pl.BlockSpec((1, CONV_HALO, CONV_CH), lambda b, t: (b, jnp.maximum(t * per - 1, 0), 0)),
                  pl.BlockSpec((CONV_K, CONV_CH), lambda b, t: (0, 0)), vec, vec, vec] + stream_specs,
        out_specs=[pl.BlockSpec((1, CONV_TC, CONV_CH), lambda b, t: (b, t, 0))] + stream_specs,
        out_shape=[jax.ShapeDtypeStruct((batch, seq, CONV_CH), BF16)]
                  + [jax.ShapeDtypeStruct(w.shape, BF16) for w, _, _ in cast_streams],
        scratch_shapes=[pltpu.VMEM((CONV_HALO + CONV_TC + SUBLANES, CONV_CH), F32)],
        compiler_params=_cparams(2), name="conv_prompt")(
        u3, u3, conv_w, conv_b[None, :], ln_g[None, :], ln_b[None, :], *[w for w, _, _ in cast_streams])
    return outs[0].reshape(batch * seq, CONV_CH), outs[1:]


KV_SLAB = 2 * HEADS_PER_GROUP
SAMPLE_SHIFT_GROUPS = (0, 1)
FFN_SHIFT_GROUP = 2
FFN_SHIFT_ROWS = 2048


def _shift_cache(cref, out_ref, d, t_new):
    slab = d * KV_SLAB
    shift = t_new * KV_SLAB
    n_rows = cref.shape[1]
    if shift % slab == 0:
        s = shift // slab
        out_ref[0, 0:(n_rows - s) * slab] = cref[0, s:n_rows].reshape((n_rows - s) * slab, HEAD_DIM)
    else:
        assert shift < slab and shift % 8 == 0
        out_ref[0, 0:slab - shift] = cref[0, 0, shift:slab]

        def body(ci, carry):
            out_ref[0, pl.ds(pl.multiple_of(ci * slab - shift, 8), slab)] = cref[0, ci]
            return carry

        lax.fori_loop(1, n_rows, body, 0)


def _new_kv_rows(qkv_ref, gi, t_new):
    slabs = []
    for t in range(t_new):
        slabs += [qkv_ref[0, t, N_GROUPS + gi], qkv_ref[0, t, 2 * N_GROUPS + gi]]
    return jnp.concatenate(slabs, axis=0)


def _sample_kernel(qkv_ref, c0_ref, c1_ref, c2_ref, st_ref, u_ref, w_ref, b_ref, g_ref, beta_ref,
                   mix_ref, nconv_ref, *rest, t_new):
    cache_refs = (c0_ref, c1_ref, c2_ref)
    nk_refs = dict(zip(SAMPLE_SHIFT_GROUPS, rest[:-1]))
    hist = rest[-1]
    hp = HEADS_PER_GROUP
    outs = [[None] * N_GROUPS for _ in range(t_new)]
    lses = [[None] * N_GROUPS for _ in range(t_new)]
    for gi, (_, d) in enumerate(DILATION_GROUPS):
        cref = cache_refs[gi]
        n_rows = cref.shape[1]
        row = lax.broadcasted_iota(jnp.int32, (n_rows, hp, 1), 0)
        if gi in nk_refs:
            _shift_cache(cref, nk_refs[gi], d, t_new)
            keep = nk_refs[gi].shape[1] - t_new * KV_SLAB
            nk_refs[gi][0, keep:, :] = _new_kv_rows(qkv_ref, gi, t_new)
        for t in range(t_new):
            s0 = (t % d) * KV_SLAB
            first_row = t // d
            q = qkv_ref[0, t, gi][None]
            kc = cref[0, :, s0:s0 + hp, :]
            vc = cref[0, :, s0 + hp:s0 + 2 * hp, :]
            s_c = jnp.sum(kc * q, axis=2, keepdims=True) * ATTN_SCALE
            if first_row > 0:
                s_c = jnp.where(row >= first_row, s_c, NEG_BIG)
            newest = [t - d * jj for jj in range(t // d + 1)]
            s_n, v_n = [], []
            for tn in newest:
                kn = qkv_ref[0, tn, N_GROUPS + gi][None]
                v_n.append(qkv_ref[0, tn, 2 * N_GROUPS + gi][None])
                s_n.append(jnp.sum(kn * q, axis=2, keepdims=True) * ATTN_SCALE)
            m = jnp.max(s_c, axis=0, keepdims=True)
            for sn in s_n:
                m = jnp.maximum(m, sn)
            e_c = jnp.exp(s_c - m)
            den = jnp.sum(e_c, axis=0, keepdims=True)
            acc = jnp.sum(e_c * vc, axis=0, keepdims=True)
            for sn, v1 in zip(s_n, v_n):
                e_n = jnp.exp(sn - m)
                den = den + e_n
                acc = acc + e_n * v1
            outs[t][gi] = acc / den
            lses[t][gi] = m + jnp.log(den)
    for t in range(t_new):
        ls = lses[t]
        m = functools.reduce(jnp.maximum, ls)
        es = [jnp.exp(l - m) for l in ls]
        tot = functools.reduce(lambda a, b: a + b, es)
        for gi in range(N_GROUPS):
            slab = (outs[t][gi] * (es[gi] / tot))[0]
            for h in range(hp):
                c0 = gi * GROUP_W + h * HEAD_DIM
                mix_ref[0, t:t + 1, c0:c0 + HEAD_DIM] = slab[h:h + 1, :]
    n_state = CONV_K - 1
    hist[0:n_state, :] = st_ref[0]
    hist[n_state:n_state + t_new, :] = u_ref[0]
    acc = jnp.broadcast_to(b_ref[...], (t_new, CONV_CH))
    for k in range(CONV_K):
        acc = acc + hist[k:k + t_new, :] * w_ref[k:k + 1, :]
    y = _layer_norm_rows(acc, g_ref[...], beta_ref[...])
    mix_ref[0, :, ATTN_WIDTH:] = y * jax.nn.sigmoid(y)
    nconv_ref[0] = hist[t_new:t_new + n_state, :]


def _sample_mixers(qkvs, us, caches, state, conv_w, conv_b, ln_g, ln_b, dec_batch, t_new):
    n_qkv = qkvs.shape[1]
    hp = HEADS_PER_GROUP
    qkv5 = qkvs.reshape(dec_batch, t_new, n_qkv, hp, HEAD_DIM)
    u3 = us.reshape(dec_batch, t_new, CONV_CH)
    cache_in, cache_specs, nk_shapes, nk_specs = [], [], [], []
    for gi, ((window, d), cache) in enumerate(zip(DILATION_GROUPS, caches)):
        buf = cache.shape[1]
        assert buf == window and buf % d == 0
        n_rows = buf // d
        cache_in.append(cache.reshape(dec_batch, n_rows, d * KV_SLAB, HEAD_DIM))
        if gi in SAMPLE_SHIFT_GROUPS:
            cache_specs.append(pl.BlockSpec((1, n_rows, d * KV_SLAB, HEAD_DIM), lambda b: (b, 0, 0, 0)))
            nk_shapes.append(jax.ShapeDtypeStruct((dec_batch, buf * KV_SLAB, HEAD_DIM), F32))
            nk_specs.append(pl.BlockSpec((1, buf * KV_SLAB, HEAD_DIM), lambda b: (b, 0, 0)))
        else:
            used = min(d, t_new) * KV_SLAB
            cache_specs.append(pl.BlockSpec((1, n_rows, used, HEAD_DIM), lambda b: (b, 0, 0, 0)))
    vec = pl.BlockSpec((1, CONV_CH), lambda b: (0, 0))
    n_state = CONV_K - 1
    mix, nconv, *nks = pl.pallas_call(
        functools.partial(_sample_kernel, t_new=t_new),
        grid=(dec_batch,),
        in_specs=[pl.BlockSpec((1, t_new, n_qkv, hp, HEAD_DIM), lambda b: (b, 0, 0, 0, 0))] + cache_specs + [
            pl.BlockSpec((1, n_state, CONV_CH), lambda b: (b, 0, 0)),
            pl.BlockSpec((1, t_new, CONV_CH), lambda b: (b, 0, 0)),
            pl.BlockSpec((CONV_K, CONV_CH), lambda b: (0, 0)), vec, vec, vec],
        out_specs=[pl.BlockSpec((1, t_new, D_MODEL), lambda b: (b, 0, 0)),
                   pl.BlockSpec((1, n_state, CONV_CH), lambda b: (b, 0, 0))] + nk_specs,
        out_shape=[jax.ShapeDtypeStruct((dec_batch, t_new, D_MODEL), F32),
                   jax.ShapeDtypeStruct((dec_batch, n_state, CONV_CH), F32)] + nk_shapes,
        scratch_shapes=[pltpu.VMEM((n_state + t_new + 6, CONV_CH), F32)],
        compiler_params=_cparams(1), name="sample_mixers")(
        qkv5, *cache_in, state, u3, conv_w, conv_b[None, :], ln_g[None, :], ln_b[None, :])
    new_caches = {gi: nk.reshape(caches[gi].shape) for gi, nk in zip(SAMPLE_SHIFT_GROUPS, nks)}
    return mix.reshape(dec_batch * t_new, D_MODEL), nconv, new_caches


OUT_TM = 512
OUT_ROWS = 128


def _out_proj_kernel(o0_ref, o1_ref, o2_ref, l0_ref, l1_ref, l2_ref, c_ref, x_ref, w_ref, g_ref, b_ref,
                     mixs_ref, xs_ref, x1_ref, x1bf_ref, x1s_ref, mix_scr, y_scr, *, n_i, alpha):
    i = pl.program_id(0)
    o_refs = (o0_ref, o1_ref, o2_ref)
    chunks = [slice(r0, r0 + OUT_ROWS) for r0 in range(0, OUT_TM, OUT_ROWS)]

    def project(rows):
        ls = [l0_ref[rows, :], l1_ref[rows, :], l2_ref[rows, :]]
        m = jnp.maximum(jnp.maximum(ls[0], ls[1]), ls[2])
        es = [jnp.exp(l - m) for l in ls]
        inv = 1.0 / (es[0] + es[1] + es[2])
        for gi in range(N_GROUPS):
            a = es[gi] * inv
            for h in range(HEADS_PER_GROUP):
                c0 = h * HEAD_DIM
                og = o_refs[gi][rows, c0:c0 + HEAD_DIM].astype(F32)
                mix_scr[rows, gi * GROUP_W + c0:gi * GROUP_W + c0 + HEAD_DIM] = (og * a[:, h:h + 1]).astype(BF16)
        mix_scr[rows, ATTN_WIDTH:] = c_ref[rows, :]
        y_scr[rows, :] = (jnp.dot(mix_scr[rows, :], w_ref[...], preferred_element_type=F32)
                          + alpha * x_ref[rows, :])

    def normalize(rows):
        for r0 in range(rows.start, rows.stop, LN_STRIP):
            strip = slice(r0, r0 + LN_STRIP)
            x1 = _layer_norm_rows(y_scr[strip, :], g_ref[...], b_ref[...])
            x1_ref[strip, :] = x1
            x1bf_ref[strip, :] = x1.astype(BF16)

    project(chunks[0])
    for c in range(1, len(chunks)):
        project(chunks[c])
        normalize(chunks[c - 1])
    normalize(chunks[-1])

    @pl.when(i == n_i - 1)
    def _():
        ys = jnp.dot(mixs_ref[...].astype(BF16), w_ref[...], preferred_element_type=F32) + alpha * xs_ref[...]
        x1s_ref[...] = _layer_norm_rows(ys, g_ref[...], b_ref[...])


def _out_proj(os, lses, c, x2, w_bf, ln_g, ln_b, mixs, xs2, alpha):
    m = x2.shape[0]
    ms = xs2.shape[0]
    n_i = m // OUT_TM
    row = lambda w: pl.BlockSpec((OUT_TM, w), lambda i: (i, 0))
    whole = lambda a: pl.BlockSpec(a.shape, lambda i: (0,) * a.ndim)
    g2, b2 = ln_g[None, :], ln_b[None, :]
    return pl.pallas_call(
        functools.partial(_out_proj_kernel, n_i=n_i, alpha=alpha),
        grid=(n_i,),
        in_specs=[row(GROUP_W)] * 3 + [row(HEAD_DIM)] * 3 + [row(CONV_CH), row(D_MODEL),
                  whole(w_bf), whole(g2), whole(b2), whole(mixs), whole(xs2)],
        out_specs=[row(D_MODEL), row(D_MODEL), pl.BlockSpec((ms, D_MODEL), lambda i: (0, 0))],
        out_shape=[jax.ShapeDtypeStruct((m, D_MODEL), F32), jax.ShapeDtypeStruct((m, D_MODEL), BF16),
                   jax.ShapeDtypeStruct((ms, D_MODEL), F32)],
        scratch_shapes=[pltpu.VMEM((OUT_TM, D_MODEL), BF16), pltpu.VMEM((OUT_TM, D_MODEL), F32)],
        compiler_params=_cparams(1), name="out_proj")(
        *os, *lses, c, x2, w_bf, g2, b2, mixs, xs2)


FFN_TM = 1024
FFN_TF = 512
FFN_ROWS = 1024
FFN_RES_W = 256


def _ffn_kernel(xbf_ref, xres_ref, wg_ref, wu_ref, wd_ref, g_ref, b_ref, xs_ref, ca_ref, cb_ref, qkv_ref,
                y_ref, ys_ref, nk_ref, xsbf, *, n_i, n_f, alpha, t_new, n_chunks, chunks_per_entry):
    i = pl.program_id(0)
    f = pl.program_id(1)

    shift = t_new * KV_SLAB
    rows = ca_ref.shape[1]
    chunk = jnp.minimum(i * n_f + f, n_chunks - 1) % chunks_per_entry
    nk_ref[0, 0:rows - shift, :] = ca_ref[0, shift:rows, :]
    nk_ref[0, rows - shift:rows, :] = jnp.where(chunk == chunks_per_entry - 1,
                                                _new_kv_rows(qkv_ref, FFN_SHIFT_GROUP, t_new), cb_ref[0])

    def swiglu_down(xb):
        gate = jnp.dot(xb, wg_ref[...], preferred_element_type=F32)
        up = jnp.dot(xb, wu_ref[...], preferred_element_type=F32)
        act = (gate * jax.nn.sigmoid(gate) * up).astype(BF16)
        return jnp.dot(act, wd_ref[...], preferred_element_type=F32)

    @pl.when(f == 0)
    def _():
        y_ref[...] = jnp.zeros_like(y_ref)

    for r0 in range(0, FFN_TM, FFN_ROWS):
        y_ref[r0:r0 + FFN_ROWS, :] += swiglu_down(xbf_ref[r0:r0 + FFN_ROWS, :])

    for c in range(D_MODEL // FFN_RES_W):
        @pl.when(f == c)
        def _(c=c):
            y_ref[:, c * FFN_RES_W:(c + 1) * FFN_RES_W] += alpha * xres_ref[...]

    @pl.when(f == n_f - 1)
    def _():
        y_ref[...] = _layer_norm_rows(y_ref[...], g_ref[...], b_ref[...])

    @pl.when(i == n_i - 1)
    def _():
        @pl.when(f == 0)
        def _():
            xsbf[...] = xs_ref[...].astype(BF16)

        downs = swiglu_down(xsbf[...])

        @pl.when(f == 0)
        def _():
            ys_ref[...] = downs + alpha * xs_ref[...]

        @pl.when(f > 0)
        def _():
            ys_ref[...] += downs

        @pl.when(f == n_f - 1)
        def _():
            ys_ref[...] = _layer_norm_rows(ys_ref[...], g_ref[...], b_ref[...])


def _ffn(x1, x1bf, x1s, wg_bf, wu_bf, wd_bf, ln_g, ln_b, alpha, cache, qkvs, t_new):
    m = x1.shape[0]
    ms = x1s.shape[0]
    hidden = wg_bf.shape[1]
    n_i = m // FFN_TM
    n_f = hidden // FFN_TF
    n_res = D_MODEL // FFN_RES_W
    assert n_f >= n_res
    g2, b2 = ln_g[None, :], ln_b[None, :]
    dec_batch = cache.shape[0]
    cache_flat = cache.reshape(dec_batch, -1, HEAD_DIM)
    entry_rows = cache_flat.shape[1]
    shift = t_new * KV_SLAB
    cpe = entry_rows // FFN_SHIFT_ROWS
    n_chunks = dec_batch * cpe
    assert cpe * FFN_SHIFT_ROWS == entry_rows and n_chunks <= n_i * n_f and FFN_SHIFT_ROWS % shift == 0
    qkv5 = qkvs.reshape(dec_batch, t_new, qkvs.shape[1], HEADS_PER_GROUP, HEAD_DIM)

    def chunk_of(i, f):
        c = jnp.minimum(i * n_f + f, n_chunks - 1)
        return c // cpe, c % cpe

    def chunk_map(i, f):
        b, k = chunk_of(i, f)
        return b, k, 0

    def follow_map(i, f):
        b, k = chunk_of(i, f)
        return b, jnp.minimum((k + 1) * (FFN_SHIFT_ROWS // shift), entry_rows // shift - 1), 0

    y, ys, nk = pl.pallas_call(
        functools.partial(_ffn_kernel, n_i=n_i, n_f=n_f, alpha=alpha, t_new=t_new, n_chunks=n_chunks,
                          chunks_per_entry=cpe),
        grid=(n_i, n_f),
        in_specs=[pl.BlockSpec((FFN_TM, D_MODEL), lambda i, f: (i, 0)),
                  pl.BlockSpec((FFN_TM, FFN_RES_W), lambda i, f: (i, jnp.minimum(f, n_res - 1))),
                  pl.BlockSpec((D_MODEL, FFN_TF), lambda i, f: (0, f)),
                  pl.BlockSpec((D_MODEL, FFN_TF), lambda i, f: (0, f)),
                  pl.BlockSpec((FFN_TF, D_MODEL), lambda i, f: (f, 0)),
                  pl.BlockSpec((1, D_MODEL), lambda i, f: (0, 0)),
                  pl.BlockSpec((1, D_MODEL), lambda i, f: (0, 0)),
                  pl.BlockSpec((ms, D_MODEL), lambda i, f: (0, 0)),
                  pl.BlockSpec((1, FFN_SHIFT_ROWS, HEAD_DIM), chunk_map),
                  pl.BlockSpec((1, shift, HEAD_DIM), follow_map),
                  pl.BlockSpec((1,) + qkv5.shape[1:], lambda i, f: (chunk_of(i, f)[0], 0, 0, 0, 0))],
        out_specs=[pl.BlockSpec((FFN_TM, D_MODEL), lambda i, f: (i, 0)),
                   pl.BlockSpec((ms, D_MODEL), lambda i, f: (0, 0)),
                   pl.BlockSpec((1, FFN_SHIFT_ROWS, HEAD_DIM), chunk_map)],
        out_shape=[jax.ShapeDtypeStruct((m, D_MODEL), F32), jax.ShapeDtypeStruct((ms, D_MODEL), F32),
                   jax.ShapeDtypeStruct(cache_flat.shape, F32)],
        scratch_shapes=[pltpu.VMEM((ms, D_MODEL), BF16)],
        compiler_params=_cparams(2), name="ffn")(
        x1bf, x1, wg_bf, wu_bf, wd_bf, g2, b2, x1s, cache_flat, cache_flat, qkv5)
    return y, ys, nk.reshape(cache.shape)


def kernel(x_prompt, x_sample, cache_kv_w128, cache_kv_w512, cache_kv_w2048, state_conv, w_in, w_out,
           conv_w, conv_b, conv_ln_g, conv_ln_b, ln1_g, ln1_b, w_gate, w_up, w_down, ln2_g, ln2_b):
    depth = w_in.shape[0]
    batch, seq, _ = x_prompt.shape
    dec_batch, t_new, _ = x_sample.shape
    caches = (cache_kv_w128, cache_kv_w512, cache_kv_w2048)
    alpha = (2.0 * depth) ** 0.25

    cos_p, sin_p = _rope_tables(np.arange(seq))
    cos_s, sin_s = _rope_tables(np.tile(PAST_LEN + np.arange(t_new), dec_batch))

    xp = x_prompt.reshape(batch * seq, D_MODEL)
    xs = x_sample.reshape(dec_batch * t_new, D_MODEL)
    kvp = [[] for _ in range(N_GROUPS)]
    kvs = [[] for _ in range(N_GROUPS)]
    convp, convs = [], []
    for l in range(depth):
        cm0, cm1, cm2, u, kv0, kv1, kv2, qkvs, us = _in_proj(
            xp, xs, w_in[l].astype(BF16), cos_p, sin_p, cos_s, sin_s, batch, seq)
        os, lses = zip(*[_attn_group(cm, gi, batch, seq) for gi, cm in enumerate((cm0, cm1, cm2))])
        conv_steps = batch * seq // CONV_TC
        c, (wo_bf, wg_bf, wu_bf, wd_bf) = _conv_prompt(
            u, conv_w[l], conv_b[l], conv_ln_g[l], conv_ln_b[l], batch, seq,
            [(w, 0, w.shape[0] // conv_steps) for w in (w_out[l], w_gate[l], w_up[l], w_down[l])])
        layer_caches = [cc[l] for cc in caches]
        mixs, nconv_s, new_caches = _sample_mixers(qkvs, us, layer_caches, state_conv[l], conv_w[l], conv_b[l],
                                                   conv_ln_g[l], conv_ln_b[l], dec_batch, t_new)
        x1, x1bf, x1s = _out_proj(os, lses, c, xp, wo_bf, ln1_g[l], ln1_b[l], mixs, xs, alpha)
        xp, xs, new_caches[FFN_SHIFT_GROUP] = _ffn(x1, x1bf, x1s, wg_bf, wu_bf, wd_bf, ln2_g[l], ln2_b[l], alpha,
                                                   layer_caches[FFN_SHIFT_GROUP], qkvs, t_new)
        for gi, kv in enumerate((kv0, kv1, kv2)):
            kvp[gi].append(kv.reshape(batch, kv.shape[1], 2, HEADS_PER_GROUP, HEAD_DIM))
            kvs[gi].append(new_caches[gi])
        convp.append(u.reshape(batch, seq, CONV_CH)[:, seq - (CONV_K - 1):])
        convs.append(nconv_s)

    y_prompt = xp.reshape(batch, seq, D_MODEL)
    y_sample = xs.reshape(dec_batch, t_new, D_MODEL)
    return (y_prompt, y_sample, jnp.stack(kvp[0]), jnp.stack(kvp[1]), jnp.stack(kvp[2]), jnp.stack(convp),
            jnp.stack(kvs[0]), jnp.stack(kvs[1]), jnp.stack(kvs[2]), jnp.stack(convs))
```

```python
import functools

import numpy as np
import jax
import jax.numpy as jnp
from jax import lax
from jax.experimental import pallas as pl
from jax.experimental.pallas import tpu as pltpu

D_MODEL = 2048
HEAD_DIM = 128
CONV_CH = D_MODEL // 4
ATTN_WIDTH = D_MODEL - CONV_CH
DILATION_GROUPS = ((128, 1), (512, 4), (2048, 16))
N_GROUPS = len(DILATION_GROUPS)
HEADS_PER_GROUP = ATTN_WIDTH // HEAD_DIM // N_GROUPS
GROUP_W = HEADS_PER_GROUP * HEAD_DIM
CONV_K = 31
ROPE_THETA = 10000.0
LN_EPS = 1e-5
Q_BLOCK = 128
ATTN_SCALE = HEAD_DIM ** -0.5
NEG_BIG = -1e30
PAST_LEN = 16384

F32 = jnp.float32
BF16 = jnp.bfloat16

VMEM_LIMIT = 56 * 1024 * 1024
LN_STRIP = 16
SUBLANES = 8


def _cparams(n_axes):
    return pltpu.CompilerParams(dimension_semantics=("arbitrary",) * n_axes,
                                vmem_limit_bytes=VMEM_LIMIT)


def _layer_norm_rows(y, g, b):
    mu = jnp.mean(y, axis=-1, keepdims=True)
    yc = y - mu
    var = jnp.mean(yc * yc, axis=-1, keepdims=True)
    return yc * lax.rsqrt(var + LN_EPS) * g + b


def _rope_tables(pos):
    half = HEAD_DIM // 2
    inv = ROPE_THETA ** (-np.arange(half, dtype=np.float64) / half)
    ang = np.asarray(pos, np.float64)[:, None] * inv[None, :]
    cos, sin = np.cos(ang), np.sin(ang)
    return (jnp.asarray(np.concatenate([cos, cos], axis=1), F32),
            jnp.asarray(np.concatenate([-sin, sin], axis=1), F32))


def _rope(h, cos, sin):
    parts = []
    for hh in range(HEADS_PER_GROUP):
        hs = h[:, hh * HEAD_DIM:(hh + 1) * HEAD_DIM]
        parts.append(hs * cos + pltpu.roll(hs, HEAD_DIM // 2, axis=1) * sin)
    return jnp.concatenate(parts, axis=1)


IN_TM = 512
IN_TN = GROUP_W
IN_ROWS = 256
PH_Q, PH_K, PH_V, PH_GLU = range(4)
N_PHASES = 4


def _kv_window_plan(seq, batch):
    tiles_per_batch = seq // IN_TM
    n_i = batch * tiles_per_batch
    plans = []
    for window, _ in DILATION_GROUPS:
        keep = min(window, seq)
        rb = min(keep, IN_TM)
        first_tile = (seq - keep) // IN_TM
        row_lo = (seq - keep) - first_tile * IN_TM
        writes = []
        for i in range(n_i):
            b, it = divmod(i, tiles_per_batch)
            if it >= first_tile:
                for phase in (PH_K, PH_V):
                    writes.append((i * N_PHASES + phase, (b, it - first_tile)))
        writes.sort()
        tab = np.zeros((n_i * N_PHASES, 2), np.int32)
        w = 0
        for step in range(n_i * N_PHASES):
            while w < len(writes) - 1 and writes[w][0] < step:
                w += 1
            tab[step] = writes[w][1]
        plans.append(dict(keep=keep, rb=rb, row_lo=row_lo, tab=tab))
    return plans


def _in_proj_kernel(tab_ref, x_ref, w_ref, cos_ref, sin_ref, xs_ref, coss_ref, sins_ref,
                    cm0_ref, cm1_ref, cm2_ref, u_ref, kv0_ref, kv1_ref, kv2_ref, qkvs_ref, us_ref,
                    xbf, de_scr, xsbf, *, n_i, plans):
    del tab_ref
    i = pl.program_id(0)
    phase = pl.program_id(1)
    kv_refs = (kv0_ref, kv1_ref, kv2_ref)
    cm_refs = (cm0_ref, cm1_ref, cm2_ref)
    glu_col = 3 * N_GROUPS

    chunks = [slice(c * IN_ROWS, (c + 1) * IN_ROWS) for c in range(IN_TM // IN_ROWS)]

    def col_dot(lhs, col):
        return jnp.dot(lhs, w_ref[:, col * IN_TN:(col + 1) * IN_TN], preferred_element_type=F32)

    def rope_rows(val, rows):
        return _rope(val, cos_ref[rows, :], sin_ref[rows, :])

    def store_class_major(gi, c, val):
        d = DILATION_GROUPS[gi][1]
        ref = cm_refs[gi]
        per = IN_ROWS // d
        if d == 1:
            ref[0, 0, 0, chunks[c], :] = val.astype(BF16)
            return
        for h in range(HEADS_PER_GROUP):
            de_scr[gi, c, h] = val[:, h * HEAD_DIM:(h + 1) * HEAD_DIM]
        for r in range(d):
            for h in range(HEADS_PER_GROUP):
                ref[0, 0, r, c * per:(c + 1) * per, h * HEAD_DIM:(h + 1) * HEAD_DIM] = (
                    de_scr[gi, c, h, pl.ds(r, per, stride=d), :].astype(BF16))

    def store_window_rows(gi, rows, val, first_slab_row):
        row_lo = plans[gi]["row_lo"]
        lo = max(rows.start, row_lo)
        hi = min(rows.stop, row_lo + plans[gi]["rb"])
        if lo < hi:
            for h in range(HEADS_PER_GROUP):
                dst = pl.ds((lo - row_lo) * KV_SLAB + first_slab_row + h, hi - lo, stride=KV_SLAB)
                kv_refs[gi][0, dst, :] = val[lo - rows.start:hi - rows.start, h * HEAD_DIM:(h + 1) * HEAD_DIM]

    def store_heads(slot, val):
        for h in range(HEADS_PER_GROUP):
            qkvs_ref[:, slot, h, :] = val[:, h * HEAD_DIM:(h + 1) * HEAD_DIM]

    is_last = i == n_i - 1

    @pl.when(phase == PH_Q)
    def _():
        for c, rows in enumerate(chunks):
            xbf[rows, :] = x_ref[rows, :].astype(BF16)
            for gi in range(N_GROUPS):
                store_class_major(gi, c, rope_rows(col_dot(xbf[rows, :], gi), rows))

        @pl.when(is_last)
        def _():
            xsbf[...] = xs_ref[...].astype(BF16)
            for gi in range(N_GROUPS):
                store_heads(gi, _rope(col_dot(xsbf[...], gi), coss_ref[...], sins_ref[...]))

    @pl.when(phase == PH_K)
    def _():
        for c, rows in enumerate(chunks):
            for gi in range(N_GROUPS):
                r = rope_rows(col_dot(xbf[rows, :], N_GROUPS + gi), rows)
                store_class_major(gi, c, r)
                store_window_rows(gi, rows, r, 0)

        @pl.when(is_last)
        def _():
            for gi in range(N_GROUPS):
                store_heads(gi, _rope(col_dot(xsbf[...], N_GROUPS + gi), coss_ref[...], sins_ref[...]))

    @pl.when(phase == PH_V)
    def _():
        for c, rows in enumerate(chunks):
            for gi in range(N_GROUPS):
                acc = col_dot(xbf[rows, :], 2 * N_GROUPS + gi)
                store_class_major(gi, c, acc)
                store_window_rows(gi, rows, acc, HEADS_PER_GROUP)

        @pl.when(is_last)
        def _():
            for gi in range(N_GROUPS):
                store_heads(gi, col_dot(xsbf[...], 2 * N_GROUPS + gi))

    @pl.when(phase == PH_GLU)
    def _():
        for rows in chunks:
            u_ref[rows, :] = col_dot(xbf[rows, :], glu_col) * jax.nn.sigmoid(col_dot(xbf[rows, :], glu_col + 1))

        @pl.when(is_last)
        def _():
            us_ref[...] = col_dot(xsbf[...], glu_col) * jax.nn.sigmoid(col_dot(xsbf[...], glu_col + 1))


def _in_proj(x2, xs2, w_bf, cos_p, sin_p, cos_s, sin_s, batch, seq):
    m = x2.shape[0]
    ms = xs2.shape[0]
    n_i = m // IN_TM
    n_qkv = 3 * N_GROUPS
    assert w_bf.shape[1] == (n_qkv + 2) * IN_TN
    tiles_per_batch = seq // IN_TM
    plans = _kv_window_plan(seq, batch)
    tab = jnp.asarray(np.stack([p["tab"] for p in plans]).reshape(-1))
    n_steps = n_i * N_PHASES

    def kv_map(gi):
        def f(i, p, tab_ref):
            base = (gi * n_steps + i * N_PHASES + p) * 2
            return tab_ref[base], tab_ref[base + 1], 0
        return f

    def cm_map(i, p, tab_ref):
        return i // tiles_per_batch, jnp.minimum(p, PH_V), 0, i % tiles_per_batch, 0

    last = n_i - 1
    in_specs = [
        pl.BlockSpec((IN_TM, D_MODEL), lambda i, j, t: (i, 0)),
        pl.BlockSpec(w_bf.shape, lambda i, j, t: (0, 0)),
        pl.BlockSpec((IN_TM, HEAD_DIM), lambda i, j, t: (i % tiles_per_batch, 0)),
        pl.BlockSpec((IN_TM, HEAD_DIM), lambda i, j, t: (i % tiles_per_batch, 0)),
        pl.BlockSpec((ms, D_MODEL), lambda i, j, t: (0, 0)),
        pl.BlockSpec((ms, HEAD_DIM), lambda i, j, t: (0, 0)),
        pl.BlockSpec((ms, HEAD_DIM), lambda i, j, t: (0, 0)),
    ]
    out_shape, out_specs = [], []
    for gi, (_, d) in enumerate(DILATION_GROUPS):
        out_shape.append(jax.ShapeDtypeStruct((batch, 3, d, seq // d, GROUP_W), BF16))
        out_specs.append(pl.BlockSpec((1, 1, d, IN_TM // d, GROUP_W), cm_map))
    out_shape.append(jax.ShapeDtypeStruct((m, CONV_CH), F32))
    out_specs.append(pl.BlockSpec((IN_TM, CONV_CH), lambda i, j, t: (i, 0)))
    for gi, p in enumerate(plans):
        out_shape.append(jax.ShapeDtypeStruct((batch, p["keep"] * KV_SLAB, HEAD_DIM), F32))
        out_specs.append(pl.BlockSpec((1, p["rb"] * KV_SLAB, HEAD_DIM), kv_map(gi)))
    out_shape += [jax.ShapeDtypeStruct((ms, n_qkv, HEADS_PER_GROUP, HEAD_DIM), F32),
                  jax.ShapeDtypeStruct((ms, CONV_CH), F32)]
    out_specs += [
        pl.BlockSpec((ms, N_GROUPS, HEADS_PER_GROUP, HEAD_DIM),
                     lambda i, p, t: (0, jnp.where(i == last, jnp.minimum(p, PH_V), 0), 0, 0)),
        pl.BlockSpec((ms, CONV_CH), lambda i, j, t: (0, 0)),
    ]
    grid_spec = pltpu.PrefetchScalarGridSpec(
        num_scalar_prefetch=1, grid=(n_i, N_PHASES), in_specs=in_specs, out_specs=out_specs,
        scratch_shapes=[pltpu.VMEM((IN_TM, D_MODEL), BF16),
                        pltpu.VMEM((N_GROUPS, IN_TM // IN_ROWS, HEADS_PER_GROUP, IN_ROWS, HEAD_DIM), F32),
                        pltpu.VMEM((ms, D_MODEL), BF16)])
    kern = functools.partial(_in_proj_kernel, n_i=n_i, plans=plans)
    return pl.pallas_call(kern, grid_spec=grid_spec, out_shape=out_shape,
                          compiler_params=_cparams(2), name="in_proj")(
        tab, x2, w_bf, cos_p, sin_p, xs2, cos_s, sin_s)


ATTN_TP = 2048


def _attn_kernel(q_ref, k_ref, v_ref, o_ref, lse_ref, k_scr, v_scr, o_scr, lse_scr, *, d, lc):
    n = pl.program_id(1)
    n_qb = lc // Q_BLOCK

    @pl.when(n == 0)
    def _():
        k_scr[:, 0:Q_BLOCK, :] = jnp.zeros((d, Q_BLOCK, GROUP_W), BF16)
        v_scr[:, 0:Q_BLOCK, :] = jnp.zeros((d, Q_BLOCK, GROUP_W), BF16)

    k_scr[:, Q_BLOCK:, :] = k_ref[0, 0]
    v_scr[:, Q_BLOCK:, :] = v_ref[0, 0]
    qq = lax.broadcasted_iota(jnp.int32, (Q_BLOCK, 2 * Q_BLOCK), 0)
    kk = lax.broadcasted_iota(jnp.int32, (Q_BLOCK, 2 * Q_BLOCK), 1)
    band = (kk >= qq) & (kk <= qq + Q_BLOCK)
    lane = lax.broadcasted_iota(jnp.int32, (Q_BLOCK, HEAD_DIM), 1)

    def unit(u, carry):
        r = u // n_qb
        qb = u % n_qb
        r0 = pl.multiple_of(qb * Q_BLOCK, Q_BLOCK)
        valid = band & ((kk >= Q_BLOCK) | (n > 0) | (qb > 0))
        if d == 1:
            rows = pl.ds(r0, Q_BLOCK)
        else:
            rows = pl.ds(r0 * d + r, Q_BLOCK, stride=d)
        lse_blk = jnp.zeros((Q_BLOCK, HEAD_DIM), F32)
        for h in range(HEADS_PER_GROUP):
            c0 = h * HEAD_DIM
            q = q_ref[0, 0, r, pl.ds(r0, Q_BLOCK), c0:c0 + HEAD_DIM]
            k = k_scr[r, pl.ds(r0, 2 * Q_BLOCK), c0:c0 + HEAD_DIM]
            v = v_scr[r, pl.ds(r0, 2 * Q_BLOCK), c0:c0 + HEAD_DIM]
            s = lax.dot_general(q, k, (((1,), (1,)), ((), ())), preferred_element_type=F32)
            s = jnp.where(valid, s * ATTN_SCALE, NEG_BIG)
            m = jnp.max(s, axis=1, keepdims=True)
            e = jnp.exp(s - m)
            den = jnp.sum(e, axis=1, keepdims=True)
            o_scr[h, rows, :] = jnp.dot(e.astype(BF16), v, preferred_element_type=F32) / den
            lse_blk = jnp.where(lane == h, m + jnp.log(den), lse_blk)
        lse_scr[rows, :] = lse_blk
        return carry

    lax.fori_loop(0, d * n_qb, unit, 0, unroll=4)
    k_scr[:, 0:Q_BLOCK, :] = k_scr[:, lc:lc + Q_BLOCK, :]
    v_scr[:, 0:Q_BLOCK, :] = v_scr[:, lc:lc + Q_BLOCK, :]
    for h in range(HEADS_PER_GROUP):
        o_ref[:, h * HEAD_DIM:(h + 1) * HEAD_DIM] = o_scr[h].astype(BF16)
    lse_ref[...] = lse_scr[...]


def _attn_group(cm, gi, batch, seq):
    _, d = DILATION_GROUPS[gi]
    lc = ATTN_TP // d
    n_t = seq // ATTN_TP

    def plane(p):
        return pl.BlockSpec((1, 1, d, lc, GROUP_W), lambda b, n: (b, p, 0, n, 0))

    return pl.pallas_call(
        functools.partial(_attn_kernel, d=d, lc=lc),
        grid=(batch, n_t),
        in_specs=[plane(0), plane(1), plane(2)],
        out_specs=[pl.BlockSpec((ATTN_TP, GROUP_W), lambda b, n: (b * n_t + n, 0)),
                   pl.BlockSpec((ATTN_TP, HEAD_DIM), lambda b, n: (b * n_t + n, 0))],
        out_shape=[jax.ShapeDtypeStruct((batch * seq, GROUP_W), BF16),
                   jax.ShapeDtypeStruct((batch * seq, HEAD_DIM), F32)],
        scratch_shapes=[pltpu.VMEM((d, lc + Q_BLOCK, GROUP_W), BF16),
                        pltpu.VMEM((d, lc + Q_BLOCK, GROUP_W), BF16),
                        pltpu.VMEM((HEADS_PER_GROUP, ATTN_TP, HEAD_DIM), F32),
                        pltpu.VMEM((ATTN_TP, HEAD_DIM), F32)],
        compiler_params=_cparams(2), name=f"attn_g{gi}")(cm, cm, cm)


CONV_TC = 512
CONV_HALO = 32
CONV_ROWS = 64


def _conv_kernel(u_ref, halo_ref, w_ref, b_ref, g_ref, beta_ref, *rest, n_t, stream_chunks):
    n_s = len(stream_chunks)
    src_refs, c_ref, dst_refs, hist = rest[:n_s], rest[n_s], rest[n_s + 1:2 * n_s + 1], rest[2 * n_s + 1]
    t = pl.program_id(1)
    step = pl.program_id(0) * n_t + t

    for src, dst, n_chunks in zip(src_refs, dst_refs, stream_chunks):
        @pl.when(step < n_chunks)
        def _(src=src, dst=dst):
            dst[...] = src[...].astype(BF16)

    hist[0:CONV_HALO, :] = jnp.where(t > 0, halo_ref[0], 0.0)
    hist[CONV_HALO:CONV_HALO + CONV_TC, :] = u_ref[0]
    hist[CONV_HALO + CONV_TC:, :] = jnp.zeros((SUBLANES, CONV_CH), F32)
    lead = CONV_HALO - (CONV_K - 1)
    for rc in range(CONV_TC // CONV_ROWS):
        r0 = rc * CONV_ROWS
        acc = jnp.broadcast_to(b_ref[...], (CONV_ROWS, CONV_CH))
        for s in range(SUBLANES):
            grp = None
            for c in range(s, lead + CONV_K, SUBLANES):
                if c < lead:
                    continue
                term = hist[r0 + c - s:r0 + c - s + CONV_ROWS + SUBLANES, :] * w_ref[c - lead:c - lead + 1, :]
                grp = term if grp is None else grp + term
            acc = acc + grp[s:s + CONV_ROWS, :]
        y = _layer_norm_rows(acc, g_ref[...], beta_ref[...])
        c_ref[0, r0:r0 + CONV_ROWS, :] = (y * jax.nn.sigmoid(y)).astype(BF16)


def _conv_prompt(u2, conv_w, conv_b, ln_g, ln_b, batch, seq, cast_streams):
    u3 = u2.reshape(batch, seq, CONV_CH)
    per = CONV_TC // CONV_HALO
    n_t = seq // CONV_TC
    vec = pl.BlockSpec((1, CONV_CH), lambda b, t: (0, 0))
    stream_specs, stream_chunks = [], []
    for w, axis, chunk in cast_streams:
        n_chunks = w.shape[axis] // chunk
        assert w.ndim == 2 and n_chunks * chunk == w.shape[axis] and n_chunks <= batch * n_t
        block = (chunk, w.shape[1]) if axis == 0 else (w.shape[0], chunk)

        def index_map(b, t, axis=axis, n_chunks=n_chunks):
            k = jnp.minimum(b * n_t + t, n_chunks - 1)
            return (k, 0) if axis == 0 else (0, k)

        stream_specs.append(pl.BlockSpec(block, index_map))
        stream_chunks.append(n_chunks)
    outs = pl.pallas_call(
        functools.partial(_conv_kernel, n_t=n_t, stream_chunks=tuple(stream_chunks)),
        grid=(batch, n_t),
        in_specs=[pl.BlockSpec((1, CONV_TC, CONV_CH), lambda b, t: (b, t, 0)),
                  pl.BlockSpec((1, CONV_HALO, CONV_CH), lambda b, t: (b, jnp.maximum(t * per - 1, 0), 0)),
                  pl.BlockSpec((CONV_K, CONV_CH), lambda b, t: (0, 0)), vec, vec, vec] + stream_specs,
        out_specs=[pl.BlockSpec((1, CONV_TC, CONV_CH), lambda b, t: (b, t, 0))] + stream_specs,
        out_shape=[jax.ShapeDtypeStruct((batch, seq, CONV_CH), BF16)]
                  + [jax.ShapeDtypeStruct(w.shape, BF16) for w, _, _ in cast_streams],
        scratch_shapes=[pltpu.VMEM((CONV_HALO + CONV_TC + SUBLANES, CONV_CH), F32)],
        compiler_params=_cparams(2), name="conv_prompt")(
        u3, u3, conv_w, conv_b[None, :], ln_g[None, :], ln_b[None, :], *[w for w, _, _ in cast_streams])
    return outs[0].reshape(batch * seq, CONV_CH), outs[1:]


KV_SLAB = 2 * HEADS_PER_GROUP
SAMPLE_SHIFT_GROUPS = (0, 1)
FFN_SHIFT_GROUP = 2
FFN_SHIFT_ROWS = 2048


def _shift_cache(cref, out_ref, d, t_new):
    slab = d * KV_SLAB
    shift = t_new * KV_SLAB
    n_rows = cref.shape[1]
    if shift % slab == 0:
        s = shift // slab
        out_ref[0, 0:(n_rows - s) * slab] = cref[0, s:n_rows].reshape((n_rows - s) * slab, HEAD_DIM)
    else:
        assert shift < slab and shift % 8 == 0
        out_ref[0, 0:slab - shift] = cref[0, 0, shift:slab]

        def body(ci, carry):
            out_ref[0, pl.ds(pl.multiple_of(ci * slab - shift, 8), slab)] = cref[0, ci]
            return carry

        lax.fori_loop(1, n_rows, body, 0)


def _new_kv_rows(qkv_ref, gi, t_new):
    slabs = []
    for t in range(t_new):
        slabs += [qkv_ref[0, t, N_GROUPS + gi], qkv_ref[0, t, 2 * N_GROUPS + gi]]
    return jnp.concatenate(slabs, axis=0)


def _sample_kernel(qkv_ref, c0_ref, c1_ref, c2_ref, st_ref, u_ref, w_ref, b_ref, g_ref, beta_ref,
                   mix_ref, nconv_ref, *rest, t_new):
    cache_refs = (c0_ref, c1_ref, c2_ref)
    nk_refs = dict(zip(SAMPLE_SHIFT_GROUPS, rest[:-1]))
    hist = rest[-1]
    hp = HEADS_PER_GROUP
    outs = [[None] * N_GROUPS for _ in range(t_new)]
    lses = [[None] * N_GROUPS for _ in range(t_new)]
    for gi, (_, d) in enumerate(DILATION_GROUPS):
        cref = cache_refs[gi]
        n_rows = cref.shape[1]
        row = lax.broadcasted_iota(jnp.int32, (n_rows, hp, 1), 0)
        if gi in nk_refs:
            _shift_cache(cref, nk_refs[gi], d, t_new)
            keep = nk_refs[gi].shape[1] - t_new * KV_SLAB
            nk_refs[gi][0, keep:, :] = _new_kv_rows(qkv_ref, gi, t_new)
        for t in range(t_new):
            s0 = (t % d) * KV_SLAB
            first_row = t // d
            q = qkv_ref[0, t, gi][None]
            kc = cref[0, :, s0:s0 + hp, :]
            vc = cref[0, :, s0 + hp:s0 + 2 * hp, :]
            s_c = jnp.sum(kc * q, axis=2, keepdims=True) * ATTN_SCALE
            if first_row > 0:
                s_c = jnp.where(row >= first_row, s_c, NEG_BIG)
            newest = [t - d * jj for jj in range(t // d + 1)]
            s_n, v_n = [], []
            for tn in newest:
                kn = qkv_ref[0, tn, N_GROUPS + gi][None]
                v_n.append(qkv_ref[0, tn, 2 * N_GROUPS + gi][None])
                s_n.append(jnp.sum(kn * q, axis=2, keepdims=True) * ATTN_SCALE)
            m = jnp.max(s_c, axis=0, keepdims=True)
            for sn in s_n:
                m = jnp.maximum(m, sn)
            e_c = jnp.exp(s_c - m)
            den = jnp.sum(e_c, axis=0, keepdims=True)
            acc = jnp.sum(e_c * vc, axis=0, keepdims=True)
            for sn, v1 in zip(s_n, v_n):
                e_n = jnp.exp(sn - m)
                den = den + e_n
                acc = acc + e_n * v1
            outs[t][gi] = acc / den
            lses[t][gi] = m + jnp.log(den)
    for t in range(t_new):
        ls = lses[t]
        m = functools.reduce(jnp.maximum, ls)
        es = [jnp.exp(l - m) for l in ls]
        tot = functools.reduce(lambda a, b: a + b, es)
        for gi in range(N_GROUPS):
            slab = (outs[t][gi] * (es[gi] / tot))[0]
            for h in range(hp):
                c0 = gi * GROUP_W + h * HEAD_DIM
                mix_ref[0, t:t + 1, c0:c0 + HEAD_DIM] = slab[h:h + 1, :]
    n_state = CONV_K - 1
    hist[0:n_state, :] = st_ref[0]
    hist[n_state:n_state + t_new, :] = u_ref[0]
    acc = jnp.broadcast_to(b_ref[...], (t_new, CONV_CH))
    for k in range(CONV_K):
        acc = acc + hist[k:k + t_new, :] * w_ref[k:k + 1, :]
    y = _layer_norm_rows(acc, g_ref[...], beta_ref[...])
    mix_ref[0, :, ATTN_WIDTH:] = y * jax.nn.sigmoid(y)
    nconv_ref[0] = hist[t_new:t_new + n_state, :]


def _sample_mixers(qkvs, us, caches, state, conv_w, conv_b, ln_g, ln_b, dec_batch, t_new):
    n_qkv = qkvs.shape[1]
    hp = HEADS_PER_GROUP
    qkv5 = qkvs.reshape(dec_batch, t_new, n_qkv, hp, HEAD_DIM)
    u3 = us.reshape(dec_batch, t_new, CONV_CH)
    cache_in, cache_specs, nk_shapes, nk_specs = [], [], [], []
    for gi, ((window, d), cache) in enumerate(zip(DILATION_GROUPS, caches)):
        buf = cache.shape[1]
        assert buf == window and buf % d == 0
        n_rows = buf // d
        cache_in.append(cache.reshape(dec_batch, n_rows, d * KV_SLAB, HEAD_DIM))
        if gi in SAMPLE_SHIFT_GROUPS:
            cache_specs.append(pl.BlockSpec((1, n_rows, d * KV_SLAB, HEAD_DIM), lambda b: (b, 0, 0, 0)))
            nk_shapes.append(jax.ShapeDtypeStruct((dec_batch, buf * KV_SLAB, HEAD_DIM), F32))
            nk_specs.append(pl.BlockSpec((1, buf * KV_SLAB, HEAD_DIM), lambda b: (b, 0, 0)))
        else:
            used = min(d, t_new) * KV_SLAB
            cache_specs.append(pl.BlockSpec((1, n_rows, used, HEAD_DIM), lambda b: (b, 0, 0, 0)))
    vec = pl.BlockSpec((1, CONV_CH), lambda b: (0, 0))
    n_state = CONV_K - 1
    mix, nconv, *nks = pl.pallas_call(
        functools.partial(_sample_kernel, t_new=t_new),
        grid=(dec_batch,),
        in_specs=[pl.BlockSpec((1, t_new, n_qkv, hp, HEAD_DIM), lambda b: (b, 0, 0, 0, 0))] + cache_specs + [
            pl.BlockSpec((1, n_state, CONV_CH), lambda b: (b, 0, 0)),
            pl.BlockSpec((1, t_new, CONV_CH), lambda b: (b, 0, 0)),
            pl.BlockSpec((CONV_K, CONV_CH), lambda b: (0, 0)), vec, vec, vec],
        out_specs=[pl.BlockSpec((1, t_new, D_MODEL), lambda b: (b, 0, 0)),
                   pl.BlockSpec((1, n_state, CONV_CH), lambda b: (b, 0, 0))] + nk_specs,
        out_shape=[jax.ShapeDtypeStruct((dec_batch, t_new, D_MODEL), F32),
                   jax.ShapeDtypeStruct((dec_batch, n_state, CONV_CH), F32)] + nk_shapes,
        scratch_shapes=[pltpu.VMEM((n_state + t_new + 6, CONV_CH), F32)],
        compiler_params=_cparams(1), name="sample_mixers")(
        qkv5, *cache_in, state, u3, conv_w, conv_b[None, :], ln_g[None, :], ln_b[None, :])
    new_caches = {gi: nk.reshape(caches[gi].shape) for gi, nk in zip(SAMPLE_SHIFT_GROUPS, nks)}
    return mix.reshape(dec_batch * t_new, D_MODEL), nconv, new_caches


OUT_TM = 512
OUT_ROWS = 128


def _out_proj_kernel(o0_ref, o1_ref, o2_ref, l0_ref, l1_ref, l2_ref, c_ref, x_ref, w_ref, g_ref, b_ref,
                     mixs_ref, xs_ref, x1_ref, x1bf_ref, x1s_ref, mix_scr, y_scr, *, n_i, alpha):
    i = pl.program_id(0)
    o_refs = (o0_ref, o1_ref, o2_ref)
    chunks = [slice(r0, r0 + OUT_ROWS) for r0 in range(0, OUT_TM, OUT_ROWS)]

    def project(rows):
        ls = [l0_ref[rows, :], l1_ref[rows, :], l2_ref[rows, :]]
        m = jnp.maximum(jnp.maximum(ls[0], ls[1]), ls[2])
        es = [jnp.exp(l - m) for l in ls]
        inv = 1.0 / (es[0] + es[1] + es[2])
        for gi in range(N_GROUPS):
            a = es[gi] * inv
            for h in range(HEADS_PER_GROUP):
                c0 = h * HEAD_DIM
                og = o_refs[gi][rows, c0:c0 + HEAD_DIM].astype(F32)
                mix_scr[rows, gi * GROUP_W + c0:gi * GROUP_W + c0 + HEAD_DIM] = (og * a[:, h:h + 1]).astype(BF16)
        mix_scr[rows, ATTN_WIDTH:] = c_ref[rows, :]
        y_scr[rows, :] = (jnp.dot(mix_scr[rows, :], w_ref[...], preferred_element_type=F32)
                          + alpha * x_ref[rows, :])

    def normalize(rows):
        for r0 in range(rows.start, rows.stop, LN_STRIP):
            strip = slice(r0, r0 + LN_STRIP)
            x1 = _layer_norm_rows(y_scr[strip, :], g_ref[...], b_ref[...])
            x1_ref[strip, :] = x1
            x1bf_ref[strip, :] = x1.astype(BF16)

    project(chunks[0])
    for c in range(1, len(chunks)):
        project(chunks[c])
        normalize(chunks[c - 1])
    normalize(chunks[-1])

    @pl.when(i == n_i - 1)
    def _():
        ys = jnp.dot(mixs_ref[...].astype(BF16), w_ref[...], preferred_element_type=F32) + alpha * xs_ref[...]
        x1s_ref[...] = _layer_norm_rows(ys, g_ref[...], b_ref[...])


def _out_proj(os, lses, c, x2, w_bf, ln_g, ln_b, mixs, xs2, alpha):
    m = x2.shape[0]
    ms = xs2.shape[0]
    n_i = m // OUT_TM
    row = lambda w: pl.BlockSpec((OUT_TM, w), lambda i: (i, 0))
    whole = lambda a: pl.BlockSpec(a.shape, lambda i: (0,) * a.ndim)
    g2, b2 = ln_g[None, :], ln_b[None, :]
    return pl.pallas_call(
        functools.partial(_out_proj_kernel, n_i=n_i, alpha=alpha),
        grid=(n_i,),
        in_specs=[row(GROUP_W)] * 3 + [row(HEAD_DIM)] * 3 + [row(CONV_CH), row(D_MODEL),
                  whole(w_bf), whole(g2), whole(b2), whole(mixs), whole(xs2)],
        out_specs=[row(D_MODEL), row(D_MODEL), pl.BlockSpec((ms, D_MODEL), lambda i: (0, 0))],
        out_shape=[jax.ShapeDtypeStruct((m, D_MODEL), F32), jax.ShapeDtypeStruct((m, D_MODEL), BF16),
                   jax.ShapeDtypeStruct((ms, D_MODEL), F32)],
        scratch_shapes=[pltpu.VMEM((OUT_TM, D_MODEL), BF16), pltpu.VMEM((OUT_TM, D_MODEL), F32)],
        compiler_params=_cparams(1), name="out_proj")(
        *os, *lses, c, x2, w_bf, g2, b2, mixs, xs2)


FFN_TM = 1024
FFN_TF = 512
FFN_ROWS = 1024
FFN_RES_W = 256


def _ffn_kernel(xbf_ref, xres_ref, wg_ref, wu_ref, wd_ref, g_ref, b_ref, xs_ref, ca_ref, cb_ref, qkv_ref,
                y_ref, ys_ref, nk_ref, xsbf, *, n_i, n_f, alpha, t_new, n_chunks, chunks_per_entry):
    i = pl.program_id(0)
    f = pl.program_id(1)

    shift = t_new * KV_SLAB
    rows = ca_ref.shape[1]
    chunk = jnp.minimum(i * n_f + f, n_chunks - 1) % chunks_per_entry
    nk_ref[0, 0:rows - shift, :] = ca_ref[0, shift:rows, :]
    nk_ref[0, rows - shift:rows, :] = jnp.where(chunk == chunks_per_entry - 1,
                                                _new_kv_rows(qkv_ref, FFN_SHIFT_GROUP, t_new), cb_ref[0])

    def swiglu_down(xb):
        gate = jnp.dot(xb, wg_ref[...], preferred_element_type=F32)
        up = jnp.dot(xb, wu_ref[...], preferred_element_type=F32)
        act = (gate * jax.nn.sigmoid(gate) * up).astype(BF16)
        return jnp.dot(act, wd_ref[...], preferred_element_type=F32)

    @pl.when(f == 0)
    def _():
        y_ref[...] = jnp.zeros_like(y_ref)

    for r0 in range(0, FFN_TM, FFN_ROWS):
        y_ref[r0:r0 + FFN_ROWS, :] += swiglu_down(xbf_ref[r0:r0 + FFN_ROWS, :])

    for c in range(D_MODEL // FFN_RES_W):
        @pl.when(f == c)
        def _(c=c):
            y_ref[:, c * FFN_RES_W:(c + 1) * FFN_RES_W] += alpha * xres_ref[...]

    @pl.when(f == n_f - 1)
    def _():
        y_ref[...] = _layer_norm_rows(y_ref[...], g_ref[...], b_ref[...])

    @pl.when(i == n_i - 1)
    def _():
        @pl.when(f == 0)
        def _():
            xsbf[...] = xs_ref[...].astype(BF16)

        downs = swiglu_down(xsbf[...])

        @pl.when(f == 0)
        def _():
            ys_ref[...] = downs + alpha * xs_ref[...]

        @pl.when(f > 0)
        def _():
            ys_ref[...] += downs

        @pl.when(f == n_f - 1)
        def _():
            ys_ref[...] = _layer_norm_rows(ys_ref[...], g_ref[...], b_ref[...])


def _ffn(x1, x1bf, x1s, wg_bf, wu_bf, wd_bf, ln_g, ln_b, alpha, cache, qkvs, t_new):
    m = x1.shape[0]
    ms = x1s.shape[0]
    hidden = wg_bf.shape[1]
    n_i = m // FFN_TM
    n_f = hidden // FFN_TF
    n_res = D_MODEL // FFN_RES_W
    assert n_f >= n_res
    g2, b2 = ln_g[None, :], ln_b[None, :]
    dec_batch = cache.shape[0]
    cache_flat = cache.reshape(dec_batch, -1, HEAD_DIM)
    entry_rows = cache_flat.shape[1]
    shift = t_new * KV_SLAB
    cpe = entry_rows // FFN_SHIFT_ROWS
    n_chunks = dec_batch * cpe
    assert cpe * FFN_SHIFT_ROWS == entry_rows and n_chunks <= n_i * n_f and FFN_SHIFT_ROWS % shift == 0
    qkv5 = qkvs.reshape(dec_batch, t_new, qkvs.shape[1], HEADS_PER_GROUP, HEAD_DIM)

    def chunk_of(i, f):
        c = jnp.minimum(i * n_f + f, n_chunks - 1)
        return c // cpe, c % cpe

    def chunk_map(i, f):
        b, k = chunk_of(i, f)
        return b, k, 0

    def follow_map(i, f):
        b, k = chunk_of(i, f)
        return b, jnp.minimum((k + 1) * (FFN_SHIFT_ROWS // shift), entry_rows // shift - 1), 0

    y, ys, nk = pl.pallas_call(
        functools.partial(_ffn_kernel, n_i=n_i, n_f=n_f, alpha=alpha, t_new=t_new, n_chunks=n_chunks,
                          chunks_per_entry=cpe),
        grid=(n_i, n_f),
        in_specs=[pl.BlockSpec((FFN_TM, D_MODEL), lambda i, f: (i, 0)),
                  pl.BlockSpec((FFN_TM, FFN_RES_W), lambda i, f: (i, jnp.minimum(f, n_res - 1))),
                  pl.BlockSpec((D_MODEL, FFN_TF), lambda i, f: (0, f)),
                  pl.BlockSpec((D_MODEL, FFN_TF), lambda i, f: (0, f)),
                  pl.BlockSpec((FFN_TF, D_MODEL), lambda i, f: (f, 0)),
                  pl.BlockSpec((1, D_MODEL), lambda i, f: (0, 0)),
                  pl.BlockSpec((1, D_MODEL), lambda i, f: (0, 0)),
                  pl.BlockSpec((ms, D_MODEL), lambda i, f: (0, 0)),
                  pl.BlockSpec((1, FFN_SHIFT_ROWS, HEAD_DIM), chunk_map),
                  pl.BlockSpec((1, shift, HEAD_DIM), follow_map),
                  pl.BlockSpec((1,) + qkv5.shape[1:], lambda i, f: (chunk_of(i, f)[0], 0, 0, 0, 0))],
        out_specs=[pl.BlockSpec((FFN_TM, D_MODEL), lambda i, f: (i, 0)),
                   pl.BlockSpec((ms, D_MODEL), lambda i, f: (0, 0)),
                   pl.BlockSpec((1, FFN_SHIFT_ROWS, HEAD_DIM), chunk_map)],
        out_shape=[jax.ShapeDtypeStruct((m, D_MODEL), F32), jax.ShapeDtypeStruct((ms, D_MODEL), F32),
                   jax.ShapeDtypeStruct(cache_flat.shape, F32)],
        scratch_shapes=[pltpu.VMEM((ms, D_MODEL), BF16)],
        compiler_params=_cparams(2), name="ffn")(
        x1bf, x1, wg_bf, wu_bf, wd_bf, g2, b2, x1s, cache_flat, cache_flat, qkv5)
    return y, ys, nk.reshape(cache.shape)


def kernel(x_prompt, x_sample, cache_kv_w128, cache_kv_w512, cache_kv_w2048, state_conv, w_in, w_out,
           conv_w, conv_b, conv_ln_g, conv_ln_b, ln1_g, ln1_b, w_gate, w_up, w_down, ln2_g, ln2_b):
    depth = w_in.shape[0]
    batch, seq, _ = x_prompt.shape
    dec_batch, t_new, _ = x_sample.shape
    caches = (cache_kv_w128, cache_kv_w512, cache_kv_w2048)
    alpha = (2.0 * depth) ** 0.25

    cos_p, sin_p = _rope_tables(np.arange(seq))
    cos_s, sin_s = _rope_tables(np.tile(PAST_LEN + np.arange(t_new), dec_batch))

    xp = x_prompt.reshape(batch * seq, D_MODEL)
    xs = x_sample.reshape(dec_batch * t_new, D_MODEL)
    kvp = [[] for _ in range(N_GROUPS)]
    kvs = [[] for _ in range(N_GROUPS)]
    convp, convs = [], []
    for l in range(depth):
        cm0, cm1, cm2, u, kv0, kv1, kv2, qkvs, us = _in_proj(
            xp, xs, w_in[l].astype(BF16), cos_p, sin_p, cos_s, sin_s, batch, seq)
        os, lses = zip(*[_attn_group(cm, gi, batch, seq) for gi, cm in enumerate((cm0, cm1, cm2))])
        conv_steps = batch * seq // CONV_TC
        c, (wo_bf, wg_bf, wu_bf, wd_bf) = _conv_prompt(
            u, conv_w[l], conv_b[l], conv_ln_g[l], conv_ln_b[l], batch, seq,
            [(w, 0, w.shape[0] // conv_steps) for w in (w_out[l], w_gate[l], w_up[l], w_down[l])])
        layer_caches = [cc[l] for cc in caches]
        mixs, nconv_s, new_caches = _sample_mixers(qkvs, us, layer_caches, state_conv[l], conv_w[l], conv_b[l],
                                                   conv_ln_g[l], conv_ln_b[l], dec_batch, t_new)
        x1, x1bf, x1s = _out_proj(os, lses, c, xp, wo_bf, ln1_g[l], ln1_b[l], mixs, xs, alpha)
        xp, xs, new_caches[FFN_SHIFT_GROUP] = _ffn(x1, x1bf, x1s, wg_bf, wu_bf, wd_bf, ln2_g[l], ln2_b[l], alpha,
                                                   layer_caches[FFN_SHIFT_GROUP], qkvs, t_new)
        for gi, kv in enumerate((kv0, kv1, kv2)):
            kvp[gi].append(kv.reshape(batch, kv.shape[1] // KV_SLAB, 2, HEADS_PER_GROUP, HEAD_DIM))
            kvs[gi].append(new_caches[gi])
        convp.append(u.reshape(batch, seq, CONV_CH)[:, seq - (CONV_K - 1):])
        convs.append(nconv_s)

    y_prompt = xp.reshape(batch, seq, D_MODEL)
    y_sample = xs.reshape(dec_batch, t_new, D_MODEL)
    return (y_prompt, y_sample, jnp.stack(kvp[0]), jnp.stack(kvp[1]), jnp.stack(kvp[2]), jnp.stack(convp),
            jnp.stack(kvs[0]), jnp.stack(kvs[1]), jnp.stack(kvs[2]), jnp.stack(convs))
```

```python
import functools

import numpy as np
import jax
import jax.numpy as jnp
from jax import lax
from jax.experimental import pallas as pl
from jax.experimental.pallas import tpu as pltpu

D_MODEL = 2048
HEAD_DIM = 128
CONV_CH = D_MODEL // 4
ATTN_WIDTH = D_MODEL - CONV_CH
DILATION_GROUPS = ((128, 1), (512, 4), (2048, 16))
N_GROUPS = len(DILATION_GROUPS)
DILATED_GROUPS = tuple(gi for gi, (_, d) in enumerate(DILATION_GROUPS) if d > 1)
HEADS_PER_GROUP = ATTN_WIDTH // HEAD_DIM // N_GROUPS
GROUP_W = HEADS_PER_GROUP * HEAD_DIM
CONV_K = 31
ROPE_THETA = 10000.0
LN_EPS = 1e-5
Q_BLOCK = 128
ATTN_SCALE = HEAD_DIM ** -0.5
NEG_BIG = -1e30
PAST_LEN = 16384

F32 = jnp.float32
BF16 = jnp.bfloat16

VMEM_LIMIT = 58 * 1024 * 1024
LN_STRIP = 16
SUBLANES = 8


def _cparams(n_axes):
    return pltpu.CompilerParams(dimension_semantics=("arbitrary",) * n_axes,
                                vmem_limit_bytes=VMEM_LIMIT)


def _layer_norm_rows(y, g, b):
    mu = jnp.mean(y, axis=-1, keepdims=True)
    yc = y - mu
    var = jnp.mean(yc * yc, axis=-1, keepdims=True)
    return yc * lax.rsqrt(var + LN_EPS) * g + b


def _rope_tables(pos):
    half = HEAD_DIM // 2
    inv = ROPE_THETA ** (-np.arange(half, dtype=np.float64) / half)
    ang = np.asarray(pos, np.float64)[:, None] * inv[None, :]
    cos, sin = np.cos(ang), np.sin(ang)
    return (jnp.asarray(np.concatenate([cos, cos], axis=1), F32),
            jnp.asarray(np.concatenate([-sin, sin], axis=1), F32))


def _rope(h, cos, sin):
    parts = []
    for hh in range(HEADS_PER_GROUP):
        hs = h[:, hh * HEAD_DIM:(hh + 1) * HEAD_DIM]
        parts.append(hs * cos + pltpu.roll(hs, HEAD_DIM // 2, axis=1) * sin)
    return jnp.concatenate(parts, axis=1)


IN_TM = 512
IN_TN = GROUP_W
IN_ROWS = 256
PH_Q_GLU, PH_KV = range(2)
N_PHASES = 2


def _kv_window_plan(seq, batch):
    tiles_per_batch = seq // IN_TM
    n_i = batch * tiles_per_batch
    plans = []
    for window, _ in DILATION_GROUPS:
        keep = min(window, seq)
        rb = min(keep, IN_TM)
        first_tile = (seq - keep) // IN_TM
        row_lo = (seq - keep) - first_tile * IN_TM
        writes = []
        for i in range(n_i):
            b, it = divmod(i, tiles_per_batch)
            if it >= first_tile:
                writes.append((i * N_PHASES + PH_KV, (b, it - first_tile)))
        writes.sort()
        tab = np.zeros((n_i * N_PHASES, 2), np.int32)
        w = 0
        for step in range(n_i * N_PHASES):
            while w < len(writes) - 1 and writes[w][0] < step:
                w += 1
            tab[step] = writes[w][1]
        plans.append(dict(keep=keep, rb=rb, row_lo=row_lo, tab=tab))
    return plans


def _in_proj_kernel(tab_ref, x_ref, w_ref, cos_ref, sin_ref, xs_ref, coss_ref, sins_ref,
                    q0_ref, q1_ref, q2_ref, kc0_ref, kc1_ref, kc2_ref, u_ref, kv0_ref, kv1_ref, kv2_ref,
                    qkvs_ref, us_ref, xbf, de_scr, xsbf, *, n_i, plans):
    del tab_ref
    i = pl.program_id(0)
    phase = pl.program_id(1)
    kv_refs = (kv0_ref, kv1_ref, kv2_ref)
    q_refs = (q0_ref, q1_ref, q2_ref)
    kc_refs = (kc0_ref, kc1_ref, kc2_ref)
    glu_col = 3 * N_GROUPS

    chunks = [slice(c * IN_ROWS, (c + 1) * IN_ROWS) for c in range(IN_TM // IN_ROWS)]

    def col_dot(lhs, col):
        return jnp.dot(lhs, w_ref[:, col * IN_TN:(col + 1) * IN_TN], preferred_element_type=F32)

    def rope_rows(val, rows):
        return _rope(val, cos_ref[rows, :], sin_ref[rows, :])

    def store_class_major(ref, lead, gi, c, val):
        d = DILATION_GROUPS[gi][1]
        per = IN_ROWS // d
        if d == 1:
            ref[lead + (0, chunks[c], slice(None))] = val.astype(BF16)
            return
        slot = DILATED_GROUPS.index(gi)
        for h in range(HEADS_PER_GROUP):
            de_scr[slot, c, h] = val[:, h * HEAD_DIM:(h + 1) * HEAD_DIM]
        for r in range(d):
            for h in range(HEADS_PER_GROUP):
                ref[lead + (r, slice(c * per, (c + 1) * per), slice(h * HEAD_DIM, (h + 1) * HEAD_DIM))] = (
                    de_scr[slot, c, h, pl.ds(r, per, stride=d), :].astype(BF16))

    def store_window_rows(gi, rows, val, first_slab_row):
        row_lo = plans[gi]["row_lo"]
        lo = max(rows.start, row_lo)
        hi = min(rows.stop, row_lo + plans[gi]["rb"])
        if lo < hi:
            for h in range(HEADS_PER_GROUP):
                dst = pl.ds((lo - row_lo) * KV_SLAB + first_slab_row + h, hi - lo, stride=KV_SLAB)
                kv_refs[gi][0, dst, :] = val[lo - rows.start:hi - rows.start, h * HEAD_DIM:(h + 1) * HEAD_DIM]

    def store_heads(slot, val):
        for h in range(HEADS_PER_GROUP):
            qkvs_ref[:, slot, h, :] = val[:, h * HEAD_DIM:(h + 1) * HEAD_DIM]

    is_last = i == n_i - 1

    @pl.when(phase == PH_Q_GLU)
    def _():
        for c, rows in enumerate(chunks):
            xbf[rows, :] = x_ref[rows, :].astype(BF16)
            for gi in range(N_GROUPS):
                store_class_major(q_refs[gi], (0,), gi, c, rope_rows(col_dot(xbf[rows, :], gi), rows))
            u_ref[rows, :] = col_dot(xbf[rows, :], glu_col) * jax.nn.sigmoid(col_dot(xbf[rows, :], glu_col + 1))

        @pl.when(is_last)
        def _():
            xsbf[...] = xs_ref[...].astype(BF16)
            for gi in range(N_GROUPS):
                store_heads(gi, _rope(col_dot(xsbf[...], gi), coss_ref[...], sins_ref[...]))
            us_ref[...] = col_dot(xsbf[...], glu_col) * jax.nn.sigmoid(col_dot(xsbf[...], glu_col + 1))

    @pl.when(phase == PH_KV)
    def _():
        for c, rows in enumerate(chunks):
            for gi in range(N_GROUPS):
                r = rope_rows(col_dot(xbf[rows, :], N_GROUPS + gi), rows)
                store_class_major(kc_refs[gi], (0, 0), gi, c, r)
                store_window_rows(gi, rows, r, 0)
                acc = col_dot(xbf[rows, :], 2 * N_GROUPS + gi)
                store_class_major(kc_refs[gi], (0, 1), gi, c, acc)
                store_window_rows(gi, rows, acc, HEADS_PER_GROUP)

        @pl.when(is_last)
        def _():
            for gi in range(N_GROUPS):
                store_heads(N_GROUPS + gi, _rope(col_dot(xsbf[...], N_GROUPS + gi), coss_ref[...], sins_ref[...]))
                store_heads(2 * N_GROUPS + gi, col_dot(xsbf[...], 2 * N_GROUPS + gi))


def _in_proj(x2, xs2, w_bf, cos_p, sin_p, cos_s, sin_s, batch, seq):
    m = x2.shape[0]
    ms = xs2.shape[0]
    n_i = m // IN_TM
    n_qkv = 3 * N_GROUPS
    assert w_bf.shape[1] == (n_qkv + 2) * IN_TN
    tiles_per_batch = seq // IN_TM
    plans = _kv_window_plan(seq, batch)
    tab = jnp.asarray(np.stack([p["tab"] for p in plans]).reshape(-1))
    n_steps = n_i * N_PHASES

    def kv_map(gi):
        def f(i, p, tab_ref):
            base = (gi * n_steps + i * N_PHASES + p) * 2
            return tab_ref[base], tab_ref[base + 1], 0
        return f

    in_specs = [
        pl.BlockSpec((IN_TM, D_MODEL), lambda i, j, t: (i, 0)),
        pl.BlockSpec(w_bf.shape, lambda i, j, t: (0, 0)),
        pl.BlockSpec((IN_TM, HEAD_DIM), lambda i, j, t: (i % tiles_per_batch, 0)),
        pl.BlockSpec((IN_TM, HEAD_DIM), lambda i, j, t: (i % tiles_per_batch, 0)),
        pl.BlockSpec((ms, D_MODEL), lambda i, j, t: (0, 0)),
        pl.BlockSpec((ms, HEAD_DIM), lambda i, j, t: (0, 0)),
        pl.BlockSpec((ms, HEAD_DIM), lambda i, j, t: (0, 0)),
    ]
    out_shape, out_specs = [], []
    for _, d in DILATION_GROUPS:
        out_shape.append(jax.ShapeDtypeStruct((batch, d, seq // d, GROUP_W), BF16))
        out_specs.append(pl.BlockSpec((1, d, IN_TM // d, GROUP_W),
                                      lambda i, p, t: (i // tiles_per_batch, 0, i % tiles_per_batch, 0)))
    for _, d in DILATION_GROUPS:
        out_shape.append(jax.ShapeDtypeStruct((batch, 2, d, seq // d, GROUP_W), BF16))
        out_specs.append(pl.BlockSpec((1, 2, d, IN_TM // d, GROUP_W),
                                      lambda i, p, t: (i // tiles_per_batch, 0, 0, i % tiles_per_batch, 0)))
    out_shape.append(jax.ShapeDtypeStruct((m, CONV_CH), F32))
    out_specs.append(pl.BlockSpec((IN_TM, CONV_CH), lambda i, j, t: (i, 0)))
    for gi, p in enumerate(plans):
        out_shape.append(jax.ShapeDtypeStruct((batch, p["keep"] * KV_SLAB, HEAD_DIM), F32))
        out_specs.append(pl.BlockSpec((1, p["rb"] * KV_SLAB, HEAD_DIM), kv_map(gi)))
    out_shape += [jax.ShapeDtypeStruct((ms, n_qkv, HEADS_PER_GROUP, HEAD_DIM), F32),
                  jax.ShapeDtypeStruct((ms, CONV_CH), F32)]
    out_specs += [
        pl.BlockSpec((ms, n_qkv, HEADS_PER_GROUP, HEAD_DIM), lambda i, p, t: (0, 0, 0, 0)),
        pl.BlockSpec((ms, CONV_CH), lambda i, j, t: (0, 0)),
    ]
    grid_spec = pltpu.PrefetchScalarGridSpec(
        num_scalar_prefetch=1, grid=(n_i, N_PHASES), in_specs=in_specs, out_specs=out_specs,
        scratch_shapes=[pltpu.VMEM((IN_TM, D_MODEL), BF16),
                        pltpu.VMEM((len(DILATED_GROUPS), IN_TM // IN_ROWS, HEADS_PER_GROUP, IN_ROWS, HEAD_DIM), F32),
                        pltpu.VMEM((ms, D_MODEL), BF16)])
    kern = functools.partial(_in_proj_kernel, n_i=n_i, plans=plans)
    return pl.pallas_call(kern, grid_spec=grid_spec, out_shape=out_shape,
                          compiler_params=_cparams(2), name="in_proj")(
        tab, x2, w_bf, cos_p, sin_p, xs2, cos_s, sin_s)


ATTN_TP = 2048


def _attn_kernel(q_ref, k_ref, v_ref, o_ref, lse_ref, k_scr, v_scr, o_scr, lse_scr, *, d, lc):
    n = pl.program_id(1)
    n_qb = lc // Q_BLOCK

    @pl.when(n == 0)
    def _():
        k_scr[:, 0:Q_BLOCK, :] = jnp.zeros((d, Q_BLOCK, GROUP_W), BF16)
        v_scr[:, 0:Q_BLOCK, :] = jnp.zeros((d, Q_BLOCK, GROUP_W), BF16)

    k_scr[:, Q_BLOCK:, :] = k_ref[0, 0]
    v_scr[:, Q_BLOCK:, :] = v_ref[0, 0]
    qq = lax.broadcasted_iota(jnp.int32, (Q_BLOCK, 2 * Q_BLOCK), 0)
    kk = lax.broadcasted_iota(jnp.int32, (Q_BLOCK, 2 * Q_BLOCK), 1)
    band = (kk >= qq) & (kk <= qq + Q_BLOCK)
    lane = lax.broadcasted_iota(jnp.int32, (Q_BLOCK, HEAD_DIM), 1)

    def unit(u, carry):
        r = u // n_qb
        qb = u % n_qb
        r0 = pl.multiple_of(qb * Q_BLOCK, Q_BLOCK)
        valid = band & ((kk >= Q_BLOCK) | (n > 0) | (qb > 0))
        if d == 1:
            rows = pl.ds(r0, Q_BLOCK)
        else:
            rows = pl.ds(r0 * d + r, Q_BLOCK, stride=d)
        lse_blk = jnp.zeros((Q_BLOCK, HEAD_DIM), F32)
        for h in range(HEADS_PER_GROUP):
            c0 = h * HEAD_DIM
            q = q_ref[0, r, pl.ds(r0, Q_BLOCK), c0:c0 + HEAD_DIM]
            k = k_scr[r, pl.ds(r0, 2 * Q_BLOCK), c0:c0 + HEAD_DIM]
            v = v_scr[r, pl.ds(r0, 2 * Q_BLOCK), c0:c0 + HEAD_DIM]
            s = lax.dot_general(q, k, (((1,), (1,)), ((), ())), preferred_element_type=F32)
            s = jnp.where(valid, s * ATTN_SCALE, NEG_BIG)
            m = jnp.max(s, axis=1, keepdims=True)
            e = jnp.exp(s - m)
            den = jnp.sum(e, axis=1, keepdims=True)
            o_scr[h, rows, :] = jnp.dot(e.astype(BF16), v, preferred_element_type=F32) / den
            lse_blk = jnp.where(lane == h, m + jnp.log(den), lse_blk)
        lse_scr[rows, :] = lse_blk
        return carry

    lax.fori_loop(0, d * n_qb, unit, 0, unroll=4)
    k_scr[:, 0:Q_BLOCK, :] = k_scr[:, lc:lc + Q_BLOCK, :]
    v_scr[:, 0:Q_BLOCK, :] = v_scr[:, lc:lc + Q_BLOCK, :]
    for h in range(HEADS_PER_GROUP):
        o_ref[:, h * HEAD_DIM:(h + 1) * HEAD_DIM] = o_scr[h].astype(BF16)
    lse_ref[...] = lse_scr[...]


def _attn_group(q_cm, kv_cm, gi, batch, seq):
    _, d = DILATION_GROUPS[gi]
    lc = ATTN_TP // d
    n_t = seq // ATTN_TP

    def plane(p):
        return pl.BlockSpec((1, 1, d, lc, GROUP_W), lambda b, n: (b, p, 0, n, 0))

    return pl.pallas_call(
        functools.partial(_attn_kernel, d=d, lc=lc),
        grid=(batch, n_t),
        in_specs=[pl.BlockSpec((1, d, lc, GROUP_W), lambda b, n: (b, 0, n, 0)), plane(0), plane(1)],
        out_specs=[pl.BlockSpec((ATTN_TP, GROUP_W), lambda b, n: (b * n_t + n, 0)),
                   pl.BlockSpec((ATTN_TP, HEAD_DIM), lambda b, n: (b * n_t + n, 0))],
        out_shape=[jax.ShapeDtypeStruct((batch * seq, GROUP_W), BF16),
                   jax.ShapeDtypeStruct((batch * seq, HEAD_DIM), F32)],
        scratch_shapes=[pltpu.VMEM((d, lc + Q_BLOCK, GROUP_W), BF16),
                        pltpu.VMEM((d, lc + Q_BLOCK, GROUP_W), BF16),
                        pltpu.VMEM((HEADS_PER_GROUP, ATTN_TP, HEAD_DIM), F32),
                        pltpu.VMEM((ATTN_TP, HEAD_DIM), F32)],
        compiler_params=_cparams(2), name=f"attn_g{gi}")(q_cm, kv_cm, kv_cm)


CONV_TC = 512
CONV_HALO = 32
CONV_ROWS = 64


def _conv_kernel(u_ref, halo_ref, w_ref, b_ref, g_ref, beta_ref, *rest, n_t, stream_chunks):
    n_s = len(stream_chunks)
    src_refs, c_ref, dst_refs, hist = rest[:n_s], rest[n_s], rest[n_s + 1:2 * n_s + 1], rest[2 * n_s + 1]
    t = pl.program_id(1)
    step = pl.program_id(0) * n_t + t

    for src, dst, n_chunks in zip(src_refs, dst_refs, stream_chunks):
        @pl.when(step < n_chunks)
        def _(src=src, dst=dst):
            dst[...] = src[...].astype(BF16)

    hist[0:CONV_HALO, :] = jnp.where(t > 0, halo_ref[0], 0.0)
    hist[CONV_HALO:CONV_HALO + CONV_TC, :] = u_ref[0]
    hist[CONV_HALO + CONV_TC:, :] = jnp.zeros((SUBLANES, CONV_CH), F32)
    lead = CONV_HALO - (CONV_K - 1)
    for rc in range(CONV_TC // CONV_ROWS):
        r0 = rc * CONV_ROWS
        acc = jnp.broadcast_to(b_ref[...], (CONV_ROWS, CONV_CH))
        for s in range(SUBLANES):
            grp = None
            for c in range(s, lead + CONV_K, SUBLANES):
                if c < lead:
                    continue
                term = hist[r0 + c - s:r0 + c - s + CONV_ROWS + SUBLANES, :] * w_ref[c - lead:c - lead + 1, :]
                grp = term if grp is None else grp + term
            acc = acc + grp[s:s + CONV_ROWS, :]
        y = _layer_norm_rows(acc, g_ref[...], beta_ref[...])
        c_ref[0, r0:r0 + CONV_ROWS, :] = (y * jax.nn.sigmoid(y)).astype(BF16)


def _conv_prompt(u2, conv_w, conv_b, ln_g, ln_b, batch, seq, cast_streams):
    u3 = u2.reshape(batch, seq, CONV_CH)
    per = CONV_TC // CONV_HALO
    n_t = seq // CONV_TC
    vec = pl.BlockSpec((1, CONV_CH), lambda b, t: (0, 0))
    stream_specs, stream_chunks = [], []
    for w, axis, chunk in cast_streams:
        n_chunks = w.shape[axis] // chunk
        assert w.ndim == 2 and n_chunks * chunk == w.shape[axis] and n_chunks <= batch * n_t
        block = (chunk, w.shape[1]) if axis == 0 else (w.shape[0], chunk)

        def index_map(b, t, axis=axis, n_chunks=n_chunks):
            k = jnp.minimum(b * n_t + t, n_chunks - 1)
            return (k, 0) if axis == 0 else (0, k)

        stream_specs.append(pl.BlockSpec(block, index_map))
        stream_chunks.append(n_chunks)
    outs = pl.pallas_call(
        functools.partial(_conv_kernel, n_t=n_t, stream_chunks=tuple(stream_chunks)),
        grid=(batch, n_t),
        in_specs=[pl.BlockSpec((1, CONV_TC, CONV_CH), lambda b, t: (b, t, 0)),
                  pl.BlockSpec((1, CONV_HALO, CONV_CH), lambda b, t: (b, jnp.maximum(t * per - 1, 0), 0)),
                  pl.BlockSpec((CONV_K, CONV_CH), lambda b, t: (0, 0)), vec, vec, vec] + stream_specs,
        out_specs=[pl.BlockSpec((1, CONV_TC, CONV_CH), lambda b, t: (b, t, 0))] + stream_specs,
        out_shape=[jax.ShapeDtypeStruct((batch, seq, CONV_CH), BF16)]
                  + [jax.ShapeDtypeStruct(w.shape, BF16) for w, _, _ in cast_streams],
        scratch_shapes=[pltpu.VMEM((CONV_HALO + CONV_TC + SUBLANES, CONV_CH), F32)],
        compiler_params=_cparams(2), name="conv_prompt")(
        u3, u3, conv_w, conv_b[None, :], ln_g[None, :], ln_b[None, :], *[w for w, _, _ in cast_streams])
    return outs[0].reshape(batch * seq, CONV_CH), outs[1:]


KV_SLAB = 2 * HEADS_PER_GROUP
SAMPLE_SHIFT_GROUPS = (0, 1)
FFN_SHIFT_GROUP = 2
FFN_SHIFT_ROWS = 2048


def _shift_cache(cref, out_ref, d, t_new):
    slab = d * KV_SLAB
    shift = t_new * KV_SLAB
    n_rows = cref.shape[1]
    if shift % slab == 0:
        s = shift // slab
        out_ref[0, 0:(n_rows - s) * slab] = cref[0, s:n_rows].reshape((n_rows - s) * slab, HEAD_DIM)
    else:
        assert shift < slab and shift % 8 == 0
        out_ref[0, 0:slab - shift] = cref[0, 0, shift:slab]

        def body(ci, carry):
            out_ref[0, pl.ds(pl.multiple_of(ci * slab - shift, 8), slab)] = cref[0, ci]
            return carry

        lax.fori_loop(1, n_rows, body, 0)


def _new_kv_rows(qkv_ref, gi, t_new):
    slabs = []
    for t in range(t_new):
        slabs += [qkv_ref[0, t, N_GROUPS + gi], qkv_ref[0, t, 2 * N_GROUPS + gi]]
    return jnp.concatenate(slabs, axis=0)


def _sample_kernel(qkv_ref, c0_ref, c1_ref, c2_ref, st_ref, u_ref, w_ref, b_ref, g_ref, beta_ref,
                   mix_ref, nconv_ref, *rest, t_new):
    cache_refs = (c0_ref, c1_ref, c2_ref)
    nk_refs = dict(zip(SAMPLE_SHIFT_GROUPS, rest[:-1]))
    hist = rest[-1]
    hp = HEADS_PER_GROUP
    outs = [[None] * N_GROUPS for _ in range(t_new)]
    lses = [[None] * N_GROUPS for _ in range(t_new)]
    for gi, (_, d) in enumerate(DILATION_GROUPS):
        cref = cache_refs[gi]
        n_rows = cref.shape[1]
        row = lax.broadcasted_iota(jnp.int32, (n_rows, hp, 1), 0)
        if gi in nk_refs:
            _shift_cache(cref, nk_refs[gi], d, t_new)
            keep = nk_refs[gi].shape[1] - t_new * KV_SLAB
            nk_refs[gi][0, keep:, :] = _new_kv_rows(qkv_ref, gi, t_new)
        for t in range(t_new):
            s0 = (t % d) * KV_SLAB
            first_row = t // d
            q = qkv_ref[0, t, gi][None]
            kc = cref[0, :, s0:s0 + hp, :]
            vc = cref[0, :, s0 + hp:s0 + 2 * hp, :]
            s_c = jnp.sum(kc * q, axis=2, keepdims=True) * ATTN_SCALE
            if first_row > 0:
                s_c = jnp.where(row >= first_row, s_c, NEG_BIG)
            newest = [t - d * jj for jj in range(t // d + 1)]
            s_n, v_n = [], []
            for tn in newest:
                kn = qkv_ref[0, tn, N_GROUPS + gi][None]
                v_n.append(qkv_ref[0, tn, 2 * N_GROUPS + gi][None])
                s_n.append(jnp.sum(kn * q, axis=2, keepdims=True) * ATTN_SCALE)
            m = jnp.max(s_c, axis=0, keepdims=True)
            for sn in s_n:
                m = jnp.maximum(m, sn)
            e_c = jnp.exp(s_c - m)
            den = jnp.sum(e_c, axis=0, keepdims=True)
            acc = jnp.sum(e_c * vc, axis=0, keepdims=True)
            for sn, v1 in zip(s_n, v_n):
                e_n = jnp.exp(sn - m)
                den = den + e_n
                acc = acc + e_n * v1
            outs[t][gi] = acc / den
            lses[t][gi] = m + jnp.log(den)
    for t in range(t_new):
        ls = lses[t]
        m = functools.reduce(jnp.maximum, ls)
        es = [jnp.exp(l - m) for l in ls]
        tot = functools.reduce(lambda a, b: a + b, es)
        for gi in range(N_GROUPS):
            slab = (outs[t][gi] * (es[gi] / tot))[0]
            for h in range(hp):
                c0 = gi * GROUP_W + h * HEAD_DIM
                mix_ref[0, t:t + 1, c0:c0 + HEAD_DIM] = slab[h:h + 1, :]
    n_state = CONV_K - 1
    hist[0:n_state, :] = st_ref[0]
    hist[n_state:n_state + t_new, :] = u_ref[0]
    acc = jnp.broadcast_to(b_ref[...], (t_new, CONV_CH))
    for k in range(CONV_K):
        acc = acc + hist[k:k + t_new, :] * w_ref[k:k + 1, :]
    y = _layer_norm_rows(acc, g_ref[...], beta_ref[...])
    mix_ref[0, :, ATTN_WIDTH:] = y * jax.nn.sigmoid(y)
    nconv_ref[0] = hist[t_new:t_new + n_state, :]


def _sample_mixers(qkvs, us, caches, state, conv_w, conv_b, ln_g, ln_b, dec_batch, t_new):
    n_qkv = qkvs.shape[1]
    hp = HEADS_PER_GROUP
    qkv5 = qkvs.reshape(dec_batch, t_new, n_qkv, hp, HEAD_DIM)
    u3 = us.reshape(dec_batch, t_new, CONV_CH)
    cache_in, cache_specs, nk_shapes, nk_specs = [], [], [], []
    for gi, ((window, d), cache) in enumerate(zip(DILATION_GROUPS, caches)):
        buf = cache.shape[1]
        assert buf == window and buf % d == 0
        n_rows = buf // d
        cache_in.append(cache.reshape(dec_batch, n_rows, d * KV_SLAB, HEAD_DIM))
        if gi in SAMPLE_SHIFT_GROUPS:
            cache_specs.append(pl.BlockSpec((1, n_rows, d * KV_SLAB, HEAD_DIM), lambda b: (b, 0, 0, 0)))
            nk_shapes.append(jax.ShapeDtypeStruct((dec_batch, buf * KV_SLAB, HEAD_DIM), F32))
            nk_specs.append(pl.BlockSpec((1, buf * KV_SLAB, HEAD_DIM), lambda b: (b, 0, 0)))
        else:
            used = min(d, t_new) * KV_SLAB
            cache_specs.append(pl.BlockSpec((1, n_rows, used, HEAD_DIM), lambda b: (b, 0, 0, 0)))
    vec = pl.BlockSpec((1, CONV_CH), lambda b: (0, 0))
    n_state = CONV_K - 1
    mix, nconv, *nks = pl.pallas_call(
        functools.partial(_sample_kernel, t_new=t_new),
        grid=(dec_batch,),
        in_specs=[pl.BlockSpec((1, t_new, n_qkv, hp, HEAD_DIM), lambda b: (b, 0, 0, 0, 0))] + cache_specs + [
            pl.BlockSpec((1, n_state, CONV_CH), lambda b: (b, 0, 0)),
            pl.BlockSpec((1, t_new, CONV_CH), lambda b: (b, 0, 0)),
            pl.BlockSpec((CONV_K, CONV_CH), lambda b: (0, 0)), vec, vec, vec],
        out_specs=[pl.BlockSpec((1, t_new, D_MODEL), lambda b: (b, 0, 0)),
                   pl.BlockSpec((1, n_state, CONV_CH), lambda b: (b, 0, 0))] + nk_specs,
        out_shape=[jax.ShapeDtypeStruct((dec_batch, t_new, D_MODEL), F32),
                   jax.ShapeDtypeStruct((dec_batch, n_state, CONV_CH), F32)] + nk_shapes,
        scratch_shapes=[pltpu.VMEM((n_state + t_new + 6, CONV_CH), F32)],
        compiler_params=_cparams(1), name="sample_mixers")(
        qkv5, *cache_in, state, u3, conv_w, conv_b[None, :], ln_g[None, :], ln_b[None, :])
    new_caches = {gi: nk.reshape(caches[gi].shape) for gi, nk in zip(SAMPLE_SHIFT_GROUPS, nks)}
    return mix.reshape(dec_batch * t_new, D_MODEL), nconv, new_caches


OUT_TM = 512
OUT_ROWS = 128


def _out_proj_kernel(o0_ref, o1_ref, o2_ref, l0_ref, l1_ref, l2_ref, c_ref, x_ref, w_ref, g_ref, b_ref,
                     mixs_ref, xs_ref, x1_ref, x1bf_ref, x1s_ref, mix_scr, y_scr, *, n_i, alpha):
    i = pl.program_id(0)
    o_refs = (o0_ref, o1_ref, o2_ref)
    chunks = [slice(r0, r0 + OUT_ROWS) for r0 in range(0, OUT_TM, OUT_ROWS)]

    def project(rows):
        ls = [l0_ref[rows, :], l1_ref[rows, :], l2_ref[rows, :]]
        m = jnp.maximum(jnp.maximum(ls[0], ls[1]), ls[2])
        es = [jnp.exp(l - m) for l in ls]
        inv = 1.0 / (es[0] + es[1] + es[2])
        for gi in range(N_GROUPS):
            a = es[gi] * inv
            for h in range(HEADS_PER_GROUP):
                c0 = h * HEAD_DIM
                og = o_refs[gi][rows, c0:c0 + HEAD_DIM].astype(F32)
                mix_scr[rows, gi * GROUP_W + c0:gi * GROUP_W + c0 + HEAD_DIM] = (og * a[:, h:h + 1]).astype(BF16)
        mix_scr[rows, ATTN_WIDTH:] = c_ref[rows, :]
        y_scr[rows, :] = (jnp.dot(mix_scr[rows, :], w_ref[...], preferred_element_type=F32)
                          + alpha * x_ref[rows, :])

    def normalize(rows):
        for r0 in range(rows.start, rows.stop, LN_STRIP):
            strip = slice(r0, r0 + LN_STRIP)
            x1 = _layer_norm_rows(y_scr[strip, :], g_ref[...], b_ref[...])
            x1_ref[strip, :] = x1
            x1bf_ref[strip, :] = x1.astype(BF16)

    project(chunks[0])
    for c in range(1, len(chunks)):
        project(chunks[c])
        normalize(chunks[c - 1])
    normalize(chunks[-1])

    @pl.when(i == n_i - 1)
    def _():
        ys = jnp.dot(mixs_ref[...].astype(BF16), w_ref[...], preferred_element_type=F32) + alpha * xs_ref[...]
        x1s_ref[...] = _layer_norm_rows(ys, g_ref[...], b_ref[...])


def _out_proj(os, lses, c, x2, w_bf, ln_g, ln_b, mixs, xs2, alpha):
    m = x2.shape[0]
    ms = xs2.shape[0]
    n_i = m // OUT_TM
    row = lambda w: pl.BlockSpec((OUT_TM, w), lambda i: (i, 0))
    whole = lambda a: pl.BlockSpec(a.shape, lambda i: (0,) * a.ndim)
    g2, b2 = ln_g[None, :], ln_b[None, :]
    return pl.pallas_call(
        functools.partial(_out_proj_kernel, n_i=n_i, alpha=alpha),
        grid=(n_i,),
        in_specs=[row(GROUP_W)] * 3 + [row(HEAD_DIM)] * 3 + [row(CONV_CH), row(D_MODEL),
                  whole(w_bf), whole(g2), whole(b2), whole(mixs), whole(xs2)],
        out_specs=[row(D_MODEL), row(D_MODEL), pl.BlockSpec((ms, D_MODEL), lambda i: (0, 0))],
        out_shape=[jax.ShapeDtypeStruct((m, D_MODEL), F32), jax.ShapeDtypeStruct((m, D_MODEL), BF16),
                   jax.ShapeDtypeStruct((ms, D_MODEL), F32)],
        scratch_shapes=[pltpu.VMEM((OUT_TM, D_MODEL), BF16), pltpu.VMEM((OUT_TM, D_MODEL), F32)],
        compiler_params=_cparams(1), name="out_proj")(
        *os, *lses, c, x2, w_bf, g2, b2, mixs, xs2)


FFN_TM = 1024
FFN_TF = 512
FFN_ROWS = 1024
FFN_RES_W = 256


def _ffn_kernel(xbf_ref, xres_ref, wg_ref, wu_ref, wd_ref, g_ref, b_ref, xs_ref, ca_ref, cb_ref, qkv_ref,
                y_ref, ys_ref, nk_ref, xsbf, *, n_i, n_f, alpha, t_new, n_chunks, chunks_per_entry):
    i = pl.program_id(0)
    f = pl.program_id(1)

    shift = t_new * KV_SLAB
    rows = ca_ref.shape[1]
    chunk = jnp.minimum(i * n_f + f, n_chunks - 1) % chunks_per_entry
    nk_ref[0, 0:rows - shift, :] = ca_ref[0, shift:rows, :]
    nk_ref[0, rows - shift:rows, :] = jnp.where(chunk == chunks_per_entry - 1,
                                                _new_kv_rows(qkv_ref, FFN_SHIFT_GROUP, t_new), cb_ref[0])

    def swiglu_down(xb):
        gate = jnp.dot(xb, wg_ref[...], preferred_element_type=F32)
        up = jnp.dot(xb, wu_ref[...], preferred_element_type=F32)
        act = (gate * jax.nn.sigmoid(gate) * up).astype(BF16)
        return jnp.dot(act, wd_ref[...], preferred_element_type=F32)

    @pl.when(f == 0)
    def _():
        y_ref[...] = jnp.zeros_like(y_ref)

    for r0 in range(0, FFN_TM, FFN_ROWS):
        y_ref[r0:r0 + FFN_ROWS, :] += swiglu_down(xbf_ref[r0:r0 + FFN_ROWS, :])

    for c in range(D_MODEL // FFN_RES_W):
        @pl.when(f == c)
        def _(c=c):
            y_ref[:, c * FFN_RES_W:(c + 1) * FFN_RES_W] += alpha * xres_ref[...]

    @pl.when(f == n_f - 1)
    def _():
        y_ref[...] = _layer_norm_rows(y_ref[...], g_ref[...], b_ref[...])

    @pl.when(i == n_i - 1)
    def _():
        @pl.when(f == 0)
        def _():
            xsbf[...] = xs_ref[...].astype(BF16)

        downs = swiglu_down(xsbf[...])

        @pl.when(f == 0)
        def _():
            ys_ref[...] = downs + alpha * xs_ref[...]

        @pl.when(f > 0)
        def _():
            ys_ref[...] += downs

        @pl.when(f == n_f - 1)
        def _():
            ys_ref[...] = _layer_norm_rows(ys_ref[...], g_ref[...], b_ref[...])


def _ffn(x1, x1bf, x1s, wg_bf, wu_bf, wd_bf, ln_g, ln_b, alpha, cache, qkvs, t_new):
    m = x1.shape[0]
    ms = x1s.shape[0]
    hidden = wg_bf.shape[1]
    n_i = m // FFN_TM
    n_f = hidden // FFN_TF
    n_res = D_MODEL // FFN_RES_W
    assert n_f >= n_res
    g2, b2 = ln_g[None, :], ln_b[None, :]
    dec_batch = cache.shape[0]
    cache_flat = cache.reshape(dec_batch, -1, HEAD_DIM)
    entry_rows = cache_flat.shape[1]
    shift = t_new * KV_SLAB
    cpe = entry_rows // FFN_SHIFT_ROWS
    n_chunks = dec_batch * cpe
    assert cpe * FFN_SHIFT_ROWS == entry_rows and n_chunks <= n_i * n_f and FFN_SHIFT_ROWS % shift == 0
    qkv5 = qkvs.reshape(dec_batch, t_new, qkvs.shape[1], HEADS_PER_GROUP, HEAD_DIM)

    def chunk_of(i, f):
        c = jnp.minimum(i * n_f + f, n_chunks - 1)
        return c // cpe, c % cpe

    def chunk_map(i, f):
        b, k = chunk_of(i, f)
        return b, k, 0

    def follow_map(i, f):
        b, k = chunk_of(i, f)
        return b, jnp.minimum((k + 1) * (FFN_SHIFT_ROWS // shift), entry_rows // shift - 1), 0

    y, ys, nk = pl.pallas_call(
        functools.partial(_ffn_kernel, n_i=n_i, n_f=n_f, alpha=alpha, t_new=t_new, n_chunks=n_chunks,
                          chunks_per_entry=cpe),
        grid=(n_i, n_f),
        in_specs=[pl.BlockSpec((FFN_TM, D_MODEL), lambda i, f: (i, 0)),
                  pl.BlockSpec((FFN_TM, FFN_RES_W), lambda i, f: (i, jnp.minimum(f, n_res - 1))),
                  pl.BlockSpec((D_MODEL, FFN_TF), lambda i, f: (0, f)),
                  pl.BlockSpec((D_MODEL, FFN_TF), lambda i, f: (0, f)),
                  pl.BlockSpec((FFN_TF, D_MODEL), lambda i, f: (f, 0)),
                  pl.BlockSpec((1, D_MODEL), lambda i, f: (0, 0)),
                  pl.BlockSpec((1, D_MODEL), lambda i, f: (0, 0)),
                  pl.BlockSpec((ms, D_MODEL), lambda i, f: (0, 0)),
                  pl.BlockSpec((1, FFN_SHIFT_ROWS, HEAD_DIM), chunk_map),
                  pl.BlockSpec((1, shift, HEAD_DIM), follow_map),
                  pl.BlockSpec((1,) + qkv5.shape[1:], lambda i, f: (chunk_of(i, f)[0], 0, 0, 0, 0))],
        out_specs=[pl.BlockSpec((FFN_TM, D_MODEL), lambda i, f: (i, 0)),
                   pl.BlockSpec((ms, D_MODEL), lambda i, f: (0, 0)),
                   pl.BlockSpec((1, FFN_SHIFT_ROWS, HEAD_DIM), chunk_map)],
        out_shape=[jax.ShapeDtypeStruct((m, D_MODEL), F32), jax.ShapeDtypeStruct((ms, D_MODEL), F32),
                   jax.ShapeDtypeStruct(cache_flat.shape, F32)],
        scratch_shapes=[pltpu.VMEM((ms, D_MODEL), BF16)],
        compiler_params=_cparams(2), name="ffn")(
        x1bf, x1, wg_bf, wu_bf, wd_bf, g2, b2, x1s, cache_flat, cache_flat, qkv5)
    return y, ys, nk.reshape(cache.shape)


def kernel(x_prompt, x_sample, cache_kv_w128, cache_kv_w512, cache_kv_w2048, state_conv, w_in, w_out,
           conv_w, conv_b, conv_ln_g, conv_ln_b, ln1_g, ln1_b, w_gate, w_up, w_down, ln2_g, ln2_b):
    depth = w_in.shape[0]
    batch, seq, _ = x_prompt.shape
    dec_batch, t_new, _ = x_sample.shape
    caches = (cache_kv_w128, cache_kv_w512, cache_kv_w2048)
    alpha = (2.0 * depth) ** 0.25

    cos_p, sin_p = _rope_tables(np.arange(seq))
    cos_s, sin_s = _rope_tables(np.tile(PAST_LEN + np.arange(t_new), dec_batch))

    xp = x_prompt.reshape(batch * seq, D_MODEL)
    xs = x_sample.reshape(dec_batch * t_new, D_MODEL)
    kvp = [[] for _ in range(N_GROUPS)]
    kvs = [[] for _ in range(N_GROUPS)]
    convp, convs = [], []
    for l in range(depth):
        q0, q1, q2, kc0, kc1, kc2, u, kv0, kv1, kv2, qkvs, us = _in_proj(
            xp, xs, w_in[l].astype(BF16), cos_p, sin_p, cos_s, sin_s, batch, seq)
        os, lses = zip(*[_attn_group(q_cm, kv_cm, gi, batch, seq)
                         for gi, (q_cm, kv_cm) in enumerate(zip((q0, q1, q2), (kc0, kc1, kc2)))])
        conv_steps = batch * seq // CONV_TC
        c, (wo_bf, wg_bf, wu_bf, wd_bf) = _conv_prompt(
            u, conv_w[l], conv_b[l], conv_ln_g[l], conv_ln_b[l], batch, seq,
            [(w, 0, w.shape[0] // conv_steps) for w in (w_out[l], w_gate[l], w_up[l], w_down[l])])
        layer_caches = [cc[l] for cc in caches]
        mixs, nconv_s, new_caches = _sample_mixers(qkvs, us, layer_caches, state_conv[l], conv_w[l], conv_b[l],
                                                   conv_ln_g[l], conv_ln_b[l], dec_batch, t_new)
        x1, x1bf, x1s = _out_proj(os, lses, c, xp, wo_bf, ln1_g[l], ln1_b[l], mixs, xs, alpha)
        xp, xs, new_caches[FFN_SHIFT_GROUP] = _ffn(x1, x1bf, x1s, wg_bf, wu_bf, wd_bf, ln2_g[l], ln2_b[l], alpha,
                                                   layer_caches[FFN_SHIFT_GROUP], qkvs, t_new)
        for gi, kv in enumerate((kv0, kv1, kv2)):
            kvp[gi].append(kv.reshape(batch, kv.shape[1] // KV_SLAB, 2, HEADS_PER_GROUP, HEAD_DIM))
            kvs[gi].append(new_caches[gi])
        convp.append(u.reshape(batch, seq, CONV_CH)[:, seq - (CONV_K - 1):])
        convs.append(nconv_s)

    y_prompt = xp.reshape(batch, seq, D_MODEL)
    y_sample = xs.reshape(dec_batch, t_new, D_MODEL)
    return (y_prompt, y_sample, jnp.stack(kvp[0]), jnp.stack(kvp[1]), jnp.stack(kvp[2]), jnp.stack(convp),
            jnp.stack(kvs[0]), jnp.stack(kvs[1]), jnp.stack(kvs[2]), jnp.stack(convs))
```

```python
import functools

import numpy as np
import jax
import jax.numpy as jnp
from jax import lax
from jax.experimental import pallas as pl
from jax.experimental.pallas import tpu as pltpu

D_MODEL = 2048
HEAD_DIM = 128
CONV_CH = D_MODEL // 4
ATTN_WIDTH = D_MODEL - CONV_CH
DILATION_GROUPS = ((128, 1), (512, 4), (2048, 16))
N_GROUPS = len(DILATION_GROUPS)
DILATED_GROUPS = tuple(gi for gi, (_, d) in enumerate(DILATION_GROUPS) if d > 1)
HEADS_PER_GROUP = ATTN_WIDTH // HEAD_DIM // N_GROUPS
GROUP_W = HEADS_PER_GROUP * HEAD_DIM
CONV_K = 31
ROPE_THETA = 10000.0
LN_EPS = 1e-5
Q_BLOCK = 128
ATTN_SCALE = HEAD_DIM ** -0.5
NEG_BIG = -1e30
PAST_LEN = 16384

F32 = jnp.float32
BF16 = jnp.bfloat16

VMEM_LIMIT = 58 * 1024 * 1024
LN_STRIP = 16
SUBLANES = 8


def _cparams(n_axes):
    return pltpu.CompilerParams(dimension_semantics=("arbitrary",) * n_axes,
                                vmem_limit_bytes=VMEM_LIMIT)


def _layer_norm_rows(y, g, b):
    mu = jnp.mean(y, axis=-1, keepdims=True)
    yc = y - mu
    var = jnp.mean(yc * yc, axis=-1, keepdims=True)
    return yc * lax.rsqrt(var + LN_EPS) * g + b


def _rope_tables(pos):
    half = HEAD_DIM // 2
    inv = ROPE_THETA ** (-np.arange(half, dtype=np.float64) / half)
    ang = np.asarray(pos, np.float64)[:, None] * inv[None, :]
    cos, sin = np.cos(ang), np.sin(ang)
    return (jnp.asarray(np.concatenate([cos, cos], axis=1), F32),
            jnp.asarray(np.concatenate([-sin, sin], axis=1), F32))


def _rope(h, cos, sin):
    parts = []
    for hh in range(HEADS_PER_GROUP):
        hs = h[:, hh * HEAD_DIM:(hh + 1) * HEAD_DIM]
        parts.append(hs * cos + pltpu.roll(hs, HEAD_DIM // 2, axis=1) * sin)
    return jnp.concatenate(parts, axis=1)


IN_TM = 512
IN_TN = GROUP_W
IN_ROWS = 256
PH_Q_GLU, PH_KV = range(2)
N_PHASES = 2


def _kv_window_plan(seq, batch):
    tiles_per_batch = seq // IN_TM
    n_i = batch * tiles_per_batch
    plans = []
    for window, _ in DILATION_GROUPS:
        keep = min(window, seq)
        rb = min(keep, IN_TM)
        first_tile = (seq - keep) // IN_TM
        row_lo = (seq - keep) - first_tile * IN_TM
        writes = []
        for i in range(n_i):
            b, it = divmod(i, tiles_per_batch)
            if it >= first_tile:
                writes.append((i * N_PHASES + PH_KV, (b, it - first_tile)))
        writes.sort()
        tab = np.zeros((n_i * N_PHASES, 2), np.int32)
        w = 0
        for step in range(n_i * N_PHASES):
            while w < len(writes) - 1 and writes[w][0] < step:
                w += 1
            tab[step] = writes[w][1]
        plans.append(dict(keep=keep, rb=rb, row_lo=row_lo, tab=tab))
    return plans


def _in_proj_kernel(tab_ref, x_ref, w_ref, cos_ref, sin_ref, xs_ref, coss_ref, sins_ref,
                    q0_ref, q1_ref, q2_ref, kc0_ref, kc1_ref, kc2_ref, u_ref, kv0_ref, kv1_ref, kv2_ref,
                    qkvs_ref, us_ref, xbf, de_scr, xsbf, *, n_i, plans):
    del tab_ref
    i = pl.program_id(0)
    phase = pl.program_id(1)
    kv_refs = (kv0_ref, kv1_ref, kv2_ref)
    q_refs = (q0_ref, q1_ref, q2_ref)
    kc_refs = (kc0_ref, kc1_ref, kc2_ref)
    glu_col = 3 * N_GROUPS

    chunks = [slice(c * IN_ROWS, (c + 1) * IN_ROWS) for c in range(IN_TM // IN_ROWS)]

    def col_dot(lhs, col):
        return jnp.dot(lhs, w_ref[:, col * IN_TN:(col + 1) * IN_TN], preferred_element_type=F32)

    def rope_rows(val, rows):
        return _rope(val, cos_ref[rows, :], sin_ref[rows, :])

    def store_class_major(ref, lead, gi, c, val):
        d = DILATION_GROUPS[gi][1]
        per = IN_ROWS // d
        if d == 1:
            ref[lead + (0, chunks[c], slice(None))] = val.astype(BF16)
            return
        slot = DILATED_GROUPS.index(gi)
        for h in range(HEADS_PER_GROUP):
            de_scr[slot, c, h] = val[:, h * HEAD_DIM:(h + 1) * HEAD_DIM]
        for r in range(d):
            for h in range(HEADS_PER_GROUP):
                ref[lead + (r, slice(c * per, (c + 1) * per), slice(h * HEAD_DIM, (h + 1) * HEAD_DIM))] = (
                    de_scr[slot, c, h, pl.ds(r, per, stride=d), :].astype(BF16))

    def store_window_rows(gi, rows, val, first_slab_row):
        row_lo = plans[gi]["row_lo"]
        lo = max(rows.start, row_lo)
        hi = min(rows.stop, row_lo + plans[gi]["rb"])
        if lo < hi:
            for h in range(HEADS_PER_GROUP):
                dst = pl.ds((lo - row_lo) * KV_SLAB + first_slab_row + h, hi - lo, stride=KV_SLAB)
                kv_refs[gi][0, dst, :] = val[lo - rows.start:hi - rows.start, h * HEAD_DIM:(h + 1) * HEAD_DIM]

    def store_heads(slot, val):
        for h in range(HEADS_PER_GROUP):
            qkvs_ref[:, slot, h, :] = val[:, h * HEAD_DIM:(h + 1) * HEAD_DIM]

    is_last = i == n_i - 1

    @pl.when(phase == PH_Q_GLU)
    def _():
        for c, rows in enumerate(chunks):
            xbf[rows, :] = x_ref[rows, :].astype(BF16)
            for gi in range(N_GROUPS):
                store_class_major(q_refs[gi], (0,), gi, c, rope_rows(col_dot(xbf[rows, :], gi), rows))
            u_ref[rows, :] = col_dot(xbf[rows, :], glu_col) * jax.nn.sigmoid(col_dot(xbf[rows, :], glu_col + 1))

        @pl.when(is_last)
        def _():
            xsbf[...] = xs_ref[...].astype(BF16)
            for gi in range(N_GROUPS):
                store_heads(gi, _rope(col_dot(xsbf[...], gi), coss_ref[...], sins_ref[...]))
            us_ref[...] = col_dot(xsbf[...], glu_col) * jax.nn.sigmoid(col_dot(xsbf[...], glu_col + 1))

    @pl.when(phase == PH_KV)
    def _():
        for c, rows in enumerate(chunks):
            for gi in range(N_GROUPS):
                r = rope_rows(col_dot(xbf[rows, :], N_GROUPS + gi), rows)
                store_class_major(kc_refs[gi], (0, 0), gi, c, r)
                store_window_rows(gi, rows, r, 0)
                acc = col_dot(xbf[rows, :], 2 * N_GROUPS + gi)
                store_class_major(kc_refs[gi], (0, 1), gi, c, acc)
                store_window_rows(gi, rows, acc, HEADS_PER_GROUP)

        @pl.when(is_last)
        def _():
            for gi in range(N_GROUPS):
                store_heads(N_GROUPS + gi, _rope(col_dot(xsbf[...], N_GROUPS + gi), coss_ref[...], sins_ref[...]))
                store_heads(2 * N_GROUPS + gi, col_dot(xsbf[...], 2 * N_GROUPS + gi))


def _in_proj(x2, xs2, w_bf, cos_p, sin_p, cos_s, sin_s, batch, seq):
    m = x2.shape[0]
    ms = xs2.shape[0]
    n_i = m // IN_TM
    n_qkv = 3 * N_GROUPS
    assert w_bf.shape[1] == (n_qkv + 2) * IN_TN
    tiles_per_batch = seq // IN_TM
    plans = _kv_window_plan(seq, batch)
    tab = jnp.asarray(np.stack([p["tab"] for p in plans]).reshape(-1))
    n_steps = n_i * N_PHASES

    def kv_map(gi):
        def f(i, p, tab_ref):
            base = (gi * n_steps + i * N_PHASES + p) * 2
            return tab_ref[base], tab_ref[base + 1], 0
        return f

    in_specs = [
        pl.BlockSpec((IN_TM, D_MODEL), lambda i, j, t: (i, 0)),
        pl.BlockSpec(w_bf.shape, lambda i, j, t: (0, 0)),
        pl.BlockSpec((IN_TM, HEAD_DIM), lambda i, j, t: (i % tiles_per_batch, 0)),
        pl.BlockSpec((IN_TM, HEAD_DIM), lambda i, j, t: (i % tiles_per_batch, 0)),
        pl.BlockSpec((ms, D_MODEL), lambda i, j, t: (0, 0)),
        pl.BlockSpec((ms, HEAD_DIM), lambda i, j, t: (0, 0)),
        pl.BlockSpec((ms, HEAD_DIM), lambda i, j, t: (0, 0)),
    ]
    out_shape, out_specs = [], []
    for _, d in DILATION_GROUPS:
        out_shape.append(jax.ShapeDtypeStruct((batch, d, seq // d, GROUP_W), BF16))
        out_specs.append(pl.BlockSpec((1, d, IN_TM // d, GROUP_W),
                                      lambda i, p, t: (i // tiles_per_batch, 0, i % tiles_per_batch, 0)))
    for _, d in DILATION_GROUPS:
        out_shape.append(jax.ShapeDtypeStruct((batch, 2, d, seq // d, GROUP_W), BF16))
        out_specs.append(pl.BlockSpec((1, 2, d, IN_TM // d, GROUP_W),
                                      lambda i, p, t: (i // tiles_per_batch, 0, 0, i % tiles_per_batch, 0)))
    out_shape.append(jax.ShapeDtypeStruct((m, CONV_CH), F32))
    out_specs.append(pl.BlockSpec((IN_TM, CONV_CH), lambda i, j, t: (i, 0)))
    for gi, p in enumerate(plans):
        out_shape.append(jax.ShapeDtypeStruct((batch, p["keep"] * KV_SLAB, HEAD_DIM), F32))
        out_specs.append(pl.BlockSpec((1, p["rb"] * KV_SLAB, HEAD_DIM), kv_map(gi)))
    out_shape += [jax.ShapeDtypeStruct((ms, n_qkv, HEADS_PER_GROUP, HEAD_DIM), F32),
                  jax.ShapeDtypeStruct((ms, CONV_CH), F32)]
    out_specs += [
        pl.BlockSpec((ms, n_qkv, HEADS_PER_GROUP, HEAD_DIM), lambda i, p, t: (0, 0, 0, 0)),
        pl.BlockSpec((ms, CONV_CH), lambda i, j, t: (0, 0)),
    ]
    grid_spec = pltpu.PrefetchScalarGridSpec(
        num_scalar_prefetch=1, grid=(n_i, N_PHASES), in_specs=in_specs, out_specs=out_specs,
        scratch_shapes=[pltpu.VMEM((IN_TM, D_MODEL), BF16),
                        pltpu.VMEM((len(DILATED_GROUPS), IN_TM // IN_ROWS, HEADS_PER_GROUP, IN_ROWS, HEAD_DIM), F32),
                        pltpu.VMEM((ms, D_MODEL), BF16)])
    kern = functools.partial(_in_proj_kernel, n_i=n_i, plans=plans)
    return pl.pallas_call(kern, grid_spec=grid_spec, out_shape=out_shape,
                          compiler_params=_cparams(2), name="in_proj")(
        tab, x2, w_bf, cos_p, sin_p, xs2, cos_s, sin_s)


ATTN_TP = 2048


def _attn_kernel(q_ref, k_ref, v_ref, o_ref, lse_ref, k_scr, v_scr, o_scr, lse_scr, *, d, lc):
    n = pl.program_id(1)
    n_qb = lc // Q_BLOCK

    @pl.when(n == 0)
    def _():
        k_scr[:, 0:Q_BLOCK, :] = jnp.zeros((d, Q_BLOCK, GROUP_W), BF16)
        v_scr[:, 0:Q_BLOCK, :] = jnp.zeros((d, Q_BLOCK, GROUP_W), BF16)

    k_scr[:, Q_BLOCK:, :] = k_ref[0, 0]
    v_scr[:, Q_BLOCK:, :] = v_ref[0, 0]
    qq = lax.broadcasted_iota(jnp.int32, (Q_BLOCK, 2 * Q_BLOCK), 0)
    kk = lax.broadcasted_iota(jnp.int32, (Q_BLOCK, 2 * Q_BLOCK), 1)
    band = (kk >= qq) & (kk <= qq + Q_BLOCK)
    lane = lax.broadcasted_iota(jnp.int32, (Q_BLOCK, HEAD_DIM), 1)

    def unit(u, carry):
        r = u // n_qb
        qb = u % n_qb
        r0 = pl.multiple_of(qb * Q_BLOCK, Q_BLOCK)
        valid = band & ((kk >= Q_BLOCK) | (n > 0) | (qb > 0))
        if d == 1:
            rows = pl.ds(r0, Q_BLOCK)
        else:
            rows = pl.ds(r0 * d + r, Q_BLOCK, stride=d)
        lse_blk = jnp.zeros((Q_BLOCK, HEAD_DIM), F32)
        for h in range(HEADS_PER_GROUP):
            c0 = h * HEAD_DIM
            q = q_ref[0, r, pl.ds(r0, Q_BLOCK), c0:c0 + HEAD_DIM]
            k = k_scr[r, pl.ds(r0, 2 * Q_BLOCK), c0:c0 + HEAD_DIM]
            v = v_scr[r, pl.ds(r0, 2 * Q_BLOCK), c0:c0 + HEAD_DIM]
            s = lax.dot_general(q, k, (((1,), (1,)), ((), ())), preferred_element_type=F32)
            s = jnp.where(valid, s * ATTN_SCALE, NEG_BIG)
            m = jnp.max(s, axis=1, keepdims=True)
            e = jnp.exp(s - m)
            den = jnp.sum(e, axis=1, keepdims=True)
            o_scr[h, rows, :] = jnp.dot(e.astype(BF16), v, preferred_element_type=F32) / den
            lse_blk = jnp.where(lane == h, m + jnp.log(den), lse_blk)
        lse_scr[rows, :] = lse_blk
        return carry

    lax.fori_loop(0, d * n_qb, unit, 0, unroll=4)
    k_scr[:, 0:Q_BLOCK, :] = k_scr[:, lc:lc + Q_BLOCK, :]
    v_scr[:, 0:Q_BLOCK, :] = v_scr[:, lc:lc + Q_BLOCK, :]
    for h in range(HEADS_PER_GROUP):
        o_ref[:, h * HEAD_DIM:(h + 1) * HEAD_DIM] = o_scr[h].astype(BF16)
    lse_ref[...] = lse_scr[...]


def _attn_group(q_cm, kv_cm, gi, batch, seq):
    _, d = DILATION_GROUPS[gi]
    lc = ATTN_TP // d
    n_t = seq // ATTN_TP

    def plane(p):
        return pl.BlockSpec((1, 1, d, lc, GROUP_W), lambda b, n: (b, p, 0, n, 0))

    return pl.pallas_call(
        functools.partial(_attn_kernel, d=d, lc=lc),
        grid=(batch, n_t),
        in_specs=[pl.BlockSpec((1, d, lc, GROUP_W), lambda b, n: (b, 0, n, 0)), plane(0), plane(1)],
        out_specs=[pl.BlockSpec((ATTN_TP, GROUP_W), lambda b, n: (b * n_t + n, 0)),
                   pl.BlockSpec((ATTN_TP, HEAD_DIM), lambda b, n: (b * n_t + n, 0))],
        out_shape=[jax.ShapeDtypeStruct((batch * seq, GROUP_W), BF16),
                   jax.ShapeDtypeStruct((batch * seq, HEAD_DIM), F32)],
        scratch_shapes=[pltpu.VMEM((d, lc + Q_BLOCK, GROUP_W), BF16),
                        pltpu.VMEM((d, lc + Q_BLOCK, GROUP_W), BF16),
                        pltpu.VMEM((HEADS_PER_GROUP, ATTN_TP, HEAD_DIM), F32),
                        pltpu.VMEM((ATTN_TP, HEAD_DIM), F32)],
        compiler_params=_cparams(2), name=f"attn_g{gi}")(q_cm, kv_cm, kv_cm)


CONV_TC = 512
CONV_HALO = 32
CONV_ROWS = 64


def _conv_kernel(u_ref, halo_ref, w_ref, b_ref, g_ref, beta_ref, *rest, n_t, stream_chunks):
    n_s = len(stream_chunks)
    src_refs, c_ref, dst_refs, hist = rest[:n_s], rest[n_s], rest[n_s + 1:2 * n_s + 1], rest[2 * n_s + 1]
    t = pl.program_id(1)
    step = pl.program_id(0) * n_t + t

    for src, dst, n_chunks in zip(src_refs, dst_refs, stream_chunks):
        @pl.when(step < n_chunks)
        def _(src=src, dst=dst):
            dst[...] = src[...].astype(BF16)

    hist[0:CONV_HALO, :] = jnp.where(t > 0, halo_ref[0], 0.0)
    hist[CONV_HALO:CONV_HALO + CONV_TC, :] = u_ref[0]
    hist[CONV_HALO + CONV_TC:, :] = jnp.zeros((SUBLANES, CONV_CH), F32)
    lead = CONV_HALO - (CONV_K - 1)
    for rc in range(CONV_TC // CONV_ROWS):
        r0 = rc * CONV_ROWS
        acc = jnp.broadcast_to(b_ref[...], (CONV_ROWS, CONV_CH))
        for s in range(SUBLANES):
            grp = None
            for c in range(s, lead + CONV_K, SUBLANES):
                if c < lead:
                    continue
                term = hist[r0 + c - s:r0 + c - s + CONV_ROWS + SUBLANES, :] * w_ref[c - lead:c - lead + 1, :]
                grp = term if grp is None else grp + term
            acc = acc + grp[s:s + CONV_ROWS, :]
        y = _layer_norm_rows(acc, g_ref[...], beta_ref[...])
        c_ref[0, r0:r0 + CONV_ROWS, :] = (y * jax.nn.sigmoid(y)).astype(BF16)


def _conv_prompt(u2, conv_w, conv_b, ln_g, ln_b, batch, seq, cast_streams):
    u3 = u2.reshape(batch, seq, CONV_CH)
    per = CONV_TC // CONV_HALO
    n_t = seq // CONV_TC
    vec = pl.BlockSpec((1, CONV_CH), lambda b, t: (0, 0))
    stream_specs, stream_chunks = [], []
    for w, axis, chunk in cast_streams:
        n_chunks = w.shape[axis] // chunk
        assert w.ndim == 2 and n_chunks * chunk == w.shape[axis] and n_chunks <= batch * n_t
        block = (chunk, w.shape[1]) if axis == 0 else (w.shape[0], chunk)

        def index_map(b, t, axis=axis, n_chunks=n_chunks):
            k = jnp.minimum(b * n_t + t, n_chunks - 1)
            return (k, 0) if axis == 0 else (0, k)

        stream_specs.append(pl.BlockSpec(block, index_map))
        stream_chunks.append(n_chunks)
    outs = pl.pallas_call(
        functools.partial(_conv_kernel, n_t=n_t, stream_chunks=tuple(stream_chunks)),
        grid=(batch, n_t),
        in_specs=[pl.BlockSpec((1, CONV_TC, CONV_CH), lambda b, t: (b, t, 0)),
                  pl.BlockSpec((1, CONV_HALO, CONV_CH), lambda b, t: (b, jnp.maximum(t * per - 1, 0), 0)),
                  pl.BlockSpec((CONV_K, CONV_CH), lambda b, t: (0, 0)), vec, vec, vec] + stream_specs,
        out_specs=[pl.BlockSpec((1, CONV_TC, CONV_CH), lambda b, t: (b, t, 0))] + stream_specs,
        out_shape=[jax.ShapeDtypeStruct((batch, seq, CONV_CH), BF16)]
                  + [jax.ShapeDtypeStruct(w.shape, BF16) for w, _, _ in cast_streams],
        scratch_shapes=[pltpu.VMEM((CONV_HALO + CONV_TC + SUBLANES, CONV_CH), F32)],
        compiler_params=_cparams(2), name="conv_prompt")(
        u3, u3, conv_w, conv_b[None, :], ln_g[None, :], ln_b[None, :], *[w for w, _, _ in cast_streams])
    return outs[0].reshape(batch * seq, CONV_CH), outs[1:]


KV_SLAB = 2 * HEADS_PER_GROUP
SAMPLE_SHIFT_GROUPS = (0, 1)
FFN_SHIFT_GROUP = 2
FFN_SHIFT_ROWS = 2048


def _shift_cache(cref, out_ref, d, t_new):
    slab = d * KV_SLAB
    shift = t_new * KV_SLAB
    n_rows = cref.shape[1]
    if shift % slab == 0:
        s = shift // slab
        out_ref[0, 0:(n_rows - s) * slab] = cref[0, s:n_rows].reshape((n_rows - s) * slab, HEAD_DIM)
    else:
        assert shift < slab and shift % 8 == 0
        out_ref[0, 0:slab - shift] = cref[0, 0, shift:slab]

        def body(ci, carry):
            out_ref[0, pl.ds(pl.multiple_of(ci * slab - shift, 8), slab)] = cref[0, ci]
            return carry

        lax.fori_loop(1, n_rows, body, 0)


def _new_kv_rows(qkv_ref, gi, t_new):
    slabs = []
    for t in range(t_new):
        slabs += [qkv_ref[0, t, N_GROUPS + gi], qkv_ref[0, t, 2 * N_GROUPS + gi]]
    return jnp.concatenate(slabs, axis=0)


def _sample_kernel(qkv_ref, c0_ref, c1_ref, c2_ref, st_ref, u_ref, w_ref, b_ref, g_ref, beta_ref,
                   mix_ref, nconv_ref, *rest, t_new):
    cache_refs = (c0_ref, c1_ref, c2_ref)
    nk_refs = dict(zip(SAMPLE_SHIFT_GROUPS, rest[:-1]))
    hist = rest[-1]
    hp = HEADS_PER_GROUP
    outs = [[None] * N_GROUPS for _ in range(t_new)]
    lses = [[None] * N_GROUPS for _ in range(t_new)]
    for gi, (_, d) in enumerate(DILATION_GROUPS):
        cref = cache_refs[gi]
        n_rows = cref.shape[1]
        row = lax.broadcasted_iota(jnp.int32, (n_rows, hp, 1), 0)
        if gi in nk_refs:
            _shift_cache(cref, nk_refs[gi], d, t_new)
            keep = nk_refs[gi].shape[1] - t_new * KV_SLAB
            nk_refs[gi][0, keep:, :] = _new_kv_rows(qkv_ref, gi, t_new)
        for t in range(t_new):
            s0 = (t % d) * KV_SLAB
            first_row = t // d
            q = qkv_ref[0, t, gi][None]
            kc = cref[0, :, s0:s0 + hp, :]
            vc = cref[0, :, s0 + hp:s0 + 2 * hp, :]
            s_c = jnp.sum(kc * q, axis=2, keepdims=True) * ATTN_SCALE
            if first_row > 0:
                s_c = jnp.where(row >= first_row, s_c, NEG_BIG)
            newest = [t - d * jj for jj in range(t // d + 1)]
            s_n, v_n = [], []
            for tn in newest:
                kn = qkv_ref[0, tn, N_GROUPS + gi][None]
                v_n.append(qkv_ref[0, tn, 2 * N_GROUPS + gi][None])
                s_n.append(jnp.sum(kn * q, axis=2, keepdims=True) * ATTN_SCALE)
            m = jnp.max(s_c, axis=0, keepdims=True)
            for sn in s_n:
                m = jnp.maximum(m, sn)
            e_c = jnp.exp(s_c - m)
            den = jnp.sum(e_c, axis=0, keepdims=True)
            acc = jnp.sum(e_c * vc, axis=0, keepdims=True)
            for sn, v1 in zip(s_n, v_n):
                e_n = jnp.exp(sn - m)
                den = den + e_n
                acc = acc + e_n * v1
            outs[t][gi] = acc / den
            lses[t][gi] = m + jnp.log(den)
    for t in range(t_new):
        ls = lses[t]
        m = functools.reduce(jnp.maximum, ls)
        es = [jnp.exp(l - m) for l in ls]
        tot = functools.reduce(lambda a, b: a + b, es)
        for gi in range(N_GROUPS):
            slab = (outs[t][gi] * (es[gi] / tot))[0]
            for h in range(hp):
                c0 = gi * GROUP_W + h * HEAD_DIM
                mix_ref[0, t:t + 1, c0:c0 + HEAD_DIM] = slab[h:h + 1, :]
    n_state = CONV_K - 1
    hist[0:n_state, :] = st_ref[0]
    hist[n_state:n_state + t_new, :] = u_ref[0]
    acc = jnp.broadcast_to(b_ref[...], (t_new, CONV_CH))
    for k in range(CONV_K):
        acc = acc + hist[k:k + t_new, :] * w_ref[k:k + 1, :]
    y = _layer_norm_rows(acc, g_ref[...], beta_ref[...])
    mix_ref[0, :, ATTN_WIDTH:] = y * jax.nn.sigmoid(y)
    nconv_ref[0] = hist[t_new:t_new + n_state, :]


def _sample_mixers(qkvs, us, caches, state, conv_w, conv_b, ln_g, ln_b, dec_batch, t_new):
    n_qkv = qkvs.shape[1]
    hp = HEADS_PER_GROUP
    qkv5 = qkvs.reshape(dec_batch, t_new, n_qkv, hp, HEAD_DIM)
    u3 = us.reshape(dec_batch, t_new, CONV_CH)
    cache_in, cache_specs, nk_shapes, nk_specs = [], [], [], []
    for gi, ((window, d), cache) in enumerate(zip(DILATION_GROUPS, caches)):
        buf = cache.shape[1]
        assert buf == window and buf % d == 0
        n_rows = buf // d
        cache_in.append(cache.reshape(dec_batch, n_rows, d * KV_SLAB, HEAD_DIM))
        if gi in SAMPLE_SHIFT_GROUPS:
            cache_specs.append(pl.BlockSpec((1, n_rows, d * KV_SLAB, HEAD_DIM), lambda b: (b, 0, 0, 0)))
            nk_shapes.append(jax.ShapeDtypeStruct((dec_batch, buf * KV_SLAB, HEAD_DIM), F32))
            nk_specs.append(pl.BlockSpec((1, buf * KV_SLAB, HEAD_DIM), lambda b: (b, 0, 0)))
        else:
            used = min(d, t_new) * KV_SLAB
            cache_specs.append(pl.BlockSpec((1, n_rows, used, HEAD_DIM), lambda b: (b, 0, 0, 0)))
    vec = pl.BlockSpec((1, CONV_CH), lambda b: (0, 0))
    n_state = CONV_K - 1
    mix, nconv, *nks = pl.pallas_call(
        functools.partial(_sample_kernel, t_new=t_new),
        grid=(dec_batch,),
        in_specs=[pl.BlockSpec((1, t_new, n_qkv, hp, HEAD_DIM), lambda b: (b, 0, 0, 0, 0))] + cache_specs + [
            pl.BlockSpec((1, n_state, CONV_CH), lambda b: (b, 0, 0)),
            pl.BlockSpec((1, t_new, CONV_CH), lambda b: (b, 0, 0)),
            pl.BlockSpec((CONV_K, CONV_CH), lambda b: (0, 0)), vec, vec, vec],
        out_specs=[pl.BlockSpec((1, t_new, D_MODEL), lambda b: (b, 0, 0)),
                   pl.BlockSpec((1, n_state, CONV_CH), lambda b: (b, 0, 0))] + nk_specs,
        out_shape=[jax.ShapeDtypeStruct((dec_batch, t_new, D_MODEL), F32),
                   jax.ShapeDtypeStruct((dec_batch, n_state, CONV_CH), F32)] + nk_shapes,
        scratch_shapes=[pltpu.VMEM((n_state + t_new + 6, CONV_CH), F32)],
        compiler_params=_cparams(1), name="sample_mixers")(
        qkv5, *cache_in, state, u3, conv_w, conv_b[None, :], ln_g[None, :], ln_b[None, :])
    new_caches = {gi: nk.reshape(caches[gi].shape) for gi, nk in zip(SAMPLE_SHIFT_GROUPS, nks)}
    return mix.reshape(dec_batch * t_new, D_MODEL), nconv, new_caches


OUT_TM = 512
OUT_ROWS = 128


def _out_proj_kernel(o0_ref, o1_ref, o2_ref, l0_ref, l1_ref, l2_ref, c_ref, x_ref, w_ref, g_ref, b_ref,
                     mixs_ref, xs_ref, x1_ref, x1bf_ref, x1s_ref, mix_scr, y_scr, *, n_i, alpha):
    i = pl.program_id(0)
    o_refs = (o0_ref, o1_ref, o2_ref)
    chunks = [slice(r0, r0 + OUT_ROWS) for r0 in range(0, OUT_TM, OUT_ROWS)]

    def project(rows):
        ls = [l0_ref[rows, :], l1_ref[rows, :], l2_ref[rows, :]]
        m = jnp.maximum(jnp.maximum(ls[0], ls[1]), ls[2])
        es = [jnp.exp(l - m) for l in ls]
        inv = 1.0 / (es[0] + es[1] + es[2])
        for gi in range(N_GROUPS):
            a = es[gi] * inv
            for h in range(HEADS_PER_GROUP):
                c0 = h * HEAD_DIM
                og = o_refs[gi][rows, c0:c0 + HEAD_DIM].astype(F32)
                mix_scr[rows, gi * GROUP_W + c0:gi * GROUP_W + c0 + HEAD_DIM] = (og * a[:, h:h + 1]).astype(BF16)
        mix_scr[rows, ATTN_WIDTH:] = c_ref[rows, :]
        y_scr[rows, :] = (jnp.dot(mix_scr[rows, :], w_ref[...], preferred_element_type=F32)
                          + alpha * x_ref[rows, :])

    def normalize(rows):
        for r0 in range(rows.start, rows.stop, LN_STRIP):
            strip = slice(r0, r0 + LN_STRIP)
            x1 = _layer_norm_rows(y_scr[strip, :], g_ref[...], b_ref[...])
            x1_ref[strip, :] = x1
            x1bf_ref[strip, :] = x1.astype(BF16)

    project(chunks[0])
    for c in range(1, len(chunks)):
        project(chunks[c])
        normalize(chunks[c - 1])
    normalize(chunks[-1])

    @pl.when(i == n_i - 1)
    def _():
        ys = jnp.dot(mixs_ref[...].astype(BF16), w_ref[...], preferred_element_type=F32) + alpha * xs_ref[...]
        x1s_ref[...] = _layer_norm_rows(ys, g_ref[...], b_ref[...])


def _out_proj(os, lses, c, x2, w_bf, ln_g, ln_b, mixs, xs2, alpha):
    m = x2.shape[0]
    ms = xs2.shape[0]
    n_i = m // OUT_TM
    row = lambda w: pl.BlockSpec((OUT_TM, w), lambda i: (i, 0))
    whole = lambda a: pl.BlockSpec(a.shape, lambda i: (0,) * a.ndim)
    g2, b2 = ln_g[None, :], ln_b[None, :]
    return pl.pallas_call(
        functools.partial(_out_proj_kernel, n_i=n_i, alpha=alpha),
        grid=(n_i,),
        in_specs=[row(GROUP_W)] * 3 + [row(HEAD_DIM)] * 3 + [row(CONV_CH), row(D_MODEL),
                  whole(w_bf), whole(g2), whole(b2), whole(mixs), whole(xs2)],
        out_specs=[row(D_MODEL), row(D_MODEL), pl.BlockSpec((ms, D_MODEL), lambda i: (0, 0))],
        out_shape=[jax.ShapeDtypeStruct((m, D_MODEL), F32), jax.ShapeDtypeStruct((m, D_MODEL), BF16),
                   jax.ShapeDtypeStruct((ms, D_MODEL), F32)],
        scratch_shapes=[pltpu.VMEM((OUT_TM, D_MODEL), BF16), pltpu.VMEM((OUT_TM, D_MODEL), F32)],
        compiler_params=_cparams(1), name="out_proj")(
        *os, *lses, c, x2, w_bf, g2, b2, mixs, xs2)


FFN_TM = 1024
FFN_TF = 512
FFN_TAIL_ROWS = 256
FFN_RES_W = 256


def _ffn_kernel(xbf_ref, xres_ref, wg_ref, wu_ref, wd_ref, g_ref, b_ref, xs_ref, ca_ref, cb_ref, qkv_ref,
                y_ref, ys_ref, nk_ref, xsbf, *, n_i, n_f, alpha, t_new, n_chunks, chunks_per_entry):
    i = pl.program_id(0)
    f = pl.program_id(1)

    def shift_cache_chunk():
        shift = t_new * KV_SLAB
        rows = ca_ref.shape[1]
        chunk = jnp.minimum(i * n_f + f, n_chunks - 1) % chunks_per_entry
        nk_ref[0, 0:rows - shift, :] = ca_ref[0, shift:rows, :]
        nk_ref[0, rows - shift:rows, :] = jnp.where(chunk == chunks_per_entry - 1,
                                                    _new_kv_rows(qkv_ref, FFN_SHIFT_GROUP, t_new), cb_ref[0])

    def swiglu_down(xb):
        gate = jnp.dot(xb, wg_ref[...], preferred_element_type=F32)
        up = jnp.dot(xb, wu_ref[...], preferred_element_type=F32)
        act = (gate * jax.nn.sigmoid(gate) * up).astype(BF16)
        return jnp.dot(act, wd_ref[...], preferred_element_type=F32)

    is_last_tile = i == n_i - 1

    @pl.when(f == 0)
    def _():
        y_ref[...] = jnp.zeros_like(y_ref)

        @pl.when(is_last_tile)
        def _():
            xsbf[...] = xs_ref[...].astype(BF16)
            ys_ref[...] = alpha * xs_ref[...]

    @pl.when(i < n_i - 1)
    def _():
        shift_cache_chunk()
        y_ref[...] += swiglu_down(xbf_ref[...])

    @pl.when(is_last_tile)
    def _():
        shift_cache_chunk()
        head, tail = slice(0, FFN_TM - FFN_TAIL_ROWS), slice(FFN_TM - FFN_TAIL_ROWS, FFN_TM)
        y_ref[head, :] += swiglu_down(xbf_ref[head, :])
        down = swiglu_down(jnp.concatenate([xbf_ref[tail, :], xsbf[...]], axis=0))
        y_ref[tail, :] += down[:FFN_TAIL_ROWS]
        ys_ref[...] += down[FFN_TAIL_ROWS:]

    for c in range(D_MODEL // FFN_RES_W):
        @pl.when(f == c)
        def _(c=c):
            y_ref[:, c * FFN_RES_W:(c + 1) * FFN_RES_W] += alpha * xres_ref[...]

    @pl.when(f == n_f - 1)
    def _():
        y_ref[...] = _layer_norm_rows(y_ref[...], g_ref[...], b_ref[...])

        @pl.when(is_last_tile)
        def _():
            ys_ref[...] = _layer_norm_rows(ys_ref[...], g_ref[...], b_ref[...])


def _ffn(x1, x1bf, x1s, wg_bf, wu_bf, wd_bf, ln_g, ln_b, alpha, cache, qkvs, t_new):
    m = x1.shape[0]
    ms = x1s.shape[0]
    hidden = wg_bf.shape[1]
    n_i = m // FFN_TM
    n_f = hidden // FFN_TF
    n_res = D_MODEL // FFN_RES_W
    assert n_f >= n_res
    g2, b2 = ln_g[None, :], ln_b[None, :]
    dec_batch = cache.shape[0]
    cache_flat = cache.reshape(dec_batch, -1, HEAD_DIM)
    entry_rows = cache_flat.shape[1]
    shift = t_new * KV_SLAB
    cpe = entry_rows // FFN_SHIFT_ROWS
    n_chunks = dec_batch * cpe
    assert cpe * FFN_SHIFT_ROWS == entry_rows and n_chunks <= n_i * n_f and FFN_SHIFT_ROWS % shift == 0
    qkv5 = qkvs.reshape(dec_batch, t_new, qkvs.shape[1], HEADS_PER_GROUP, HEAD_DIM)

    def chunk_of(i, f):
        c = jnp.minimum(i * n_f + f, n_chunks - 1)
        return c // cpe, c % cpe

    def chunk_map(i, f):
        b, k = chunk_of(i, f)
        return b, k, 0

    def follow_map(i, f):
        b, k = chunk_of(i, f)
        return b, jnp.minimum((k + 1) * (FFN_SHIFT_ROWS // shift), entry_rows // shift - 1), 0

    y, ys, nk = pl.pallas_call(
        functools.partial(_ffn_kernel, n_i=n_i, n_f=n_f, alpha=alpha, t_new=t_new, n_chunks=n_chunks,
                          chunks_per_entry=cpe),
        grid=(n_i, n_f),
        in_specs=[pl.BlockSpec((FFN_TM, D_MODEL), lambda i, f: (i, 0)),
                  pl.BlockSpec((FFN_TM, FFN_RES_W), lambda i, f: (i, jnp.minimum(f, n_res - 1))),
                  pl.BlockSpec((D_MODEL, FFN_TF), lambda i, f: (0, f)),
                  pl.BlockSpec((D_MODEL, FFN_TF), lambda i, f: (0, f)),
                  pl.BlockSpec((FFN_TF, D_MODEL), lambda i, f: (f, 0)),
                  pl.BlockSpec((1, D_MODEL), lambda i, f: (0, 0)),
                  pl.BlockSpec((1, D_MODEL), lambda i, f: (0, 0)),
                  pl.BlockSpec((ms, D_MODEL), lambda i, f: (0, 0)),
                  pl.BlockSpec((1, FFN_SHIFT_ROWS, HEAD_DIM), chunk_map),
                  pl.BlockSpec((1, shift, HEAD_DIM), follow_map),
                  pl.BlockSpec((1,) + qkv5.shape[1:], lambda i, f: (chunk_of(i, f)[0], 0, 0, 0, 0))],
        out_specs=[pl.BlockSpec((FFN_TM, D_MODEL), lambda i, f: (i, 0)),
                   pl.BlockSpec((ms, D_MODEL), lambda i, f: (0, 0)),
                   pl.BlockSpec((1, FFN_SHIFT_ROWS, HEAD_DIM), chunk_map)],
        out_shape=[jax.ShapeDtypeStruct((m, D_MODEL), F32), jax.ShapeDtypeStruct((ms, D_MODEL), F32),
                   jax.ShapeDtypeStruct(cache_flat.shape, F32)],
        scratch_shapes=[pltpu.VMEM((ms, D_MODEL), BF16)],
        compiler_params=_cparams(2), name="ffn")(
        x1bf, x1, wg_bf, wu_bf, wd_bf, g2, b2, x1s, cache_flat, cache_flat, qkv5)
    return y, ys, nk.reshape(cache.shape)


def kernel(x_prompt, x_sample, cache_kv_w128, cache_kv_w512, cache_kv_w2048, state_conv, w_in, w_out,
           conv_w, conv_b, conv_ln_g, conv_ln_b, ln1_g, ln1_b, w_gate, w_up, w_down, ln2_g, ln2_b):
    depth = w_in.shape[0]
    batch, seq, _ = x_prompt.shape
    dec_batch, t_new, _ = x_sample.shape
    caches = (cache_kv_w128, cache_kv_w512, cache_kv_w2048)
    alpha = (2.0 * depth) ** 0.25

    cos_p, sin_p = _rope_tables(np.arange(seq))
    cos_s, sin_s = _rope_tables(np.tile(PAST_LEN + np.arange(t_new), dec_batch))

    xp = x_prompt.reshape(batch * seq, D_MODEL)
    xs = x_sample.reshape(dec_batch * t_new, D_MODEL)
    kvp = [[] for _ in range(N_GROUPS)]
    kvs = [[] for _ in range(N_GROUPS)]
    convp, convs = [], []
    for l in range(depth):
        q0, q1, q2, kc0, kc1, kc2, u, kv0, kv1, kv2, qkvs, us = _in_proj(
            xp, xs, w_in[l].astype(BF16), cos_p, sin_p, cos_s, sin_s, batch, seq)
        os, lses = zip(*[_attn_group(q_cm, kv_cm, gi, batch, seq)
                         for gi, (q_cm, kv_cm) in enumerate(zip((q0, q1, q2), (kc0, kc1, kc2)))])
        conv_steps = batch * seq // CONV_TC
        c, (wo_bf, wg_bf, wu_bf, wd_bf) = _conv_prompt(
            u, conv_w[l], conv_b[l], conv_ln_g[l], conv_ln_b[l], batch, seq,
            [(w, 0, w.shape[0] // conv_steps) for w in (w_out[l], w_gate[l], w_up[l], w_down[l])])
        layer_caches = [cc[l] for cc in caches]
        mixs, nconv_s, new_caches = _sample_mixers(qkvs, us, layer_caches, state_conv[l], conv_w[l], conv_b[l],
                                                   conv_ln_g[l], conv_ln_b[l], dec_batch, t_new)
        x1, x1bf, x1s = _out_proj(os, lses, c, xp, wo_bf, ln1_g[l], ln1_b[l], mixs, xs, alpha)
        xp, xs, new_caches[FFN_SHIFT_GROUP] = _ffn(x1, x1bf, x1s, wg_bf, wu_bf, wd_bf, ln2_g[l], ln2_b[l], alpha,
                                                   layer_caches[FFN_SHIFT_GROUP], qkvs, t_new)
        for gi, kv in enumerate((kv0, kv1, kv2)):
            kvp[gi].append(kv.reshape(batch, kv.shape[1] // KV_SLAB, 2, HEADS_PER_GROUP, HEAD_DIM))
            kvs[gi].append(new_caches[gi])
        convp.append(u.reshape(batch, seq, CONV_CH)[:, seq - (CONV_K - 1):])
        convs.append(nconv_s)

    y_prompt = xp.reshape(batch, seq, D_MODEL)
    y_sample = xs.reshape(dec_batch, t_new, D_MODEL)
    return (y_prompt, y_sample, jnp.stack(kvp[0]), jnp.stack(kvp[1]), jnp.stack(kvp[2]), jnp.stack(convp),
            jnp.stack(kvs[0]), jnp.stack(kvs[1]), jnp.stack(kvs[2]), jnp.stack(convs))
```

```python
import functools

import numpy as np
import jax
import jax.numpy as jnp
from jax import lax
from jax.experimental import pallas as pl
from jax.experimental.pallas import tpu as pltpu

D_MODEL = 2048
HEAD_DIM = 128
CONV_CH = D_MODEL // 4
ATTN_WIDTH = D_MODEL - CONV_CH
DILATION_GROUPS = ((128, 1), (512, 4), (2048, 16))
N_GROUPS = len(DILATION_GROUPS)
DILATED_GROUPS = tuple(gi for gi, (_, d) in enumerate(DILATION_GROUPS) if d > 1)
HEADS_PER_GROUP = ATTN_WIDTH // HEAD_DIM // N_GROUPS
GROUP_W = HEADS_PER_GROUP * HEAD_DIM
CONV_K = 31
ROPE_THETA = 10000.0
LN_EPS = 1e-5
Q_BLOCK = 128
ATTN_SCALE = HEAD_DIM ** -0.5
NEG_BIG = -1e30
PAST_LEN = 16384

F32 = jnp.float32
BF16 = jnp.bfloat16

VMEM_LIMIT = 58 * 1024 * 1024
LN_STRIP = 16
SUBLANES = 8


def _cparams(n_axes):
    return pltpu.CompilerParams(dimension_semantics=("arbitrary",) * n_axes,
                                vmem_limit_bytes=VMEM_LIMIT)


def _layer_norm_rows(y, g, b):
    mu = jnp.mean(y, axis=-1, keepdims=True)
    yc = y - mu
    var = jnp.mean(yc * yc, axis=-1, keepdims=True)
    return yc * lax.rsqrt(var + LN_EPS) * g + b


def _rope_tables(pos):
    half = HEAD_DIM // 2
    inv = ROPE_THETA ** (-np.arange(half, dtype=np.float64) / half)
    ang = np.asarray(pos, np.float64)[:, None] * inv[None, :]
    cos, sin = np.cos(ang), np.sin(ang)
    return (jnp.asarray(np.concatenate([cos, cos], axis=1), F32),
            jnp.asarray(np.concatenate([-sin, sin], axis=1), F32))


def _rope(h, cos, sin):
    parts = []
    for hh in range(HEADS_PER_GROUP):
        hs = h[:, hh * HEAD_DIM:(hh + 1) * HEAD_DIM]
        parts.append(hs * cos + pltpu.roll(hs, HEAD_DIM // 2, axis=1) * sin)
    return jnp.concatenate(parts, axis=1)


IN_TM = 512
IN_TN = GROUP_W
IN_ROWS = 256
PH_Q_GLU, PH_KV = range(2)
N_PHASES = 2


def _kv_window_plan(seq, batch):
    tiles_per_batch = seq // IN_TM
    n_i = batch * tiles_per_batch
    plans = []
    for window, _ in DILATION_GROUPS:
        keep = min(window, seq)
        rb = min(keep, IN_TM)
        first_tile = (seq - keep) // IN_TM
        row_lo = (seq - keep) - first_tile * IN_TM
        writes = []
        for i in range(n_i):
            b, it = divmod(i, tiles_per_batch)
            if it >= first_tile:
                writes.append((i * N_PHASES + PH_KV, (b, it - first_tile)))
        writes.sort()
        tab = np.zeros((n_i * N_PHASES, 2), np.int32)
        w = 0
        for step in range(n_i * N_PHASES):
            while w < len(writes) - 1 and writes[w][0] < step:
                w += 1
            tab[step] = writes[w][1]
        plans.append(dict(keep=keep, rb=rb, row_lo=row_lo, tab=tab))
    return plans


def _in_proj_kernel(tab_ref, x_ref, w_ref, cos_ref, sin_ref, xs_ref, coss_ref, sins_ref,
                    q0_ref, q1_ref, q2_ref, kc0_ref, kc1_ref, kc2_ref, u_ref, kv0_ref, kv1_ref, kv2_ref,
                    qkvs_ref, us_ref, xbf, de_scr, xsbf, *, n_i, plans):
    del tab_ref
    i = pl.program_id(0)
    phase = pl.program_id(1)
    kv_refs = (kv0_ref, kv1_ref, kv2_ref)
    q_refs = (q0_ref, q1_ref, q2_ref)
    kc_refs = (kc0_ref, kc1_ref, kc2_ref)
    glu_col = 3 * N_GROUPS

    chunks = [slice(c * IN_ROWS, (c + 1) * IN_ROWS) for c in range(IN_TM // IN_ROWS)]

    def col_dot(lhs, col):
        return jnp.dot(lhs, w_ref[:, col * IN_TN:(col + 1) * IN_TN], preferred_element_type=F32)

    def rope_rows(val, rows):
        return _rope(val, cos_ref[rows, :], sin_ref[rows, :])

    def store_class_major(ref, lead, gi, c, val):
        d = DILATION_GROUPS[gi][1]
        per = IN_ROWS // d
        if d == 1:
            ref[lead + (0, chunks[c], slice(None))] = val.astype(BF16)
            return
        slot = DILATED_GROUPS.index(gi)
        for h in range(HEADS_PER_GROUP):
            de_scr[slot, c, h] = val[:, h * HEAD_DIM:(h + 1) * HEAD_DIM]
        for r in range(d):
            for h in range(HEADS_PER_GROUP):
                ref[lead + (r, slice(c * per, (c + 1) * per), slice(h * HEAD_DIM, (h + 1) * HEAD_DIM))] = (
                    de_scr[slot, c, h, pl.ds(r, per, stride=d), :].astype(BF16))

    def store_window_rows(gi, rows, val, first_slab_row):
        row_lo = plans[gi]["row_lo"]
        lo = max(rows.start, row_lo)
        hi = min(rows.stop, row_lo + plans[gi]["rb"])
        if lo < hi:
            for h in range(HEADS_PER_GROUP):
                dst = pl.ds((lo - row_lo) * KV_SLAB + first_slab_row + h, hi - lo, stride=KV_SLAB)
                kv_refs[gi][0, dst, :] = val[lo - rows.start:hi - rows.start, h * HEAD_DIM:(h + 1) * HEAD_DIM]

    def store_heads(slot, val):
        for h in range(HEADS_PER_GROUP):
            qkvs_ref[:, slot, h, :] = val[:, h * HEAD_DIM:(h + 1) * HEAD_DIM]

    is_last = i == n_i - 1

    @pl.when(phase == PH_Q_GLU)
    def _():
        for c, rows in enumerate(chunks):
            xbf[rows, :] = x_ref[rows, :].astype(BF16)
            for gi in range(N_GROUPS):
                store_class_major(q_refs[gi], (0,), gi, c, rope_rows(col_dot(xbf[rows, :], gi), rows))
            u_ref[rows, :] = col_dot(xbf[rows, :], glu_col) * jax.nn.sigmoid(col_dot(xbf[rows, :], glu_col + 1))

        @pl.when(is_last)
        def _():
            xsbf[...] = xs_ref[...].astype(BF16)
            for gi in range(N_GROUPS):
                store_heads(gi, _rope(col_dot(xsbf[...], gi), coss_ref[...], sins_ref[...]))
            us_ref[...] = col_dot(xsbf[...], glu_col) * jax.nn.sigmoid(col_dot(xsbf[...], glu_col + 1))

    @pl.when(phase == PH_KV)
    def _():
        for c, rows in enumerate(chunks):
            for gi in range(N_GROUPS):
                r = rope_rows(col_dot(xbf[rows, :], N_GROUPS + gi), rows)
                store_class_major(kc_refs[gi], (0, 0), gi, c, r)
                store_window_rows(gi, rows, r, 0)
                acc = col_dot(xbf[rows, :], 2 * N_GROUPS + gi)
                store_class_major(kc_refs[gi], (0, 1), gi, c, acc)
                store_window_rows(gi, rows, acc, HEADS_PER_GROUP)

        @pl.when(is_last)
        def _():
            for gi in range(N_GROUPS):
                store_heads(N_GROUPS + gi, _rope(col_dot(xsbf[...], N_GROUPS + gi), coss_ref[...], sins_ref[...]))
                store_heads(2 * N_GROUPS + gi, col_dot(xsbf[...], 2 * N_GROUPS + gi))


def _in_proj(x2, xs2, w_bf, cos_p, sin_p, cos_s, sin_s, batch, seq):
    m = x2.shape[0]
    ms = xs2.shape[0]
    n_i = m // IN_TM
    n_qkv = 3 * N_GROUPS
    assert w_bf.shape[1] == (n_qkv + 2) * IN_TN
    tiles_per_batch = seq // IN_TM
    plans = _kv_window_plan(seq, batch)
    tab = jnp.asarray(np.stack([p["tab"] for p in plans]).reshape(-1))
    n_steps = n_i * N_PHASES

    def kv_map(gi):
        def f(i, p, tab_ref):
            base = (gi * n_steps + i * N_PHASES + p) * 2
            return tab_ref[base], tab_ref[base + 1], 0
        return f

    in_specs = [
        pl.BlockSpec((IN_TM, D_MODEL), lambda i, j, t: (i, 0)),
        pl.BlockSpec(w_bf.shape, lambda i, j, t: (0, 0)),
        pl.BlockSpec((IN_TM, HEAD_DIM), lambda i, j, t: (i % tiles_per_batch, 0)),
        pl.BlockSpec((IN_TM, HEAD_DIM), lambda i, j, t: (i % tiles_per_batch, 0)),
        pl.BlockSpec((ms, D_MODEL), lambda i, j, t: (0, 0)),
        pl.BlockSpec((ms, HEAD_DIM), lambda i, j, t: (0, 0)),
        pl.BlockSpec((ms, HEAD_DIM), lambda i, j, t: (0, 0)),
    ]
    out_shape, out_specs = [], []
    for _, d in DILATION_GROUPS:
        out_shape.append(jax.ShapeDtypeStruct((batch, d, seq // d, GROUP_W), BF16))
        out_specs.append(pl.BlockSpec((1, d, IN_TM // d, GROUP_W),
                                      lambda i, p, t: (i // tiles_per_batch, 0, i % tiles_per_batch, 0)))
    for _, d in DILATION_GROUPS:
        out_shape.append(jax.ShapeDtypeStruct((batch, 2, d, seq // d, GROUP_W), BF16))
        out_specs.append(pl.BlockSpec((1, 2, d, IN_TM // d, GROUP_W),
                                      lambda i, p, t: (i // tiles_per_batch, 0, 0, i % tiles_per_batch, 0)))
    out_shape.append(jax.ShapeDtypeStruct((m, CONV_CH), F32))
    out_specs.append(pl.BlockSpec((IN_TM, CONV_CH), lambda i, j, t: (i, 0)))
    for gi, p in enumerate(plans):
        out_shape.append(jax.ShapeDtypeStruct((batch, p["keep"] * KV_SLAB, HEAD_DIM), F32))
        out_specs.append(pl.BlockSpec((1, p["rb"] * KV_SLAB, HEAD_DIM), kv_map(gi)))
    out_shape += [jax.ShapeDtypeStruct((ms, n_qkv, HEADS_PER_GROUP, HEAD_DIM), F32),
                  jax.ShapeDtypeStruct((ms, CONV_CH), F32)]
    out_specs += [
        pl.BlockSpec((ms, n_qkv, HEADS_PER_GROUP, HEAD_DIM), lambda i, p, t: (0, 0, 0, 0)),
        pl.BlockSpec((ms, CONV_CH), lambda i, j, t: (0, 0)),
    ]
    grid_spec = pltpu.PrefetchScalarGridSpec(
        num_scalar_prefetch=1, grid=(n_i, N_PHASES), in_specs=in_specs, out_specs=out_specs,
        scratch_shapes=[pltpu.VMEM((IN_TM, D_MODEL), BF16),
                        pltpu.VMEM((len(DILATED_GROUPS), IN_TM // IN_ROWS, HEADS_PER_GROUP, IN_ROWS, HEAD_DIM), F32),
                        pltpu.VMEM((ms, D_MODEL), BF16)])
    kern = functools.partial(_in_proj_kernel, n_i=n_i, plans=plans)
    return pl.pallas_call(kern, grid_spec=grid_spec, out_shape=out_shape,
                          compiler_params=_cparams(2), name="in_proj")(
        tab, x2, w_bf, cos_p, sin_p, xs2, cos_s, sin_s)


ATTN_TP = 2048


def _attn_kernel(q_ref, k_ref, v_ref, o_ref, lse_ref, k_scr, v_scr, o_scr, lse_scr, *, d, lc):
    n = pl.program_id(1)
    n_qb = lc // Q_BLOCK

    @pl.when(n == 0)
    def _():
        k_scr[:, 0:Q_BLOCK, :] = jnp.zeros((d, Q_BLOCK, GROUP_W), BF16)
        v_scr[:, 0:Q_BLOCK, :] = jnp.zeros((d, Q_BLOCK, GROUP_W), BF16)

    k_scr[:, Q_BLOCK:, :] = k_ref[0, 0]
    v_scr[:, Q_BLOCK:, :] = v_ref[0, 0]
    qq = lax.broadcasted_iota(jnp.int32, (Q_BLOCK, 2 * Q_BLOCK), 0)
    kk = lax.broadcasted_iota(jnp.int32, (Q_BLOCK, 2 * Q_BLOCK), 1)
    band = (kk >= qq) & (kk <= qq + Q_BLOCK)
    lane = lax.broadcasted_iota(jnp.int32, (Q_BLOCK, HEAD_DIM), 1)

    def unit(u, carry):
        r = u // n_qb
        qb = u % n_qb
        r0 = pl.multiple_of(qb * Q_BLOCK, Q_BLOCK)
        valid = band & ((kk >= Q_BLOCK) | (n > 0) | (qb > 0))
        if d == 1:
            rows = pl.ds(r0, Q_BLOCK)
        else:
            rows = pl.ds(r0 * d + r, Q_BLOCK, stride=d)
        lse_blk = jnp.zeros((Q_BLOCK, HEAD_DIM), F32)
        for h in range(HEADS_PER_GROUP):
            c0 = h * HEAD_DIM
            q = q_ref[0, r, pl.ds(r0, Q_BLOCK), c0:c0 + HEAD_DIM]
            k = k_scr[r, pl.ds(r0, 2 * Q_BLOCK), c0:c0 + HEAD_DIM]
            v = v_scr[r, pl.ds(r0, 2 * Q_BLOCK), c0:c0 + HEAD_DIM]
            s = lax.dot_general(q, k, (((1,), (1,)), ((), ())), preferred_element_type=F32)
            s = jnp.where(valid, s * ATTN_SCALE, NEG_BIG)
            m = jnp.max(s, axis=1, keepdims=True)
            e = jnp.exp(s - m)
            den = jnp.sum(e, axis=1, keepdims=True)
            o_scr[h, rows, :] = jnp.dot(e.astype(BF16), v, preferred_element_type=F32) / den
            lse_blk = jnp.where(lane == h, m + jnp.log(den), lse_blk)
        lse_scr[rows, :] = lse_blk
        return carry

    lax.fori_loop(0, d * n_qb, unit, 0, unroll=d * n_qb)
    k_scr[:, 0:Q_BLOCK, :] = k_scr[:, lc:lc + Q_BLOCK, :]
    v_scr[:, 0:Q_BLOCK, :] = v_scr[:, lc:lc + Q_BLOCK, :]
    for h in range(HEADS_PER_GROUP):
        o_ref[:, h * HEAD_DIM:(h + 1) * HEAD_DIM] = o_scr[h].astype(BF16)
    lse_ref[...] = lse_scr[...]


def _attn_group(q_cm, kv_cm, gi, batch, seq):
    _, d = DILATION_GROUPS[gi]
    lc = ATTN_TP // d
    n_t = seq // ATTN_TP

    def plane(p):
        return pl.BlockSpec((1, 1, d, lc, GROUP_W), lambda b, n: (b, p, 0, n, 0))

    return pl.pallas_call(
        functools.partial(_attn_kernel, d=d, lc=lc),
        grid=(batch, n_t),
        in_specs=[pl.BlockSpec((1, d, lc, GROUP_W), lambda b, n: (b, 0, n, 0)), plane(0), plane(1)],
        out_specs=[pl.BlockSpec((ATTN_TP, GROUP_W), lambda b, n: (b * n_t + n, 0)),
                   pl.BlockSpec((ATTN_TP, HEAD_DIM), lambda b, n: (b * n_t + n, 0))],
        out_shape=[jax.ShapeDtypeStruct((batch * seq, GROUP_W), BF16),
                   jax.ShapeDtypeStruct((batch * seq, HEAD_DIM), F32)],
        scratch_shapes=[pltpu.VMEM((d, lc + Q_BLOCK, GROUP_W), BF16),
                        pltpu.VMEM((d, lc + Q_BLOCK, GROUP_W), BF16),
                        pltpu.VMEM((HEADS_PER_GROUP, ATTN_TP, HEAD_DIM), F32),
                        pltpu.VMEM((ATTN_TP, HEAD_DIM), F32)],
        compiler_params=_cparams(2), name=f"attn_g{gi}")(q_cm, kv_cm, kv_cm)


CONV_TC = 512
CONV_HALO = 32
CONV_ROWS = 64


def _conv_kernel(u_ref, halo_ref, w_ref, b_ref, g_ref, beta_ref, *rest, n_t, stream_chunks):
    n_s = len(stream_chunks)
    src_refs, c_ref, dst_refs, hist = rest[:n_s], rest[n_s], rest[n_s + 1:2 * n_s + 1], rest[2 * n_s + 1]
    t = pl.program_id(1)
    step = pl.program_id(0) * n_t + t

    for src, dst, n_chunks in zip(src_refs, dst_refs, stream_chunks):
        @pl.when(step < n_chunks)
        def _(src=src, dst=dst):
            dst[...] = src[...].astype(BF16)

    hist[0:CONV_HALO, :] = jnp.where(t > 0, halo_ref[0], 0.0)
    hist[CONV_HALO:CONV_HALO + CONV_TC, :] = u_ref[0]
    hist[CONV_HALO + CONV_TC:, :] = jnp.zeros((SUBLANES, CONV_CH), F32)
    lead = CONV_HALO - (CONV_K - 1)
    for rc in range(CONV_TC // CONV_ROWS):
        r0 = rc * CONV_ROWS
        acc = jnp.broadcast_to(b_ref[...], (CONV_ROWS, CONV_CH))
        for s in range(SUBLANES):
            grp = None
            for c in range(s, lead + CONV_K, SUBLANES):
                if c < lead:
                    continue
                term = hist[r0 + c - s:r0 + c - s + CONV_ROWS + SUBLANES, :] * w_ref[c - lead:c - lead + 1, :]
                grp = term if grp is None else grp + term
            acc = acc + grp[s:s + CONV_ROWS, :]
        y = _layer_norm_rows(acc, g_ref[...], beta_ref[...])
        c_ref[0, r0:r0 + CONV_ROWS, :] = (y * jax.nn.sigmoid(y)).astype(BF16)


def _conv_prompt(u2, conv_w, conv_b, ln_g, ln_b, batch, seq, cast_streams):
    u3 = u2.reshape(batch, seq, CONV_CH)
    per = CONV_TC // CONV_HALO
    n_t = seq // CONV_TC
    vec = pl.BlockSpec((1, CONV_CH), lambda b, t: (0, 0))
    stream_specs, stream_chunks = [], []
    for w, axis, chunk in cast_streams:
        n_chunks = w.shape[axis] // chunk
        assert w.ndim == 2 and n_chunks * chunk == w.shape[axis] and n_chunks <= batch * n_t
        block = (chunk, w.shape[1]) if axis == 0 else (w.shape[0], chunk)

        def index_map(b, t, axis=axis, n_chunks=n_chunks):
            k = jnp.minimum(b * n_t + t, n_chunks - 1)
            return (k, 0) if axis == 0 else (0, k)

        stream_specs.append(pl.BlockSpec(block, index_map))
        stream_chunks.append(n_chunks)
    outs = pl.pallas_call(
        functools.partial(_conv_kernel, n_t=n_t, stream_chunks=tuple(stream_chunks)),
        grid=(batch, n_t),
        in_specs=[pl.BlockSpec((1, CONV_TC, CONV_CH), lambda b, t: (b, t, 0)),
                  pl.BlockSpec((1, CONV_HALO, CONV_CH), lambda b, t: (b, jnp.maximum(t * per - 1, 0), 0)),
                  pl.BlockSpec((CONV_K, CONV_CH), lambda b, t: (0, 0)), vec, vec, vec] + stream_specs,
        out_specs=[pl.BlockSpec((1, CONV_TC, CONV_CH), lambda b, t: (b, t, 0))] + stream_specs,
        out_shape=[jax.ShapeDtypeStruct((batch, seq, CONV_CH), BF16)]
                  + [jax.ShapeDtypeStruct(w.shape, BF16) for w, _, _ in cast_streams],
        scratch_shapes=[pltpu.VMEM((CONV_HALO + CONV_TC + SUBLANES, CONV_CH), F32)],
        compiler_params=_cparams(2), name="conv_prompt")(
        u3, u3, conv_w, conv_b[None, :], ln_g[None, :], ln_b[None, :], *[w for w, _, _ in cast_streams])
    return outs[0].reshape(batch * seq, CONV_CH), outs[1:]


KV_SLAB = 2 * HEADS_PER_GROUP
SAMPLE_SHIFT_GROUPS = (0, 1)
FFN_SHIFT_GROUP = 2
FFN_SHIFT_ROWS = 2048


def _shift_cache(cref, out_ref, d, t_new):
    slab = d * KV_SLAB
    shift = t_new * KV_SLAB
    n_rows = cref.shape[1]
    if shift % slab == 0:
        s = shift // slab
        out_ref[0, 0:(n_rows - s) * slab] = cref[0, s:n_rows].reshape((n_rows - s) * slab, HEAD_DIM)
    else:
        assert shift < slab and shift % 8 == 0
        out_ref[0, 0:slab - shift] = cref[0, 0, shift:slab]

        def body(ci, carry):
            out_ref[0, pl.ds(pl.multiple_of(ci * slab - shift, 8), slab)] = cref[0, ci]
            return carry

        lax.fori_loop(1, n_rows, body, 0)


def _new_kv_rows(qkv_ref, gi, t_new):
    slabs = []
    for t in range(t_new):
        slabs += [qkv_ref[0, t, N_GROUPS + gi], qkv_ref[0, t, 2 * N_GROUPS + gi]]
    return jnp.concatenate(slabs, axis=0)


def _sample_kernel(qkv_ref, c0_ref, c1_ref, c2_ref, st_ref, u_ref, w_ref, b_ref, g_ref, beta_ref,
                   mix_ref, nconv_ref, *rest, t_new):
    cache_refs = (c0_ref, c1_ref, c2_ref)
    nk_refs = dict(zip(SAMPLE_SHIFT_GROUPS, rest[:-1]))
    hist = rest[-1]
    hp = HEADS_PER_GROUP
    outs = [[None] * N_GROUPS for _ in range(t_new)]
    lses = [[None] * N_GROUPS for _ in range(t_new)]
    for gi, (_, d) in enumerate(DILATION_GROUPS):
        cref = cache_refs[gi]
        n_rows = cref.shape[1]
        row = lax.broadcasted_iota(jnp.int32, (n_rows, hp, 1), 0)
        if gi in nk_refs:
            _shift_cache(cref, nk_refs[gi], d, t_new)
            keep = nk_refs[gi].shape[1] - t_new * KV_SLAB
            nk_refs[gi][0, keep:, :] = _new_kv_rows(qkv_ref, gi, t_new)
        for t in range(t_new):
            s0 = (t % d) * KV_SLAB
            first_row = t // d
            q = qkv_ref[0, t, gi][None]
            kc = cref[0, :, s0:s0 + hp, :]
            vc = cref[0, :, s0 + hp:s0 + 2 * hp, :]
            s_c = jnp.sum(kc * q, axis=2, keepdims=True) * ATTN_SCALE
            if first_row > 0:
                s_c = jnp.where(row >= first_row, s_c, NEG_BIG)
            newest = [t - d * jj for jj in range(t // d + 1)]
            s_n, v_n = [], []
            for tn in newest:
                kn = qkv_ref[0, tn, N_GROUPS + gi][None]
                v_n.append(qkv_ref[0, tn, 2 * N_GROUPS + gi][None])
                s_n.append(jnp.sum(kn * q, axis=2, keepdims=True) * ATTN_SCALE)
            m = jnp.max(s_c, axis=0, keepdims=True)
            for sn in s_n:
                m = jnp.maximum(m, sn)
            e_c = jnp.exp(s_c - m)
            den = jnp.sum(e_c, axis=0, keepdims=True)
            acc = jnp.sum(e_c * vc, axis=0, keepdims=True)
            for sn, v1 in zip(s_n, v_n):
                e_n = jnp.exp(sn - m)
                den = den + e_n
                acc = acc + e_n * v1
            outs[t][gi] = acc / den
            lses[t][gi] = m + jnp.log(den)
    for t in range(t_new):
        ls = lses[t]
        m = functools.reduce(jnp.maximum, ls)
        es = [jnp.exp(l - m) for l in ls]
        tot = functools.reduce(lambda a, b: a + b, es)
        for gi in range(N_GROUPS):
            slab = (outs[t][gi] * (es[gi] / tot))[0]
            for h in range(hp):
                c0 = gi * GROUP_W + h * HEAD_DIM
                mix_ref[0, t:t + 1, c0:c0 + HEAD_DIM] = slab[h:h + 1, :]
    n_state = CONV_K - 1
    hist[0:n_state, :] = st_ref[0]
    hist[n_state:n_state + t_new, :] = u_ref[0]
    acc = jnp.broadcast_to(b_ref[...], (t_new, CONV_CH))
    for k in range(CONV_K):
        acc = acc + hist[k:k + t_new, :] * w_ref[k:k + 1, :]
    y = _layer_norm_rows(acc, g_ref[...], beta_ref[...])
    mix_ref[0, :, ATTN_WIDTH:] = y * jax.nn.sigmoid(y)
    nconv_ref[0] = hist[t_new:t_new + n_state, :]


def _sample_mixers(qkvs, us, caches, state, conv_w, conv_b, ln_g, ln_b, dec_batch, t_new):
    n_qkv = qkvs.shape[1]
    hp = HEADS_PER_GROUP
    qkv5 = qkvs.reshape(dec_batch, t_new, n_qkv, hp, HEAD_DIM)
    u3 = us.reshape(dec_batch, t_new, CONV_CH)
    cache_in, cache_specs, nk_shapes, nk_specs = [], [], [], []
    for gi, ((window, d), cache) in enumerate(zip(DILATION_GROUPS, caches)):
        buf = cache.shape[1]
        assert buf == window and buf % d == 0
        n_rows = buf // d
        cache_in.append(cache.reshape(dec_batch, n_rows, d * KV_SLAB, HEAD_DIM))
        if gi in SAMPLE_SHIFT_GROUPS:
            cache_specs.append(pl.BlockSpec((1, n_rows, d * KV_SLAB, HEAD_DIM), lambda b: (b, 0, 0, 0)))
            nk_shapes.append(jax.ShapeDtypeStruct((dec_batch, buf * KV_SLAB, HEAD_DIM), F32))
            nk_specs.append(pl.BlockSpec((1, buf * KV_SLAB, HEAD_DIM), lambda b: (b, 0, 0)))
        else:
            used = min(d, t_new) * KV_SLAB
            cache_specs.append(pl.BlockSpec((1, n_rows, used, HEAD_DIM), lambda b: (b, 0, 0, 0)))
    vec = pl.BlockSpec((1, CONV_CH), lambda b: (0, 0))
    n_state = CONV_K - 1
    mix, nconv, *nks = pl.pallas_call(
        functools.partial(_sample_kernel, t_new=t_new),
        grid=(dec_batch,),
        in_specs=[pl.BlockSpec((1, t_new, n_qkv, hp, HEAD_DIM), lambda b: (b, 0, 0, 0, 0))] + cache_specs + [
            pl.BlockSpec((1, n_state, CONV_CH), lambda b: (b, 0, 0)),
            pl.BlockSpec((1, t_new, CONV_CH), lambda b: (b, 0, 0)),
            pl.BlockSpec((CONV_K, CONV_CH), lambda b: (0, 0)), vec, vec, vec],
        out_specs=[pl.BlockSpec((1, t_new, D_MODEL), lambda b: (b, 0, 0)),
                   pl.BlockSpec((1, n_state, CONV_CH), lambda b: (b, 0, 0))] + nk_specs,
        out_shape=[jax.ShapeDtypeStruct((dec_batch, t_new, D_MODEL), F32),
                   jax.ShapeDtypeStruct((dec_batch, n_state, CONV_CH), F32)] + nk_shapes,
        scratch_shapes=[pltpu.VMEM((n_state + t_new + 6, CONV_CH), F32)],
        compiler_params=_cparams(1), name="sample_mixers")(
        qkv5, *cache_in, state, u3, conv_w, conv_b[None, :], ln_g[None, :], ln_b[None, :])
    new_caches = {gi: nk.reshape(caches[gi].shape) for gi, nk in zip(SAMPLE_SHIFT_GROUPS, nks)}
    return mix.reshape(dec_batch * t_new, D_MODEL), nconv, new_caches


OUT_TM = 512
OUT_ROWS = 128


def _out_proj_kernel(o0_ref, o1_ref, o2_ref, l0_ref, l1_ref, l2_ref, c_ref, x_ref, w_ref, g_ref, b_ref,
                     mixs_ref, xs_ref, x1_ref, x1bf_ref, x1s_ref, mix_scr, y_scr, *, n_i, alpha):
    i = pl.program_id(0)
    o_refs = (o0_ref, o1_ref, o2_ref)
    chunks = [slice(r0, r0 + OUT_ROWS) for r0 in range(0, OUT_TM, OUT_ROWS)]

    def project(rows):
        ls = [l0_ref[rows, :], l1_ref[rows, :], l2_ref[rows, :]]
        m = jnp.maximum(jnp.maximum(ls[0], ls[1]), ls[2])
        es = [jnp.exp(l - m) for l in ls]
        inv = 1.0 / (es[0] + es[1] + es[2])
        for gi in range(N_GROUPS):
            a = es[gi] * inv
            for h in range(HEADS_PER_GROUP):
                c0 = h * HEAD_DIM
                og = o_refs[gi][rows, c0:c0 + HEAD_DIM].astype(F32)
                mix_scr[rows, gi * GROUP_W + c0:gi * GROUP_W + c0 + HEAD_DIM] = (og * a[:, h:h + 1]).astype(BF16)
        mix_scr[rows, ATTN_WIDTH:] = c_ref[rows, :]
        y_scr[rows, :] = (jnp.dot(mix_scr[rows, :], w_ref[...], preferred_element_type=F32)
                          + alpha * x_ref[rows, :])

    def normalize(rows):
        for r0 in range(rows.start, rows.stop, LN_STRIP):
            strip = slice(r0, r0 + LN_STRIP)
            x1 = _layer_norm_rows(y_scr[strip, :], g_ref[...], b_ref[...])
            x1_ref[strip, :] = x1
            x1bf_ref[strip, :] = x1.astype(BF16)

    project(chunks[0])
    for c in range(1, len(chunks)):
        project(chunks[c])
        normalize(chunks[c - 1])
    normalize(chunks[-1])

    @pl.when(i == n_i - 1)
    def _():
        ys = jnp.dot(mixs_ref[...].astype(BF16), w_ref[...], preferred_element_type=F32) + alpha * xs_ref[...]
        x1s_ref[...] = _layer_norm_rows(ys, g_ref[...], b_ref[...])


def _out_proj(os, lses, c, x2, w_bf, ln_g, ln_b, mixs, xs2, alpha):
    m = x2.shape[0]
    ms = xs2.shape[0]
    n_i = m // OUT_TM
    row = lambda w: pl.BlockSpec((OUT_TM, w), lambda i: (i, 0))
    whole = lambda a: pl.BlockSpec(a.shape, lambda i: (0,) * a.ndim)
    g2, b2 = ln_g[None, :], ln_b[None, :]
    return pl.pallas_call(
        functools.partial(_out_proj_kernel, n_i=n_i, alpha=alpha),
        grid=(n_i,),
        in_specs=[row(GROUP_W)] * 3 + [row(HEAD_DIM)] * 3 + [row(CONV_CH), row(D_MODEL),
                  whole(w_bf), whole(g2), whole(b2), whole(mixs), whole(xs2)],
        out_specs=[row(D_MODEL), row(D_MODEL), pl.BlockSpec((ms, D_MODEL), lambda i: (0, 0))],
        out_shape=[jax.ShapeDtypeStruct((m, D_MODEL), F32), jax.ShapeDtypeStruct((m, D_MODEL), BF16),
                   jax.ShapeDtypeStruct((ms, D_MODEL), F32)],
        scratch_shapes=[pltpu.VMEM((OUT_TM, D_MODEL), BF16), pltpu.VMEM((OUT_TM, D_MODEL), F32)],
        compiler_params=_cparams(1), name="out_proj")(
        *os, *lses, c, x2, w_bf, g2, b2, mixs, xs2)


FFN_TM = 1024
FFN_TF = 512
FFN_TAIL_ROWS = 256
FFN_RES_W = 256


def _ffn_kernel(xbf_ref, xres_ref, wg_ref, wu_ref, wd_ref, g_ref, b_ref, xs_ref, ca_ref, cb_ref, qkv_ref,
                y_ref, ys_ref, nk_ref, xsbf, *, n_i, n_f, alpha, t_new, n_chunks, chunks_per_entry):
    i = pl.program_id(0)
    f = pl.program_id(1)

    def shift_cache_chunk():
        shift = t_new * KV_SLAB
        rows = ca_ref.shape[1]
        chunk = jnp.minimum(i * n_f + f, n_chunks - 1) % chunks_per_entry
        nk_ref[0, 0:rows - shift, :] = ca_ref[0, shift:rows, :]
        nk_ref[0, rows - shift:rows, :] = jnp.where(chunk == chunks_per_entry - 1,
                                                    _new_kv_rows(qkv_ref, FFN_SHIFT_GROUP, t_new), cb_ref[0])

    def swiglu_down(xb):
        gate = jnp.dot(xb, wg_ref[...], preferred_element_type=F32)
        up = jnp.dot(xb, wu_ref[...], preferred_element_type=F32)
        act = (gate * jax.nn.sigmoid(gate) * up).astype(BF16)
        return jnp.dot(act, wd_ref[...], preferred_element_type=F32)

    is_last_tile = i == n_i - 1

    @pl.when(f == 0)
    def _():
        y_ref[...] = jnp.zeros_like(y_ref)

        @pl.when(is_last_tile)
        def _():
            xsbf[...] = xs_ref[...].astype(BF16)
            ys_ref[...] = alpha * xs_ref[...]

    @pl.when(i < n_i - 1)
    def _():
        shift_cache_chunk()
        y_ref[...] += swiglu_down(xbf_ref[...])

    @pl.when(is_last_tile)
    def _():
        shift_cache_chunk()
        head, tail = slice(0, FFN_TM - FFN_TAIL_ROWS), slice(FFN_TM - FFN_TAIL_ROWS, FFN_TM)
        y_ref[head, :] += swiglu_down(xbf_ref[head, :])
        down = swiglu_down(jnp.concatenate([xbf_ref[tail, :], xsbf[...]], axis=0))
        y_ref[tail, :] += down[:FFN_TAIL_ROWS]
        ys_ref[...] += down[FFN_TAIL_ROWS:]

    for c in range(D_MODEL // FFN_RES_W):
        @pl.when(f == c)
        def _(c=c):
            y_ref[:, c * FFN_RES_W:(c + 1) * FFN_RES_W] += alpha * xres_ref[...]

    @pl.when(f == n_f - 1)
    def _():
        y_ref[...] = _layer_norm_rows(y_ref[...], g_ref[...], b_ref[...])

        @pl.when(is_last_tile)
        def _():
            ys_ref[...] = _layer_norm_rows(ys_ref[...], g_ref[...], b_ref[...])


def _ffn(x1, x1bf, x1s, wg_bf, wu_bf, wd_bf, ln_g, ln_b, alpha, cache, qkvs, t_new):
    m = x1.shape[0]
    ms = x1s.shape[0]
    hidden = wg_bf.shape[1]
    n_i = m // FFN_TM
    n_f = hidden // FFN_TF
    n_res = D_MODEL // FFN_RES_W
    assert n_f >= n_res
    g2, b2 = ln_g[None, :], ln_b[None, :]
    dec_batch = cache.shape[0]
    cache_flat = cache.reshape(dec_batch, -1, HEAD_DIM)
    entry_rows = cache_flat.shape[1]
    shift = t_new * KV_SLAB
    cpe = entry_rows // FFN_SHIFT_ROWS
    n_chunks = dec_batch * cpe
    assert cpe * FFN_SHIFT_ROWS == entry_rows and n_chunks <= n_i * n_f and FFN_SHIFT_ROWS % shift == 0
    qkv5 = qkvs.reshape(dec_batch, t_new, qkvs.shape[1], HEADS_PER_GROUP, HEAD_DIM)

    def chunk_of(i, f):
        c = jnp.minimum(i * n_f + f, n_chunks - 1)
        return c // cpe, c % cpe

    def chunk_map(i, f):
        b, k = chunk_of(i, f)
        return b, k, 0

    def follow_map(i, f):
        b, k = chunk_of(i, f)
        return b, jnp.minimum((k + 1) * (FFN_SHIFT_ROWS // shift), entry_rows // shift - 1), 0

    y, ys, nk = pl.pallas_call(
        functools.partial(_ffn_kernel, n_i=n_i, n_f=n_f, alpha=alpha, t_new=t_new, n_chunks=n_chunks,
                          chunks_per_entry=cpe),
        grid=(n_i, n_f),
        in_specs=[pl.BlockSpec((FFN_TM, D_MODEL), lambda i, f: (i, 0)),
                  pl.BlockSpec((FFN_TM, FFN_RES_W), lambda i, f: (i, jnp.minimum(f, n_res - 1))),
                  pl.BlockSpec((D_MODEL, FFN_TF), lambda i, f: (0, f)),
                  pl.BlockSpec((D_MODEL, FFN_TF), lambda i, f: (0, f)),
                  pl.BlockSpec((FFN_TF, D_MODEL), lambda i, f: (f, 0)),
                  pl.BlockSpec((1, D_MODEL), lambda i, f: (0, 0)),
                  pl.BlockSpec((1, D_MODEL), lambda i, f: (0, 0)),
                  pl.BlockSpec((ms, D_MODEL), lambda i, f: (0, 0)),
                  pl.BlockSpec((1, FFN_SHIFT_ROWS, HEAD_DIM), chunk_map),
                  pl.BlockSpec((1, shift, HEAD_DIM), follow_map),
                  pl.BlockSpec((1,) + qkv5.shape[1:], lambda i, f: (chunk_of(i, f)[0], 0, 0, 0, 0))],
        out_specs=[pl.BlockSpec((FFN_TM, D_MODEL), lambda i, f: (i, 0)),
                   pl.BlockSpec((ms, D_MODEL), lambda i, f: (0, 0)),
                   pl.BlockSpec((1, FFN_SHIFT_ROWS, HEAD_DIM), chunk_map)],
        out_shape=[jax.ShapeDtypeStruct((m, D_MODEL), F32), jax.ShapeDtypeStruct((ms, D_MODEL), F32),
                   jax.ShapeDtypeStruct(cache_flat.shape, F32)],
        scratch_shapes=[pltpu.VMEM((ms, D_MODEL), BF16)],
        compiler_params=_cparams(2), name="ffn")(
        x1bf, x1, wg_bf, wu_bf, wd_bf, g2, b2, x1s, cache_flat, cache_flat, qkv5)
    return y, ys, nk.reshape(cache.shape)


def kernel(x_prompt, x_sample, cache_kv_w128, cache_kv_w512, cache_kv_w2048, state_conv, w_in, w_out,
           conv_w, conv_b, conv_ln_g, conv_ln_b, ln1_g, ln1_b, w_gate, w_up, w_down, ln2_g, ln2_b):
    depth = w_in.shape[0]
    batch, seq, _ = x_prompt.shape
    dec_batch, t_new, _ = x_sample.shape
    caches = (cache_kv_w128, cache_kv_w512, cache_kv_w2048)
    alpha = (2.0 * depth) ** 0.25

    cos_p, sin_p = _rope_tables(np.arange(seq))
    cos_s, sin_s = _rope_tables(np.tile(PAST_LEN + np.arange(t_new), dec_batch))

    xp = x_prompt.reshape(batch * seq, D_MODEL)
    xs = x_sample.reshape(dec_batch * t_new, D_MODEL)
    kvp = [[] for _ in range(N_GROUPS)]
    kvs = [[] for _ in range(N_GROUPS)]
    convp, convs = [], []
    for l in range(depth):
        q0, q1, q2, kc0, kc1, kc2, u, kv0, kv1, kv2, qkvs, us = _in_proj(
            xp, xs, w_in[l].astype(BF16), cos_p, sin_p, cos_s, sin_s, batch, seq)
        os, lses = zip(*[_attn_group(q_cm, kv_cm, gi, batch, seq)
                         for gi, (q_cm, kv_cm) in enumerate(zip((q0, q1, q2), (kc0, kc1, kc2)))])
        conv_steps = batch * seq // CONV_TC
        c, (wo_bf, wg_bf, wu_bf, wd_bf) = _conv_prompt(
            u, conv_w[l], conv_b[l], conv_ln_g[l], conv_ln_b[l], batch, seq,
            [(w, 0, w.shape[0] // conv_steps) for w in (w_out[l], w_gate[l], w_up[l], w_down[l])])
        layer_caches = [cc[l] for cc in caches]
        mixs, nconv_s, new_caches = _sample_mixers(qkvs, us, layer_caches, state_conv[l], conv_w[l], conv_b[l],
                                                   conv_ln_g[l], conv_ln_b[l], dec_batch, t_new)
        x1, x1bf, x1s = _out_proj(os, lses, c, xp, wo_bf, ln1_g[l], ln1_b[l], mixs, xs, alpha)
        xp, xs, new_caches[FFN_SHIFT_GROUP] = _ffn(x1, x1bf, x1s, wg_bf, wu_bf, wd_bf, ln2_g[l], ln2_b[l], alpha,
                                                   layer_caches[FFN_SHIFT_GROUP], qkvs, t_new)
        for gi, kv in enumerate((kv0, kv1, kv2)):
            kvp[gi].append(kv.reshape(batch, kv.shape[1] // KV_SLAB, 2, HEADS_PER_GROUP, HEAD_DIM))
            kvs[gi].append(new_caches[gi])
        convp.append(u.reshape(batch, seq, CONV_CH)[:, seq - (CONV_K - 1):])
        convs.append(nconv_s)

    y_prompt = xp.reshape(batch, seq, D_MODEL)
    y_sample = xs.reshape(dec_batch, t_new, D_MODEL)
    return (y_prompt, y_sample, jnp.stack(kvp[0]), jnp.stack(kvp[1]), jnp.stack(kvp[2]), jnp.stack(convp),
            jnp.stack(kvs[0]), jnp.stack(kvs[1]), jnp.stack(kvs[2]), jnp.stack(convs))
```

```python
import functools

import numpy as np
import jax
import jax.numpy as jnp
from jax import lax
from jax.experimental import pallas as pl
from jax.experimental.pallas import tpu as pltpu

D_MODEL = 2048
HEAD_DIM = 128
CONV_CH = D_MODEL // 4
ATTN_WIDTH = D_MODEL - CONV_CH
DILATION_GROUPS = ((128, 1), (512, 4), (2048, 16))
N_GROUPS = len(DILATION_GROUPS)
DILATED_GROUPS = tuple(gi for gi, (_, d) in enumerate(DILATION_GROUPS) if d > 1)
HEADS_PER_GROUP = ATTN_WIDTH // HEAD_DIM // N_GROUPS
GROUP_W = HEADS_PER_GROUP * HEAD_DIM
CONV_K = 31
ROPE_THETA = 10000.0
LN_EPS = 1e-5
Q_BLOCK = 128
ATTN_SCALE = HEAD_DIM ** -0.5
NEG_BIG = -1e30
PAST_LEN = 16384

F32 = jnp.float32
BF16 = jnp.bfloat16

VMEM_LIMIT = 58 * 1024 * 1024
SUBLANES = 8


def _cparams(n_axes):
    return pltpu.CompilerParams(dimension_semantics=("arbitrary",) * n_axes,
                                vmem_limit_bytes=VMEM_LIMIT)


def _layer_norm_rows(y, g, b):
    mu = jnp.mean(y, axis=-1, keepdims=True)
    yc = y - mu
    var = jnp.mean(yc * yc, axis=-1, keepdims=True)
    return yc * lax.rsqrt(var + LN_EPS) * g + b


def _rope_tables(pos):
    half = HEAD_DIM // 2
    inv = ROPE_THETA ** (-np.arange(half, dtype=np.float64) / half)
    ang = np.asarray(pos, np.float64)[:, None] * inv[None, :]
    cos, sin = np.cos(ang), np.sin(ang)
    return (jnp.asarray(np.concatenate([cos, cos], axis=1), F32),
            jnp.asarray(np.concatenate([-sin, sin], axis=1), F32))


def _rope(h, cos, sin):
    parts = []
    for hh in range(HEADS_PER_GROUP):
        hs = h[:, hh * HEAD_DIM:(hh + 1) * HEAD_DIM]
        parts.append(hs * cos + pltpu.roll(hs, HEAD_DIM // 2, axis=1) * sin)
    return jnp.concatenate(parts, axis=1)


IN_TM = 512
IN_TN = GROUP_W
IN_ROWS = 256
PH_Q_GLU, PH_KV = range(2)
N_PHASES = 2


def _kv_window_plan(seq, batch):
    tiles_per_batch = seq // IN_TM
    n_i = batch * tiles_per_batch
    plans = []
    for window, _ in DILATION_GROUPS:
        keep = min(window, seq)
        rb = min(keep, IN_TM)
        first_tile = (seq - keep) // IN_TM
        row_lo = (seq - keep) - first_tile * IN_TM
        writes = []
        for i in range(n_i):
            b, it = divmod(i, tiles_per_batch)
            if it >= first_tile:
                writes.append((i * N_PHASES + PH_KV, (b, it - first_tile)))
        writes.sort()
        tab = np.zeros((n_i * N_PHASES, 2), np.int32)
        w = 0
        for step in range(n_i * N_PHASES):
            while w < len(writes) - 1 and writes[w][0] < step:
                w += 1
            tab[step] = writes[w][1]
        plans.append(dict(keep=keep, rb=rb, row_lo=row_lo, tab=tab))
    return plans


def _in_proj_kernel(tab_ref, x_ref, w_ref, cos_ref, sin_ref, xs_ref, coss_ref, sins_ref,
                    q0_ref, q1_ref, q2_ref, kc0_ref, kc1_ref, kc2_ref, u_ref, kv0_ref, kv1_ref, kv2_ref,
                    qkvs_ref, us_ref, xbf, de_scr, xsbf, *, n_i, plans):
    del tab_ref
    i = pl.program_id(0)
    phase = pl.program_id(1)
    kv_refs = (kv0_ref, kv1_ref, kv2_ref)
    q_refs = (q0_ref, q1_ref, q2_ref)
    kc_refs = (kc0_ref, kc1_ref, kc2_ref)
    glu_col = 3 * N_GROUPS

    chunks = [slice(c * IN_ROWS, (c + 1) * IN_ROWS) for c in range(IN_TM // IN_ROWS)]

    def col_dot(lhs, col):
        return jnp.dot(lhs, w_ref[:, col * IN_TN:(col + 1) * IN_TN], preferred_element_type=F32)

    def rope_rows(val, rows):
        return _rope(val, cos_ref[rows, :], sin_ref[rows, :])

    def store_class_major(ref, lead, gi, c, val):
        d = DILATION_GROUPS[gi][1]
        per = IN_ROWS // d
        if d == 1:
            ref[lead + (0, chunks[c], slice(None))] = val.astype(BF16)
            return
        slot = DILATED_GROUPS.index(gi)
        for h in range(HEADS_PER_GROUP):
            de_scr[slot, c, h] = val[:, h * HEAD_DIM:(h + 1) * HEAD_DIM]
        for r in range(d):
            for h in range(HEADS_PER_GROUP):
                ref[lead + (r, slice(c * per, (c + 1) * per), slice(h * HEAD_DIM, (h + 1) * HEAD_DIM))] = (
                    de_scr[slot, c, h, pl.ds(r, per, stride=d), :].astype(BF16))

    def store_window_rows(gi, rows, val, first_slab_row):
        row_lo = plans[gi]["row_lo"]
        lo = max(rows.start, row_lo)
        hi = min(rows.stop, row_lo + plans[gi]["rb"])
        if lo < hi:
            for h in range(HEADS_PER_GROUP):
                dst = pl.ds((lo - row_lo) * KV_SLAB + first_slab_row + h, hi - lo, stride=KV_SLAB)
                kv_refs[gi][0, dst, :] = val[lo - rows.start:hi - rows.start, h * HEAD_DIM:(h + 1) * HEAD_DIM]

    def store_heads(slot, val):
        for h in range(HEADS_PER_GROUP):
            qkvs_ref[:, slot, h, :] = val[:, h * HEAD_DIM:(h + 1) * HEAD_DIM]

    is_last = i == n_i - 1

    @pl.when(phase == PH_Q_GLU)
    def _():
        for c, rows in enumerate(chunks):
            xbf[rows, :] = x_ref[rows, :].astype(BF16)
            for gi in range(N_GROUPS):
                store_class_major(q_refs[gi], (0,), gi, c, rope_rows(col_dot(xbf[rows, :], gi), rows))
            u_ref[rows, :] = col_dot(xbf[rows, :], glu_col) * jax.nn.sigmoid(col_dot(xbf[rows, :], glu_col + 1))

        @pl.when(is_last)
        def _():
            xsbf[...] = xs_ref[...].astype(BF16)
            for gi in range(N_GROUPS):
                store_heads(gi, _rope(col_dot(xsbf[...], gi), coss_ref[...], sins_ref[...]))
            us_ref[...] = col_dot(xsbf[...], glu_col) * jax.nn.sigmoid(col_dot(xsbf[...], glu_col + 1))

    @pl.when(phase == PH_KV)
    def _():
        for c, rows in enumerate(chunks):
            for gi in range(N_GROUPS):
                r = rope_rows(col_dot(xbf[rows, :], N_GROUPS + gi), rows)
                store_class_major(kc_refs[gi], (0, 0), gi, c, r)
                store_window_rows(gi, rows, r, 0)
                acc = col_dot(xbf[rows, :], 2 * N_GROUPS + gi)
                store_class_major(kc_refs[gi], (0, 1), gi, c, acc)
                store_window_rows(gi, rows, acc, HEADS_PER_GROUP)

        @pl.when(is_last)
        def _():
            for gi in range(N_GROUPS):
                store_heads(N_GROUPS + gi, _rope(col_dot(xsbf[...], N_GROUPS + gi), coss_ref[...], sins_ref[...]))
                store_heads(2 * N_GROUPS + gi, col_dot(xsbf[...], 2 * N_GROUPS + gi))


def _in_proj(x2, xs2, w_bf, cos_p, sin_p, cos_s, sin_s, batch, seq):
    m = x2.shape[0]
    ms = xs2.shape[0]
    n_i = m // IN_TM
    n_qkv = 3 * N_GROUPS
    assert w_bf.shape[1] == (n_qkv + 2) * IN_TN
    tiles_per_batch = seq // IN_TM
    plans = _kv_window_plan(seq, batch)
    tab = jnp.asarray(np.stack([p["tab"] for p in plans]).reshape(-1))
    n_steps = n_i * N_PHASES

    def kv_map(gi):
        def f(i, p, tab_ref):
            base = (gi * n_steps + i * N_PHASES + p) * 2
            return tab_ref[base], tab_ref[base + 1], 0
        return f

    in_specs = [
        pl.BlockSpec((IN_TM, D_MODEL), lambda i, j, t: (i, 0)),
        pl.BlockSpec(w_bf.shape, lambda i, j, t: (0, 0)),
        pl.BlockSpec((IN_TM, HEAD_DIM), lambda i, j, t: (i % tiles_per_batch, 0)),
        pl.BlockSpec((IN_TM, HEAD_DIM), lambda i, j, t: (i % tiles_per_batch, 0)),
        pl.BlockSpec((ms, D_MODEL), lambda i, j, t: (0, 0)),
        pl.BlockSpec((ms, HEAD_DIM), lambda i, j, t: (0, 0)),
        pl.BlockSpec((ms, HEAD_DIM), lambda i, j, t: (0, 0)),
    ]
    out_shape, out_specs = [], []
    for _, d in DILATION_GROUPS:
        out_shape.append(jax.ShapeDtypeStruct((batch, d, seq // d, GROUP_W), BF16))
        out_specs.append(pl.BlockSpec((1, d, IN_TM // d, GROUP_W),
                                      lambda i, p, t: (i // tiles_per_batch, 0, i % tiles_per_batch, 0)))
    for _, d in DILATION_GROUPS:
        out_shape.append(jax.ShapeDtypeStruct((batch, 2, d, seq // d, GROUP_W), BF16))
        out_specs.append(pl.BlockSpec((1, 2, d, IN_TM // d, GROUP_W),
                                      lambda i, p, t: (i // tiles_per_batch, 0, 0, i % tiles_per_batch, 0)))
    out_shape.append(jax.ShapeDtypeStruct((m, CONV_CH), F32))
    out_specs.append(pl.BlockSpec((IN_TM, CONV_CH), lambda i, j, t: (i, 0)))
    for gi, p in enumerate(plans):
        out_shape.append(jax.ShapeDtypeStruct((batch, p["keep"] * KV_SLAB, HEAD_DIM), F32))
        out_specs.append(pl.BlockSpec((1, p["rb"] * KV_SLAB, HEAD_DIM), kv_map(gi)))
    out_shape += [jax.ShapeDtypeStruct((ms, n_qkv, HEADS_PER_GROUP, HEAD_DIM), F32),
                  jax.ShapeDtypeStruct((ms, CONV_CH), F32)]
    out_specs += [
        pl.BlockSpec((ms, n_qkv, HEADS_PER_GROUP, HEAD_DIM), lambda i, p, t: (0, 0, 0, 0)),
        pl.BlockSpec((ms, CONV_CH), lambda i, j, t: (0, 0)),
    ]
    grid_spec = pltpu.PrefetchScalarGridSpec(
        num_scalar_prefetch=1, grid=(n_i, N_PHASES), in_specs=in_specs, out_specs=out_specs,
        scratch_shapes=[pltpu.VMEM((IN_TM, D_MODEL), BF16),
                        pltpu.VMEM((len(DILATED_GROUPS), IN_TM // IN_ROWS, HEADS_PER_GROUP, IN_ROWS, HEAD_DIM), F32),
                        pltpu.VMEM((ms, D_MODEL), BF16)])
    kern = functools.partial(_in_proj_kernel, n_i=n_i, plans=plans)
    return pl.pallas_call(kern, grid_spec=grid_spec, out_shape=out_shape,
                          compiler_params=_cparams(2), name="in_proj")(
        tab, x2, w_bf, cos_p, sin_p, xs2, cos_s, sin_s)


ATTN_TP = 2048


def _attn_kernel(q_ref, k_ref, v_ref, o_ref, lse_ref, k_scr, v_scr, o_scr, lse_scr, *, d, lc):
    n = pl.program_id(1)
    n_qb = lc // Q_BLOCK

    @pl.when(n == 0)
    def _():
        k_scr[:, 0:Q_BLOCK, :] = jnp.zeros((d, Q_BLOCK, GROUP_W), BF16)
        v_scr[:, 0:Q_BLOCK, :] = jnp.zeros((d, Q_BLOCK, GROUP_W), BF16)

    k_scr[:, Q_BLOCK:, :] = k_ref[0, 0]
    v_scr[:, Q_BLOCK:, :] = v_ref[0, 0]
    qq = lax.broadcasted_iota(jnp.int32, (Q_BLOCK, 2 * Q_BLOCK), 0)
    kk = lax.broadcasted_iota(jnp.int32, (Q_BLOCK, 2 * Q_BLOCK), 1)
    band = (kk >= qq) & (kk <= qq + Q_BLOCK)
    lane = lax.broadcasted_iota(jnp.int32, (Q_BLOCK, HEAD_DIM), 1)

    def unit(u, carry):
        r = u // n_qb
        qb = u % n_qb
        r0 = pl.multiple_of(qb * Q_BLOCK, Q_BLOCK)
        valid = band & ((kk >= Q_BLOCK) | (n > 0) | (qb > 0))
        if d == 1:
            rows = pl.ds(r0, Q_BLOCK)
        else:
            rows = pl.ds(r0 * d + r, Q_BLOCK, stride=d)
        lse_blk = jnp.zeros((Q_BLOCK, HEAD_DIM), F32)
        for h in range(HEADS_PER_GROUP):
            c0 = h * HEAD_DIM
            q = q_ref[0, r, pl.ds(r0, Q_BLOCK), c0:c0 + HEAD_DIM]
            k = k_scr[r, pl.ds(r0, 2 * Q_BLOCK), c0:c0 + HEAD_DIM]
            v = v_scr[r, pl.ds(r0, 2 * Q_BLOCK), c0:c0 + HEAD_DIM]
            s = lax.dot_general(q, k, (((1,), (1,)), ((), ())), preferred_element_type=F32)
            s = jnp.where(valid, s * ATTN_SCALE, NEG_BIG)
            m = jnp.max(s, axis=1, keepdims=True)
            e = jnp.exp(s - m)
            den = jnp.sum(e, axis=1, keepdims=True)
            o_scr[h, rows, :] = jnp.dot(e.astype(BF16), v, preferred_element_type=F32) / den
            lse_blk = jnp.where(lane == h, m + jnp.log(den), lse_blk)
        lse_scr[rows, :] = lse_blk
        return carry

    lax.fori_loop(0, d * n_qb, unit, 0, unroll=d * n_qb)
    k_scr[:, 0:Q_BLOCK, :] = k_scr[:, lc:lc + Q_BLOCK, :]
    v_scr[:, 0:Q_BLOCK, :] = v_scr[:, lc:lc + Q_BLOCK, :]
    for h in range(HEADS_PER_GROUP):
        o_ref[:, h * HEAD_DIM:(h + 1) * HEAD_DIM] = o_scr[h].astype(BF16)
    lse_ref[...] = lse_scr[...]


def _attn_group(q_cm, kv_cm, gi, batch, seq):
    _, d = DILATION_GROUPS[gi]
    lc = ATTN_TP // d
    n_t = seq // ATTN_TP

    def plane(p):
        return pl.BlockSpec((1, 1, d, lc, GROUP_W), lambda b, n: (b, p, 0, n, 0))

    return pl.pallas_call(
        functools.partial(_attn_kernel, d=d, lc=lc),
        grid=(batch, n_t),
        in_specs=[pl.BlockSpec((1, d, lc, GROUP_W), lambda b, n: (b, 0, n, 0)), plane(0), plane(1)],
        out_specs=[pl.BlockSpec((ATTN_TP, GROUP_W), lambda b, n: (b * n_t + n, 0)),
                   pl.BlockSpec((ATTN_TP, HEAD_DIM), lambda b, n: (b * n_t + n, 0))],
        out_shape=[jax.ShapeDtypeStruct((batch * seq, GROUP_W), BF16),
                   jax.ShapeDtypeStruct((batch * seq, HEAD_DIM), F32)],
        scratch_shapes=[pltpu.VMEM((d, lc + Q_BLOCK, GROUP_W), BF16),
                        pltpu.VMEM((d, lc + Q_BLOCK, GROUP_W), BF16),
                        pltpu.VMEM((HEADS_PER_GROUP, ATTN_TP, HEAD_DIM), F32),
                        pltpu.VMEM((ATTN_TP, HEAD_DIM), F32)],
        compiler_params=_cparams(2), name=f"attn_g{gi}")(q_cm, kv_cm, kv_cm)


CONV_TC = 512
CONV_HALO = 32
CONV_ROWS = 64


def _conv_kernel(u_ref, halo_ref, w_ref, b_ref, g_ref, beta_ref, *rest, n_t, stream_chunks):
    n_s = len(stream_chunks)
    src_refs, c_ref, dst_refs, hist = rest[:n_s], rest[n_s], rest[n_s + 1:2 * n_s + 1], rest[2 * n_s + 1]
    t = pl.program_id(1)
    step = pl.program_id(0) * n_t + t

    for src, dst, n_chunks in zip(src_refs, dst_refs, stream_chunks):
        @pl.when(step < n_chunks)
        def _(src=src, dst=dst):
            dst[...] = src[...].astype(BF16)

    hist[0:CONV_HALO, :] = jnp.where(t > 0, halo_ref[0], 0.0)
    hist[CONV_HALO:CONV_HALO + CONV_TC, :] = u_ref[0]
    hist[CONV_HALO + CONV_TC:, :] = jnp.zeros((SUBLANES, CONV_CH), F32)
    lead = CONV_HALO - (CONV_K - 1)
    for rc in range(CONV_TC // CONV_ROWS):
        r0 = rc * CONV_ROWS
        acc = jnp.broadcast_to(b_ref[...], (CONV_ROWS, CONV_CH))
        for s in range(SUBLANES):
            grp = None
            for c in range(s, lead + CONV_K, SUBLANES):
                if c < lead:
                    continue
                term = hist[r0 + c - s:r0 + c - s + CONV_ROWS + SUBLANES, :] * w_ref[c - lead:c - lead + 1, :]
                grp = term if grp is None else grp + term
            acc = acc + grp[s:s + CONV_ROWS, :]
        y = _layer_norm_rows(acc, g_ref[...], beta_ref[...])
        c_ref[0, r0:r0 + CONV_ROWS, :] = (y * jax.nn.sigmoid(y)).astype(BF16)


def _conv_prompt(u2, conv_w, conv_b, ln_g, ln_b, batch, seq, cast_streams):
    u3 = u2.reshape(batch, seq, CONV_CH)
    per = CONV_TC // CONV_HALO
    n_t = seq // CONV_TC
    vec = pl.BlockSpec((1, CONV_CH), lambda b, t: (0, 0))
    stream_specs, stream_chunks = [], []
    for w, axis, chunk in cast_streams:
        n_chunks = w.shape[axis] // chunk
        assert w.ndim == 2 and n_chunks * chunk == w.shape[axis] and n_chunks <= batch * n_t
        block = (chunk, w.shape[1]) if axis == 0 else (w.shape[0], chunk)

        def index_map(b, t, axis=axis, n_chunks=n_chunks):
            k = jnp.minimum(b * n_t + t, n_chunks - 1)
            return (k, 0) if axis == 0 else (0, k)

        stream_specs.append(pl.BlockSpec(block, index_map))
        stream_chunks.append(n_chunks)
    outs = pl.pallas_call(
        functools.partial(_conv_kernel, n_t=n_t, stream_chunks=tuple(stream_chunks)),
        grid=(batch, n_t),
        in_specs=[pl.BlockSpec((1, CONV_TC, CONV_CH), lambda b, t: (b, t, 0)),
                  pl.BlockSpec((1, CONV_HALO, CONV_CH), lambda b, t: (b, jnp.maximum(t * per - 1, 0), 0)),
                  pl.BlockSpec((CONV_K, CONV_CH), lambda b, t: (0, 0)), vec, vec, vec] + stream_specs,
        out_specs=[pl.BlockSpec((1, CONV_TC, CONV_CH), lambda b, t: (b, t, 0))] + stream_specs,
        out_shape=[jax.ShapeDtypeStruct((batch, seq, CONV_CH), BF16)]
                  + [jax.ShapeDtypeStruct(w.shape, BF16) for w, _, _ in cast_streams],
        scratch_shapes=[pltpu.VMEM((CONV_HALO + CONV_TC + SUBLANES, CONV_CH), F32)],
        compiler_params=_cparams(2), name="conv_prompt")(
        u3, u3, conv_w, conv_b[None, :], ln_g[None, :], ln_b[None, :], *[w for w, _, _ in cast_streams])
    return outs[0].reshape(batch * seq, CONV_CH), outs[1:]


KV_SLAB = 2 * HEADS_PER_GROUP
SAMPLE_SHIFT_GROUPS = (0, 1)
FFN_SHIFT_GROUP = 2
FFN_SHIFT_ROWS = 2048


def _shift_cache(cref, out_ref, d, t_new):
    slab = d * KV_SLAB
    shift = t_new * KV_SLAB
    n_rows = cref.shape[1]
    if shift % slab == 0:
        s = shift // slab
        out_ref[0, 0:(n_rows - s) * slab] = cref[0, s:n_rows].reshape((n_rows - s) * slab, HEAD_DIM)
    else:
        assert shift < slab and shift % 8 == 0
        out_ref[0, 0:slab - shift] = cref[0, 0, shift:slab]

        def body(ci, carry):
            out_ref[0, pl.ds(pl.multiple_of(ci * slab - shift, 8), slab)] = cref[0, ci]
            return carry

        lax.fori_loop(1, n_rows, body, 0)


def _new_kv_rows(qkv_ref, gi, t_new):
    slabs = []
    for t in range(t_new):
        slabs += [qkv_ref[0, t, N_GROUPS + gi], qkv_ref[0, t, 2 * N_GROUPS + gi]]
    return jnp.concatenate(slabs, axis=0)


def _sample_kernel(qkv_ref, c0_ref, c1_ref, c2_ref, st_ref, u_ref, w_ref, b_ref, g_ref, beta_ref,
                   mix_ref, nconv_ref, *rest, t_new):
    cache_refs = (c0_ref, c1_ref, c2_ref)
    nk_refs = dict(zip(SAMPLE_SHIFT_GROUPS, rest[:-1]))
    hist = rest[-1]
    hp = HEADS_PER_GROUP
    outs = [[None] * N_GROUPS for _ in range(t_new)]
    lses = [[None] * N_GROUPS for _ in range(t_new)]
    for gi, (_, d) in enumerate(DILATION_GROUPS):
        cref = cache_refs[gi]
        n_rows = cref.shape[1]
        row = lax.broadcasted_iota(jnp.int32, (n_rows, hp, 1), 0)
        if gi in nk_refs:
            _shift_cache(cref, nk_refs[gi], d, t_new)
            keep = nk_refs[gi].shape[1] - t_new * KV_SLAB
            nk_refs[gi][0, keep:, :] = _new_kv_rows(qkv_ref, gi, t_new)
        for t in range(t_new):
            s0 = (t % d) * KV_SLAB
            first_row = t // d
            q = qkv_ref[0, t, gi][None]
            kc = cref[0, :, s0:s0 + hp, :]
            vc = cref[0, :, s0 + hp:s0 + 2 * hp, :]
            s_c = jnp.sum(kc * q, axis=2, keepdims=True) * ATTN_SCALE
            if first_row > 0:
                s_c = jnp.where(row >= first_row, s_c, NEG_BIG)
            newest = [t - d * jj for jj in range(t // d + 1)]
            s_n, v_n = [], []
            for tn in newest:
                kn = qkv_ref[0, tn, N_GROUPS + gi][None]
                v_n.append(qkv_ref[0, tn, 2 * N_GROUPS + gi][None])
                s_n.append(jnp.sum(kn * q, axis=2, keepdims=True) * ATTN_SCALE)
            m = jnp.max(s_c, axis=0, keepdims=True)
            for sn in s_n:
                m = jnp.maximum(m, sn)
            e_c = jnp.exp(s_c - m)
            den = jnp.sum(e_c, axis=0, keepdims=True)
            acc = jnp.sum(e_c * vc, axis=0, keepdims=True)
            for sn, v1 in zip(s_n, v_n):
                e_n = jnp.exp(sn - m)
                den = den + e_n
                acc = acc + e_n * v1
            outs[t][gi] = acc / den
            lses[t][gi] = m + jnp.log(den)
    for t in range(t_new):
        ls = lses[t]
        m = functools.reduce(jnp.maximum, ls)
        es = [jnp.exp(l - m) for l in ls]
        tot = functools.reduce(lambda a, b: a + b, es)
        for gi in range(N_GROUPS):
            slab = (outs[t][gi] * (es[gi] / tot))[0]
            for h in range(hp):
                c0 = gi * GROUP_W + h * HEAD_DIM
                mix_ref[0, t:t + 1, c0:c0 + HEAD_DIM] = slab[h:h + 1, :]
    n_state = CONV_K - 1
    hist[0:n_state, :] = st_ref[0]
    hist[n_state:n_state + t_new, :] = u_ref[0]
    acc = jnp.broadcast_to(b_ref[...], (t_new, CONV_CH))
    for k in range(CONV_K):
        acc = acc + hist[k:k + t_new, :] * w_ref[k:k + 1, :]
    y = _layer_norm_rows(acc, g_ref[...], beta_ref[...])
    mix_ref[0, :, ATTN_WIDTH:] = y * jax.nn.sigmoid(y)
    nconv_ref[0] = hist[t_new:t_new + n_state, :]


def _sample_mixers(qkvs, us, caches, state, conv_w, conv_b, ln_g, ln_b, dec_batch, t_new):
    n_qkv = qkvs.shape[1]
    hp = HEADS_PER_GROUP
    qkv5 = qkvs.reshape(dec_batch, t_new, n_qkv, hp, HEAD_DIM)
    u3 = us.reshape(dec_batch, t_new, CONV_CH)
    cache_in, cache_specs, nk_shapes, nk_specs = [], [], [], []
    for gi, ((window, d), cache) in enumerate(zip(DILATION_GROUPS, caches)):
        buf = cache.shape[1]
        assert buf == window and buf % d == 0
        n_rows = buf // d
        cache_in.append(cache.reshape(dec_batch, n_rows, d * KV_SLAB, HEAD_DIM))
        if gi in SAMPLE_SHIFT_GROUPS:
            cache_specs.append(pl.BlockSpec((1, n_rows, d * KV_SLAB, HEAD_DIM), lambda b: (b, 0, 0, 0)))
            nk_shapes.append(jax.ShapeDtypeStruct((dec_batch, buf * KV_SLAB, HEAD_DIM), F32))
            nk_specs.append(pl.BlockSpec((1, buf * KV_SLAB, HEAD_DIM), lambda b: (b, 0, 0)))
        else:
            used = min(d, t_new) * KV_SLAB
            cache_specs.append(pl.BlockSpec((1, n_rows, used, HEAD_DIM), lambda b: (b, 0, 0, 0)))
    vec = pl.BlockSpec((1, CONV_CH), lambda b: (0, 0))
    n_state = CONV_K - 1
    mix, nconv, *nks = pl.pallas_call(
        functools.partial(_sample_kernel, t_new=t_new),
        grid=(dec_batch,),
        in_specs=[pl.BlockSpec((1, t_new, n_qkv, hp, HEAD_DIM), lambda b: (b, 0, 0, 0, 0))] + cache_specs + [
            pl.BlockSpec((1, n_state, CONV_CH), lambda b: (b, 0, 0)),
            pl.BlockSpec((1, t_new, CONV_CH), lambda b: (b, 0, 0)),
            pl.BlockSpec((CONV_K, CONV_CH), lambda b: (0, 0)), vec, vec, vec],
        out_specs=[pl.BlockSpec((1, t_new, D_MODEL), lambda b: (b, 0, 0)),
                   pl.BlockSpec((1, n_state, CONV_CH), lambda b: (b, 0, 0))] + nk_specs,
        out_shape=[jax.ShapeDtypeStruct((dec_batch, t_new, D_MODEL), F32),
                   jax.ShapeDtypeStruct((dec_batch, n_state, CONV_CH), F32)] + nk_shapes,
        scratch_shapes=[pltpu.VMEM((n_state + t_new + 6, CONV_CH), F32)],
        compiler_params=_cparams(1), name="sample_mixers")(
        qkv5, *cache_in, state, u3, conv_w, conv_b[None, :], ln_g[None, :], ln_b[None, :])
    new_caches = {gi: nk.reshape(caches[gi].shape) for gi, nk in zip(SAMPLE_SHIFT_GROUPS, nks)}
    return mix.reshape(dec_batch * t_new, D_MODEL), nconv, new_caches


OUT_TM = 512
OUT_ROWS = 128


def _out_proj_kernel(o0_ref, o1_ref, o2_ref, l0_ref, l1_ref, l2_ref, c_ref, x_ref, w_ref, g_ref, b_ref,
                     mixs_ref, xs_ref, x1_ref, x1bf_ref, x1s_ref, mix_scr, y_scr, *, n_i, alpha):
    i = pl.program_id(0)
    o_refs = (o0_ref, o1_ref, o2_ref)
    chunks = [slice(r0, r0 + OUT_ROWS) for r0 in range(0, OUT_TM, OUT_ROWS)]

    def project(rows):
        ls = [l0_ref[rows, :], l1_ref[rows, :], l2_ref[rows, :]]
        m = jnp.maximum(jnp.maximum(ls[0], ls[1]), ls[2])
        es = [jnp.exp(l - m) for l in ls]
        inv = 1.0 / (es[0] + es[1] + es[2])
        for gi in range(N_GROUPS):
            a = es[gi] * inv
            for h in range(HEADS_PER_GROUP):
                c0 = h * HEAD_DIM
                og = o_refs[gi][rows, c0:c0 + HEAD_DIM].astype(F32)
                mix_scr[rows, gi * GROUP_W + c0:gi * GROUP_W + c0 + HEAD_DIM] = (og * a[:, h:h + 1]).astype(BF16)
        mix_scr[rows, ATTN_WIDTH:] = c_ref[rows, :]
        y_scr[rows, :] = (jnp.dot(mix_scr[rows, :], w_ref[...], preferred_element_type=F32)
                          + alpha * x_ref[rows, :])

    def normalize(rows):
        x1 = _layer_norm_rows(y_scr[rows, :], g_ref[...], b_ref[...])
        x1_ref[rows, :] = x1
        x1bf_ref[rows, :] = x1.astype(BF16)

    project(chunks[0])
    for c in range(1, len(chunks)):
        project(chunks[c])
        normalize(chunks[c - 1])
    normalize(chunks[-1])

    @pl.when(i == n_i - 1)
    def _():
        ys = jnp.dot(mixs_ref[...].astype(BF16), w_ref[...], preferred_element_type=F32) + alpha * xs_ref[...]
        x1s_ref[...] = _layer_norm_rows(ys, g_ref[...], b_ref[...])


def _out_proj(os, lses, c, x2, w_bf, ln_g, ln_b, mixs, xs2, alpha):
    m = x2.shape[0]
    ms = xs2.shape[0]
    n_i = m // OUT_TM
    row = lambda w: pl.BlockSpec((OUT_TM, w), lambda i: (i, 0))
    whole = lambda a: pl.BlockSpec(a.shape, lambda i: (0,) * a.ndim)
    g2, b2 = ln_g[None, :], ln_b[None, :]
    return pl.pallas_call(
        functools.partial(_out_proj_kernel, n_i=n_i, alpha=alpha),
        grid=(n_i,),
        in_specs=[row(GROUP_W)] * 3 + [row(HEAD_DIM)] * 3 + [row(CONV_CH), row(D_MODEL),
                  whole(w_bf), whole(g2), whole(b2), whole(mixs), whole(xs2)],
        out_specs=[row(D_MODEL), row(D_MODEL), pl.BlockSpec((ms, D_MODEL), lambda i: (0, 0))],
        out_shape=[jax.ShapeDtypeStruct((m, D_MODEL), F32), jax.ShapeDtypeStruct((m, D_MODEL), BF16),
                   jax.ShapeDtypeStruct((ms, D_MODEL), F32)],
        scratch_shapes=[pltpu.VMEM((OUT_TM, D_MODEL), BF16), pltpu.VMEM((OUT_TM, D_MODEL), F32)],
        compiler_params=_cparams(1), name="out_proj")(
        *os, *lses, c, x2, w_bf, g2, b2, mixs, xs2)


FFN_TM = 1024
FFN_TF = 512
FFN_TAIL_ROWS = 256
FFN_RES_W = 256


def _ffn_kernel(xbf_ref, xres_ref, wg_ref, wu_ref, wd_ref, g_ref, b_ref, xs_ref, ca_ref, cb_ref, qkv_ref,
                y_ref, ys_ref, nk_ref, xsbf, *, n_i, n_f, alpha, t_new, n_chunks, chunks_per_entry):
    i = pl.program_id(0)
    f = pl.program_id(1)

    def shift_cache_chunk():
        shift = t_new * KV_SLAB
        rows = ca_ref.shape[1]
        chunk = jnp.minimum(i * n_f + f, n_chunks - 1) % chunks_per_entry
        nk_ref[0, 0:rows - shift, :] = ca_ref[0, shift:rows, :]
        nk_ref[0, rows - shift:rows, :] = jnp.where(chunk == chunks_per_entry - 1,
                                                    _new_kv_rows(qkv_ref, FFN_SHIFT_GROUP, t_new), cb_ref[0])

    def swiglu_down(xb):
        gate = jnp.dot(xb, wg_ref[...], preferred_element_type=F32)
        up = jnp.dot(xb, wu_ref[...], preferred_element_type=F32)
        act = (gate * jax.nn.sigmoid(gate) * up).astype(BF16)
        return jnp.dot(act, wd_ref[...], preferred_element_type=F32)

    is_last_tile = i == n_i - 1

    @pl.when(f == 0)
    def _():
        y_ref[...] = jnp.zeros_like(y_ref)

        @pl.when(is_last_tile)
        def _():
            xsbf[...] = xs_ref[...].astype(BF16)
            ys_ref[...] = alpha * xs_ref[...]

    @pl.when(i < n_i - 1)
    def _():
        shift_cache_chunk()
        y_ref[...] += swiglu_down(xbf_ref[...])

    @pl.when(is_last_tile)
    def _():
        shift_cache_chunk()
        head, tail = slice(0, FFN_TM - FFN_TAIL_ROWS), slice(FFN_TM - FFN_TAIL_ROWS, FFN_TM)
        y_ref[head, :] += swiglu_down(xbf_ref[head, :])
        down = swiglu_down(jnp.concatenate([xbf_ref[tail, :], xsbf[...]], axis=0))
        y_ref[tail, :] += down[:FFN_TAIL_ROWS]
        ys_ref[...] += down[FFN_TAIL_ROWS:]

    for c in range(D_MODEL // FFN_RES_W):
        @pl.when(f == c)
        def _(c=c):
            y_ref[:, c * FFN_RES_W:(c + 1) * FFN_RES_W] += alpha * xres_ref[...]

    @pl.when(f == n_f - 1)
    def _():
        y_ref[...] = _layer_norm_rows(y_ref[...], g_ref[...], b_ref[...])

        @pl.when(is_last_tile)
        def _():
            ys_ref[...] = _layer_norm_rows(ys_ref[...], g_ref[...], b_ref[...])


def _ffn(x1, x1bf, x1s, wg_bf, wu_bf, wd_bf, ln_g, ln_b, alpha, cache, qkvs, t_new):
    m = x1.shape[0]
    ms = x1s.shape[0]
    hidden = wg_bf.shape[1]
    n_i = m // FFN_TM
    n_f = hidden // FFN_TF
    n_res = D_MODEL // FFN_RES_W
    assert n_f >= n_res
    g2, b2 = ln_g[None, :], ln_b[None, :]
    dec_batch = cache.shape[0]
    cache_flat = cache.reshape(dec_batch, -1, HEAD_DIM)
    entry_rows = cache_flat.shape[1]
    shift = t_new * KV_SLAB
    cpe = entry_rows // FFN_SHIFT_ROWS
    n_chunks = dec_batch * cpe
    assert cpe * FFN_SHIFT_ROWS == entry_rows and n_chunks <= n_i * n_f and FFN_SHIFT_ROWS % shift == 0
    qkv5 = qkvs.reshape(dec_batch, t_new, qkvs.shape[1], HEADS_PER_GROUP, HEAD_DIM)

    def chunk_of(i, f):
        c = jnp.minimum(i * n_f + f, n_chunks - 1)
        return c // cpe, c % cpe

    def chunk_map(i, f):
        b, k = chunk_of(i, f)
        return b, k, 0

    def follow_map(i, f):
        b, k = chunk_of(i, f)
        return b, jnp.minimum((k + 1) * (FFN_SHIFT_ROWS // shift), entry_rows // shift - 1), 0

    y, ys, nk = pl.pallas_call(
        functools.partial(_ffn_kernel, n_i=n_i, n_f=n_f, alpha=alpha, t_new=t_new, n_chunks=n_chunks,
                          chunks_per_entry=cpe),
        grid=(n_i, n_f),
        in_specs=[pl.BlockSpec((FFN_TM, D_MODEL), lambda i, f: (i, 0)),
                  pl.BlockSpec((FFN_TM, FFN_RES_W), lambda i, f: (i, jnp.minimum(f, n_res - 1))),
                  pl.BlockSpec((D_MODEL, FFN_TF), lambda i, f: (0, f)),
                  pl.BlockSpec((D_MODEL, FFN_TF), lambda i, f: (0, f)),
                  pl.BlockSpec((FFN_TF, D_MODEL), lambda i, f: (f, 0)),
                  pl.BlockSpec((1, D_MODEL), lambda i, f: (0, 0)),
                  pl.BlockSpec((1, D_MODEL), lambda i, f: (0, 0)),
                  pl.BlockSpec((ms, D_MODEL), lambda i, f: (0, 0)),
                  pl.BlockSpec((1, FFN_SHIFT_ROWS, HEAD_DIM), chunk_map),
                  pl.BlockSpec((1, shift, HEAD_DIM), follow_map),
                  pl.BlockSpec((1,) + qkv5.shape[1:], lambda i, f: (chunk_of(i, f)[0], 0, 0, 0, 0))],
        out_specs=[pl.BlockSpec((FFN_TM, D_MODEL), lambda i, f: (i, 0)),
                   pl.BlockSpec((ms, D_MODEL), lambda i, f: (0, 0)),
                   pl.BlockSpec((1, FFN_SHIFT_ROWS, HEAD_DIM), chunk_map)],
        out_shape=[jax.ShapeDtypeStruct((m, D_MODEL), F32), jax.ShapeDtypeStruct((ms, D_MODEL), F32),
                   jax.ShapeDtypeStruct(cache_flat.shape, F32)],
        scratch_shapes=[pltpu.VMEM((ms, D_MODEL), BF16)],
        compiler_params=_cparams(2), name="ffn")(
        x1bf, x1, wg_bf, wu_bf, wd_bf, g2, b2, x1s, cache_flat, cache_flat, qkv5)
    return y, ys, nk.reshape(cache.shape)


def kernel(x_prompt, x_sample, cache_kv_w128, cache_kv_w512, cache_kv_w2048, state_conv, w_in, w_out,
           conv_w, conv_b, conv_ln_g, conv_ln_b, ln1_g, ln1_b, w_gate, w_up, w_down, ln2_g, ln2_b):
    depth = w_in.shape[0]
    batch, seq, _ = x_prompt.shape
    dec_batch, t_new, _ = x_sample.shape
    caches = (cache_kv_w128, cache_kv_w512, cache_kv_w2048)
    alpha = (2.0 * depth) ** 0.25

    cos_p, sin_p = _rope_tables(np.arange(seq))
    cos_s, sin_s = _rope_tables(np.tile(PAST_LEN + np.arange(t_new), dec_batch))

    xp = x_prompt.reshape(batch * seq, D_MODEL)
    xs = x_sample.reshape(dec_batch * t_new, D_MODEL)
    kvp = [[] for _ in range(N_GROUPS)]
    kvs = [[] for _ in range(N_GROUPS)]
    convp, convs = [], []
    for l in range(depth):
        q0, q1, q2, kc0, kc1, kc2, u, kv0, kv1, kv2, qkvs, us = _in_proj(
            xp, xs, w_in[l].astype(BF16), cos_p, sin_p, cos_s, sin_s, batch, seq)
        os, lses = zip(*[_attn_group(q_cm, kv_cm, gi, batch, seq)
                         for gi, (q_cm, kv_cm) in enumerate(zip((q0, q1, q2), (kc0, kc1, kc2)))])
        conv_steps = batch * seq // CONV_TC
        c, (wo_bf, wg_bf, wu_bf, wd_bf) = _conv_prompt(
            u, conv_w[l], conv_b[l], conv_ln_g[l], conv_ln_b[l], batch, seq,
            [(w, 0, w.shape[0] // conv_steps) for w in (w_out[l], w_gate[l], w_up[l], w_down[l])])
        layer_caches = [cc[l] for cc in caches]
        mixs, nconv_s, new_caches = _sample_mixers(qkvs, us, layer_caches, state_conv[l], conv_w[l], conv_b[l],
                                                   conv_ln_g[l], conv_ln_b[l], dec_batch, t_new)
        x1, x1bf, x1s = _out_proj(os, lses, c, xp, wo_bf, ln1_g[l], ln1_b[l], mixs, xs, alpha)
        xp, xs, new_caches[FFN_SHIFT_GROUP] = _ffn(x1, x1bf, x1s, wg_bf, wu_bf, wd_bf, ln2_g[l], ln2_b[l], alpha,
                                                   layer_caches[FFN_SHIFT_GROUP], qkvs, t_new)
        for gi, kv in enumerate((kv0, kv1, kv2)):
            kvp[gi].append(kv.reshape(batch, kv.shape[1] // KV_SLAB, 2, HEADS_PER_GROUP, HEAD_DIM))
            kvs[gi].append(new_caches[gi])
        convp.append(u.reshape(batch, seq, CONV_CH)[:, seq - (CONV_K - 1):])
        convs.append(nconv_s)

    y_prompt = xp.reshape(batch, seq, D_MODEL)
    y_sample = xs.reshape(dec_batch, t_new, D_MODEL)
    return (y_prompt, y_sample, jnp.stack(kvp[0]), jnp.stack(kvp[1]), jnp.stack(kvp[2]), jnp.stack(convp),
            jnp.stack(kvs[0]), jnp.stack(kvs[1]), jnp.stack(kvs[2]), jnp.stack(convs))
```

```python
import functools

import numpy as np
import jax
import jax.numpy as jnp
from jax import lax
from jax.experimental import pallas as pl
from jax.experimental.pallas import tpu as pltpu

D_MODEL = 2048
HEAD_DIM = 128
CONV_CH = D_MODEL // 4
ATTN_WIDTH = D_MODEL - CONV_CH
DILATION_GROUPS = ((128, 1), (512, 4), (2048, 16))
N_GROUPS = len(DILATION_GROUPS)
DILATED_GROUPS = tuple(gi for gi, (_, d) in enumerate(DILATION_GROUPS) if d > 1)
HEADS_PER_GROUP = ATTN_WIDTH // HEAD_DIM // N_GROUPS
GROUP_W = HEADS_PER_GROUP * HEAD_DIM
CONV_K = 31
ROPE_THETA = 10000.0
LN_EPS = 1e-5
Q_BLOCK = 128
ATTN_SCALE = HEAD_DIM ** -0.5
NEG_BIG = -1e30
PAST_LEN = 16384

F32 = jnp.float32
BF16 = jnp.bfloat16

VMEM_LIMIT = 58 * 1024 * 1024
SUBLANES = 8


def _cparams(n_axes):
    return pltpu.CompilerParams(dimension_semantics=("arbitrary",) * n_axes,
                                vmem_limit_bytes=VMEM_LIMIT)


def _layer_norm_rows(y, g, b):
    mu = jnp.mean(y, axis=-1, keepdims=True)
    yc = y - mu
    var = jnp.mean(yc * yc, axis=-1, keepdims=True)
    return yc * lax.rsqrt(var + LN_EPS) * g + b


def _rope_tables(pos):
    half = HEAD_DIM // 2
    inv = ROPE_THETA ** (-np.arange(half, dtype=np.float64) / half)
    ang = np.asarray(pos, np.float64)[:, None] * inv[None, :]
    cos, sin = np.cos(ang), np.sin(ang)
    return (jnp.asarray(np.concatenate([cos, cos], axis=1), F32),
            jnp.asarray(np.concatenate([-sin, sin], axis=1), F32))


def _rope(h, cos, sin):
    parts = []
    for hh in range(HEADS_PER_GROUP):
        hs = h[:, hh * HEAD_DIM:(hh + 1) * HEAD_DIM]
        parts.append(hs * cos + pltpu.roll(hs, HEAD_DIM // 2, axis=1) * sin)
    return jnp.concatenate(parts, axis=1)


IN_TM = 512
IN_TN = GROUP_W
IN_ROWS = 256
PH_Q_GLU, PH_KV = range(2)
N_PHASES = 2


def _kv_window_plan(seq, batch):
    tiles_per_batch = seq // IN_TM
    n_i = batch * tiles_per_batch
    plans = []
    for window, _ in DILATION_GROUPS:
        keep = min(window, seq)
        rb = min(keep, IN_TM)
        first_tile = (seq - keep) // IN_TM
        row_lo = (seq - keep) - first_tile * IN_TM
        writes = []
        for i in range(n_i):
            b, it = divmod(i, tiles_per_batch)
            if it >= first_tile:
                writes.append((i * N_PHASES + PH_KV, (b, it - first_tile)))
        writes.sort()
        tab = np.zeros((n_i * N_PHASES, 2), np.int32)
        w = 0
        for step in range(n_i * N_PHASES):
            while w < len(writes) - 1 and writes[w][0] < step:
                w += 1
            tab[step] = writes[w][1]
        plans.append(dict(keep=keep, rb=rb, row_lo=row_lo, tab=tab))
    return plans


def _in_proj_kernel(tab_ref, x_ref, w_ref, cos_ref, sin_ref, xs_ref, coss_ref, sins_ref,
                    q0_ref, q1_ref, q2_ref, kc0_ref, kc1_ref, kc2_ref, u_ref, kv0_ref, kv1_ref, kv2_ref,
                    qkvs_ref, us_ref, xbf, de_scr, xsbf, *, n_i, plans):
    del tab_ref
    i = pl.program_id(0)
    phase = pl.program_id(1)
    kv_refs = (kv0_ref, kv1_ref, kv2_ref)
    q_refs = (q0_ref, q1_ref, q2_ref)
    kc_refs = (kc0_ref, kc1_ref, kc2_ref)
    glu_col = 3 * N_GROUPS

    chunks = [slice(c * IN_ROWS, (c + 1) * IN_ROWS) for c in range(IN_TM // IN_ROWS)]

    def col_dot(lhs, col):
        return jnp.dot(lhs, w_ref[:, col * IN_TN:(col + 1) * IN_TN], preferred_element_type=F32)

    def rope_rows(val, rows):
        return _rope(val, cos_ref[rows, :], sin_ref[rows, :])

    def store_class_major(ref, lead, gi, c, val):
        d = DILATION_GROUPS[gi][1]
        per = IN_ROWS // d
        if d == 1:
            ref[lead + (0, chunks[c], slice(None))] = val.astype(BF16)
            return
        slot = DILATED_GROUPS.index(gi)
        for h in range(HEADS_PER_GROUP):
            de_scr[slot, c, h] = val[:, h * HEAD_DIM:(h + 1) * HEAD_DIM]
        for r in range(d):
            for h in range(HEADS_PER_GROUP):
                ref[lead + (r, slice(c * per, (c + 1) * per), slice(h * HEAD_DIM, (h + 1) * HEAD_DIM))] = (
                    de_scr[slot, c, h, pl.ds(r, per, stride=d), :].astype(BF16))

    def store_window_rows(gi, rows, val, first_slab_row):
        row_lo = plans[gi]["row_lo"]
        lo = max(rows.start, row_lo)
        hi = min(rows.stop, row_lo + plans[gi]["rb"])
        if lo < hi:
            for h in range(HEADS_PER_GROUP):
                dst = pl.ds((lo - row_lo) * KV_SLAB + first_slab_row + h, hi - lo, stride=KV_SLAB)
                kv_refs[gi][0, dst, :] = val[lo - rows.start:hi - rows.start, h * HEAD_DIM:(h + 1) * HEAD_DIM]

    def store_heads(slot, val):
        for h in range(HEADS_PER_GROUP):
            qkvs_ref[:, slot, h, :] = val[:, h * HEAD_DIM:(h + 1) * HEAD_DIM]

    is_last = i == n_i - 1

    @pl.when(phase == PH_Q_GLU)
    def _():
        for c, rows in enumerate(chunks):
            xbf[rows, :] = x_ref[rows, :].astype(BF16)
            for gi in range(N_GROUPS):
                store_class_major(q_refs[gi], (0,), gi, c, rope_rows(col_dot(xbf[rows, :], gi), rows))
            u_ref[rows, :] = col_dot(xbf[rows, :], glu_col) * jax.nn.sigmoid(col_dot(xbf[rows, :], glu_col + 1))

        @pl.when(is_last)
        def _():
            xsbf[...] = xs_ref[...].astype(BF16)
            for gi in range(N_GROUPS):
                store_heads(gi, _rope(col_dot(xsbf[...], gi), coss_ref[...], sins_ref[...]))
            us_ref[...] = col_dot(xsbf[...], glu_col) * jax.nn.sigmoid(col_dot(xsbf[...], glu_col + 1))

    @pl.when(phase == PH_KV)
    def _():
        for c, rows in enumerate(chunks):
            for gi in range(N_GROUPS):
                r = rope_rows(col_dot(xbf[rows, :], N_GROUPS + gi), rows)
                store_class_major(kc_refs[gi], (0, 0), gi, c, r)
                store_window_rows(gi, rows, r, 0)
            for gi in range(N_GROUPS):
                acc = col_dot(xbf[rows, :], 2 * N_GROUPS + gi)
                store_class_major(kc_refs[gi], (0, 1), gi, c, acc)
                store_window_rows(gi, rows, acc, HEADS_PER_GROUP)

        @pl.when(is_last)
        def _():
            for gi in range(N_GROUPS):
                store_heads(N_GROUPS + gi, _rope(col_dot(xsbf[...], N_GROUPS + gi), coss_ref[...], sins_ref[...]))
                store_heads(2 * N_GROUPS + gi, col_dot(xsbf[...], 2 * N_GROUPS + gi))


def _in_proj(x2, xs2, w_bf, cos_p, sin_p, cos_s, sin_s, batch, seq):
    m = x2.shape[0]
    ms = xs2.shape[0]
    n_i = m // IN_TM
    n_qkv = 3 * N_GROUPS
    assert w_bf.shape[1] == (n_qkv + 2) * IN_TN
    tiles_per_batch = seq // IN_TM
    plans = _kv_window_plan(seq, batch)
    tab = jnp.asarray(np.stack([p["tab"] for p in plans]).reshape(-1))
    n_steps = n_i * N_PHASES

    def kv_map(gi):
        def f(i, p, tab_ref):
            base = (gi * n_steps + i * N_PHASES + p) * 2
            return tab_ref[base], tab_ref[base + 1], 0
        return f

    in_specs = [
        pl.BlockSpec((IN_TM, D_MODEL), lambda i, j, t: (i, 0)),
        pl.BlockSpec(w_bf.shape, lambda i, j, t: (0, 0)),
        pl.BlockSpec((IN_TM, HEAD_DIM), lambda i, j, t: (i % tiles_per_batch, 0)),
        pl.BlockSpec((IN_TM, HEAD_DIM), lambda i, j, t: (i % tiles_per_batch, 0)),
        pl.BlockSpec((ms, D_MODEL), lambda i, j, t: (0, 0)),
        pl.BlockSpec((ms, HEAD_DIM), lambda i, j, t: (0, 0)),
        pl.BlockSpec((ms, HEAD_DIM), lambda i, j, t: (0, 0)),
    ]
    out_shape, out_specs = [], []
    for _, d in DILATION_GROUPS:
        out_shape.append(jax.ShapeDtypeStruct((batch, d, seq // d, GROUP_W), BF16))
        out_specs.append(pl.BlockSpec((1, d, IN_TM // d, GROUP_W),
                                      lambda i, p, t: (i // tiles_per_batch, 0, i % tiles_per_batch, 0)))
    for _, d in DILATION_GROUPS:
        out_shape.append(jax.ShapeDtypeStruct((batch, 2, d, seq // d, GROUP_W), BF16))
        out_specs.append(pl.BlockSpec((1, 2, d, IN_TM // d, GROUP_W),
                                      lambda i, p, t: (i // tiles_per_batch, 0, 0, i % tiles_per_batch, 0)))
    out_shape.append(jax.ShapeDtypeStruct((m, CONV_CH), F32))
    out_specs.append(pl.BlockSpec((IN_TM, CONV_CH), lambda i, j, t: (i, 0)))
    for gi, p in enumerate(plans):
        out_shape.append(jax.ShapeDtypeStruct((batch, p["keep"] * KV_SLAB, HEAD_DIM), F32))
        out_specs.append(pl.BlockSpec((1, p["rb"] * KV_SLAB, HEAD_DIM), kv_map(gi)))
    out_shape += [jax.ShapeDtypeStruct((ms, n_qkv, HEADS_PER_GROUP, HEAD_DIM), F32),
                  jax.ShapeDtypeStruct((ms, CONV_CH), F32)]
    out_specs += [
        pl.BlockSpec((ms, n_qkv, HEADS_PER_GROUP, HEAD_DIM), lambda i, p, t: (0, 0, 0, 0)),
        pl.BlockSpec((ms, CONV_CH), lambda i, j, t: (0, 0)),
    ]
    grid_spec = pltpu.PrefetchScalarGridSpec(
        num_scalar_prefetch=1, grid=(n_i, N_PHASES), in_specs=in_specs, out_specs=out_specs,
        scratch_shapes=[pltpu.VMEM((IN_TM, D_MODEL), BF16),
                        pltpu.VMEM((len(DILATED_GROUPS), IN_TM // IN_ROWS, HEADS_PER_GROUP, IN_ROWS, HEAD_DIM), F32),
                        pltpu.VMEM((ms, D_MODEL), BF16)])
    kern = functools.partial(_in_proj_kernel, n_i=n_i, plans=plans)
    return pl.pallas_call(kern, grid_spec=grid_spec, out_shape=out_shape,
                          compiler_params=_cparams(2), name="in_proj")(
        tab, x2, w_bf, cos_p, sin_p, xs2, cos_s, sin_s)


ATTN_TP = 2048


def _attn_kernel(q_ref, k_ref, v_ref, o_ref, lse_ref, k_scr, v_scr, o_scr, lse_scr, *, d, lc):
    n = pl.program_id(1)
    n_qb = lc // Q_BLOCK

    @pl.when(n == 0)
    def _():
        k_scr[:, 0:Q_BLOCK, :] = jnp.zeros((d, Q_BLOCK, GROUP_W), BF16)
        v_scr[:, 0:Q_BLOCK, :] = jnp.zeros((d, Q_BLOCK, GROUP_W), BF16)

    k_scr[:, Q_BLOCK:, :] = k_ref[0, 0]
    v_scr[:, Q_BLOCK:, :] = v_ref[0, 0]
    qq = lax.broadcasted_iota(jnp.int32, (Q_BLOCK, 2 * Q_BLOCK), 0)
    kk = lax.broadcasted_iota(jnp.int32, (Q_BLOCK, 2 * Q_BLOCK), 1)
    band = (kk >= qq) & (kk <= qq + Q_BLOCK)
    lane = lax.broadcasted_iota(jnp.int32, (Q_BLOCK, HEAD_DIM), 1)

    def unit(u, carry):
        r = u // n_qb
        qb = u % n_qb
        r0 = pl.multiple_of(qb * Q_BLOCK, Q_BLOCK)
        valid = band & ((kk >= Q_BLOCK) | (n > 0) | (qb > 0))
        if d == 1:
            rows = pl.ds(r0, Q_BLOCK)
        else:
            rows = pl.ds(r0 * d + r, Q_BLOCK, stride=d)
        lse_blk = jnp.zeros((Q_BLOCK, HEAD_DIM), F32)
        for h in range(HEADS_PER_GROUP):
            c0 = h * HEAD_DIM
            q = q_ref[0, r, pl.ds(r0, Q_BLOCK), c0:c0 + HEAD_DIM]
            k = k_scr[r, pl.ds(r0, 2 * Q_BLOCK), c0:c0 + HEAD_DIM]
            v = v_scr[r, pl.ds(r0, 2 * Q_BLOCK), c0:c0 + HEAD_DIM]
            s = lax.dot_general(q, k, (((1,), (1,)), ((), ())), preferred_element_type=F32)
            s = jnp.where(valid, s * ATTN_SCALE, NEG_BIG)
            m = jnp.max(s, axis=1, keepdims=True)
            e = jnp.exp(s - m)
            den = jnp.sum(e, axis=1, keepdims=True)
            o_scr[h, rows, :] = jnp.dot(e.astype(BF16), v, preferred_element_type=F32) / den
            lse_blk = jnp.where(lane == h, m + jnp.log(den), lse_blk)
        lse_scr[rows, :] = lse_blk
        return carry

    lax.fori_loop(0, d * n_qb, unit, 0, unroll=d * n_qb)
    k_scr[:, 0:Q_BLOCK, :] = k_scr[:, lc:lc + Q_BLOCK, :]
    v_scr[:, 0:Q_BLOCK, :] = v_scr[:, lc:lc + Q_BLOCK, :]
    for h in range(HEADS_PER_GROUP):
        o_ref[:, h * HEAD_DIM:(h + 1) * HEAD_DIM] = o_scr[h].astype(BF16)
    lse_ref[...] = lse_scr[...]


def _attn_group(q_cm, kv_cm, gi, batch, seq):
    _, d = DILATION_GROUPS[gi]
    lc = ATTN_TP // d
    n_t = seq // ATTN_TP

    def plane(p):
        return pl.BlockSpec((1, 1, d, lc, GROUP_W), lambda b, n: (b, p, 0, n, 0))

    return pl.pallas_call(
        functools.partial(_attn_kernel, d=d, lc=lc),
        grid=(batch, n_t),
        in_specs=[pl.BlockSpec((1, d, lc, GROUP_W), lambda b, n: (b, 0, n, 0)), plane(0), plane(1)],
        out_specs=[pl.BlockSpec((ATTN_TP, GROUP_W), lambda b, n: (b * n_t + n, 0)),
                   pl.BlockSpec((ATTN_TP, HEAD_DIM), lambda b, n: (b * n_t + n, 0))],
        out_shape=[jax.ShapeDtypeStruct((batch * seq, GROUP_W), BF16),
                   jax.ShapeDtypeStruct((batch * seq, HEAD_DIM), F32)],
        scratch_shapes=[pltpu.VMEM((d, lc + Q_BLOCK, GROUP_W), BF16),
                        pltpu.VMEM((d, lc + Q_BLOCK, GROUP_W), BF16),
                        pltpu.VMEM((HEADS_PER_GROUP, ATTN_TP, HEAD_DIM), F32),
                        pltpu.VMEM((ATTN_TP, HEAD_DIM), F32)],
        compiler_params=_cparams(2), name=f"attn_g{gi}")(q_cm, kv_cm, kv_cm)


CONV_TC = 512
CONV_HALO = 32
CONV_ROWS = 256


def _conv_kernel(u_ref, halo_ref, w_ref, b_ref, g_ref, beta_ref, *rest, n_t, stream_chunks):
    n_s = len(stream_chunks)
    src_refs, c_ref, dst_refs, hist = rest[:n_s], rest[n_s], rest[n_s + 1:2 * n_s + 1], rest[2 * n_s + 1]
    t = pl.program_id(1)
    step = pl.program_id(0) * n_t + t

    for src, dst, n_chunks in zip(src_refs, dst_refs, stream_chunks):
        @pl.when(step < n_chunks)
        def _(src=src, dst=dst):
            dst[...] = src[...].astype(BF16)

    hist[0:CONV_HALO, :] = jnp.where(t > 0, halo_ref[0], 0.0)
    hist[CONV_HALO:CONV_HALO + CONV_TC, :] = u_ref[0]
    hist[CONV_HALO + CONV_TC:, :] = jnp.zeros((SUBLANES, CONV_CH), F32)
    lead = CONV_HALO - (CONV_K - 1)
    for rc in range(CONV_TC // CONV_ROWS):
        r0 = rc * CONV_ROWS
        acc = jnp.broadcast_to(b_ref[...], (CONV_ROWS, CONV_CH))
        for s in range(SUBLANES):
            grp = None
            for c in range(s, lead + CONV_K, SUBLANES):
                if c < lead:
                    continue
                term = hist[r0 + c - s:r0 + c - s + CONV_ROWS + SUBLANES, :] * w_ref[c - lead:c - lead + 1, :]
                grp = term if grp is None else grp + term
            acc = acc + grp[s:s + CONV_ROWS, :]
        y = _layer_norm_rows(acc, g_ref[...], beta_ref[...])
        c_ref[0, r0:r0 + CONV_ROWS, :] = (y * jax.nn.sigmoid(y)).astype(BF16)


def _conv_prompt(u2, conv_w, conv_b, ln_g, ln_b, batch, seq, cast_streams):
    u3 = u2.reshape(batch, seq, CONV_CH)
    per = CONV_TC // CONV_HALO
    n_t = seq // CONV_TC
    vec = pl.BlockSpec((1, CONV_CH), lambda b, t: (0, 0))
    stream_specs, stream_chunks = [], []
    for w, axis, chunk in cast_streams:
        n_chunks = w.shape[axis] // chunk
        assert w.ndim == 2 and n_chunks * chunk == w.shape[axis] and n_chunks <= batch * n_t
        block = (chunk, w.shape[1]) if axis == 0 else (w.shape[0], chunk)

        def index_map(b, t, axis=axis, n_chunks=n_chunks):
            k = jnp.minimum(b * n_t + t, n_chunks - 1)
            return (k, 0) if axis == 0 else (0, k)

        stream_specs.append(pl.BlockSpec(block, index_map))
        stream_chunks.append(n_chunks)
    outs = pl.pallas_call(
        functools.partial(_conv_kernel, n_t=n_t, stream_chunks=tuple(stream_chunks)),
        grid=(batch, n_t),
        in_specs=[pl.BlockSpec((1, CONV_TC, CONV_CH), lambda b, t: (b, t, 0)),
                  pl.BlockSpec((1, CONV_HALO, CONV_CH), lambda b, t: (b, jnp.maximum(t * per - 1, 0), 0)),
                  pl.BlockSpec((CONV_K, CONV_CH), lambda b, t: (0, 0)), vec, vec, vec] + stream_specs,
        out_specs=[pl.BlockSpec((1, CONV_TC, CONV_CH), lambda b, t: (b, t, 0))] + stream_specs,
        out_shape=[jax.ShapeDtypeStruct((batch, seq, CONV_CH), BF16)]
                  + [jax.ShapeDtypeStruct(w.shape, BF16) for w, _, _ in cast_streams],
        scratch_shapes=[pltpu.VMEM((CONV_HALO + CONV_TC + SUBLANES, CONV_CH), F32)],
        compiler_params=_cparams(2), name="conv_prompt")(
        u3, u3, conv_w, conv_b[None, :], ln_g[None, :], ln_b[None, :], *[w for w, _, _ in cast_streams])
    return outs[0].reshape(batch * seq, CONV_CH), outs[1:]


KV_SLAB = 2 * HEADS_PER_GROUP
SAMPLE_SHIFT_GROUPS = (0, 1)
FFN_SHIFT_GROUP = 2
FFN_SHIFT_ROWS = 2048


def _shift_cache(cref, out_ref, d, t_new):
    slab = d * KV_SLAB
    shift = t_new * KV_SLAB
    n_rows = cref.shape[1]
    if shift % slab == 0:
        s = shift // slab
        out_ref[0, 0:(n_rows - s) * slab] = cref[0, s:n_rows].reshape((n_rows - s) * slab, HEAD_DIM)
    else:
        assert shift < slab and shift % 8 == 0
        out_ref[0, 0:slab - shift] = cref[0, 0, shift:slab]

        def body(ci, carry):
            out_ref[0, pl.ds(pl.multiple_of(ci * slab - shift, 8), slab)] = cref[0, ci]
            return carry

        lax.fori_loop(1, n_rows, body, 0)


def _new_kv_rows(qkv_ref, gi, t_new):
    slabs = []
    for t in range(t_new):
        slabs += [qkv_ref[0, t, N_GROUPS + gi], qkv_ref[0, t, 2 * N_GROUPS + gi]]
    return jnp.concatenate(slabs, axis=0)


def _sample_kernel(qkv_ref, c0_ref, c1_ref, c2_ref, st_ref, u_ref, w_ref, b_ref, g_ref, beta_ref,
                   mix_ref, nconv_ref, *rest, t_new):
    cache_refs = (c0_ref, c1_ref, c2_ref)
    nk_refs = dict(zip(SAMPLE_SHIFT_GROUPS, rest[:-1]))
    hist = rest[-1]
    hp = HEADS_PER_GROUP
    outs = [[None] * N_GROUPS for _ in range(t_new)]
    lses = [[None] * N_GROUPS for _ in range(t_new)]
    for gi, (_, d) in enumerate(DILATION_GROUPS):
        cref = cache_refs[gi]
        n_rows = cref.shape[1]
        row = lax.broadcasted_iota(jnp.int32, (n_rows, hp, 1), 0)
        if gi in nk_refs:
            _shift_cache(cref, nk_refs[gi], d, t_new)
            keep = nk_refs[gi].shape[1] - t_new * KV_SLAB
            nk_refs[gi][0, keep:, :] = _new_kv_rows(qkv_ref, gi, t_new)
        for t in range(t_new):
            s0 = (t % d) * KV_SLAB
            first_row = t // d
            q = qkv_ref[0, t, gi][None]
            kc = cref[0, :, s0:s0 + hp, :]
            vc = cref[0, :, s0 + hp:s0 + 2 * hp, :]
            s_c = jnp.sum(kc * q, axis=2, keepdims=True) * ATTN_SCALE
            if first_row > 0:
                s_c = jnp.where(row >= first_row, s_c, NEG_BIG)
            newest = [t - d * jj for jj in range(t // d + 1)]
            s_n, v_n = [], []
            for tn in newest:
                kn = qkv_ref[0, tn, N_GROUPS + gi][None]
                v_n.append(qkv_ref[0, tn, 2 * N_GROUPS + gi][None])
                s_n.append(jnp.sum(kn * q, axis=2, keepdims=True) * ATTN_SCALE)
            m = jnp.max(s_c, axis=0, keepdims=True)
            for sn in s_n:
                m = jnp.maximum(m, sn)
            e_c = jnp.exp(s_c - m)
            den = jnp.sum(e_c, axis=0, keepdims=True)
            acc = jnp.sum(e_c * vc, axis=0, keepdims=True)
            for sn, v1 in zip(s_n, v_n):
                e_n = jnp.exp(sn - m)
                den = den + e_n
                acc = acc + e_n * v1
            outs[t][gi] = acc / den
            lses[t][gi] = m + jnp.log(den)
    for t in range(t_new):
        ls = lses[t]
        m = functools.reduce(jnp.maximum, ls)
        es = [jnp.exp(l - m) for l in ls]
        tot = functools.reduce(lambda a, b: a + b, es)
        for gi in range(N_GROUPS):
            slab = (outs[t][gi] * (es[gi] / tot))[0]
            for h in range(hp):
                c0 = gi * GROUP_W + h * HEAD_DIM
                mix_ref[0, t:t + 1, c0:c0 + HEAD_DIM] = slab[h:h + 1, :]
    n_state = CONV_K - 1
    hist[0:n_state, :] = st_ref[0]
    hist[n_state:n_state + t_new, :] = u_ref[0]
    acc = jnp.broadcast_to(b_ref[...], (t_new, CONV_CH))
    for k in range(CONV_K):
        acc = acc + hist[k:k + t_new, :] * w_ref[k:k + 1, :]
    y = _layer_norm_rows(acc, g_ref[...], beta_ref[...])
    mix_ref[0, :, ATTN_WIDTH:] = y * jax.nn.sigmoid(y)
    nconv_ref[0] = hist[t_new:t_new + n_state, :]


def _sample_mixers(qkvs, us, caches, state, conv_w, conv_b, ln_g, ln_b, dec_batch, t_new):
    n_qkv = qkvs.shape[1]
    hp = HEADS_PER_GROUP
    qkv5 = qkvs.reshape(dec_batch, t_new, n_qkv, hp, HEAD_DIM)
    u3 = us.reshape(dec_batch, t_new, CONV_CH)
    cache_in, cache_specs, nk_shapes, nk_specs = [], [], [], []
    for gi, ((window, d), cache) in enumerate(zip(DILATION_GROUPS, caches)):
        buf = cache.shape[1]
        assert buf == window and buf % d == 0
        n_rows = buf // d
        cache_in.append(cache.reshape(dec_batch, n_rows, d * KV_SLAB, HEAD_DIM))
        if gi in SAMPLE_SHIFT_GROUPS:
            cache_specs.append(pl.BlockSpec((1, n_rows, d * KV_SLAB, HEAD_DIM), lambda b: (b, 0, 0, 0)))
            nk_shapes.append(jax.ShapeDtypeStruct((dec_batch, buf * KV_SLAB, HEAD_DIM), F32))
            nk_specs.append(pl.BlockSpec((1, buf * KV_SLAB, HEAD_DIM), lambda b: (b, 0, 0)))
        else:
            used = min(d, t_new) * KV_SLAB
            cache_specs.append(pl.BlockSpec((1, n_rows, used, HEAD_DIM), lambda b: (b, 0, 0, 0)))
    vec = pl.BlockSpec((1, CONV_CH), lambda b: (0, 0))
    n_state = CONV_K - 1
    mix, nconv, *nks = pl.pallas_call(
        functools.partial(_sample_kernel, t_new=t_new),
        grid=(dec_batch,),
        in_specs=[pl.BlockSpec((1, t_new, n_qkv, hp, HEAD_DIM), lambda b: (b, 0, 0, 0, 0))] + cache_specs + [
            pl.BlockSpec((1, n_state, CONV_CH), lambda b: (b, 0, 0)),
            pl.BlockSpec((1, t_new, CONV_CH), lambda b: (b, 0, 0)),
            pl.BlockSpec((CONV_K, CONV_CH), lambda b: (0, 0)), vec, vec, vec],
        out_specs=[pl.BlockSpec((1, t_new, D_MODEL), lambda b: (b, 0, 0)),
                   pl.BlockSpec((1, n_state, CONV_CH), lambda b: (b, 0, 0))] + nk_specs,
        out_shape=[jax.ShapeDtypeStruct((dec_batch, t_new, D_MODEL), F32),
                   jax.ShapeDtypeStruct((dec_batch, n_state, CONV_CH), F32)] + nk_shapes,
        scratch_shapes=[pltpu.VMEM((n_state + t_new + 6, CONV_CH), F32)],
        compiler_params=_cparams(1), name="sample_mixers")(
        qkv5, *cache_in, state, u3, conv_w, conv_b[None, :], ln_g[None, :], ln_b[None, :])
    new_caches = {gi: nk.reshape(caches[gi].shape) for gi, nk in zip(SAMPLE_SHIFT_GROUPS, nks)}
    return mix.reshape(dec_batch * t_new, D_MODEL), nconv, new_caches


OUT_TM = 512
OUT_ROWS = 128


def _out_proj_kernel(o0_ref, o1_ref, o2_ref, l0_ref, l1_ref, l2_ref, c_ref, x_ref, w_ref, g_ref, b_ref,
                     mixs_ref, xs_ref, x1_ref, x1bf_ref, x1s_ref, mix_scr, y_scr, *, n_i, alpha):
    i = pl.program_id(0)
    o_refs = (o0_ref, o1_ref, o2_ref)
    chunks = [slice(r0, r0 + OUT_ROWS) for r0 in range(0, OUT_TM, OUT_ROWS)]

    def project(rows):
        ls = [l0_ref[rows, :], l1_ref[rows, :], l2_ref[rows, :]]
        m = jnp.maximum(jnp.maximum(ls[0], ls[1]), ls[2])
        es = [jnp.exp(l - m) for l in ls]
        inv = 1.0 / (es[0] + es[1] + es[2])
        for gi in range(N_GROUPS):
            a = es[gi] * inv
            for h in range(HEADS_PER_GROUP):
                c0 = h * HEAD_DIM
                og = o_refs[gi][rows, c0:c0 + HEAD_DIM].astype(F32)
                mix_scr[rows, gi * GROUP_W + c0:gi * GROUP_W + c0 + HEAD_DIM] = (og * a[:, h:h + 1]).astype(BF16)
        mix_scr[rows, ATTN_WIDTH:] = c_ref[rows, :]
        y_scr[rows, :] = (jnp.dot(mix_scr[rows, :], w_ref[...], preferred_element_type=F32)
                          + alpha * x_ref[rows, :])

    def normalize(rows):
        x1 = _layer_norm_rows(y_scr[rows, :], g_ref[...], b_ref[...])
        x1_ref[rows, :] = x1
        x1bf_ref[rows, :] = x1.astype(BF16)

    project(chunks[0])
    for c in range(1, len(chunks)):
        project(chunks[c])
        normalize(chunks[c - 1])
    normalize(chunks[-1])

    @pl.when(i == n_i - 1)
    def _():
        ys = jnp.dot(mixs_ref[...].astype(BF16), w_ref[...], preferred_element_type=F32) + alpha * xs_ref[...]
        x1s_ref[...] = _layer_norm_rows(ys, g_ref[...], b_ref[...])


def _out_proj(os, lses, c, x2, w_bf, ln_g, ln_b, mixs, xs2, alpha):
    m = x2.shape[0]
    ms = xs2.shape[0]
    n_i = m // OUT_TM
    row = lambda w: pl.BlockSpec((OUT_TM, w), lambda i: (i, 0))
    whole = lambda a: pl.BlockSpec(a.shape, lambda i: (0,) * a.ndim)
    g2, b2 = ln_g[None, :], ln_b[None, :]
    return pl.pallas_call(
        functools.partial(_out_proj_kernel, n_i=n_i, alpha=alpha),
        grid=(n_i,),
        in_specs=[row(GROUP_W)] * 3 + [row(HEAD_DIM)] * 3 + [row(CONV_CH), row(D_MODEL),
                  whole(w_bf), whole(g2), whole(b2), whole(mixs), whole(xs2)],
        out_specs=[row(D_MODEL), row(D_MODEL), pl.BlockSpec((ms, D_MODEL), lambda i: (0, 0))],
        out_shape=[jax.ShapeDtypeStruct((m, D_MODEL), F32), jax.ShapeDtypeStruct((m, D_MODEL), BF16),
                   jax.ShapeDtypeStruct((ms, D_MODEL), F32)],
        scratch_shapes=[pltpu.VMEM((OUT_TM, D_MODEL), BF16), pltpu.VMEM((OUT_TM, D_MODEL), F32)],
        compiler_params=_cparams(1), name="out_proj")(
        *os, *lses, c, x2, w_bf, g2, b2, mixs, xs2)


FFN_TM = 1024
FFN_TF = 512
FFN_TAIL_ROWS = 256
FFN_RES_W = 256


def _ffn_kernel(xbf_ref, xres_ref, wg_ref, wu_ref, wd_ref, g_ref, b_ref, xs_ref, ca_ref, cb_ref, qkv_ref,
                y_ref, ys_ref, nk_ref, xsbf, *, n_i, n_f, alpha, t_new, n_chunks, chunks_per_entry):
    i = pl.program_id(0)
    f = pl.program_id(1)

    def shift_cache_chunk():
        shift = t_new * KV_SLAB
        rows = ca_ref.shape[1]
        chunk = jnp.minimum(i * n_f + f, n_chunks - 1) % chunks_per_entry
        nk_ref[0, 0:rows - shift, :] = ca_ref[0, shift:rows, :]
        nk_ref[0, rows - shift:rows, :] = jnp.where(chunk == chunks_per_entry - 1,
                                                    _new_kv_rows(qkv_ref, FFN_SHIFT_GROUP, t_new), cb_ref[0])

    def swiglu_down(xb):
        gate = jnp.dot(xb, wg_ref[...], preferred_element_type=F32)
        up = jnp.dot(xb, wu_ref[...], preferred_element_type=F32)
        act = (gate * jax.nn.sigmoid(gate) * up).astype(BF16)
        return jnp.dot(act, wd_ref[...], preferred_element_type=F32)

    is_last_tile = i == n_i - 1

    @pl.when(f == 0)
    def _():
        y_ref[...] = jnp.zeros_like(y_ref)

        @pl.when(is_last_tile)
        def _():
            xsbf[...] = xs_ref[...].astype(BF16)
            ys_ref[...] = alpha * xs_ref[...]

    @pl.when(i < n_i - 1)
    def _():
        shift_cache_chunk()
        y_ref[...] += swiglu_down(xbf_ref[...])

    @pl.when(is_last_tile)
    def _():
        shift_cache_chunk()
        head, tail = slice(0, FFN_TM - FFN_TAIL_ROWS), slice(FFN_TM - FFN_TAIL_ROWS, FFN_TM)
        y_ref[head, :] += swiglu_down(xbf_ref[head, :])
        down = swiglu_down(jnp.concatenate([xbf_ref[tail, :], xsbf[...]], axis=0))
        y_ref[tail, :] += down[:FFN_TAIL_ROWS]
        ys_ref[...] += down[FFN_TAIL_ROWS:]

    for c in range(D_MODEL // FFN_RES_W):
        @pl.when(f == c)
        def _(c=c):
            y_ref[:, c * FFN_RES_W:(c + 1) * FFN_RES_W] += alpha * xres_ref[...]

    @pl.when(f == n_f - 1)
    def _():
        y_ref[...] = _layer_norm_rows(y_ref[...], g_ref[...], b_ref[...])

        @pl.when(is_last_tile)
        def _():
            ys_ref[...] = _layer_norm_rows(ys_ref[...], g_ref[...], b_ref[...])


def _ffn(x1, x1bf, x1s, wg_bf, wu_bf, wd_bf, ln_g, ln_b, alpha, cache, qkvs, t_new):
    m = x1.shape[0]
    ms = x1s.shape[0]
    hidden = wg_bf.shape[1]
    n_i = m // FFN_TM
    n_f = hidden // FFN_TF
    n_res = D_MODEL // FFN_RES_W
    assert n_f >= n_res
    g2, b2 = ln_g[None, :], ln_b[None, :]
    dec_batch = cache.shape[0]
    cache_flat = cache.reshape(dec_batch, -1, HEAD_DIM)
    entry_rows = cache_flat.shape[1]
    shift = t_new * KV_SLAB
    cpe = entry_rows // FFN_SHIFT_ROWS
    n_chunks = dec_batch * cpe
    assert cpe * FFN_SHIFT_ROWS == entry_rows and n_chunks <= n_i * n_f and FFN_SHIFT_ROWS % shift == 0
    qkv5 = qkvs.reshape(dec_batch, t_new, qkvs.shape[1], HEADS_PER_GROUP, HEAD_DIM)

    def chunk_of(i, f):
        c = jnp.minimum(i * n_f + f, n_chunks - 1)
        return c // cpe, c % cpe

    def chunk_map(i, f):
        b, k = chunk_of(i, f)
        return b, k, 0

    def follow_map(i, f):
        b, k = chunk_of(i, f)
        return b, jnp.minimum((k + 1) * (FFN_SHIFT_ROWS // shift), entry_rows // shift - 1), 0

    y, ys, nk = pl.pallas_call(
        functools.partial(_ffn_kernel, n_i=n_i, n_f=n_f, alpha=alpha, t_new=t_new, n_chunks=n_chunks,
                          chunks_per_entry=cpe),
        grid=(n_i, n_f),
        in_specs=[pl.BlockSpec((FFN_TM, D_MODEL), lambda i, f: (i, 0)),
                  pl.BlockSpec((FFN_TM, FFN_RES_W), lambda i, f: (i, jnp.minimum(f, n_res - 1))),
                  pl.BlockSpec((D_MODEL, FFN_TF), lambda i, f: (0, f)),
                  pl.BlockSpec((D_MODEL, FFN_TF), lambda i, f: (0, f)),
                  pl.BlockSpec((FFN_TF, D_MODEL), lambda i, f: (f, 0)),
                  pl.BlockSpec((1, D_MODEL), lambda i, f: (0, 0)),
                  pl.BlockSpec((1, D_MODEL), lambda i, f: (0, 0)),
                  pl.BlockSpec((ms, D_MODEL), lambda i, f: (0, 0)),
                  pl.BlockSpec((1, FFN_SHIFT_ROWS, HEAD_DIM), chunk_map),
                  pl.BlockSpec((1, shift, HEAD_DIM), follow_map),
                  pl.BlockSpec((1,) + qkv5.shape[1:], lambda i, f: (chunk_of(i, f)[0], 0, 0, 0, 0))],
        out_specs=[pl.BlockSpec((FFN_TM, D_MODEL), lambda i, f: (i, 0)),
                   pl.BlockSpec((ms, D_MODEL), lambda i, f: (0, 0)),
                   pl.BlockSpec((1, FFN_SHIFT_ROWS, HEAD_DIM), chunk_map)],
        out_shape=[jax.ShapeDtypeStruct((m, D_MODEL), F32), jax.ShapeDtypeStruct((ms, D_MODEL), F32),
                   jax.ShapeDtypeStruct(cache_flat.shape, F32)],
        scratch_shapes=[pltpu.VMEM((ms, D_MODEL), BF16)],
        compiler_params=_cparams(2), name="ffn")(
        x1bf, x1, wg_bf, wu_bf, wd_bf, g2, b2, x1s, cache_flat, cache_flat, qkv5)
    return y, ys, nk.reshape(cache.shape)


def kernel(x_prompt, x_sample, cache_kv_w128, cache_kv_w512, cache_kv_w2048, state_conv, w_in, w_out,
           conv_w, conv_b, conv_ln_g, conv_ln_b, ln1_g, ln1_b, w_gate, w_up, w_down, ln2_g, ln2_b):
    depth = w_in.shape[0]
    batch, seq, _ = x_prompt.shape
    dec_batch, t_new, _ = x_sample.shape
    caches = (cache_kv_w128, cache_kv_w512, cache_kv_w2048)
    alpha = (2.0 * depth) ** 0.25

    cos_p, sin_p = _rope_tables(np.arange(seq))
    cos_s, sin_s = _rope_tables(np.tile(PAST_LEN + np.arange(t_new), dec_batch))

    xp = x_prompt.reshape(batch * seq, D_MODEL)
    xs = x_sample.reshape(dec_batch * t_new, D_MODEL)
    kvp = [[] for _ in range(N_GROUPS)]
    kvs = [[] for _ in range(N_GROUPS)]
    convp, convs = [], []
    for l in range(depth):
        q0, q1, q2, kc0, kc1, kc2, u, kv0, kv1, kv2, qkvs, us = _in_proj(
            xp, xs, w_in[l].astype(BF16), cos_p, sin_p, cos_s, sin_s, batch, seq)
        os, lses = zip(*[_attn_group(q_cm, kv_cm, gi, batch, seq)
                         for gi, (q_cm, kv_cm) in enumerate(zip((q0, q1, q2), (kc0, kc1, kc2)))])
        conv_steps = batch * seq // CONV_TC
        c, (wo_bf, wg_bf, wu_bf, wd_bf) = _conv_prompt(
            u, conv_w[l], conv_b[l], conv_ln_g[l], conv_ln_b[l], batch, seq,
            [(w, 0, w.shape[0] // conv_steps) for w in (w_out[l], w_gate[l], w_up[l], w_down[l])])
        layer_caches = [cc[l] for cc in caches]
        mixs, nconv_s, new_caches = _sample_mixers(qkvs, us, layer_caches, state_conv[l], conv_w[l], conv_b[l],
                                                   conv_ln_g[l], conv_ln_b[l], dec_batch, t_new)
        x1, x1bf, x1s = _out_proj(os, lses, c, xp, wo_bf, ln1_g[l], ln1_b[l], mixs, xs, alpha)
        xp, xs, new_caches[FFN_SHIFT_GROUP] = _ffn(x1, x1bf, x1s, wg_bf, wu_bf, wd_bf, ln2_g[l], ln2_b[l], alpha,
                                                   layer_caches[FFN_SHIFT_GROUP], qkvs, t_new)
        for gi, kv in enumerate((kv0, kv1, kv2)):
            kvp[gi].append(kv.reshape(batch, kv.shape[1] // KV_SLAB, 2, HEADS_PER_GROUP, HEAD_DIM))
            kvs[gi].append(new_caches[gi])
        convp.append(u.reshape(batch, seq, CONV_CH)[:, seq - (CONV_K - 1):])
        convs.append(nconv_s)

    y_prompt = xp.reshape(batch, seq, D_MODEL)
    y_sample = xs.reshape(dec_batch, t_new, D_MODEL)
    return (y_prompt, y_sample, jnp.stack(kvp[0]), jnp.stack(kvp[1]), jnp.stack(kvp[2]), jnp.stack(convp),
            jnp.stack(kvs[0]), jnp.stack(kvs[1]), jnp.stack(kvs[2]), jnp.stack(convs))
```
